```python
import jax, jax.numpy as jnp
from jax import lax
import numpy as np

D_MODEL = 1024
BATCH = 16
SEQ = 256
DEPTH = 2
DEC_BATCH = 2
DEC_SEQ = 2048
PAST_LEN = 256

GRID_W = 64
EPS = 1e-6
N_BRANCH = 3
BRANCH_DIM = 512
CONV_DIM = BRANCH_DIM
CONV_K = 3
N_HEADS = 8
QK_NOPE = 64
QK_ROPE = 32
V_HEAD = 64
Q_LORA = 384
KV_LORA = 256
MLA_LATENT = KV_LORA + QK_ROPE
MLA_V_WIDTH = N_HEADS * V_HEAD
ROPE_THETA = 10000.0
Q_BLOCK = 128
POOL_WINDOWS = (2, 4, 8, 16)
N_POOL = 4
POOL_GROUP = 128
POOL_DIM = N_POOL * POOL_GROUP
SPLIT_SIZES = (CONV_DIM, CONV_DIM, CONV_DIM, CONV_DIM, Q_LORA, MLA_LATENT, MLA_V_WIDTH, POOL_DIM, POOL_DIM, N_BRANCH * D_MODEL)
W_IN_COLS = sum(SPLIT_SIZES)

kernel_name = 'hybrid_diffusion_conv_mla_pool_step'


def rms_norm(x, g):
    xf = x.astype(jnp.float32)
    y = xf * lax.rsqrt(jnp.mean(xf * xf, axis=-1, keepdims=True) + EPS)
    return (y * g.astype(jnp.float32)).astype(x.dtype)


def axial_rope_tables(n_tokens):
    t = jnp.arange(n_tokens)
    row = (t // GRID_W).astype(jnp.float32)
    col = (t % GRID_W).astype(jnp.float32)
    axis_dim = QK_ROPE // 2
    inv = 1.0 / (ROPE_THETA ** (jnp.arange(0, axis_dim, 2, dtype=jnp.float32) / axis_dim))
    ang = jnp.stack([row[:, None] * inv, col[:, None] * inv], axis=1)
    return jnp.cos(ang), jnp.sin(ang)


def apply_axial_rope(x, cos, sin):
    xs = x.reshape(x.shape[:-1] + (2, 2, QK_ROPE // 4))
    x1, x2 = xs[..., 0, :], xs[..., 1, :]
    c = cos[:, None].astype(x.dtype)
    s = sin[:, None].astype(x.dtype)
    out = jnp.stack([x1 * c - x2 * s, x2 * c + x1 * s], axis=-2)
    return out.reshape(x.shape)


def blocked_attention(q, k, v):
    b, sq, h, dk = q.shape
    nb = sq // Q_BLOCK
    qb = q.reshape(b, nb, Q_BLOCK, h, dk).transpose(1, 0, 2, 3, 4)
    scale = dk ** -0.5

    def one_block(qblk):
        s = jnp.einsum('bqhd,bkhd->bhqk', qblk, k).astype(jnp.float32) * scale
        p = jax.nn.softmax(s, axis=-1).astype(v.dtype)
        return jnp.einsum('bhqk,bkhd->bqhd', p, v)

    o = lax.map(one_block, qb)
    return o.transpose(1, 0, 2, 3, 4).reshape(b, sq, h, v.shape[-1])


def centred_conv3(u, w, bias):
    up = jnp.pad(u, ((0, 0), (1, 1), (0, 0)))
    return up[:, :-2] * w[0] + up[:, 1:-1] * w[1] + up[:, 2:] * w[2] + bias


def multiscale_pool(u, pool_w, pool_scale):
    b, s, _ = u.shape
    ug = u.reshape(b, s, N_POOL, POOL_GROUP)
    cs = jnp.pad(jnp.cumsum(ug.astype(jnp.float32), axis=1), ((0, 0), (1, 0), (0, 0), (0, 0)))
    t = np.arange(s)
    means = []
    for gi, win in enumerate(POOL_WINDOWS):
        lo = np.clip(t - win // 2, 0, s)
        hi = np.clip(t + win - win // 2, 0, s)
        total = cs[:, hi, gi, :] - cs[:, lo, gi, :]
        means.append(total / (hi - lo).astype(np.float32)[:, None])
    pooled = jnp.stack(means, axis=2).astype(u.dtype) - ug
    mixed = jnp.einsum('bsgc,gcd->bsgd', pooled, pool_w).reshape(b, s, POOL_DIM)
    return mixed * pool_scale


def trunk_layer(x, cond, w_mod, b_mod, g_pre, g_post, w_in, conv_w, conv_b, g_q, w_uq, g_kv, w_ukv,
                pool_w, pool_scale, w_branch, w_o, ctx_latent, cos, sin):
    b, s, _ = x.shape
    mod = jax.nn.silu(cond) @ w_mod + b_mod
    shift, scale, gate = jnp.split(mod[:, None, :], 3, axis=-1)
    hn = rms_norm(x, g_pre) * (1 + scale) + shift
    proj = hn @ w_in
    split_points = np.cumsum(SPLIT_SIZES)[:-1].tolist()
    (a_b, a_c, a_x, a_z, q_down, kv_down, b_z, c_u, c_z, merge_logits) = jnp.split(proj, split_points, axis=-1)

    y_a = jax.nn.silu(a_z) * (a_b * centred_conv3(a_c * a_x, conv_w, conv_b))

    q = (rms_norm(q_down, g_q) @ w_uq).reshape(b, s, N_HEADS, QK_NOPE + QK_ROPE)
    q_nope, q_rope = q[..., :QK_NOPE], q[..., QK_NOPE:]
    own_latent = jnp.concatenate([rms_norm(kv_down[..., :KV_LORA], g_kv), kv_down[..., KV_LORA:]], axis=-1)
    ckv = own_latent[..., :KV_LORA]
    krope = own_latent[..., None, KV_LORA:]
    if cos is not None:
        q_rope = apply_axial_rope(q_rope, cos, sin)
        krope = apply_axial_rope(krope, cos, sin)
    if ctx_latent is not None:
        ckv = jnp.concatenate([ctx_latent[..., :KV_LORA], ckv], axis=1)
        krope = jnp.concatenate([ctx_latent[..., None, KV_LORA:], krope], axis=1)
    sk = ckv.shape[1]
    kv = (ckv @ w_ukv).reshape(b, sk, N_HEADS, QK_NOPE + V_HEAD)
    k = jnp.concatenate([kv[..., :QK_NOPE], jnp.broadcast_to(krope, (b, sk, N_HEADS, QK_ROPE))], axis=-1)
    qf = jnp.concatenate([q_nope, q_rope], axis=-1)
    attn = blocked_attention(qf, k, kv[..., QK_NOPE:]).reshape(b, s, MLA_V_WIDTH)
    y_b = jax.nn.silu(b_z) * attn

    y_c = jax.nn.silu(c_z) * multiscale_pool(c_u, pool_w, pool_scale)

    branches = jnp.einsum('bsnc,ncd->bsnd', jnp.stack([y_a, y_b, y_c], axis=2), w_branch)
    gates = jax.nn.sigmoid(merge_logits.astype(jnp.float32)).astype(x.dtype).reshape(b, s, N_BRANCH, D_MODEL)
    merged = jnp.sum(gates * branches, axis=2)
    out = rms_norm(merged @ w_o, g_post)
    return x + gate * out, own_latent


def setup_inputs(seed: int = 0) -> dict:
    key = jax.random.key(seed)
    ks = jax.random.split(key, 24)

    def nrm(k, shape, s):
        return jax.random.normal(k, shape, jnp.float32) * s

    def gain(k, shape):
        return 1.0 + 0.1 * jax.random.normal(k, shape, jnp.float32)

    return {
        'x_prompt': nrm(ks[0], (BATCH, SEQ, D_MODEL), 1.0),
        'x_sample': nrm(ks[1], (DEC_BATCH, DEC_SEQ, D_MODEL), 1.0),
        'cache_mla_latent': nrm(ks[2], (DEC_BATCH, DEPTH, PAST_LEN, MLA_LATENT), 1.0),
        'c': nrm(ks[3], (DEC_BATCH, D_MODEL), 1.0),
        'c_ctx': nrm(ks[4], (D_MODEL,), 1.0),
        'w_mod': nrm(ks[5], (DEPTH, D_MODEL, 3 * D_MODEL), 0.5 * D_MODEL ** -0.5),
        'b_mod': nrm(ks[6], (DEPTH, 3 * D_MODEL), 0.02),
        'g_pre': gain(ks[7], (DEPTH, D_MODEL)),
        'g_post': gain(ks[8], (DEPTH, D_MODEL)),
        'w_in': nrm(ks[9], (DEPTH, D_MODEL, W_IN_COLS), D_MODEL ** -0.5),
        'conv_w': nrm(ks[10], (DEPTH, CONV_K, CONV_DIM), CONV_K ** -0.5),
        'conv_b': nrm(ks[11], (DEPTH, CONV_DIM), 0.01),
        'g_q': gain(ks[12], (DEPTH, Q_LORA)),
        'w_uq': nrm(ks[13], (DEPTH, Q_LORA, N_HEADS * (QK_NOPE + QK_ROPE)), Q_LORA ** -0.5),
        'g_kv': gain(ks[14], (DEPTH, KV_LORA)),
        'w_ukv': nrm(ks[15], (DEPTH, KV_LORA, N_HEADS * (QK_NOPE + V_HEAD)), KV_LORA ** -0.5),
        'pool_w': nrm(ks[16], (DEPTH, N_POOL, POOL_GROUP, POOL_GROUP), POOL_GROUP ** -0.5),
        'pool_scale': gain(ks[17], (DEPTH, POOL_DIM)),
        'w_branch': nrm(ks[18], (DEPTH, N_BRANCH, BRANCH_DIM, D_MODEL), BRANCH_DIM ** -0.5),
        'w_o': nrm(ks[19], (DEPTH, D_MODEL, D_MODEL), D_MODEL ** -0.5),
    }


def reference(x_prompt, x_sample, cache_mla_latent, c, c_ctx, w_mod, b_mod, g_pre, g_post, w_in,
              conv_w, conv_b, g_q, w_uq, g_kv, w_ukv, pool_w, pool_scale, w_branch, w_o):
    h = x_prompt
    cond_ctx = jnp.broadcast_to(c_ctx, (x_prompt.shape[0], D_MODEL))
    ctx_states = []
    for l in range(DEPTH):
        h, lat = trunk_layer(h, cond_ctx, w_mod[l], b_mod[l], g_pre[l], g_post[l], w_in[l], conv_w[l], conv_b[l],
                             g_q[l], w_uq[l], g_kv[l], w_ukv[l], pool_w[l], pool_scale[l], w_branch[l], w_o[l],
                             None, None, None)
        ctx_states.append(lat)
    state_mla_latent = jnp.stack(ctx_states, axis=1)

    cos, sin = axial_rope_tables(x_sample.shape[1])
    hs = x_sample
    for l in range(DEPTH):
        hs, _ = trunk_layer(hs, c, w_mod[l], b_mod[l], g_pre[l], g_post[l], w_in[l], conv_w[l], conv_b[l],
                            g_q[l], w_uq[l], g_kv[l], w_ukv[l], pool_w[l], pool_scale[l], w_branch[l], w_o[l],
                            cache_mla_latent[:, l], cos, sin)
    return (h, hs, state_mla_latent)
```

```python
import functools

import numpy as np
import jax
import jax.numpy as jnp
from jax import lax
from jax.experimental import pallas as pl
from jax.experimental.pallas import tpu as pltpu

D_MODEL = 1024
DEPTH = 2
GRID_W = 64
EPS = 1e-6
BRANCH_DIM = 512
N_HEADS = 8
QK_NOPE = 64
QK_ROPE = 32
V_HEAD = 64
Q_LORA = 384
KV_LORA = 256
MLA_LATENT = KV_LORA + QK_ROPE
ROPE_THETA = 10000.0
POOL_WINDOWS = (2, 4, 8, 16)
POOL_GROUP = 128

LANES = 128
HEAD_PAD = LANES
HP = N_HEADS * HEAD_PAD
TOKEN_TILE = 256
HALO = 16
COND_ROWS = 8
VMEM_LIMIT = 56 * 1024 * 1024

_EXT_COLS = 3 * BRANCH_DIM
_AB0, _AZ0 = 0, BRANCH_DIM
_QD0 = 2 * BRANCH_DIM
_CKV0 = _QD0 + Q_LORA
_KR0 = _CKV0 + KV_LORA
_BZ0 = _KR0 + LANES
_CZ0 = _BZ0 + BRANCH_DIM
_MAIN_COLS = _CZ0 + BRANCH_DIM
_LAT_EXT = KV_LORA + LANES


def _dot(a, b):
    return jnp.dot(a, b, preferred_element_type=jnp.float32)


def _dot_nt(a, b):
    return lax.dot_general(a, b, (((1,), (1,)), ((), ())), preferred_element_type=jnp.float32)


def _rms(x, g):
    return x * lax.rsqrt(jnp.mean(x * x, axis=-1, keepdims=True) + EPS) * g


def _silu(x):
    return x * jax.nn.sigmoid(x)


def _modulated_norm(x, g_pre, mod_row):
    shift = mod_row[:, 0:D_MODEL]
    scale = mod_row[:, D_MODEL:2 * D_MODEL]
    return _rms(x, g_pre) * (1.0 + scale) + shift


def _mod_kernel(cond_ref, w_ref, b_ref, out_ref):
    h = _silu(cond_ref[...]).astype(jnp.bfloat16)
    out_ref[...] = _dot(h, w_ref[...].astype(jnp.bfloat16)) + b_ref[...]


def _modulation(cond, w_mod, b_mod):
    col_tile = D_MODEL
    return pl.pallas_call(
        _mod_kernel,
        grid=(DEPTH, 3 * D_MODEL // col_tile),
        in_specs=[
            pl.BlockSpec((COND_ROWS, D_MODEL), lambda l, n: (0, 0)),
            pl.BlockSpec((None, D_MODEL, col_tile), lambda l, n: (l, 0, n)),
            pl.BlockSpec((None, 1, col_tile), lambda l, n: (l, 0, n)),
        ],
        out_specs=pl.BlockSpec((None, COND_ROWS, col_tile), lambda l, n: (l, 0, n)),
        out_shape=jax.ShapeDtypeStruct((DEPTH, COND_ROWS, 3 * D_MODEL), jnp.float32),
        compiler_params=pltpu.CompilerParams(vmem_limit_bytes=VMEM_LIMIT),
        name="modulation",
    )(cond, w_mod, b_mod.reshape(DEPTH, 1, 3 * D_MODEL))


def _cache_kv_kernel(lat_ref, wl2k_ref, wv_ref, k_ref, v_ref):
    lat = lat_ref[...].astype(jnp.bfloat16)
    k = _dot(lat, wl2k_ref[0:MLA_LATENT, :]).astype(jnp.bfloat16)
    v = _dot(lat[:, 0:KV_LORA], wv_ref[...]).astype(jnp.bfloat16)
    for h in range(N_HEADS):
        k_ref[h] = k[:, h * HEAD_PAD:(h + 1) * HEAD_PAD]
        v_ref[h] = v[:, h * HEAD_PAD:(h + 1) * HEAD_PAD]


def _cache_kv(cache, wl2k, wv):
    nb, _, past, _ = cache.shape
    kv_shape = jax.ShapeDtypeStruct((DEPTH, nb, N_HEADS, past, HEAD_PAD), jnp.bfloat16)
    kv_spec = pl.BlockSpec((None, None, N_HEADS, past, HEAD_PAD), lambda l, b: (l, b, 0, 0, 0))
    return pl.pallas_call(
        _cache_kv_kernel,
        grid=(DEPTH, nb),
        in_specs=[
            pl.BlockSpec((None, None, past, MLA_LATENT), lambda l, b: (b, l, 0, 0)),
            pl.BlockSpec((None, _LAT_EXT, HP), lambda l, b: (l, 0, 0)),
            pl.BlockSpec((None, KV_LORA, HP), lambda l, b: (l, 0, 0)),
        ],
        out_specs=[kv_spec, kv_spec],
        out_shape=[kv_shape, kv_shape],
        compiler_params=pltpu.CompilerParams(vmem_limit_bytes=VMEM_LIMIT),
        name="cache_kv",
    )(cache, wl2k, wv)


def _front_kernel(seq_len, emit_latent, x_ref, xp_ref, xn_ref, mod_ref, gpre_ref, wa_ref, convw_ref, convb_ref,
                  gq_ref, wuq_ref, gkv_ref, wl2k_ref, wv_ref, poolw_ref, pscale_ref, qtab_ref, ktab_ref, *rest):
    if emit_latent:
        q_out, k_out, v_out, ya_out, yc_out, sbz_out, lat_out, u_scr, cu_scr = rest
    else:
        q_out, k_out, v_out, ya_out, yc_out, sbz_out, u_scr, cu_scr = rest
    tm = TOKEN_TILE
    j = pl.program_id(1)
    last_j = seq_len // tm - 1
    mod_row = mod_ref[...]
    g_pre = gpre_ref[...]

    x_ext = jnp.concatenate([xp_ref[...], x_ref[...], xn_ref[...]], axis=0)
    hn_ext = _modulated_norm(x_ext, g_pre, mod_row).astype(jnp.bfloat16)
    hn = hn_ext[HALO:HALO + tm]

    pe = _dot(hn_ext, wa_ref[:, 0:_EXT_COLS])
    row = lax.broadcasted_iota(jnp.int32, (tm + 2 * HALO, 1), 0)
    first_valid = jnp.where(j > 0, 0, HALO)
    end_valid = jnp.where(j < last_j, tm + 2 * HALO, tm + HALO)
    valid = jnp.logical_and(row >= first_valid, row < end_valid)
    u_scr[...] = jnp.where(valid, pe[:, 0:BRANCH_DIM] * pe[:, BRANCH_DIM:2 * BRANCH_DIM], 0.0)
    cu_scr[...] = jnp.where(valid, pe[:, 2 * BRANCH_DIM:3 * BRANCH_DIM], 0.0)

    pm = _dot(hn, wa_ref[:, _EXT_COLS:_EXT_COLS + _MAIN_COLS])

    conv = (u_scr[HALO - 1:HALO - 1 + tm, :] * convw_ref[0:1, :]
            + u_scr[HALO:HALO + tm, :] * convw_ref[1:2, :]
            + u_scr[HALO + 1:HALO + 1 + tm, :] * convw_ref[2:3, :]
            + convb_ref[...])
    ya = _silu(pm[:, _AZ0:_AZ0 + BRANCH_DIM]) * (pm[:, _AB0:_AB0 + BRANCH_DIM] * conv)
    ya_out[...] = ya.astype(jnp.bfloat16)

    t = j * tm + lax.broadcasted_iota(jnp.int32, (tm, POOL_GROUP), 0)
    mixed = []
    for gi, win in enumerate(POOL_WINDOWS):
        half = win // 2
        cols = slice(gi * POOL_GROUP, (gi + 1) * POOL_GROUP)
        total = cu_scr[HALO - half:HALO - half + tm, cols]
        for k in range(-half + 1, half):
            total = total + cu_scr[HALO + k:HALO + k + tm, cols]
        count = jnp.minimum(t + half, seq_len) - jnp.maximum(t - half, 0)
        pooled = total / count.astype(jnp.float32) - cu_scr[HALO:HALO + tm, cols]
        mixed.append(_dot(pooled.astype(jnp.bfloat16), poolw_ref[gi]))
    mixed = jnp.concatenate(mixed, axis=1) * pscale_ref[...]
    yc_out[...] = (_silu(pm[:, _CZ0:_CZ0 + BRANCH_DIM]) * mixed).astype(jnp.bfloat16)

    qn = _rms(pm[:, _QD0:_QD0 + Q_LORA], gq_ref[...]).astype(jnp.bfloat16)
    q = _dot(qn, wuq_ref[...])
    qtab = qtab_ref[...]
    ckv = _rms(pm[:, _CKV0:_CKV0 + KV_LORA], gkv_ref[...])
    kr = pm[:, _KR0:_KR0 + LANES]
    if emit_latent:
        lat_out[:, 0:KV_LORA] = ckv
        lat_out[:, KV_LORA:MLA_LATENT] = kr[:, 0:QK_ROPE]
    ext = jnp.concatenate([ckv, kr * ktab_ref[...]], axis=1).astype(jnp.bfloat16)
    k = _dot(ext, wl2k_ref[...]).astype(jnp.bfloat16)
    v = _dot(ext[:, 0:KV_LORA], wv_ref[...]).astype(jnp.bfloat16)
    for h in range(N_HEADS):
        cols = slice(h * HEAD_PAD, (h + 1) * HEAD_PAD)
        q_out[h] = (q[:, cols] * qtab).astype(jnp.bfloat16)
        k_out[h] = k[:, cols]
        v_out[h] = v[:, cols]
    sbz_out[...] = _silu(pm[:, _BZ0:_BZ0 + BRANCH_DIM]).astype(jnp.bfloat16)


def _front(l, x, mod_l, cond_row, w, qtab, ktab, emit_latent):
    nb, seq, _ = x.shape
    tm = TOKEN_TILE
    nt = seq // tm
    hb = tm // HALO
    n_hblk = seq // HALO

    def wspec(shape):
        return pl.BlockSpec((None,) + shape, lambda b, j: (l,) + (0,) * len(shape))

    in_specs = [
        pl.BlockSpec((None, tm, D_MODEL), lambda b, j: (b, j, 0)),
        pl.BlockSpec((None, HALO, D_MODEL), lambda b, j: (b, jnp.maximum(j * hb - 1, 0), 0)),
        pl.BlockSpec((None, HALO, D_MODEL), lambda b, j: (b, jnp.minimum((j + 1) * hb, n_hblk - 1), 0)),
        pl.BlockSpec((None, 1, 3 * D_MODEL), lambda b, j: (cond_row(b), 0, 0)),
        wspec((1, D_MODEL)),
        wspec((D_MODEL, _EXT_COLS + _MAIN_COLS)),
        wspec((3, BRANCH_DIM)),
        wspec((1, BRANCH_DIM)),
        wspec((1, Q_LORA)),
        wspec((Q_LORA, HP)),
        wspec((1, KV_LORA)),
        wspec((_LAT_EXT, HP)),
        wspec((KV_LORA, HP)),
        wspec((len(POOL_WINDOWS), POOL_GROUP, POOL_GROUP)),
        wspec((1, BRANCH_DIM)),
        pl.BlockSpec((tm, LANES), lambda b, j: (j, 0)),
        pl.BlockSpec((tm, LANES), lambda b, j: (j, 0)),
    ]
    head_shape = jax.ShapeDtypeStruct((nb, N_HEADS, seq, HEAD_PAD), jnp.bfloat16)
    head_spec = pl.BlockSpec((None, N_HEADS, tm, HEAD_PAD), lambda b, j: (b, 0, j, 0))
    br_shape = jax.ShapeDtypeStruct((nb, seq, BRANCH_DIM), jnp.bfloat16)
    br_spec = pl.BlockSpec((None, tm, BRANCH_DIM), lambda b, j: (b, j, 0))
    out_shape = [head_shape] * 3 + [br_shape] * 3
    out_specs = [head_spec] * 3 + [br_spec] * 3
    if emit_latent:
        out_shape.append(jax.ShapeDtypeStruct((nb, seq, MLA_LATENT), jnp.float32))
        out_specs.append(pl.BlockSpec((None, tm, MLA_LATENT), lambda b, j: (b, j, 0)))
    return pl.pallas_call(
        functools.partial(_front_kernel, seq, emit_latent),
        grid=(nb, nt),
        in_specs=in_specs,
        out_specs=out_specs,
        out_shape=out_shape,
        scratch_shapes=[pltpu.VMEM((tm + 2 * HALO, BRANCH_DIM), jnp.float32),
                        pltpu.VMEM((tm + 2 * HALO, BRANCH_DIM), jnp.float32)],
        compiler_params=pltpu.CompilerParams(vmem_limit_bytes=VMEM_LIMIT),
        name="front",
    )(x, x, x, mod_l, w["g_pre"], w["wa"], w["conv_w"], w["conv_b"], w["g_q"], w["wuq"], w["g_kv"],
      w["wl2k"], w["wv"], w["pool_w"], w["pool_scale"], qtab, ktab)


def _back_kernel(has_cache, x_ref, mod_ref, gpre_ref, wg_ref, q_ref, k_ref, v_ref, *rest):
    if has_cache:
        kc_ref, vc_ref, ya_ref, yc_ref, sbz_ref, wbr_ref, wo_ref, gpost_ref, out_ref = rest
    else:
        ya_ref, yc_ref, sbz_ref, wbr_ref, wo_ref, gpost_ref, out_ref = rest
    x = x_ref[...]
    mod_row = mod_ref[...]

    pairs = []
    for p in range(N_HEADS // 2):
        acc = None
        for h in (2 * p, 2 * p + 1):
            q = q_ref[h]
            s = _dot_nt(q, k_ref[h])
            m = jnp.max(s, axis=-1, keepdims=True)
            if has_cache:
                sc = _dot_nt(q, kc_ref[h])
                m = jnp.maximum(m, jnp.max(sc, axis=-1, keepdims=True))
            e = jnp.exp(s - m)
            denom = jnp.sum(e, axis=-1, keepdims=True)
            o = _dot(e.astype(jnp.bfloat16), v_ref[h])
            if has_cache:
                ec = jnp.exp(sc - m)
                denom = denom + jnp.sum(ec, axis=-1, keepdims=True)
                o = o + _dot(ec.astype(jnp.bfloat16), vc_ref[h])
            o = o / denom
            acc = o if acc is None else acc + o
        pairs.append(acc)
    attn = jnp.concatenate(pairs, axis=1)
    yb = (sbz_ref[...].astype(jnp.float32) * attn).astype(jnp.bfloat16)

    hn = _modulated_norm(x, gpre_ref[...], mod_row).astype(jnp.bfloat16)
    merged = None
    for n, y in enumerate((ya_ref[...], yb, yc_ref[...])):
        gate = jax.nn.sigmoid(_dot(hn, wg_ref[:, n * D_MODEL:(n + 1) * D_MODEL]))
        term = gate * _dot(y, wbr_ref[n])
        merged = term if merged is None else merged + term
    out = _rms(_dot(merged.astype(jnp.bfloat16), wo_ref[...]), gpost_ref[...])
    out_ref[...] = x + mod_row[:, 2 * D_MODEL:3 * D_MODEL] * out


def _back(l, x, mod_l, cond_row, w, q, k, v, kc, vc, ya, yc, sbz):
    nb, seq, _ = x.shape
    tq = TOKEN_TILE
    has_cache = kc is not None

    def wspec(shape):
        return pl.BlockSpec((None,) + shape, lambda b, j: (l,) + (0,) * len(shape))

    tile_spec = pl.BlockSpec((None, tq, D_MODEL), lambda b, j: (b, j, 0))
    br_spec = pl.BlockSpec((None, tq, BRANCH_DIM), lambda b, j: (b, j, 0))
    seq_spec = pl.BlockSpec((None, N_HEADS, seq, HEAD_PAD), lambda b, j: (b, 0, 0, 0))
    in_specs = [
        tile_spec,
        pl.BlockSpec((None, 1, 3 * D_MODEL), lambda b, j: (cond_row(b), 0, 0)),
        wspec((1, D_MODEL)),
        wspec((D_MODEL, 3 * D_MODEL)),
        pl.BlockSpec((None, N_HEADS, tq, HEAD_PAD), lambda b, j: (b, 0, j, 0)),
        seq_spec, seq_spec,
    ]
    args = [x, mod_l, w["g_pre"], w["wg"], q, k, v]
    if has_cache:
        past = kc.shape[3]
        cache_spec = pl.BlockSpec((None, None, N_HEADS, past, HEAD_PAD), lambda b, j: (l, b, 0, 0, 0))
        in_specs += [cache_spec, cache_spec]
        args += [kc, vc]
    in_specs += [br_spec, br_spec, br_spec,
                 wspec((3, BRANCH_DIM, D_MODEL)), wspec((D_MODEL, D_MODEL)), wspec((1, D_MODEL))]
    args += [ya, yc, sbz, w["w_branch"], w["w_o"], w["g_post"]]
    return pl.pallas_call(
        functools.partial(_back_kernel, has_cache),
        grid=(nb, seq // tq),
        in_specs=in_specs,
        out_specs=tile_spec,
        out_shape=jax.ShapeDtypeStruct(x.shape, jnp.float32),
        compiler_params=pltpu.CompilerParams(vmem_limit_bytes=VMEM_LIMIT),
        name="back",
    )(*args)


def _rope_swap_perm():
    quarter = QK_ROPE // 4
    idx = np.arange(QK_ROPE).reshape(2, 2, quarter)
    return idx[:, ::-1, :].reshape(-1)


def _rope_tables(n_tokens, rotate):
    if rotate:
        t = np.arange(n_tokens)
        pos = np.stack([t // GRID_W, t % GRID_W], axis=1).astype(np.float32)
        axis_dim = QK_ROPE // 2
        inv = (1.0 / (ROPE_THETA ** (jnp.arange(0, axis_dim, 2, dtype=jnp.float32) / axis_dim)))
        ang = jnp.asarray(pos)[:, :, None] * inv
        cos, sin = jnp.cos(ang), jnp.sin(ang)
        cfull = jnp.stack([cos, cos], axis=2).reshape(n_tokens, QK_ROPE)
        sfull = jnp.stack([-sin, sin], axis=2).reshape(n_tokens, QK_ROPE)
    else:
        cfull = jnp.ones((n_tokens, QK_ROPE), jnp.float32)
        sfull = jnp.zeros((n_tokens, QK_ROPE), jnp.float32)
    scale = (QK_NOPE + QK_ROPE) ** -0.5
    qtab = jnp.concatenate([jnp.full((n_tokens, QK_NOPE), scale, jnp.float32), cfull * scale, sfull * scale], axis=1)
    ktab = jnp.concatenate([cfull, sfull, jnp.zeros((n_tokens, LANES - 2 * QK_ROPE), jnp.float32)], axis=1)
    return qtab, ktab


def _prepare_weights(g_pre, g_post, w_in, conv_w, conv_b, g_q, w_uq, g_kv, w_ukv, pool_w, pool_scale, w_branch, w_o):
    bf = jnp.bfloat16
    perm = _rope_swap_perm()
    c = np.cumsum((0, 512, 512, 512, 512, Q_LORA, MLA_LATENT, 512, 512, 512))
    a_b, a_c, a_x, a_z = (w_in[:, :, c[i]:c[i + 1]] for i in range(4))
    q_down = w_in[:, :, c[4]:c[5]]
    ckv_w = w_in[:, :, c[5]:c[5] + KV_LORA]
    kr_w = w_in[:, :, c[5] + KV_LORA:c[6]]
    b_z, c_u, c_z = (w_in[:, :, c[i]:c[i + 1]] for i in (6, 7, 8))
    kr_blk = jnp.concatenate([kr_w, kr_w[:, :, perm], jnp.zeros_like(w_in[:, :, :LANES - 2 * QK_ROPE])], axis=2)
    wa = jnp.concatenate([a_c, a_x, c_u, a_b, a_z, q_down, ckv_w, kr_blk, b_z, c_z], axis=2).astype(bf)
    wg = w_in[:, :, c[9]:].astype(bf)

    uq = w_uq.reshape(DEPTH, Q_LORA, N_HEADS, QK_NOPE + QK_ROPE)
    rope_q = uq[..., QK_NOPE:]
    wuq = jnp.concatenate([uq, rope_q[..., perm]], axis=-1).reshape(DEPTH, Q_LORA, HP).astype(bf)

    ukv = w_ukv.reshape(DEPTH, KV_LORA, N_HEADS, QK_NOPE + V_HEAD)
    k_part = jnp.concatenate([ukv[..., :QK_NOPE], jnp.zeros_like(ukv[..., :HEAD_PAD - QK_NOPE])], axis=-1)
    place = np.zeros((LANES, N_HEADS, HEAD_PAD), np.float32)
    for d in range(QK_ROPE):
        place[d, :, QK_NOPE + d] = 1.0
        place[d, :, QK_NOPE + QK_ROPE + d] = 1.0
        place[QK_ROPE + d, :, QK_NOPE + d] = 1.0
        place[QK_ROPE + d, :, QK_NOPE + QK_ROPE + d] = 1.0
    place = jnp.broadcast_to(jnp.asarray(place), (DEPTH, LANES, N_HEADS, HEAD_PAD))
    wl2k = jnp.concatenate([k_part, place], axis=1).reshape(DEPTH, _LAT_EXT, HP).astype(bf)

    v_w = ukv[..., QK_NOPE:]
    zeros_v = jnp.zeros_like(v_w)
    even = jnp.concatenate([v_w, zeros_v], axis=-1)
    odd = jnp.concatenate([zeros_v, v_w], axis=-1)
    is_odd = (np.arange(N_HEADS) % 2 == 1)[None, None, :, None]
    wv = jnp.where(is_odd, odd, even).reshape(DEPTH, KV_LORA, HP).astype(bf)

    return {
        "g_pre": g_pre.reshape(DEPTH, 1, D_MODEL), "g_post": g_post.reshape(DEPTH, 1, D_MODEL),
        "wa": wa, "wg": wg, "conv_w": conv_w, "conv_b": conv_b.reshape(DEPTH, 1, BRANCH_DIM),
        "g_q": g_q.reshape(DEPTH, 1, Q_LORA), "wuq": wuq, "g_kv": g_kv.reshape(DEPTH, 1, KV_LORA),
        "wl2k": wl2k, "wv": wv, "pool_w": pool_w.astype(bf),
        "pool_scale": pool_scale.reshape(DEPTH, 1, BRANCH_DIM),
        "w_branch": w_branch.astype(bf), "w_o": w_o.astype(bf),
    }


def kernel(x_prompt, x_sample, cache_mla_latent, c, c_ctx, w_mod, b_mod, g_pre, g_post, w_in, conv_w, conv_b,
           g_q, w_uq, g_kv, w_ukv, pool_w, pool_scale, w_branch, w_o):
    n_dec = x_sample.shape[0]
    assert 1 + n_dec <= COND_ROWS
    assert x_prompt.shape[1] % TOKEN_TILE == 0 and x_sample.shape[1] % TOKEN_TILE == 0
    w = _prepare_weights(g_pre, g_post, w_in, conv_w, conv_b, g_q, w_uq, g_kv, w_ukv, pool_w, pool_scale,
                         w_branch, w_o)
    cond = jnp.concatenate([c_ctx[None, :], c, jnp.zeros((COND_ROWS - 1 - n_dec, D_MODEL), jnp.float32)], axis=0)
    mod = _modulation(cond, w_mod, b_mod).reshape(DEPTH, COND_ROWS, 1, 3 * D_MODEL)
    kc, vc = _cache_kv(cache_mla_latent, w["wl2k"], w["wv"])

    ctx_row = lambda b: 0
    dec_row = lambda b: b + 1

    h = x_prompt
    qtab, ktab = _rope_tables(x_prompt.shape[1], rotate=False)
    latents = []
    for l in range(DEPTH):
        q, k, v, ya, yc, sbz, lat = _front(l, h, mod[l], ctx_row, w, qtab, ktab, True)
        h = _back(l, h, mod[l], ctx_row, w, q, k, v, None, None, ya, yc, sbz)
        latents.append(lat)
    state = jnp.stack(latents, axis=1)

    hs = x_sample
    qtab, ktab = _rope_tables(x_sample.shape[1], rotate=True)
    for l in range(DEPTH):
        q, k, v, ya, yc, sbz = _front(l, hs, mod[l], dec_row, w, qtab, ktab, False)
        hs = _back(l, hs, mod[l], dec_row, w, q, k, v, kc, vc, ya, yc, sbz)
    return (h, hs, state)
```

```python
import functools

import numpy as np
import jax
import jax.numpy as jnp
from jax import lax
from jax.experimental import pallas as pl
from jax.experimental.pallas import tpu as pltpu

D_MODEL = 1024
DEPTH = 2
GRID_W = 64
EPS = 1e-6
BRANCH_DIM = 512
N_HEADS = 8
QK_NOPE = 64
QK_ROPE = 32
V_HEAD = 64
Q_LORA = 384
KV_LORA = 256
MLA_LATENT = KV_LORA + QK_ROPE
ROPE_THETA = 10000.0
POOL_WINDOWS = (2, 4, 8, 16)
POOL_GROUP = 128

LANES = 128
HEAD_PAD = LANES
HP = N_HEADS * HEAD_PAD
TOKEN_TILE = 256
HALO = 16
COND_ROWS = 8
VMEM_LIMIT = 56 * 1024 * 1024

_W1_COLS = 4 * BRANCH_DIM + Q_LORA + KV_LORA + LANES
_W2_START = 4 * BRANCH_DIM + Q_LORA + MLA_LATENT
_W2_COLS = 3 * BRANCH_DIM
_MERGE_START = _W2_START + _W2_COLS
_ACX0 = BRANCH_DIM
_AZ0 = 3 * BRANCH_DIM
_QD0 = BRANCH_DIM
_CKV0 = _QD0 + Q_LORA
_KR0 = _CKV0 + KV_LORA
_LAT_EXT = KV_LORA + LANES


def _dot(a, b):
    return jnp.dot(a, b, preferred_element_type=jnp.float32)


def _dot_nt(a, b):
    return lax.dot_general(a, b, (((1,), (1,)), ((), ())), preferred_element_type=jnp.float32)


def _rms(x, g):
    return x * lax.rsqrt(jnp.mean(x * x, axis=-1, keepdims=True) + EPS) * g


def _silu(x):
    return x * jax.nn.sigmoid(x)


def _modulated_norm(x, g_pre, mod_row):
    shift = mod_row[:, 0:D_MODEL]
    scale = mod_row[:, D_MODEL:2 * D_MODEL]
    return _rms(x, g_pre) * (1.0 + scale) + shift


def _mod_kernel(cond_ref, w_ref, b_ref, out_ref):
    h = _silu(cond_ref[...]).astype(jnp.bfloat16)
    out_ref[...] = _dot(h, w_ref[...].astype(jnp.bfloat16)) + b_ref[...]


def _modulation(cond, w_mod, b_mod):
    col_tile = D_MODEL
    return pl.pallas_call(
        _mod_kernel,
        grid=(DEPTH, 3 * D_MODEL // col_tile),
        in_specs=[
            pl.BlockSpec((COND_ROWS, D_MODEL), lambda l, n: (0, 0)),
            pl.BlockSpec((None, D_MODEL, col_tile), lambda l, n: (l, 0, n)),
            pl.BlockSpec((None, 1, col_tile), lambda l, n: (l, 0, n)),
        ],
        out_specs=pl.BlockSpec((None, COND_ROWS, col_tile), lambda l, n: (l, 0, n)),
        out_shape=jax.ShapeDtypeStruct((DEPTH, COND_ROWS, 3 * D_MODEL), jnp.float32),
        compiler_params=pltpu.CompilerParams(vmem_limit_bytes=VMEM_LIMIT),
        name="modulation",
    )(cond, w_mod, b_mod.reshape(DEPTH, 1, 3 * D_MODEL))


def _cache_kv_kernel(lat_ref, wl2k_ref, wv_ref, k_ref, v_ref):
    lat = lat_ref[...].astype(jnp.bfloat16)
    k = _dot(lat, wl2k_ref[0:MLA_LATENT, :]).astype(jnp.bfloat16)
    v = _dot(lat[:, 0:KV_LORA], wv_ref[...]).astype(jnp.bfloat16)
    for h in range(N_HEADS):
        k_ref[h] = k[:, h * HEAD_PAD:(h + 1) * HEAD_PAD]
        v_ref[h] = v[:, h * HEAD_PAD:(h + 1) * HEAD_PAD]


def _cache_kv(cache, wl2k, wv):
    nb, _, past, _ = cache.shape
    kv_shape = jax.ShapeDtypeStruct((DEPTH, nb, N_HEADS, past, HEAD_PAD), jnp.bfloat16)
    kv_spec = pl.BlockSpec((None, None, N_HEADS, past, HEAD_PAD), lambda l, b: (l, b, 0, 0, 0))
    return pl.pallas_call(
        _cache_kv_kernel,
        grid=(DEPTH, nb),
        in_specs=[
            pl.BlockSpec((None, None, past, MLA_LATENT), lambda l, b: (b, l, 0, 0)),
            pl.BlockSpec((None, _LAT_EXT, HP), lambda l, b: (l, 0, 0)),
            pl.BlockSpec((None, KV_LORA, HP), lambda l, b: (l, 0, 0)),
        ],
        out_specs=[kv_spec, kv_spec],
        out_shape=[kv_shape, kv_shape],
        compiler_params=pltpu.CompilerParams(vmem_limit_bytes=VMEM_LIMIT),
        name="cache_kv",
    )(cache, wl2k, wv)


def _front_kernel(seq_len, emit_latent, x_ref, xp_ref, xn_ref, mod_ref, gpre_ref, w1_ref, w2_ref, convw_ref, convb_ref,
                  gq_ref, wuq_ref, gkv_ref, wl2k_ref, wv_ref, poolw_ref, pscale_ref, qtab_ref, kcos_ref, ksin_ref,
                  *rest):
    if emit_latent:
        q_out, k_out, v_out, ya_out, yc_out, sbz_out, lat_out, u_scr, cu_scr = rest
    else:
        q_out, k_out, v_out, ya_out, yc_out, sbz_out, u_scr, cu_scr = rest
    tm = TOKEN_TILE
    j = pl.program_id(1)
    last_j = seq_len // tm - 1
    mod_row = mod_ref[...]
    g_pre = gpre_ref[...]

    x_ext = jnp.concatenate([xp_ref[...], x_ref[...], xn_ref[...]], axis=0)
    hn_ext = _modulated_norm(x_ext, g_pre, mod_row).astype(jnp.bfloat16)
    hn = hn_ext[HALO:HALO + tm]

    acx = _dot(hn_ext, w1_ref[:, _ACX0:_ACX0 + 2 * BRANCH_DIM])
    row = lax.broadcasted_iota(jnp.int32, (tm + 2 * HALO, 1), 0)
    first_valid = jnp.where(j > 0, 0, HALO)
    end_valid = jnp.where(j < last_j, tm + 2 * HALO, tm + HALO)
    valid = jnp.logical_and(row >= first_valid, row < end_valid)
    u_scr[...] = jnp.where(valid, acx[:, 0:BRANCH_DIM] * acx[:, BRANCH_DIM:2 * BRANCH_DIM], 0.0)
    cu_scr[...] = jnp.where(valid, _dot(hn_ext, w2_ref[:, BRANCH_DIM:2 * BRANCH_DIM]), 0.0)

    pm = _dot(hn, w1_ref[:, _AZ0:_W1_COLS])
    conv = (u_scr[HALO - 1:HALO - 1 + tm, :] * convw_ref[0:1, :]
            + u_scr[HALO:HALO + tm, :] * convw_ref[1:2, :]
            + u_scr[HALO + 1:HALO + 1 + tm, :] * convw_ref[2:3, :]
            + convb_ref[...])
    ya = _silu(pm[:, 0:BRANCH_DIM]) * (_dot(hn, w1_ref[:, 0:BRANCH_DIM]) * conv)
    ya_out[...] = ya.astype(jnp.bfloat16)

    t = j * tm + lax.broadcasted_iota(jnp.int32, (tm, POOL_GROUP), 0)
    mixed = []
    for gi, win in enumerate(POOL_WINDOWS):
        half = win // 2
        cols = slice(gi * POOL_GROUP, (gi + 1) * POOL_GROUP)
        total = cu_scr[HALO - half:HALO - half + tm, cols]
        for k in range(-half + 1, half):
            total = total + cu_scr[HALO + k:HALO + k + tm, cols]
        count = jnp.minimum(t + half, seq_len) - jnp.maximum(t - half, 0)
        pooled = total / count.astype(jnp.float32) - cu_scr[HALO:HALO + tm, cols]
        mixed.append(_dot(pooled.astype(jnp.bfloat16), poolw_ref[gi]))
    mixed = jnp.concatenate(mixed, axis=1) * pscale_ref[...]
    c_z = _dot(hn, w2_ref[:, 2 * BRANCH_DIM:3 * BRANCH_DIM])
    yc_out[...] = (_silu(c_z) * mixed).astype(jnp.bfloat16)

    qn = _rms(pm[:, _QD0:_QD0 + Q_LORA], gq_ref[...]).astype(jnp.bfloat16)
    q = _dot(qn, wuq_ref[...])
    qtab = qtab_ref[...]
    ckv = _rms(pm[:, _CKV0:_CKV0 + KV_LORA], gkv_ref[...])
    kr = pm[:, _KR0:_KR0 + LANES]
    if emit_latent:
        lat_out[:, 0:KV_LORA] = ckv
        lat_out[:, KV_LORA:MLA_LATENT] = kr[:, 0:QK_ROPE]
    lane = lax.broadcasted_iota(jnp.int32, (tm, LANES), 1)
    quarter = QK_ROPE // 4
    partner = jnp.where(lane % (2 * quarter) < quarter,
                        pltpu.roll(kr, LANES - quarter, 1), pltpu.roll(kr, quarter, 1))
    kr_rot = jnp.where(lane < QK_ROPE, kr * kcos_ref[...] + partner * ksin_ref[...], 0.0)
    ext = jnp.concatenate([ckv, kr_rot], axis=1).astype(jnp.bfloat16)
    k = _dot(ext, wl2k_ref[...]).astype(jnp.bfloat16)
    v = _dot(ext[:, 0:KV_LORA], wv_ref[...]).astype(jnp.bfloat16)
    for h in range(N_HEADS):
        cols = slice(h * HEAD_PAD, (h + 1) * HEAD_PAD)
        q_out[h] = (q[:, cols] * qtab).astype(jnp.bfloat16)
        k_out[h] = k[:, cols]
        v_out[h] = v[:, cols]
    sbz_out[...] = _silu(_dot(hn, w2_ref[:, 0:BRANCH_DIM])).astype(jnp.bfloat16)


def _front(l, x, mod_l, cond_row, w, tabs, emit_latent):
    nb, seq, _ = x.shape
    tm = TOKEN_TILE
    nt = seq // tm
    hb = tm // HALO
    n_hblk = seq // HALO

    def wspec(shape):
        return pl.BlockSpec((None,) + shape, lambda b, j: (l,) + (0,) * len(shape))

    in_specs = [
        pl.BlockSpec((None, tm, D_MODEL), lambda b, j: (b, j, 0)),
        pl.BlockSpec((None, HALO, D_MODEL), lambda b, j: (b, jnp.maximum(j * hb - 1, 0), 0)),
        pl.BlockSpec((None, HALO, D_MODEL), lambda b, j: (b, jnp.minimum((j + 1) * hb, n_hblk - 1), 0)),
        pl.BlockSpec((None, 1, 3 * D_MODEL), lambda b, j: (cond_row(b), 0, 0)),
        wspec((1, D_MODEL)),
        wspec((D_MODEL, _W1_COLS)),
        wspec((D_MODEL, _W2_COLS)),
        wspec((3, BRANCH_DIM)),
        wspec((1, BRANCH_DIM)),
        wspec((1, Q_LORA)),
        wspec((Q_LORA, HP)),
        wspec((1, KV_LORA)),
        wspec((_LAT_EXT, HP)),
        wspec((KV_LORA, HP)),
        wspec((len(POOL_WINDOWS), POOL_GROUP, POOL_GROUP)),
        wspec((1, BRANCH_DIM)),
        pl.BlockSpec((tm, LANES), lambda b, j: (j, 0)),
        pl.BlockSpec((tm, LANES), lambda b, j: (j, 0)),
        pl.BlockSpec((tm, LANES), lambda b, j: (j, 0)),
    ]
    head_shape = jax.ShapeDtypeStruct((nb, N_HEADS, seq, HEAD_PAD), jnp.bfloat16)
    head_spec = pl.BlockSpec((None, N_HEADS, tm, HEAD_PAD), lambda b, j: (b, 0, j, 0))
    br_shape = jax.ShapeDtypeStruct((nb, seq, BRANCH_DIM), jnp.bfloat16)
    br_spec = pl.BlockSpec((None, tm, BRANCH_DIM), lambda b, j: (b, j, 0))
    out_shape = [head_shape] * 3 + [br_shape] * 3
    out_specs = [head_spec] * 3 + [br_spec] * 3
    if emit_latent:
        out_shape.append(jax.ShapeDtypeStruct((nb, seq, MLA_LATENT), jnp.float32))
        out_specs.append(pl.BlockSpec((None, tm, MLA_LATENT), lambda b, j: (b, j, 0)))
    return pl.pallas_call(
        functools.partial(_front_kernel, seq, emit_latent),
        grid=(nb, nt),
        in_specs=in_specs,
        out_specs=out_specs,
        out_shape=out_shape,
        scratch_shapes=[pltpu.VMEM((tm + 2 * HALO, BRANCH_DIM), jnp.float32),
                        pltpu.VMEM((tm + 2 * HALO, BRANCH_DIM), jnp.float32)],
        compiler_params=pltpu.CompilerParams(vmem_limit_bytes=VMEM_LIMIT),
        name="front",
    )(x, x, x, mod_l, w["g_pre"], w["w1"], w["w2"], w["conv_w"], w["conv_b"], w["g_q"], w["wuq"], w["g_kv"],
      w["wl2k"], w["wv"], w["pool_w"], w["pool_scale"], *tabs)


def _back_kernel(has_cache, x_ref, mod_ref, gpre_ref, wg_ref, q_ref, k_ref, v_ref, *rest):
    if has_cache:
        kc_ref, vc_ref, ya_ref, yc_ref, sbz_ref, wbr_ref, wo_ref, gpost_ref, out_ref = rest
    else:
        ya_ref, yc_ref, sbz_ref, wbr_ref, wo_ref, gpost_ref, out_ref = rest
    x = x_ref[...]
    mod_row = mod_ref[...]

    pairs = []
    for p in range(N_HEADS // 2):
        acc = None
        for h in (2 * p, 2 * p + 1):
            q = q_ref[h]
            s = _dot_nt(q, k_ref[h])
            m = jnp.max(s, axis=-1, keepdims=True)
            if has_cache:
                sc = _dot_nt(q, kc_ref[h])
                m = jnp.maximum(m, jnp.max(sc, axis=-1, keepdims=True))
            e = jnp.exp(s - m)
            denom = jnp.sum(e, axis=-1, keepdims=True)
            o = _dot(e.astype(jnp.bfloat16), v_ref[h])
            if has_cache:
                ec = jnp.exp(sc - m)
                denom = denom + jnp.sum(ec, axis=-1, keepdims=True)
                o = o + _dot(ec.astype(jnp.bfloat16), vc_ref[h])
            o = o / denom
            acc = o if acc is None else acc + o
        pairs.append(acc)
    attn = jnp.concatenate(pairs, axis=1)
    yb = (sbz_ref[...].astype(jnp.float32) * attn).astype(jnp.bfloat16)

    hn = _modulated_norm(x, gpre_ref[...], mod_row).astype(jnp.bfloat16)
    merged = None
    for n, y in enumerate((ya_ref[...], yb, yc_ref[...])):
        gate = jax.nn.sigmoid(_dot(hn, wg_ref[:, n * D_MODEL:(n + 1) * D_MODEL]))
        term = gate * _dot(y, wbr_ref[n])
        merged = term if merged is None else merged + term
    out = _rms(_dot(merged.astype(jnp.bfloat16), wo_ref[...]), gpost_ref[...])
    out_ref[...] = x + mod_row[:, 2 * D_MODEL:3 * D_MODEL] * out


def _back(l, x, mod_l, cond_row, w, q, k, v, kc, vc, ya, yc, sbz):
    nb, seq, _ = x.shape
    tq = TOKEN_TILE
    has_cache = kc is not None

    def wspec(shape):
        return pl.BlockSpec((None,) + shape, lambda b, j: (l,) + (0,) * len(shape))

    tile_spec = pl.BlockSpec((None, tq, D_MODEL), lambda b, j: (b, j, 0))
    br_spec = pl.BlockSpec((None, tq, BRANCH_DIM), lambda b, j: (b, j, 0))
    seq_spec = pl.BlockSpec((None, N_HEADS, seq, HEAD_PAD), lambda b, j: (b, 0, 0, 0))
    in_specs = [
        tile_spec,
        pl.BlockSpec((None, 1, 3 * D_MODEL), lambda b, j: (cond_row(b), 0, 0)),
        wspec((1, D_MODEL)),
        wspec((D_MODEL, 3 * D_MODEL)),
        pl.BlockSpec((None, N_HEADS, tq, HEAD_PAD), lambda b, j: (b, 0, j, 0)),
        seq_spec, seq_spec,
    ]
    args = [x, mod_l, w["g_pre"], w["wg"], q, k, v]
    if has_cache:
        past = kc.shape[3]
        cache_spec = pl.BlockSpec((None, None, N_HEADS, past, HEAD_PAD), lambda b, j: (l, b, 0, 0, 0))
        in_specs += [cache_spec, cache_spec]
        args += [kc, vc]
    in_specs += [br_spec, br_spec, br_spec,
                 wspec((3, BRANCH_DIM, D_MODEL)), wspec((D_MODEL, D_MODEL)), wspec((1, D_MODEL))]
    args += [ya, yc, sbz, w["w_branch"], w["w_o"], w["g_post"]]
    return pl.pallas_call(
        functools.partial(_back_kernel, has_cache),
        grid=(nb, seq // tq),
        in_specs=in_specs,
        out_specs=tile_spec,
        out_shape=jax.ShapeDtypeStruct(x.shape, jnp.float32),
        compiler_params=pltpu.CompilerParams(vmem_limit_bytes=VMEM_LIMIT),
        name="back",
    )(*args)


def _rope_swap_perm():
    quarter = QK_ROPE // 4
    idx = np.arange(QK_ROPE).reshape(2, 2, quarter)
    return idx[:, ::-1, :].reshape(-1)


def _rope_tables(n_tokens, rotate):
    if rotate:
        t = np.arange(n_tokens)
        pos = np.stack([t // GRID_W, t % GRID_W], axis=1).astype(np.float32)
        axis_dim = QK_ROPE // 2
        inv = (1.0 / (ROPE_THETA ** (jnp.arange(0, axis_dim, 2, dtype=jnp.float32) / axis_dim)))
        ang = jnp.asarray(pos)[:, :, None] * inv
        cos, sin = jnp.cos(ang), jnp.sin(ang)
        cfull = jnp.stack([cos, cos], axis=2).reshape(n_tokens, QK_ROPE)
        sfull = jnp.stack([-sin, sin], axis=2).reshape(n_tokens, QK_ROPE)
    else:
        cfull = jnp.ones((n_tokens, QK_ROPE), jnp.float32)
        sfull = jnp.zeros((n_tokens, QK_ROPE), jnp.float32)
    scale = (QK_NOPE + QK_ROPE) ** -0.5
    qtab = jnp.concatenate([jnp.full((n_tokens, QK_NOPE), scale, jnp.float32), cfull * scale, sfull * scale], axis=1)
    pad = jnp.zeros((n_tokens, LANES - QK_ROPE), jnp.float32)
    return qtab, jnp.concatenate([cfull, pad], axis=1), jnp.concatenate([sfull, pad], axis=1)


def _prepare_weights(g_pre, g_post, w_in, conv_w, conv_b, g_q, w_uq, g_kv, w_ukv, pool_w, pool_scale, w_branch, w_o):
    bf = jnp.bfloat16
    perm = _rope_swap_perm()
    w1 = w_in[:, :, 0:_W1_COLS].astype(bf)
    w2 = w_in[:, :, _W2_START:_W2_START + _W2_COLS].astype(bf)
    wg = w_in[:, :, _MERGE_START:].astype(bf)

    uq = w_uq.reshape(DEPTH, Q_LORA, N_HEADS, QK_NOPE + QK_ROPE)
    rope_q = uq[..., QK_NOPE:]
    wuq = jnp.concatenate([uq, rope_q[..., perm]], axis=-1).reshape(DEPTH, Q_LORA, HP).astype(bf)

    ukv = w_ukv.reshape(DEPTH, KV_LORA, N_HEADS, QK_NOPE + V_HEAD)
    k_part = jnp.concatenate([ukv[..., :QK_NOPE], jnp.zeros_like(ukv[..., :HEAD_PAD - QK_NOPE])], axis=-1)
    place = np.zeros((LANES, N_HEADS, HEAD_PAD), np.float32)
    for d in range(QK_ROPE):
        place[d, :, QK_NOPE + d] = 1.0
        place[d, :, QK_NOPE + QK_ROPE + d] = 1.0
    place = jnp.broadcast_to(jnp.asarray(place), (DEPTH, LANES, N_HEADS, HEAD_PAD))
    wl2k = jnp.concatenate([k_part, place], axis=1).reshape(DEPTH, _LAT_EXT, HP).astype(bf)

    v_w = ukv[..., QK_NOPE:]
    zeros_v = jnp.zeros_like(v_w)
    even = jnp.concatenate([v_w, zeros_v], axis=-1)
    odd = jnp.concatenate([zeros_v, v_w], axis=-1)
    is_odd = (np.arange(N_HEADS) % 2 == 1)[None, None, :, None]
    wv = jnp.where(is_odd, odd, even).reshape(DEPTH, KV_LORA, HP).astype(bf)

    return {
        "g_pre": g_pre.reshape(DEPTH, 1, D_MODEL), "g_post": g_post.reshape(DEPTH, 1, D_MODEL),
        "w1": w1, "w2": w2, "wg": wg, "conv_w": conv_w, "conv_b": conv_b.reshape(DEPTH, 1, BRANCH_DIM),
        "g_q": g_q.reshape(DEPTH, 1, Q_LORA), "wuq": wuq, "g_kv": g_kv.reshape(DEPTH, 1, KV_LORA),
        "wl2k": wl2k, "wv": wv, "pool_w": pool_w.astype(bf),
        "pool_scale": pool_scale.reshape(DEPTH, 1, BRANCH_DIM),
        "w_branch": w_branch.astype(bf), "w_o": w_o.astype(bf),
    }


def kernel(x_prompt, x_sample, cache_mla_latent, c, c_ctx, w_mod, b_mod, g_pre, g_post, w_in, conv_w, conv_b,
           g_q, w_uq, g_kv, w_ukv, pool_w, pool_scale, w_branch, w_o):
    n_dec = x_sample.shape[0]
    assert 1 + n_dec <= COND_ROWS
    assert x_prompt.shape[1] % TOKEN_TILE == 0 and x_sample.shape[1] % TOKEN_TILE == 0
    w = _prepare_weights(g_pre, g_post, w_in, conv_w, conv_b, g_q, w_uq, g_kv, w_ukv, pool_w, pool_scale,
                         w_branch, w_o)
    cond = jnp.concatenate([c_ctx[None, :], c, jnp.zeros((COND_ROWS - 1 - n_dec, D_MODEL), jnp.float32)], axis=0)
    mod = _modulation(cond, w_mod, b_mod).reshape(DEPTH, COND_ROWS, 1, 3 * D_MODEL)
    kc, vc = _cache_kv(cache_mla_latent, w["wl2k"], w["wv"])

    ctx_row = lambda b: 0
    dec_row = lambda b: b + 1

    h = x_prompt
    tabs = _rope_tables(x_prompt.shape[1], rotate=False)
    latents = []
    for l in range(DEPTH):
        q, k, v, ya, yc, sbz, lat = _front(l, h, mod[l], ctx_row, w, tabs, True)
        h = _back(l, h, mod[l], ctx_row, w, q, k, v, None, None, ya, yc, sbz)
        latents.append(lat)
    state = jnp.stack(latents, axis=1)

    hs = x_sample
    tabs = _rope_tables(x_sample.shape[1], rotate=True)
    for l in range(DEPTH):
        q, k, v, ya, yc, sbz = _front(l, hs, mod[l], dec_row, w, tabs, False)
        hs = _back(l, hs, mod[l], dec_row, w, q, k, v, kc, vc, ya, yc, sbz)
    return (h, hs, state)
```

```python
import functools

import numpy as np
import jax
import jax.numpy as jnp
from jax import lax
from jax.experimental import pallas as pl
from jax.experimental.pallas import tpu as pltpu

D_MODEL = 1024
DEPTH = 2
GRID_W = 64
EPS = 1e-6
BRANCH_DIM = 512
N_HEADS = 8
QK_NOPE = 64
QK_ROPE = 32
V_HEAD = 64
Q_LORA = 384
KV_LORA = 256
MLA_LATENT = KV_LORA + QK_ROPE
ROPE_THETA = 10000.0
POOL_WINDOWS = (2, 4, 8, 16)
POOL_GROUP = 128

LANES = 128
HEAD_PAD = LANES
HP = N_HEADS * HEAD_PAD
TOKEN_TILE = 256
HALO = 16
COND_ROWS = 8
VMEM_LIMIT = 56 * 1024 * 1024

_W1_COLS = 4 * BRANCH_DIM + Q_LORA + KV_LORA + LANES
_W2_START = 4 * BRANCH_DIM + Q_LORA + MLA_LATENT
_W2_COLS = 3 * BRANCH_DIM
_MERGE_START = _W2_START + _W2_COLS
_ACX0 = BRANCH_DIM
_AZ0 = 3 * BRANCH_DIM
_QD0 = BRANCH_DIM
_CKV0 = _QD0 + Q_LORA
_KR0 = _CKV0 + KV_LORA
_LAT_EXT = KV_LORA + LANES


def _dot(a, b):
    return jnp.dot(a, b, preferred_element_type=jnp.float32)


def _dot_nt(a, b):
    return lax.dot_general(a, b, (((1,), (1,)), ((), ())), preferred_element_type=jnp.float32)


def _rms(x, g):
    return x * lax.rsqrt(jnp.mean(x * x, axis=-1, keepdims=True) + EPS) * g


def _silu(x):
    return x * jax.nn.sigmoid(x)


def _modulated_norm(x, g_pre, mod_row):
    shift = mod_row[:, 0:D_MODEL]
    scale = mod_row[:, D_MODEL:2 * D_MODEL]
    return _rms(x, g_pre) * (1.0 + scale) + shift


def _mod_kernel(cond_ref, w_ref, b_ref, out_ref):
    h = _silu(cond_ref[...]).astype(jnp.bfloat16)
    out_ref[...] = _dot(h, w_ref[...].astype(jnp.bfloat16)) + b_ref[...]


def _modulation(cond, w_mod, b_mod):
    col_tile = D_MODEL
    return pl.pallas_call(
        _mod_kernel,
        grid=(DEPTH, 3 * D_MODEL // col_tile),
        in_specs=[
            pl.BlockSpec((COND_ROWS, D_MODEL), lambda l, n: (0, 0)),
            pl.BlockSpec((None, D_MODEL, col_tile), lambda l, n: (l, 0, n)),
            pl.BlockSpec((None, 1, col_tile), lambda l, n: (l, 0, n)),
        ],
        out_specs=pl.BlockSpec((None, COND_ROWS, col_tile), lambda l, n: (l, 0, n)),
        out_shape=jax.ShapeDtypeStruct((DEPTH, COND_ROWS, 3 * D_MODEL), jnp.float32),
        compiler_params=pltpu.CompilerParams(vmem_limit_bytes=VMEM_LIMIT),
        name="modulation",
    )(cond, w_mod, b_mod.reshape(DEPTH, 1, 3 * D_MODEL))


def _split_w_in_kernel(w_ref, w1_ref, w2_ref, wg_ref):
    w1_ref[...] = w_ref[:, 0:_W1_COLS].astype(jnp.bfloat16)
    w2_ref[...] = w_ref[:, _W2_START:_W2_START + _W2_COLS].astype(jnp.bfloat16)
    wg_ref[...] = w_ref[:, _MERGE_START:_MERGE_START + 3 * D_MODEL].astype(jnp.bfloat16)


def _split_w_in(w_in):
    rows = 256
    widths = (_W1_COLS, _W2_COLS, 3 * D_MODEL)
    return pl.pallas_call(
        _split_w_in_kernel,
        grid=(DEPTH, D_MODEL // rows),
        in_specs=[pl.BlockSpec((None, rows, w_in.shape[2]), lambda l, r: (l, r, 0))],
        out_specs=[pl.BlockSpec((None, rows, n), lambda l, r: (l, r, 0)) for n in widths],
        out_shape=[jax.ShapeDtypeStruct((DEPTH, D_MODEL, n), jnp.bfloat16) for n in widths],
        compiler_params=pltpu.CompilerParams(vmem_limit_bytes=VMEM_LIMIT),
        name="split_w_in",
    )(w_in)


def _cache_kv_kernel(lat_ref, wl2k_ref, wv_ref, k_ref, v_ref):
    lat = lat_ref[...].astype(jnp.bfloat16)
    k = _dot(lat, wl2k_ref[0:MLA_LATENT, :]).astype(jnp.bfloat16)
    v = _dot(lat[:, 0:KV_LORA], wv_ref[...]).astype(jnp.bfloat16)
    for h in range(N_HEADS):
        k_ref[h] = k[:, h * HEAD_PAD:(h + 1) * HEAD_PAD]
        v_ref[h] = v[:, h * HEAD_PAD:(h + 1) * HEAD_PAD]


def _cache_kv(cache, wl2k, wv):
    nb, _, past, _ = cache.shape
    kv_shape = jax.ShapeDtypeStruct((DEPTH, nb, N_HEADS, past, HEAD_PAD), jnp.bfloat16)
    kv_spec = pl.BlockSpec((None, None, N_HEADS, past, HEAD_PAD), lambda l, b: (l, b, 0, 0, 0))
    return pl.pallas_call(
        _cache_kv_kernel,
        grid=(DEPTH, nb),
        in_specs=[
            pl.BlockSpec((None, None, past, MLA_LATENT), lambda l, b: (b, l, 0, 0)),
            pl.BlockSpec((None, _LAT_EXT, HP), lambda l, b: (l, 0, 0)),
            pl.BlockSpec((None, KV_LORA, HP), lambda l, b: (l, 0, 0)),
        ],
        out_specs=[kv_spec, kv_spec],
        out_shape=[kv_shape, kv_shape],
        compiler_params=pltpu.CompilerParams(vmem_limit_bytes=VMEM_LIMIT),
        name="cache_kv",
    )(cache, wl2k, wv)


def _front_kernel(seq_len, emit_latent, x_ref, xp_ref, xn_ref, mod_ref, gpre_ref, w1_ref, w2_ref, convw_ref, convb_ref,
                  gq_ref, wuq_ref, gkv_ref, wl2k_ref, wv_ref, poolw_ref, pscale_ref, qtab_ref, kcos_ref, ksin_ref,
                  *rest):
    if emit_latent:
        q_out, k_out, v_out, ya_out, yc_out, sbz_out, lat_out, u_scr, cu_scr = rest
    else:
        q_out, k_out, v_out, ya_out, yc_out, sbz_out, u_scr, cu_scr = rest
    tm = TOKEN_TILE
    j = pl.program_id(1)
    last_j = seq_len // tm - 1
    mod_row = mod_ref[...]
    g_pre = gpre_ref[...]

    x_ext = jnp.concatenate([xp_ref[...], x_ref[...], xn_ref[...]], axis=0)
    hn_ext = _modulated_norm(x_ext, g_pre, mod_row).astype(jnp.bfloat16)
    hn = hn_ext[HALO:HALO + tm]

    acx = _dot(hn_ext, w1_ref[:, _ACX0:_ACX0 + 2 * BRANCH_DIM])
    row = lax.broadcasted_iota(jnp.int32, (tm + 2 * HALO, 1), 0)
    first_valid = jnp.where(j > 0, 0, HALO)
    end_valid = jnp.where(j < last_j, tm + 2 * HALO, tm + HALO)
    valid = jnp.logical_and(row >= first_valid, row < end_valid)
    u_scr[...] = jnp.where(valid, acx[:, 0:BRANCH_DIM] * acx[:, BRANCH_DIM:2 * BRANCH_DIM], 0.0)
    cu_scr[...] = jnp.where(valid, _dot(hn_ext, w2_ref[:, BRANCH_DIM:2 * BRANCH_DIM]), 0.0)

    pm = _dot(hn, w1_ref[:, _AZ0:_W1_COLS])
    conv = (u_scr[HALO - 1:HALO - 1 + tm, :] * convw_ref[0:1, :]
            + u_scr[HALO:HALO + tm, :] * convw_ref[1:2, :]
            + u_scr[HALO + 1:HALO + 1 + tm, :] * convw_ref[2:3, :]
            + convb_ref[...])
    ya = _silu(pm[:, 0:BRANCH_DIM]) * (_dot(hn, w1_ref[:, 0:BRANCH_DIM]) * conv)
    ya_out[...] = ya.astype(jnp.bfloat16)

    t = j * tm + lax.broadcasted_iota(jnp.int32, (tm, POOL_GROUP), 0)
    mixed = []
    for gi, win in enumerate(POOL_WINDOWS):
        half = win // 2
        cols = slice(gi * POOL_GROUP, (gi + 1) * POOL_GROUP)
        total = cu_scr[HALO - half:HALO - half + tm, cols]
        for k in range(-half + 1, half):
            total = total + cu_scr[HALO + k:HALO + k + tm, cols]
        count = jnp.minimum(t + half, seq_len) - jnp.maximum(t - half, 0)
        pooled = total / count.astype(jnp.float32) - cu_scr[HALO:HALO + tm, cols]
        mixed.append(_dot(pooled.astype(jnp.bfloat16), poolw_ref[gi]))
    mixed = jnp.concatenate(mixed, axis=1) * pscale_ref[...]
    c_z = _dot(hn, w2_ref[:, 2 * BRANCH_DIM:3 * BRANCH_DIM])
    yc_out[...] = (_silu(c_z) * mixed).astype(jnp.bfloat16)

    qn = _rms(pm[:, _QD0:_QD0 + Q_LORA], gq_ref[...]).astype(jnp.bfloat16)
    q = _dot(qn, wuq_ref[...])
    qtab = qtab_ref[...]
    ckv = _rms(pm[:, _CKV0:_CKV0 + KV_LORA], gkv_ref[...])
    kr = pm[:, _KR0:_KR0 + LANES]
    if emit_latent:
        lat_out[:, 0:KV_LORA] = ckv
        lat_out[:, KV_LORA:MLA_LATENT] = kr[:, 0:QK_ROPE]
    lane = lax.broadcasted_iota(jnp.int32, (tm, LANES), 1)
    quarter = QK_ROPE // 4
    partner = jnp.where(lane % (2 * quarter) < quarter,
                        pltpu.roll(kr, LANES - quarter, 1), pltpu.roll(kr, quarter, 1))
    kr_rot = jnp.where(lane < QK_ROPE, kr * kcos_ref[...] + partner * ksin_ref[...], 0.0)
    ext = jnp.concatenate([ckv, kr_rot], axis=1).astype(jnp.bfloat16)
    k = _dot(ext, wl2k_ref[...]).astype(jnp.bfloat16)
    v = _dot(ext[:, 0:KV_LORA], wv_ref[...]).astype(jnp.bfloat16)
    for h in range(N_HEADS):
        cols = slice(h * HEAD_PAD, (h + 1) * HEAD_PAD)
        q_out[h] = (q[:, cols] * qtab).astype(jnp.bfloat16)
        k_out[h] = k[:, cols]
        v_out[h] = v[:, cols]
    sbz_out[...] = _silu(_dot(hn, w2_ref[:, 0:BRANCH_DIM])).astype(jnp.bfloat16)


def _front(l, x, mod_l, cond_row, w, tabs, emit_latent):
    nb, seq, _ = x.shape
    tm = TOKEN_TILE
    nt = seq // tm
    hb = tm // HALO
    n_hblk = seq // HALO

    def wspec(shape):
        return pl.BlockSpec((None,) + shape, lambda b, j: (l,) + (0,) * len(shape))

    in_specs = [
        pl.BlockSpec((None, tm, D_MODEL), lambda b, j: (b, j, 0)),
        pl.BlockSpec((None, HALO, D_MODEL), lambda b, j: (b, jnp.maximum(j * hb - 1, 0), 0)),
        pl.BlockSpec((None, HALO, D_MODEL), lambda b, j: (b, jnp.minimum((j + 1) * hb, n_hblk - 1), 0)),
        pl.BlockSpec((None, 1, 3 * D_MODEL), lambda b, j: (cond_row(b), 0, 0)),
        wspec((1, D_MODEL)),
        wspec((D_MODEL, _W1_COLS)),
        wspec((D_MODEL, _W2_COLS)),
        wspec((3, BRANCH_DIM)),
        wspec((1, BRANCH_DIM)),
        wspec((1, Q_LORA)),
        wspec((Q_LORA, HP)),
        wspec((1, KV_LORA)),
        wspec((_LAT_EXT, HP)),
        wspec((KV_LORA, HP)),
        wspec((len(POOL_WINDOWS), POOL_GROUP, POOL_GROUP)),
        wspec((1, BRANCH_DIM)),
        pl.BlockSpec((tm, LANES), lambda b, j: (j, 0)),
        pl.BlockSpec((tm, LANES), lambda b, j: (j, 0)),
        pl.BlockSpec((tm, LANES), lambda b, j: (j, 0)),
    ]
    head_shape = jax.ShapeDtypeStruct((nb, N_HEADS, seq, HEAD_PAD), jnp.bfloat16)
    head_spec = pl.BlockSpec((None, N_HEADS, tm, HEAD_PAD), lambda b, j: (b, 0, j, 0))
    br_shape = jax.ShapeDtypeStruct((nb, seq, BRANCH_DIM), jnp.bfloat16)
    br_spec = pl.BlockSpec((None, tm, BRANCH_DIM), lambda b, j: (b, j, 0))
    out_shape = [head_shape] * 3 + [br_shape] * 3
    out_specs = [head_spec] * 3 + [br_spec] * 3
    if emit_latent:
        out_shape.append(jax.ShapeDtypeStruct((nb, seq, MLA_LATENT), jnp.float32))
        out_specs.append(pl.BlockSpec((None, tm, MLA_LATENT), lambda b, j: (b, j, 0)))
    return pl.pallas_call(
        functools.partial(_front_kernel, seq, emit_latent),
        grid=(nb, nt),
        in_specs=in_specs,
        out_specs=out_specs,
        out_shape=out_shape,
        scratch_shapes=[pltpu.VMEM((tm + 2 * HALO, BRANCH_DIM), jnp.float32),
                        pltpu.VMEM((tm + 2 * HALO, BRANCH_DIM), jnp.float32)],
        compiler_params=pltpu.CompilerParams(vmem_limit_bytes=VMEM_LIMIT),
        name="front",
    )(x, x, x, mod_l, w["g_pre"], w["w1"], w["w2"], w["conv_w"], w["conv_b"], w["g_q"], w["wuq"], w["g_kv"],
      w["wl2k"], w["wv"], w["pool_w"], w["pool_scale"], *tabs)


def _back_kernel(has_cache, x_ref, mod_ref, gpre_ref, wg_ref, q_ref, k_ref, v_ref, *rest):
    if has_cache:
        kc_ref, vc_ref, ya_ref, yc_ref, sbz_ref, wbr_ref, wo_ref, gpost_ref, out_ref = rest
    else:
        ya_ref, yc_ref, sbz_ref, wbr_ref, wo_ref, gpost_ref, out_ref = rest
    x = x_ref[...]
    mod_row = mod_ref[...]

    pairs = []
    for p in range(N_HEADS // 2):
        acc = None
        for h in (2 * p, 2 * p + 1):
            q = q_ref[h]
            s = _dot_nt(q, k_ref[h])
            m = jnp.max(s, axis=-1, keepdims=True)
            if has_cache:
                sc = _dot_nt(q, kc_ref[h])
                m = jnp.maximum(m, jnp.max(sc, axis=-1, keepdims=True))
            e = jnp.exp(s - m)
            denom = jnp.sum(e, axis=-1, keepdims=True)
            o = _dot(e.astype(jnp.bfloat16), v_ref[h])
            if has_cache:
                ec = jnp.exp(sc - m)
                denom = denom + jnp.sum(ec, axis=-1, keepdims=True)
                o = o + _dot(ec.astype(jnp.bfloat16), vc_ref[h])
            o = o / denom
            acc = o if acc is None else acc + o
        pairs.append(acc)
    attn = jnp.concatenate(pairs, axis=1)
    yb = (sbz_ref[...].astype(jnp.float32) * attn).astype(jnp.bfloat16)

    hn = _modulated_norm(x, gpre_ref[...], mod_row).astype(jnp.bfloat16)
    merged = None
    for n, y in enumerate((ya_ref[...], yb, yc_ref[...])):
        gate = jax.nn.sigmoid(_dot(hn, wg_ref[:, n * D_MODEL:(n + 1) * D_MODEL]))
        term = gate * _dot(y, wbr_ref[n])
        merged = term if merged is None else merged + term
    out = _rms(_dot(merged.astype(jnp.bfloat16), wo_ref[...]), gpost_ref[...])
    out_ref[...] = x + mod_row[:, 2 * D_MODEL:3 * D_MODEL] * out


def _back(l, x, mod_l, cond_row, w, q, k, v, kc, vc, ya, yc, sbz):
    nb, seq, _ = x.shape
    tq = TOKEN_TILE
    has_cache = kc is not None

    def wspec(shape):
        return pl.BlockSpec((None,) + shape, lambda b, j: (l,) + (0,) * len(shape))

    tile_spec = pl.BlockSpec((None, tq, D_MODEL), lambda b, j: (b, j, 0))
    br_spec = pl.BlockSpec((None, tq, BRANCH_DIM), lambda b, j: (b, j, 0))
    seq_spec = pl.BlockSpec((None, N_HEADS, seq, HEAD_PAD), lambda b, j: (b, 0, 0, 0))
    in_specs = [
        tile_spec,
        pl.BlockSpec((None, 1, 3 * D_MODEL), lambda b, j: (cond_row(b), 0, 0)),
        wspec((1, D_MODEL)),
        wspec((D_MODEL, 3 * D_MODEL)),
        pl.BlockSpec((None, N_HEADS, tq, HEAD_PAD), lambda b, j: (b, 0, j, 0)),
        seq_spec, seq_spec,
    ]
    args = [x, mod_l, w["g_pre"], w["wg"], q, k, v]
    if has_cache:
        past = kc.shape[3]
        cache_spec = pl.BlockSpec((None, None, N_HEADS, past, HEAD_PAD), lambda b, j: (l, b, 0, 0, 0))
        in_specs += [cache_spec, cache_spec]
        args += [kc, vc]
    in_specs += [br_spec, br_spec, br_spec,
                 wspec((3, BRANCH_DIM, D_MODEL)), wspec((D_MODEL, D_MODEL)), wspec((1, D_MODEL))]
    args += [ya, yc, sbz, w["w_branch"], w["w_o"], w["g_post"]]
    return pl.pallas_call(
        functools.partial(_back_kernel, has_cache),
        grid=(nb, seq // tq),
        in_specs=in_specs,
        out_specs=tile_spec,
        out_shape=jax.ShapeDtypeStruct(x.shape, jnp.float32),
        compiler_params=pltpu.CompilerParams(vmem_limit_bytes=VMEM_LIMIT),
        name="back",
    )(*args)


def _rope_swap_perm():
    quarter = QK_ROPE // 4
    idx = np.arange(QK_ROPE).reshape(2, 2, quarter)
    return idx[:, ::-1, :].reshape(-1)


def _rope_tables(n_tokens, rotate):
    if rotate:
        t = np.arange(n_tokens)
        pos = np.stack([t // GRID_W, t % GRID_W], axis=1).astype(np.float32)
        axis_dim = QK_ROPE // 2
        inv = (1.0 / (ROPE_THETA ** (jnp.arange(0, axis_dim, 2, dtype=jnp.float32) / axis_dim)))
        ang = jnp.asarray(pos)[:, :, None] * inv
        cos, sin = jnp.cos(ang), jnp.sin(ang)
        cfull = jnp.stack([cos, cos], axis=2).reshape(n_tokens, QK_ROPE)
        sfull = jnp.stack([-sin, sin], axis=2).reshape(n_tokens, QK_ROPE)
    else:
        cfull = jnp.ones((n_tokens, QK_ROPE), jnp.float32)
        sfull = jnp.zeros((n_tokens, QK_ROPE), jnp.float32)
    scale = (QK_NOPE + QK_ROPE) ** -0.5
    qtab = jnp.concatenate([jnp.full((n_tokens, QK_NOPE), scale, jnp.float32), cfull * scale, sfull * scale], axis=1)
    pad = jnp.zeros((n_tokens, LANES - QK_ROPE), jnp.float32)
    return qtab, jnp.concatenate([cfull, pad], axis=1), jnp.concatenate([sfull, pad], axis=1)


def _prepare_weights(g_pre, g_post, w_in, conv_w, conv_b, g_q, w_uq, g_kv, w_ukv, pool_w, pool_scale, w_branch, w_o):
    bf = jnp.bfloat16
    perm = _rope_swap_perm()
    assert w_in.shape[2] == _MERGE_START + 3 * D_MODEL
    w1, w2, wg = _split_w_in(w_in)

    uq = w_uq.reshape(DEPTH, Q_LORA, N_HEADS, QK_NOPE + QK_ROPE)
    rope_q = uq[..., QK_NOPE:]
    wuq = jnp.concatenate([uq, rope_q[..., perm]], axis=-1).reshape(DEPTH, Q_LORA, HP).astype(bf)

    ukv = w_ukv.reshape(DEPTH, KV_LORA, N_HEADS, QK_NOPE + V_HEAD)
    k_part = jnp.concatenate([ukv[..., :QK_NOPE], jnp.zeros_like(ukv[..., :HEAD_PAD - QK_NOPE])], axis=-1)
    place = np.zeros((LANES, N_HEADS, HEAD_PAD), np.float32)
    for d in range(QK_ROPE):
        place[d, :, QK_NOPE + d] = 1.0
        place[d, :, QK_NOPE + QK_ROPE + d] = 1.0
    place = jnp.broadcast_to(jnp.asarray(place), (DEPTH, LANES, N_HEADS, HEAD_PAD))
    wl2k = jnp.concatenate([k_part, place], axis=1).reshape(DEPTH, _LAT_EXT, HP).astype(bf)

    v_w = ukv[..., QK_NOPE:]
    zeros_v = jnp.zeros_like(v_w)
    even = jnp.concatenate([v_w, zeros_v], axis=-1)
    odd = jnp.concatenate([zeros_v, v_w], axis=-1)
    is_odd = (np.arange(N_HEADS) % 2 == 1)[None, None, :, None]
    wv = jnp.where(is_odd, odd, even).reshape(DEPTH, KV_LORA, HP).astype(bf)

    return {
        "g_pre": g_pre.reshape(DEPTH, 1, D_MODEL), "g_post": g_post.reshape(DEPTH, 1, D_MODEL),
        "w1": w1, "w2": w2, "wg": wg, "conv_w": conv_w, "conv_b": conv_b.reshape(DEPTH, 1, BRANCH_DIM),
        "g_q": g_q.reshape(DEPTH, 1, Q_LORA), "wuq": wuq, "g_kv": g_kv.reshape(DEPTH, 1, KV_LORA),
        "wl2k": wl2k, "wv": wv, "pool_w": pool_w.astype(bf),
        "pool_scale": pool_scale.reshape(DEPTH, 1, BRANCH_DIM),
        "w_branch": w_branch.astype(bf), "w_o": w_o.astype(bf),
    }


def kernel(x_prompt, x_sample, cache_mla_latent, c, c_ctx, w_mod, b_mod, g_pre, g_post, w_in, conv_w, conv_b,
           g_q, w_uq, g_kv, w_ukv, pool_w, pool_scale, w_branch, w_o):
    n_dec = x_sample.shape[0]
    assert 1 + n_dec <= COND_ROWS
    assert x_prompt.shape[1] % TOKEN_TILE == 0 and x_sample.shape[1] % TOKEN_TILE == 0
    w = _prepare_weights(g_pre, g_post, w_in, conv_w, conv_b, g_q, w_uq, g_kv, w_ukv, pool_w, pool_scale,
                         w_branch, w_o)
    cond = jnp.concatenate([c_ctx[None, :], c, jnp.zeros((COND_ROWS - 1 - n_dec, D_MODEL), jnp.float32)], axis=0)
    mod = _modulation(cond, w_mod, b_mod).reshape(DEPTH, COND_ROWS, 1, 3 * D_MODEL)
    kc, vc = _cache_kv(cache_mla_latent, w["wl2k"], w["wv"])

    ctx_row = lambda b: 0
    dec_row = lambda b: b + 1

    h = x_prompt
    tabs = _rope_tables(x_prompt.shape[1], rotate=False)
    latents = []
    for l in range(DEPTH):
        q, k, v, ya, yc, sbz, lat = _front(l, h, mod[l], ctx_row, w, tabs, True)
        h = _back(l, h, mod[l], ctx_row, w, q, k, v, None, None, ya, yc, sbz)
        latents.append(lat)
    state = jnp.stack(latents, axis=1)

    hs = x_sample
    tabs = _rope_tables(x_sample.shape[1], rotate=True)
    for l in range(DEPTH):
        q, k, v, ya, yc, sbz = _front(l, hs, mod[l], dec_row, w, tabs, False)
        hs = _back(l, hs, mod[l], dec_row, w, q, k, v, kc, vc, ya, yc, sbz)
    return (h, hs, state)
```

```python
import functools

import numpy as np
import jax
import jax.numpy as jnp
from jax import lax
from jax.experimental import pallas as pl
from jax.experimental.pallas import tpu as pltpu

D_MODEL = 1024
DEPTH = 2
GRID_W = 64
EPS = 1e-6
BRANCH_DIM = 512
N_HEADS = 8
QK_NOPE = 64
QK_ROPE = 32
V_HEAD = 64
Q_LORA = 384
KV_LORA = 256
MLA_LATENT = KV_LORA + QK_ROPE
ROPE_THETA = 10000.0
POOL_WINDOWS = (2, 4, 8, 16)
POOL_GROUP = 128

LANES = 128
HEAD_PAD = LANES
HP = N_HEADS * HEAD_PAD
TOKEN_TILE = 256
HALO = 16
COND_ROWS = 8
VMEM_LIMIT = 56 * 1024 * 1024

_R_AB = 0
_R_ACX = BRANCH_DIM
_R_MAIN = 3 * BRANCH_DIM
_MAIN_ROWS = BRANCH_DIM + Q_LORA + KV_LORA + LANES
_R_BZ = 4 * BRANCH_DIM + Q_LORA + MLA_LATENT
_R_CU = _R_BZ + BRANCH_DIM
_R_CZ = _R_CU + BRANCH_DIM
_R_MERGE = _R_CZ + BRANCH_DIM
W_IN_COLS = _R_MERGE + 3 * D_MODEL
_FRONT_ROWS = 4352
_BACK_ROWS = W_IN_COLS // 2
_QD0 = BRANCH_DIM
_CKV0 = _QD0 + Q_LORA
_KR0 = _CKV0 + KV_LORA
_LAT_EXT = KV_LORA + LANES
assert _FRONT_ROWS >= _R_MERGE and _BACK_ROWS <= _R_MERGE and _BACK_ROWS % 16 == 0


def _dot(a, b):
    return jnp.dot(a, b, preferred_element_type=jnp.float32)


def _dot_nt(a, b):
    return lax.dot_general(a, b, (((1,), (1,)), ((), ())), preferred_element_type=jnp.float32)


def _rms(x, g):
    return x * lax.rsqrt(jnp.mean(x * x, axis=-1, keepdims=True) + EPS) * g


def _silu(x):
    return x * jax.nn.sigmoid(x)


def _modulated_norm(x, g_pre, mod_row):
    shift = mod_row[:, 0:D_MODEL]
    scale = mod_row[:, D_MODEL:2 * D_MODEL]
    return _rms(x, g_pre) * (1.0 + scale) + shift


def _mod_kernel(cond_ref, w_ref, b_ref, out_ref):
    h = _silu(cond_ref[...]).astype(jnp.bfloat16)
    out_ref[...] = _dot(h, w_ref[...].astype(jnp.bfloat16)) + b_ref[...]


def _modulation(cond, w_mod, b_mod):
    col_tile = D_MODEL
    return pl.pallas_call(
        _mod_kernel,
        grid=(DEPTH, 3 * D_MODEL // col_tile),
        in_specs=[
            pl.BlockSpec((COND_ROWS, D_MODEL), lambda l, n: (0, 0)),
            pl.BlockSpec((None, D_MODEL, col_tile), lambda l, n: (l, 0, n)),
            pl.BlockSpec((None, 1, col_tile), lambda l, n: (l, 0, n)),
        ],
        out_specs=pl.BlockSpec((None, COND_ROWS, col_tile), lambda l, n: (l, 0, n)),
        out_shape=jax.ShapeDtypeStruct((DEPTH, COND_ROWS, 3 * D_MODEL), jnp.float32),
        compiler_params=pltpu.CompilerParams(vmem_limit_bytes=VMEM_LIMIT),
        name="modulation",
    )(cond, w_mod, b_mod.reshape(DEPTH, 1, 3 * D_MODEL))


def _cast_kernel(w_ref, out_ref):
    out_ref[...] = w_ref[...].astype(jnp.bfloat16)


def _cast_w_in(w_in_t):
    rows = 512
    spec = pl.BlockSpec((None, rows, D_MODEL), lambda l, r: (l, r, 0))
    return pl.pallas_call(
        _cast_kernel,
        grid=(DEPTH, pl.cdiv(W_IN_COLS, rows)),
        in_specs=[spec],
        out_specs=spec,
        out_shape=jax.ShapeDtypeStruct(w_in_t.shape, jnp.bfloat16),
        compiler_params=pltpu.CompilerParams(vmem_limit_bytes=VMEM_LIMIT),
        name="cast_w_in",
    )(w_in_t)


def _cache_kv_kernel(lat_t_ref, wl2k_ref, wv_ref, k_ref, v_ref):
    lat_t = lat_t_ref[...]
    pad = jnp.zeros((_LAT_EXT - MLA_LATENT, lat_t.shape[1]), jnp.float32)
    lat = jnp.concatenate([lat_t, pad], axis=0).T.astype(jnp.bfloat16)
    k = _dot(lat, wl2k_ref[...]).astype(jnp.bfloat16)
    v = _dot(lat[:, 0:KV_LORA], wv_ref[...]).astype(jnp.bfloat16)
    for h in range(N_HEADS):
        k_ref[h] = k[:, h * HEAD_PAD:(h + 1) * HEAD_PAD]
        v_ref[h] = v[:, h * HEAD_PAD:(h + 1) * HEAD_PAD]


def _cache_kv(cache_t, wl2k, wv):
    nb, _, _, past = cache_t.shape
    kv_shape = jax.ShapeDtypeStruct((DEPTH, nb, N_HEADS, past, HEAD_PAD), jnp.bfloat16)
    kv_spec = pl.BlockSpec((None, None, N_HEADS, past, HEAD_PAD), lambda l, b: (l, b, 0, 0, 0))
    return pl.pallas_call(
        _cache_kv_kernel,
        grid=(DEPTH, nb),
        in_specs=[
            pl.BlockSpec((None, None, MLA_LATENT, past), lambda l, b: (b, l, 0, 0)),
            pl.BlockSpec((None, _LAT_EXT, HP), lambda l, b: (l, 0, 0)),
            pl.BlockSpec((None, KV_LORA, HP), lambda l, b: (l, 0, 0)),
        ],
        out_specs=[kv_spec, kv_spec],
        out_shape=[kv_shape, kv_shape],
        compiler_params=pltpu.CompilerParams(vmem_limit_bytes=VMEM_LIMIT),
        name="cache_kv",
    )(cache_t, wl2k, wv)


def _front_kernel(seq_len, emit_latent, x_ref, xp_ref, xn_ref, mod_ref, gpre_ref, w_ref, convw_ref, convb_ref,
                  gq_ref, wuq_ref, gkv_ref, wl2k_ref, wv_ref, poolw_ref, pscale_ref, qtab_ref, kcos_ref, ksin_ref,
                  *rest):
    if emit_latent:
        q_out, k_out, v_out, ya_out, yc_out, sbz_out, lat_out, u_scr, cu_scr = rest
    else:
        q_out, k_out, v_out, ya_out, yc_out, sbz_out, u_scr, cu_scr = rest
    tm = TOKEN_TILE
    j = pl.program_id(1)
    last_j = seq_len // tm - 1
    mod_row = mod_ref[...]
    g_pre = gpre_ref[...]

    x_ext = jnp.concatenate([xp_ref[...], x_ref[...], xn_ref[...]], axis=0)
    hn_ext = _modulated_norm(x_ext, g_pre, mod_row).astype(jnp.bfloat16)
    hn = hn_ext[HALO:HALO + tm]

    acx = _dot_nt(hn_ext, w_ref[_R_ACX:_R_ACX + 2 * BRANCH_DIM, :])
    row = lax.broadcasted_iota(jnp.int32, (tm + 2 * HALO, 1), 0)
    first_valid = jnp.where(j > 0, 0, HALO)
    end_valid = jnp.where(j < last_j, tm + 2 * HALO, tm + HALO)
    valid = jnp.logical_and(row >= first_valid, row < end_valid)
    u_scr[...] = jnp.where(valid, acx[:, 0:BRANCH_DIM] * acx[:, BRANCH_DIM:2 * BRANCH_DIM], 0.0)
    cu_scr[...] = jnp.where(valid, _dot_nt(hn_ext, w_ref[_R_CU:_R_CU + BRANCH_DIM, :]), 0.0)

    pm = _dot_nt(hn, w_ref[_R_MAIN:_R_MAIN + _MAIN_ROWS, :])
    conv = (u_scr[HALO - 1:HALO - 1 + tm, :] * convw_ref[0:1, :]
            + u_scr[HALO:HALO + tm, :] * convw_ref[1:2, :]
            + u_scr[HALO + 1:HALO + 1 + tm, :] * convw_ref[2:3, :]
            + convb_ref[...])
    ya = _silu(pm[:, 0:BRANCH_DIM]) * (_dot_nt(hn, w_ref[_R_AB:_R_AB + BRANCH_DIM, :]) * conv)
    ya_out[...] = ya.astype(jnp.bfloat16)

    t = j * tm + lax.broadcasted_iota(jnp.int32, (tm, POOL_GROUP), 0)
    mixed = []
    for gi, win in enumerate(POOL_WINDOWS):
        half = win // 2
        cols = slice(gi * POOL_GROUP, (gi + 1) * POOL_GROUP)
        total = cu_scr[HALO - half:HALO - half + tm, cols]
        for k in range(-half + 1, half):
            total = total + cu_scr[HALO + k:HALO + k + tm, cols]
        count = jnp.minimum(t + half, seq_len) - jnp.maximum(t - half, 0)
        pooled = total / count.astype(jnp.float32) - cu_scr[HALO:HALO + tm, cols]
        mixed.append(_dot(pooled.astype(jnp.bfloat16), poolw_ref[gi]))
    mixed = jnp.concatenate(mixed, axis=1) * pscale_ref[...]
    c_z = _dot_nt(hn, w_ref[_R_CZ:_R_CZ + BRANCH_DIM, :])
    yc_out[...] = (_silu(c_z) * mixed).astype(jnp.bfloat16)

    qn = _rms(pm[:, _QD0:_QD0 + Q_LORA], gq_ref[...]).astype(jnp.bfloat16)
    q = _dot(qn, wuq_ref[...])
    qtab = qtab_ref[...]
    ckv = _rms(pm[:, _CKV0:_CKV0 + KV_LORA], gkv_ref[...])
    kr = pm[:, _KR0:_KR0 + LANES]
    if emit_latent:
        lat_out[0:KV_LORA, :] = ckv.T
        lat_out[KV_LORA:MLA_LATENT, :] = kr.T[0:QK_ROPE, :]
    lane = lax.broadcasted_iota(jnp.int32, (tm, LANES), 1)
    quarter = QK_ROPE // 4
    partner = jnp.where(lane % (2 * quarter) < quarter,
                        pltpu.roll(kr, LANES - quarter, 1), pltpu.roll(kr, quarter, 1))
    kr_rot = jnp.where(lane < QK_ROPE, kr * kcos_ref[...] + partner * ksin_ref[...], 0.0)
    ext = jnp.concatenate([ckv, kr_rot], axis=1).astype(jnp.bfloat16)
    k = _dot(ext, wl2k_ref[...]).astype(jnp.bfloat16)
    v = _dot(ext[:, 0:KV_LORA], wv_ref[...]).astype(jnp.bfloat16)
    for h in range(N_HEADS):
        cols = slice(h * HEAD_PAD, (h + 1) * HEAD_PAD)
        q_out[h] = (q[:, cols] * qtab).astype(jnp.bfloat16)
        k_out[h] = k[:, cols]
        v_out[h] = v[:, cols]
    sbz_out[...] = _silu(_dot_nt(hn, w_ref[_R_BZ:_R_BZ + BRANCH_DIM, :])).astype(jnp.bfloat16)


def _front(l, x, mod_l, cond_row, w, tabs, emit_latent):
    nb, seq, _ = x.shape
    tm = TOKEN_TILE
    nt = seq // tm
    hb = tm // HALO
    n_hblk = seq // HALO

    def wspec(shape):
        return pl.BlockSpec((None,) + shape, lambda b, j: (l,) + (0,) * len(shape))

    in_specs = [
        pl.BlockSpec((None, tm, D_MODEL), lambda b, j: (b, j, 0)),
        pl.BlockSpec((None, HALO, D_MODEL), lambda b, j: (b, jnp.maximum(j * hb - 1, 0), 0)),
        pl.BlockSpec((None, HALO, D_MODEL), lambda b, j: (b, jnp.minimum((j + 1) * hb, n_hblk - 1), 0)),
        pl.BlockSpec((None, 1, 3 * D_MODEL), lambda b, j: (cond_row(b), 0, 0)),
        wspec((1, D_MODEL)),
        wspec((_FRONT_ROWS, D_MODEL)),
        wspec((3, BRANCH_DIM)),
        wspec((1, BRANCH_DIM)),
        wspec((1, Q_LORA)),
        wspec((Q_LORA, HP)),
        wspec((1, KV_LORA)),
        wspec((_LAT_EXT, HP)),
        wspec((KV_LORA, HP)),
        wspec((len(POOL_WINDOWS), POOL_GROUP, POOL_GROUP)),
        wspec((1, BRANCH_DIM)),
        pl.BlockSpec((tm, LANES), lambda b, j: (j, 0)),
        pl.BlockSpec((tm, LANES), lambda b, j: (j, 0)),
        pl.BlockSpec((tm, LANES), lambda b, j: (j, 0)),
    ]
    head_shape = jax.ShapeDtypeStruct((nb, N_HEADS, seq, HEAD_PAD), jnp.bfloat16)
    head_spec = pl.BlockSpec((None, N_HEADS, tm, HEAD_PAD), lambda b, j: (b, 0, j, 0))
    br_shape = jax.ShapeDtypeStruct((nb, seq, BRANCH_DIM), jnp.bfloat16)
    br_spec = pl.BlockSpec((None, tm, BRANCH_DIM), lambda b, j: (b, j, 0))
    out_shape = [head_shape] * 3 + [br_shape] * 3
    out_specs = [head_spec] * 3 + [br_spec] * 3
    if emit_latent:
        out_shape.append(jax.ShapeDtypeStruct((nb, MLA_LATENT, seq), jnp.float32))
        out_specs.append(pl.BlockSpec((None, MLA_LATENT, tm), lambda b, j: (b, 0, j)))
    return pl.pallas_call(
        functools.partial(_front_kernel, seq, emit_latent),
        grid=(nb, nt),
        in_specs=in_specs,
        out_specs=out_specs,
        out_shape=out_shape,
        scratch_shapes=[pltpu.VMEM((tm + 2 * HALO, BRANCH_DIM), jnp.float32),
                        pltpu.VMEM((tm + 2 * HALO, BRANCH_DIM), jnp.float32)],
        compiler_params=pltpu.CompilerParams(vmem_limit_bytes=VMEM_LIMIT),
        name="front",
    )(x, x, x, mod_l, w["g_pre"], w["w_in_t"], w["conv_w"], w["conv_b"], w["g_q"], w["wuq"], w["g_kv"],
      w["wl2k"], w["wv"], w["pool_w"], w["pool_scale"], *tabs)


def _back_kernel(has_cache, x_ref, mod_ref, gpre_ref, wg_ref, q_ref, k_ref, v_ref, *rest):
    if has_cache:
        kc_ref, vc_ref, ya_ref, yc_ref, sbz_ref, wbr_ref, wo_ref, gpost_ref, out_ref = rest
    else:
        ya_ref, yc_ref, sbz_ref, wbr_ref, wo_ref, gpost_ref, out_ref = rest
    x = x_ref[...]
    mod_row = mod_ref[...]

    pairs = []
    for p in range(N_HEADS // 2):
        acc = None
        for h in (2 * p, 2 * p + 1):
            q = q_ref[h]
            s = _dot_nt(q, k_ref[h])
            m = jnp.max(s, axis=-1, keepdims=True)
            if has_cache:
                sc = _dot_nt(q, kc_ref[h])
                m = jnp.maximum(m, jnp.max(sc, axis=-1, keepdims=True))
            e = jnp.exp(s - m)
            denom = jnp.sum(e, axis=-1, keepdims=True)
            o = _dot(e.astype(jnp.bfloat16), v_ref[h])
            if has_cache:
                ec = jnp.exp(sc - m)
                denom = denom + jnp.sum(ec, axis=-1, keepdims=True)
                o = o + _dot(ec.astype(jnp.bfloat16), vc_ref[h])
            o = o / denom
            acc = o if acc is None else acc + o
        pairs.append(acc)
    attn = jnp.concatenate(pairs, axis=1)
    yb = (sbz_ref[...].astype(jnp.float32) * attn).astype(jnp.bfloat16)

    hn = _modulated_norm(x, gpre_ref[...], mod_row).astype(jnp.bfloat16)
    merged = None
    for n, y in enumerate((ya_ref[...], yb, yc_ref[...])):
        r0 = _R_MERGE - _BACK_ROWS + n * D_MODEL
        gate = jax.nn.sigmoid(_dot_nt(hn, wg_ref[r0:r0 + D_MODEL, :]))
        term = gate * _dot(y, wbr_ref[n])
        merged = term if merged is None else merged + term
    out = _rms(_dot(merged.astype(jnp.bfloat16), wo_ref[...]), gpost_ref[...])
    out_ref[...] = x + mod_row[:, 2 * D_MODEL:3 * D_MODEL] * out


def _back(l, x, mod_l, cond_row, w, q, k, v, kc, vc, ya, yc, sbz):
    nb, seq, _ = x.shape
    tq = TOKEN_TILE
    has_cache = kc is not None

    def wspec(shape):
        return pl.BlockSpec((None,) + shape, lambda b, j: (l,) + (0,) * len(shape))

    tile_spec = pl.BlockSpec((None, tq, D_MODEL), lambda b, j: (b, j, 0))
    br_spec = pl.BlockSpec((None, tq, BRANCH_DIM), lambda b, j: (b, j, 0))
    seq_spec = pl.BlockSpec((None, N_HEADS, seq, HEAD_PAD), lambda b, j: (b, 0, 0, 0))
    in_specs = [
        tile_spec,
        pl.BlockSpec((None, 1, 3 * D_MODEL), lambda b, j: (cond_row(b), 0, 0)),
        wspec((1, D_MODEL)),
        pl.BlockSpec((None, _BACK_ROWS, D_MODEL), lambda b, j: (l, 1, 0)),
        pl.BlockSpec((None, N_HEADS, tq, HEAD_PAD), lambda b, j: (b, 0, j, 0)),
        seq_spec, seq_spec,
    ]
    args = [x, mod_l, w["g_pre"], w["w_in_t"], q, k, v]
    if has_cache:
        past = kc.shape[3]
        cache_spec = pl.BlockSpec((None, None, N_HEADS, past, HEAD_PAD), lambda b, j: (l, b, 0, 0, 0))
        in_specs += [cache_spec, cache_spec]
        args += [kc, vc]
    in_specs += [br_spec, br_spec, br_spec,
                 wspec((3, BRANCH_DIM, D_MODEL)), wspec((D_MODEL, D_MODEL)), wspec((1, D_MODEL))]
    args += [ya, yc, sbz, w["w_branch"], w["w_o"], w["g_post"]]
    return pl.pallas_call(
        functools.partial(_back_kernel, has_cache),
        grid=(nb, seq // tq),
        in_specs=in_specs,
        out_specs=tile_spec,
        out_shape=jax.ShapeDtypeStruct(x.shape, jnp.float32),
        compiler_params=pltpu.CompilerParams(vmem_limit_bytes=VMEM_LIMIT),
        name="back",
    )(*args)


def _rope_swap_perm():
    quarter = QK_ROPE // 4
    idx = np.arange(QK_ROPE).reshape(2, 2, quarter)
    return idx[:, ::-1, :].reshape(-1)


def _rope_tables(n_tokens, rotate):
    if rotate:
        t = np.arange(n_tokens)
        pos = np.stack([t // GRID_W, t % GRID_W], axis=1).astype(np.float32)
        axis_dim = QK_ROPE // 2
        inv = (1.0 / (ROPE_THETA ** (jnp.arange(0, axis_dim, 2, dtype=jnp.float32) / axis_dim)))
        ang = jnp.asarray(pos)[:, :, None] * inv
        cos, sin = jnp.cos(ang), jnp.sin(ang)
        cfull = jnp.stack([cos, cos], axis=2).reshape(n_tokens, QK_ROPE)
        sfull = jnp.stack([-sin, sin], axis=2).reshape(n_tokens, QK_ROPE)
    else:
        cfull = jnp.ones((n_tokens, QK_ROPE), jnp.float32)
        sfull = jnp.zeros((n_tokens, QK_ROPE), jnp.float32)
    scale = (QK_NOPE + QK_ROPE) ** -0.5
    qtab = jnp.concatenate([jnp.full((n_tokens, QK_NOPE), scale, jnp.float32), cfull * scale, sfull * scale], axis=1)
    pad = jnp.zeros((n_tokens, LANES - QK_ROPE), jnp.float32)
    return qtab, jnp.concatenate([cfull, pad], axis=1), jnp.concatenate([sfull, pad], axis=1)


def _prepare_weights(g_pre, g_post, w_in, conv_w, conv_b, g_q, w_uq, g_kv, w_ukv, pool_w, pool_scale, w_branch, w_o):
    bf = jnp.bfloat16
    perm = _rope_swap_perm()
    assert w_in.shape == (DEPTH, D_MODEL, W_IN_COLS)
    w_in_t = _cast_w_in(jnp.swapaxes(w_in, 1, 2))

    uq = w_uq.reshape(DEPTH, Q_LORA, N_HEADS, QK_NOPE + QK_ROPE)
    rope_q = uq[..., QK_NOPE:]
    wuq = jnp.concatenate([uq, rope_q[..., perm]], axis=-1).reshape(DEPTH, Q_LORA, HP).astype(bf)

    ukv = w_ukv.reshape(DEPTH, KV_LORA, N_HEADS, QK_NOPE + V_HEAD)
    k_part = jnp.concatenate([ukv[..., :QK_NOPE], jnp.zeros_like(ukv[..., :HEAD_PAD - QK_NOPE])], axis=-1)
    place = np.zeros((LANES, N_HEADS, HEAD_PAD), np.float32)
    for d in range(QK_ROPE):
        place[d, :, QK_NOPE + d] = 1.0
        place[d, :, QK_NOPE + QK_ROPE + d] = 1.0
    place = jnp.broadcast_to(jnp.asarray(place), (DEPTH, LANES, N_HEADS, HEAD_PAD))
    wl2k = jnp.concatenate([k_part, place], axis=1).reshape(DEPTH, _LAT_EXT, HP).astype(bf)

    v_w = ukv[..., QK_NOPE:]
    zeros_v = jnp.zeros_like(v_w)
    even = jnp.concatenate([v_w, zeros_v], axis=-1)
    odd = jnp.concatenate([zeros_v, v_w], axis=-1)
    is_odd = (np.arange(N_HEADS) % 2 == 1)[None, None, :, None]
    wv = jnp.where(is_odd, odd, even).reshape(DEPTH, KV_LORA, HP).astype(bf)

    return {
        "g_pre": g_pre.reshape(DEPTH, 1, D_MODEL), "g_post": g_post.reshape(DEPTH, 1, D_MODEL),
        "w_in_t": w_in_t, "conv_w": conv_w, "conv_b": conv_b.reshape(DEPTH, 1, BRANCH_DIM),
        "g_q": g_q.reshape(DEPTH, 1, Q_LORA), "wuq": wuq, "g_kv": g_kv.reshape(DEPTH, 1, KV_LORA),
        "wl2k": wl2k, "wv": wv, "pool_w": pool_w.astype(bf),
        "pool_scale": pool_scale.reshape(DEPTH, 1, BRANCH_DIM),
        "w_branch": w_branch.astype(bf), "w_o": w_o.astype(bf),
    }


def kernel(x_prompt, x_sample, cache_mla_latent, c, c_ctx, w_mod, b_mod, g_pre, g_post, w_in, conv_w, conv_b,
           g_q, w_uq, g_kv, w_ukv, pool_w, pool_scale, w_branch, w_o):
    n_dec = x_sample.shape[0]
    assert 1 + n_dec <= COND_ROWS
    assert x_prompt.shape[1] % TOKEN_TILE == 0 and x_sample.shape[1] % TOKEN_TILE == 0
    w = _prepare_weights(g_pre, g_post, w_in, conv_w, conv_b, g_q, w_uq, g_kv, w_ukv, pool_w, pool_scale,
                         w_branch, w_o)
    cond = jnp.concatenate([c_ctx[None, :], c, jnp.zeros((COND_ROWS - 1 - n_dec, D_MODEL), jnp.float32)], axis=0)
    mod = _modulation(cond, w_mod, b_mod).reshape(DEPTH, COND_ROWS, 1, 3 * D_MODEL)
    kc, vc = _cache_kv(jnp.swapaxes(cache_mla_latent, 2, 3), w["wl2k"], w["wv"])

    ctx_row = lambda b: 0
    dec_row = lambda b: b + 1

    h = x_prompt
    tabs = _rope_tables(x_prompt.shape[1], rotate=False)
    latents = []
    for l in range(DEPTH):
        q, k, v, ya, yc, sbz, lat = _front(l, h, mod[l], ctx_row, w, tabs, True)
        h = _back(l, h, mod[l], ctx_row, w, q, k, v, None, None, ya, yc, sbz)
        latents.append(lat)
    state = jnp.swapaxes(jnp.stack(latents, axis=1), 2, 3)

    hs = x_sample
    tabs = _rope_tables(x_sample.shape[1], rotate=True)
    for l in range(DEPTH):
        q, k, v, ya, yc, sbz = _front(l, hs, mod[l], dec_row, w, tabs, False)
        hs = _back(l, hs, mod[l], dec_row, w, q, k, v, kc, vc, ya, yc, sbz)
    return (h, hs, state)
```

```python
import functools

import numpy as np
import jax
import jax.numpy as jnp
from jax import lax
from jax.experimental import pallas as pl
from jax.experimental.pallas import tpu as pltpu

D_MODEL = 1024
DEPTH = 2
GRID_W = 64
EPS = 1e-6
BRANCH_DIM = 512
N_HEADS = 8
QK_NOPE = 64
QK_ROPE = 32
V_HEAD = 64
Q_LORA = 384
KV_LORA = 256
MLA_LATENT = KV_LORA + QK_ROPE
ROPE_THETA = 10000.0
POOL_WINDOWS = (2, 4, 8, 16)
POOL_GROUP = 128

LANES = 128
HEAD_PAD = LANES
HP = N_HEADS * HEAD_PAD
N_PAIRS = N_HEADS * V_HEAD // LANES
TOKEN_TILE = 256
HALO = 16
COND_ROWS = 8
VMEM_LIMIT = 56 * 1024 * 1024

_R_AB = 0
_R_ACX = BRANCH_DIM
_R_MAIN = 3 * BRANCH_DIM
_MAIN_ROWS = BRANCH_DIM + Q_LORA + KV_LORA + LANES
_R_BZ = 4 * BRANCH_DIM + Q_LORA + MLA_LATENT
_R_CU = _R_BZ + BRANCH_DIM
_R_CZ = _R_CU + BRANCH_DIM
_R_MERGE = _R_CZ + BRANCH_DIM
W_IN_COLS = _R_MERGE + 3 * D_MODEL
_FRONT_ROWS = 4352
_BACK_ROWS = W_IN_COLS // 2
_QD0 = BRANCH_DIM
_CKV0 = _QD0 + Q_LORA
_KR0 = _CKV0 + KV_LORA
_LAT_EXT = KV_LORA + LANES
assert _FRONT_ROWS >= _R_MERGE and _BACK_ROWS <= _R_MERGE and _BACK_ROWS % 16 == 0


def _dot(a, b):
    return jnp.dot(a, b, preferred_element_type=jnp.float32)


def _dot_nt(a, b):
    return lax.dot_general(a, b, (((1,), (1,)), ((), ())), preferred_element_type=jnp.float32)


def _rms(x, g):
    return x * lax.rsqrt(jnp.mean(x * x, axis=-1, keepdims=True) + EPS) * g


def _silu(x):
    return x * jax.nn.sigmoid(x)


def _modulated_norm(x, g_pre, mod_row):
    shift = mod_row[:, 0:D_MODEL]
    scale = mod_row[:, D_MODEL:2 * D_MODEL]
    return _rms(x, g_pre) * (1.0 + scale) + shift


def _params(n_grid_dims=1):
    return pltpu.CompilerParams(dimension_semantics=("arbitrary",) * n_grid_dims, vmem_limit_bytes=VMEM_LIMIT)


def _mod_kernel(cond_ref, w_ref, b_ref, out_ref):
    h = _silu(cond_ref[...]).astype(jnp.bfloat16)
    out_ref[...] = _dot(h, w_ref[...].astype(jnp.bfloat16)) + b_ref[...]


def _modulation(cond, w_mod, b_mod):
    col_tile = D_MODEL
    return pl.pallas_call(
        _mod_kernel,
        grid=(DEPTH, 3 * D_MODEL // col_tile),
        in_specs=[
            pl.BlockSpec((COND_ROWS, D_MODEL), lambda l, n: (0, 0)),
            pl.BlockSpec((None, D_MODEL, col_tile), lambda l, n: (l, 0, n)),
            pl.BlockSpec((None, 1, col_tile), lambda l, n: (l, 0, n)),
        ],
        out_specs=pl.BlockSpec((None, COND_ROWS, col_tile), lambda l, n: (l, 0, n)),
        out_shape=jax.ShapeDtypeStruct((DEPTH, COND_ROWS, 3 * D_MODEL), jnp.float32),
        compiler_params=_params(2),
        name="modulation",
    )(cond, w_mod, b_mod.reshape(DEPTH, 1, 3 * D_MODEL))


def _cast_kernel(w_ref, out_ref):
    out_ref[...] = w_ref[...].astype(jnp.bfloat16)


def _cast_w_in(w_in_t):
    rows = 512
    spec = pl.BlockSpec((None, rows, D_MODEL), lambda l, r: (l, r, 0))
    return pl.pallas_call(
        _cast_kernel,
        grid=(DEPTH, pl.cdiv(W_IN_COLS, rows)),
        in_specs=[spec],
        out_specs=spec,
        out_shape=jax.ShapeDtypeStruct(w_in_t.shape, jnp.bfloat16),
        compiler_params=_params(2),
        name="cast_w_in",
    )(w_in_t)


def _cache_kv_kernel(lat_t_ref, wl2k_ref, wv_ref, k_ref, v_ref):
    lat_t = lat_t_ref[...]
    pad = jnp.zeros((_LAT_EXT - MLA_LATENT, lat_t.shape[1]), jnp.float32)
    lat = jnp.concatenate([lat_t, pad], axis=0).T.astype(jnp.bfloat16)
    k = _dot(lat, wl2k_ref[...]).astype(jnp.bfloat16)
    v = _dot(lat[:, 0:KV_LORA], wv_ref[...]).astype(jnp.bfloat16)
    for h in range(N_HEADS):
        k_ref[h] = k[:, h * HEAD_PAD:(h + 1) * HEAD_PAD]
    for p in range(N_PAIRS):
        v_ref[p] = v[:, p * LANES:(p + 1) * LANES]


def _cache_kv(cache_t, wl2k, wv):
    nb, _, _, past = cache_t.shape

    def kv(n):
        return (jax.ShapeDtypeStruct((DEPTH, nb, n, past, LANES), jnp.bfloat16),
                pl.BlockSpec((None, None, n, past, LANES), lambda l, b: (l, b, 0, 0, 0)))

    (k_shape, k_spec), (v_shape, v_spec) = kv(N_HEADS), kv(N_PAIRS)
    return pl.pallas_call(
        _cache_kv_kernel,
        grid=(DEPTH, nb),
        in_specs=[
            pl.BlockSpec((None, None, MLA_LATENT, past), lambda l, b: (b, l, 0, 0)),
            pl.BlockSpec((None, _LAT_EXT, HP), lambda l, b: (l, 0, 0)),
            pl.BlockSpec((None, KV_LORA, N_HEADS * V_HEAD), lambda l, b: (l, 0, 0)),
        ],
        out_specs=[k_spec, v_spec],
        out_shape=[k_shape, v_shape],
        compiler_params=_params(2),
        name="cache_kv",
    )(cache_t, wl2k, wv)


class _Pass:
    def __init__(self, x, first_step, first_cond_row, shared_cond):
        self.nb, self.seq, _ = x.shape
        assert self.seq % TOKEN_TILE == 0
        self.nt = self.seq // TOKEN_TILE
        self.steps = self.nb * self.nt
        self.first = first_step
        self.first_cond_row = first_cond_row
        self.shared_cond = shared_cond

    def local(self, i):
        return jnp.clip(i - self.first, 0, self.steps - 1)

    def batch(self, i):
        return self.local(i) // self.nt

    def tile(self, i):
        return self.local(i) % self.nt

    def cond_row(self, i):
        return self.first_cond_row if self.shared_cond else self.first_cond_row + self.batch(i)


def _passes(x_prompt, x_sample):
    ctx = _Pass(x_prompt, 0, 0, True)
    dec = _Pass(x_sample, ctx.steps, 1, False)
    return ctx, dec


def _const_spec(l, shape):
    return pl.BlockSpec((None,) + shape, lambda i: (l,) + (0,) * len(shape), pipeline_mode=pl.Buffered(1))


_N_FRONT_WEIGHTS = 11


def _front_tile(seq_len, j, x_ref, xp_ref, xn_ref, qtab_ref, kcos_ref, ksin_ref, mod_row, weights, outs, scratch):
    gpre_ref, w_ref, convw_ref, convb_ref, gq_ref, wuq_ref, gkv_ref, wl2k_ref, wv_ref, poolw_ref, pscale_ref = weights
    q_out, k_out, v_out, ya_out, yc_out, sbz_out = outs[:6]
    u_scr, cu_scr = scratch
    tm = TOKEN_TILE
    last_j = seq_len // tm - 1

    x_ext = jnp.concatenate([xp_ref[...], x_ref[...], xn_ref[...]], axis=0)
    hn_ext = _modulated_norm(x_ext, gpre_ref[...], mod_row).astype(jnp.bfloat16)
    hn = hn_ext[HALO:HALO + tm]

    acx = _dot_nt(hn_ext, w_ref[_R_ACX:_R_ACX + 2 * BRANCH_DIM, :])
    row = lax.broadcasted_iota(jnp.int32, (tm + 2 * HALO, 1), 0)
    first_valid = jnp.where(j > 0, 0, HALO)
    end_valid = jnp.where(j < last_j, tm + 2 * HALO, tm + HALO)
    valid = jnp.logical_and(row >= first_valid, row < end_valid)
    u_scr[...] = jnp.where(valid, acx[:, 0:BRANCH_DIM] * acx[:, BRANCH_DIM:2 * BRANCH_DIM], 0.0)
    cu_scr[...] = jnp.where(valid, _dot_nt(hn_ext, w_ref[_R_CU:_R_CU + BRANCH_DIM, :]), 0.0)

    pm = _dot_nt(hn, w_ref[_R_MAIN:_R_MAIN + _MAIN_ROWS, :])
    conv = (u_scr[HALO - 1:HALO - 1 + tm, :] * convw_ref[0:1, :]
            + u_scr[HALO:HALO + tm, :] * convw_ref[1:2, :]
            + u_scr[HALO + 1:HALO + 1 + tm, :] * convw_ref[2:3, :]
            + convb_ref[...])
    ya = _silu(pm[:, 0:BRANCH_DIM]) * (_dot_nt(hn, w_ref[_R_AB:_R_AB + BRANCH_DIM, :]) * conv)
    ya_out[...] = ya.astype(jnp.bfloat16)

    t = j * tm + lax.broadcasted_iota(jnp.int32, (tm, POOL_GROUP), 0)
    mixed = []
    for gi, win in enumerate(POOL_WINDOWS):
        half = win // 2
        cols = slice(gi * POOL_GROUP, (gi + 1) * POOL_GROUP)
        total = cu_scr[HALO - half:HALO - half + tm, cols]
        for k in range(-half + 1, half):
            total = total + cu_scr[HALO + k:HALO + k + tm, cols]
        count = jnp.minimum(t + half, seq_len) - jnp.maximum(t - half, 0)
        pooled = total / count.astype(jnp.float32) - cu_scr[HALO:HALO + tm, cols]
        mixed.append(_dot(pooled.astype(jnp.bfloat16), poolw_ref[gi]))
    mixed = jnp.concatenate(mixed, axis=1) * pscale_ref[...]
    c_z = _dot_nt(hn, w_ref[_R_CZ:_R_CZ + BRANCH_DIM, :])
    yc_out[...] = (_silu(c_z) * mixed).astype(jnp.bfloat16)

    qn = _rms(pm[:, _QD0:_QD0 + Q_LORA], gq_ref[...]).astype(jnp.bfloat16)
    q = _dot(qn, wuq_ref[...])
    qtab = qtab_ref[...]
    ckv = _rms(pm[:, _CKV0:_CKV0 + KV_LORA], gkv_ref[...])
    kr = pm[:, _KR0:_KR0 + LANES]
    if len(outs) > 6:
        lat_out = outs[6]
        lat_out[0:KV_LORA, :] = ckv.T
        lat_out[KV_LORA:MLA_LATENT, :] = kr.T[0:QK_ROPE, :]
    lane = lax.broadcasted_iota(jnp.int32, (tm, LANES), 1)
    quarter = QK_ROPE // 4
    partner = jnp.where(lane % (2 * quarter) < quarter,
                        pltpu.roll(kr, LANES - quarter, 1), pltpu.roll(kr, quarter, 1))
    kr_rot = jnp.where(lane < QK_ROPE, kr * kcos_ref[...] + partner * ksin_ref[...], 0.0)
    ext = jnp.concatenate([ckv, kr_rot], axis=1).astype(jnp.bfloat16)
    k = _dot(ext, wl2k_ref[...]).astype(jnp.bfloat16)
    v = _dot(ext[:, 0:KV_LORA], wv_ref[...]).astype(jnp.bfloat16)
    for h in range(N_HEADS):
        cols = slice(h * HEAD_PAD, (h + 1) * HEAD_PAD)
        q_out[h] = (q[:, cols] * qtab).astype(jnp.bfloat16)
        k_out[h] = k[:, cols]
    for p in range(N_PAIRS):
        v_out[p] = v[:, p * LANES:(p + 1) * LANES]
    sbz_out[...] = _silu(_dot_nt(hn, w_ref[_R_BZ:_R_BZ + BRANCH_DIM, :])).astype(jnp.bfloat16)


def _front_kernel(ctx, dec, *refs):
    n_in = 6
    ctx_in, dec_in = refs[0:n_in], refs[n_in:2 * n_in]
    mod_ref = refs[2 * n_in]
    weights = refs[2 * n_in + 1:2 * n_in + 1 + _N_FRONT_WEIGHTS]
    outs = refs[2 * n_in + 1 + _N_FRONT_WEIGHTS:-2]
    ctx_out, dec_out = outs[0:7], outs[7:13]
    scratch = refs[-2:]
    i = pl.program_id(0)

    @pl.when(i < dec.first)
    def _():
        _front_tile(ctx.seq, ctx.tile(i), *ctx_in, mod_ref[...], weights, ctx_out, scratch)

    @pl.when(i >= dec.first)
    def _():
        _front_tile(dec.seq, dec.tile(i), *dec_in, mod_ref[...], weights, dec_out, scratch)


def _front(l, x_prompt, x_sample, mod_l, w, ctx_tabs, dec_tabs):
    ctx, dec = _passes(x_prompt, x_sample)
    tm = TOKEN_TILE
    hb = tm // HALO

    def pass_in_specs(p):
        n_hblk = p.seq // HALO
        tab = pl.BlockSpec((tm, LANES), lambda i: (p.tile(i), 0))
        return [
            pl.BlockSpec((None, tm, D_MODEL), lambda i: (p.batch(i), p.tile(i), 0)),
            pl.BlockSpec((None, HALO, D_MODEL), lambda i: (p.batch(i), jnp.maximum(p.tile(i) * hb - 1, 0), 0)),
            pl.BlockSpec((None, HALO, D_MODEL),
                         lambda i: (p.batch(i), jnp.minimum((p.tile(i) + 1) * hb, n_hblk - 1), 0)),
            tab, tab, tab,
        ]

    def pass_out(p, with_latent):
        def blocks(n):
            return (jax.ShapeDtypeStruct((p.nb, n, p.seq, LANES), jnp.bfloat16),
                    pl.BlockSpec((None, n, tm, LANES), lambda i: (p.batch(i), 0, p.tile(i), 0)))

        (head_shape, head_spec), (pair_shape, pair_spec) = blocks(N_HEADS), blocks(N_PAIRS)
        br_shape = jax.ShapeDtypeStruct((p.nb, p.seq, BRANCH_DIM), jnp.bfloat16)
        br_spec = pl.BlockSpec((None, tm, BRANCH_DIM), lambda i: (p.batch(i), p.tile(i), 0))
        shapes = [head_shape, head_shape, pair_shape] + [br_shape] * 3
        specs = [head_spec, head_spec, pair_spec] + [br_spec] * 3
        if with_latent:
            shapes.append(jax.ShapeDtypeStruct((p.nb, MLA_LATENT, p.seq), jnp.float32))
            specs.append(pl.BlockSpec((None, MLA_LATENT, tm), lambda i: (p.batch(i), 0, p.tile(i))))
        return shapes, specs

    def cond_row(i):
        return jnp.where(i < dec.first, ctx.cond_row(i), dec.cond_row(i))

    weight_specs = [
        _const_spec(l, (1, D_MODEL)),
        _const_spec(l, (_FRONT_ROWS, D_MODEL)),
        _const_spec(l, (3, BRANCH_DIM)),
        _const_spec(l, (1, BRANCH_DIM)),
        _const_spec(l, (1, Q_LORA)),
        _const_spec(l, (Q_LORA, HP)),
        _const_spec(l, (1, KV_LORA)),
        _const_spec(l, (_LAT_EXT, HP)),
        _const_spec(l, (KV_LORA, N_HEADS * V_HEAD)),
        _const_spec(l, (len(POOL_WINDOWS), POOL_GROUP, POOL_GROUP)),
        _const_spec(l, (1, BRANCH_DIM)),
    ]
    assert len(weight_specs) == _N_FRONT_WEIGHTS
    ctx_shapes, ctx_specs = pass_out(ctx, True)
    dec_shapes, dec_specs = pass_out(dec, False)
    outs = pl.pallas_call(
        functools.partial(_front_kernel, ctx, dec),
        grid=(ctx.steps + dec.steps,),
        in_specs=(pass_in_specs(ctx) + pass_in_specs(dec)
                  + [pl.BlockSpec((None, 1, 3 * D_MODEL), lambda i: (cond_row(i), 0, 0))] + weight_specs),
        out_specs=ctx_specs + dec_specs,
        out_shape=ctx_shapes + dec_shapes,
        scratch_shapes=[pltpu.VMEM((tm + 2 * HALO, BRANCH_DIM), jnp.float32),
                        pltpu.VMEM((tm + 2 * HALO, BRANCH_DIM), jnp.float32)],
        compiler_params=_params(),
        name="front",
    )(x_prompt, x_prompt, x_prompt, *ctx_tabs, x_sample, x_sample, x_sample, *dec_tabs, mod_l,
      w["g_pre"], w["w_in_t"], w["conv_w"], w["conv_b"], w["g_q"], w["wuq"], w["g_kv"],
      w["wl2k"], w["wv"], w["pool_w"], w["pool_scale"])
    return outs[:7], outs[7:]


_N_BACK_WEIGHTS = 5


def _back_tile(x_ref, q_ref, k_ref, v_ref, cache, ya_ref, yc_ref, sbz_ref, mod_row, weights, out_ref):
    gpre_ref, wg_ref, wbr_ref, wo_ref, gpost_ref = weights
    x = x_ref[...]

    low_half = lax.broadcasted_iota(jnp.int32, (x.shape[0], LANES), 1) < V_HEAD
    pairs = []
    for p in range(N_PAIRS):
        halves = []
        for h in (2 * p, 2 * p + 1):
            q = q_ref[h]
            s = _dot_nt(q, k_ref[h])
            m = jnp.max(s, axis=-1, keepdims=True)
            if cache is not None:
                kc_ref, vc_ref = cache
                sc = _dot_nt(q, kc_ref[h])
                m = jnp.maximum(m, jnp.max(sc, axis=-1, keepdims=True))
            e = jnp.exp(s - m)
            denom = jnp.sum(e, axis=-1, keepdims=True)
            o = _dot(e.astype(jnp.bfloat16), v_ref[p])
            if cache is not None:
                ec = jnp.exp(sc - m)
                denom = denom + jnp.sum(ec, axis=-1, keepdims=True)
                o = o + _dot(ec.astype(jnp.bfloat16), vc_ref[p])
            halves.append(o / denom)
        pairs.append(jnp.where(low_half, halves[0], halves[1]))
    attn = jnp.concatenate(pairs, axis=1)
    yb = (sbz_ref[...].astype(jnp.float32) * attn).astype(jnp.bfloat16)

    hn = _modulated_norm(x, gpre_ref[...], mod_row).astype(jnp.bfloat16)
    merged = None
    for n, y in enumerate((ya_ref[...], yb, yc_ref[...])):
        r0 = _R_MERGE - _BACK_ROWS + n * D_MODEL
        gate = jax.nn.sigmoid(_dot_nt(hn, wg_ref[r0:r0 + D_MODEL, :]))
        term = gate * _dot(y, wbr_ref[n])
        merged = term if merged is None else merged + term
    out = _rms(_dot(merged.astype(jnp.bfloat16), wo_ref[...]), gpost_ref[...])
    out_ref[...] = x + mod_row[:, 2 * D_MODEL:3 * D_MODEL] * out


def _back_kernel(dec_first, *refs):
    x_c, q_c, k_c, v_c, ya_c, yc_c, sbz_c = refs[0:7]
    x_d, q_d, k_d, v_d, kc_ref, vc_ref, ya_d, yc_d, sbz_d = refs[7:16]
    mod_ref = refs[16]
    weights = refs[17:17 + _N_BACK_WEIGHTS]
    out_c, out_d = refs[17 + _N_BACK_WEIGHTS:]
    i = pl.program_id(0)

    @pl.when(i < dec_first)
    def _():
        _back_tile(x_c, q_c, k_c, v_c, None, ya_c, yc_c, sbz_c, mod_ref[...], weights, out_c)

    @pl.when(i >= dec_first)
    def _():
        _back_tile(x_d, q_d, k_d, v_d, (kc_ref, vc_ref), ya_d, yc_d, sbz_d, mod_ref[...], weights, out_d)


def _back(l, x_prompt, x_sample, mod_l, w, ctx_fr, dec_fr, kc, vc):
    ctx, dec = _passes(x_prompt, x_sample)
    tq = TOKEN_TILE
    past = kc.shape[3]

    def pass_specs(p):
        tile = pl.BlockSpec((None, tq, D_MODEL), lambda i: (p.batch(i), p.tile(i), 0))
        q = pl.BlockSpec((None, N_HEADS, tq, HEAD_PAD), lambda i: (p.batch(i), 0, p.tile(i), 0))
        k = pl.BlockSpec((None, N_HEADS, p.seq, LANES), lambda i: (p.batch(i), 0, 0, 0))
        v = pl.BlockSpec((None, N_PAIRS, p.seq, LANES), lambda i: (p.batch(i), 0, 0, 0))
        br = pl.BlockSpec((None, tq, BRANCH_DIM), lambda i: (p.batch(i), p.tile(i), 0))
        return tile, q, k, v, br

    def cond_row(i):
        return jnp.where(i < dec.first, ctx.cond_row(i), dec.cond_row(i))

    tile_c, q_c, k_c, v_c, br_c = pass_specs(ctx)
    tile_d, q_d, k_d, v_d, br_d = pass_specs(dec)
    kc_spec = pl.BlockSpec((None, None, N_HEADS, past, LANES), lambda i: (l, dec.batch(i), 0, 0, 0))
    vc_spec = pl.BlockSpec((None, None, N_PAIRS, past, LANES), lambda i: (l, dec.batch(i), 0, 0, 0))
    in_specs = ([tile_c, q_c, k_c, v_c, br_c, br_c, br_c]
                + [tile_d, q_d, k_d, v_d, kc_spec, vc_spec, br_d, br_d, br_d]
                + [pl.BlockSpec((None, 1, 3 * D_MODEL), lambda i: (cond_row(i), 0, 0)),
                   _const_spec(l, (1, D_MODEL)),
                   pl.BlockSpec((None, _BACK_ROWS, D_MODEL), lambda i: (l, 1, 0),
                                pipeline_mode=pl.Buffered(1)),
                   _const_spec(l, (3, BRANCH_DIM, D_MODEL)),
                   _const_spec(l, (D_MODEL, D_MODEL)),
                   _const_spec(l, (1, D_MODEL))])
    q_cx, k_cx, v_cx, ya_cx, yc_cx, sbz_cx = ctx_fr
    q_dx, k_dx, v_dx, ya_dx, yc_dx, sbz_dx = dec_fr
    return pl.pallas_call(
        functools.partial(_back_kernel, dec.first),
        grid=(ctx.steps + dec.steps,),
        in_specs=in_specs,
        out_specs=[tile_c, tile_d],
        out_shape=[jax.ShapeDtypeStruct(x_prompt.shape, jnp.float32),
                   jax.ShapeDtypeStruct(x_sample.shape, jnp.float32)],
        compiler_params=_params(),
        name="back",
    )(x_prompt, q_cx, k_cx, v_cx, ya_cx, yc_cx, sbz_cx,
      x_sample, q_dx, k_dx, v_dx, kc, vc, ya_dx, yc_dx, sbz_dx,
      mod_l, w["g_pre"], w["w_in_t"], w["w_branch"], w["w_o"], w["g_post"])


def _rope_swap_perm():
    quarter = QK_ROPE // 4
    idx = np.arange(QK_ROPE).reshape(2, 2, quarter)
    return idx[:, ::-1, :].reshape(-1)


def _rope_tables(n_tokens, rotate):
    if rotate:
        t = np.arange(n_tokens)
        pos = np.stack([t // GRID_W, t % GRID_W], axis=1).astype(np.float32)
        axis_dim = QK_ROPE // 2
        inv = (1.0 / (ROPE_THETA ** (jnp.arange(0, axis_dim, 2, dtype=jnp.float32) / axis_dim)))
        ang = jnp.asarray(pos)[:, :, None] * inv
        cos, sin = jnp.cos(ang), jnp.sin(ang)
        cfull = jnp.stack([cos, cos], axis=2).reshape(n_tokens, QK_ROPE)
        sfull = jnp.stack([-sin, sin], axis=2).reshape(n_tokens, QK_ROPE)
    else:
        cfull = jnp.ones((n_tokens, QK_ROPE), jnp.float32)
        sfull = jnp.zeros((n_tokens, QK_ROPE), jnp.float32)
    scale = (QK_NOPE + QK_ROPE) ** -0.5
    qtab = jnp.concatenate([jnp.full((n_tokens, QK_NOPE), scale, jnp.float32), cfull * scale, sfull * scale], axis=1)
    pad = jnp.zeros((n_tokens, LANES - QK_ROPE), jnp.float32)
    return qtab, jnp.concatenate([cfull, pad], axis=1), jnp.concatenate([sfull, pad], axis=1)


def _prepare_weights(g_pre, g_post, w_in, conv_w, conv_b, g_q, w_uq, g_kv, w_ukv, pool_w, pool_scale, w_branch, w_o):
    bf = jnp.bfloat16
    perm = _rope_swap_perm()
    assert w_in.shape == (DEPTH, D_MODEL, W_IN_COLS)
    w_in_t = _cast_w_in(jnp.swapaxes(w_in, 1, 2))

    uq = w_uq.reshape(DEPTH, Q_LORA, N_HEADS, QK_NOPE + QK_ROPE)
    rope_q = uq[..., QK_NOPE:]
    wuq = jnp.concatenate([uq, rope_q[..., perm]], axis=-1).reshape(DEPTH, Q_LORA, HP).astype(bf)

    ukv = w_ukv.reshape(DEPTH, KV_LORA, N_HEADS, QK_NOPE + V_HEAD)
    k_part = jnp.concatenate([ukv[..., :QK_NOPE], jnp.zeros_like(ukv[..., :HEAD_PAD - QK_NOPE])], axis=-1)
    place = np.zeros((LANES, N_HEADS, HEAD_PAD), np.float32)
    for d in range(QK_ROPE):
        place[d, :, QK_NOPE + d] = 1.0
        place[d, :, QK_NOPE + QK_ROPE + d] = 1.0
    place = jnp.broadcast_to(jnp.asarray(place), (DEPTH, LANES, N_HEADS, HEAD_PAD))
    wl2k = jnp.concatenate([k_part, place], axis=1).reshape(DEPTH, _LAT_EXT, HP).astype(bf)

    wv = ukv[..., QK_NOPE:].reshape(DEPTH, KV_LORA, N_HEADS * V_HEAD).astype(bf)

    return {
        "g_pre": g_pre.reshape(DEPTH, 1, D_MODEL), "g_post": g_post.reshape(DEPTH, 1, D_MODEL),
        "w_in_t": w_in_t, "conv_w": conv_w, "conv_b": conv_b.reshape(DEPTH, 1, BRANCH_DIM),
        "g_q": g_q.reshape(DEPTH, 1, Q_LORA), "wuq": wuq, "g_kv": g_kv.reshape(DEPTH, 1, KV_LORA),
        "wl2k": wl2k, "wv": wv, "pool_w": pool_w.astype(bf),
        "pool_scale": pool_scale.reshape(DEPTH, 1, BRANCH_DIM),
        "w_branch": w_branch.astype(bf), "w_o": w_o.astype(bf),
    }


def kernel(x_prompt, x_sample, cache_mla_latent, c, c_ctx, w_mod, b_mod, g_pre, g_post, w_in, conv_w, conv_b,
           g_q, w_uq, g_kv, w_ukv, pool_w, pool_scale, w_branch, w_o):
    n_dec = x_sample.shape[0]
    assert 1 + n_dec <= COND_ROWS
    w = _prepare_weights(g_pre, g_post, w_in, conv_w, conv_b, g_q, w_uq, g_kv, w_ukv, pool_w, pool_scale,
                         w_branch, w_o)
    cond = jnp.concatenate([c_ctx[None, :], c, jnp.zeros((COND_ROWS - 1 - n_dec, D_MODEL), jnp.float32)], axis=0)
    mod = _modulation(cond, w_mod, b_mod).reshape(DEPTH, COND_ROWS, 1, 3 * D_MODEL)
    kc, vc = _cache_kv(jnp.swapaxes(cache_mla_latent, 2, 3), w["wl2k"], w["wv"])
    ctx_tabs = _rope_tables(x_prompt.shape[1], rotate=False)
    dec_tabs = _rope_tables(x_sample.shape[1], rotate=True)

    h, hs = x_prompt, x_sample
    latents = []
    for l in range(DEPTH):
        ctx_fr, dec_fr = _front(l, h, hs, mod[l], w, ctx_tabs, dec_tabs)
        latents.append(ctx_fr[6])
        h, hs = _back(l, h, hs, mod[l], w, ctx_fr[:6], dec_fr, kc, vc)
    state = jnp.swapaxes(jnp.stack(latents, axis=1), 2, 3)
    return (h, hs, state)
```

```python
import functools

import numpy as np
import jax
import jax.numpy as jnp
from jax import lax
from jax.experimental import pallas as pl
from jax.experimental.pallas import tpu as pltpu

D_MODEL = 1024
DEPTH = 2
GRID_W = 64
EPS = 1e-6
BRANCH_DIM = 512
N_HEADS = 8
QK_NOPE = 64
QK_ROPE = 32
V_HEAD = 64
Q_LORA = 384
KV_LORA = 256
MLA_LATENT = KV_LORA + QK_ROPE
ROPE_THETA = 10000.0
POOL_WINDOWS = (2, 4, 8, 16)
POOL_GROUP = 128

LANES = 128
HEAD_PAD = LANES
HP = N_HEADS * HEAD_PAD
N_PAIRS = N_HEADS * V_HEAD // LANES
FRONT_TILE = 512
BACK_TILE = 256
HALO = 16
COND_ROWS = 8
VMEM_LIMIT = 56 * 1024 * 1024

_R_AB = 0
_R_ACX = BRANCH_DIM
_R_MAIN = 3 * BRANCH_DIM
_MAIN_ROWS = BRANCH_DIM + Q_LORA + KV_LORA + LANES
_R_BZ = 4 * BRANCH_DIM + Q_LORA + MLA_LATENT
_R_CU = _R_BZ + BRANCH_DIM
_R_CZ = _R_CU + BRANCH_DIM
_R_MERGE = _R_CZ + BRANCH_DIM
W_IN_COLS = _R_MERGE + 3 * D_MODEL
_FRONT_ROWS = 4352
_BACK_ROWS = W_IN_COLS // 2
_QD0 = BRANCH_DIM
_CKV0 = _QD0 + Q_LORA
_KR0 = _CKV0 + KV_LORA
_LAT_EXT = KV_LORA + LANES
assert _FRONT_ROWS >= _R_MERGE and _BACK_ROWS <= _R_MERGE and _BACK_ROWS % 16 == 0


def _dot(a, b):
    return jnp.dot(a, b, preferred_element_type=jnp.float32)


def _dot_nt(a, b):
    return lax.dot_general(a, b, (((1,), (1,)), ((), ())), preferred_element_type=jnp.float32)


def _rms(x, g):
    return x * lax.rsqrt(jnp.mean(x * x, axis=-1, keepdims=True) + EPS) * g


def _silu(x):
    return x * jax.nn.sigmoid(x)


def _modulated_norm(x, g_pre, mod_row):
    shift = mod_row[:, 0:D_MODEL]
    scale = mod_row[:, D_MODEL:2 * D_MODEL]
    return _rms(x, g_pre) * (1.0 + scale) + shift


def _params(n_grid_dims=1):
    return pltpu.CompilerParams(dimension_semantics=("arbitrary",) * n_grid_dims, vmem_limit_bytes=VMEM_LIMIT)


def _mod_kernel(cond_ref, w_ref, b_ref, out_ref):
    h = _silu(cond_ref[...]).astype(jnp.bfloat16)
    out_ref[...] = _dot(h, w_ref[...].astype(jnp.bfloat16)) + b_ref[...]


def _modulation(cond, w_mod, b_mod):
    col_tile = D_MODEL
    return pl.pallas_call(
        _mod_kernel,
        grid=(DEPTH, 3 * D_MODEL // col_tile),
        in_specs=[
            pl.BlockSpec((COND_ROWS, D_MODEL), lambda l, n: (0, 0)),
            pl.BlockSpec((None, D_MODEL, col_tile), lambda l, n: (l, 0, n)),
            pl.BlockSpec((None, 1, col_tile), lambda l, n: (l, 0, n)),
        ],
        out_specs=pl.BlockSpec((None, COND_ROWS, col_tile), lambda l, n: (l, 0, n)),
        out_shape=jax.ShapeDtypeStruct((DEPTH, COND_ROWS, 3 * D_MODEL), jnp.float32),
        compiler_params=_params(2),
        name="modulation",
    )(cond, w_mod, b_mod.reshape(DEPTH, 1, 3 * D_MODEL))


def _cast_kernel(w_ref, out_ref):
    out_ref[...] = w_ref[...].astype(jnp.bfloat16)


def _cast_w_in(w_in_t):
    rows = 512
    spec = pl.BlockSpec((None, rows, D_MODEL), lambda l, r: (l, r, 0))
    return pl.pallas_call(
        _cast_kernel,
        grid=(DEPTH, pl.cdiv(W_IN_COLS, rows)),
        in_specs=[spec],
        out_specs=spec,
        out_shape=jax.ShapeDtypeStruct(w_in_t.shape, jnp.bfloat16),
        compiler_params=_params(2),
        name="cast_w_in",
    )(w_in_t)


def _cache_kv_kernel(lat_t_ref, wl2k_ref, wv_ref, k_ref, v_ref):
    lat_t = lat_t_ref[...]
    pad = jnp.zeros((_LAT_EXT - MLA_LATENT, lat_t.shape[1]), jnp.float32)
    lat = jnp.concatenate([lat_t, pad], axis=0).T.astype(jnp.bfloat16)
    k = _dot(lat, wl2k_ref[...]).astype(jnp.bfloat16)
    v = _dot(lat[:, 0:KV_LORA], wv_ref[...]).astype(jnp.bfloat16)
    for h in range(N_HEADS):
        k_ref[h] = k[:, h * HEAD_PAD:(h + 1) * HEAD_PAD]
    for p in range(N_PAIRS):
        v_ref[p] = v[:, p * LANES:(p + 1) * LANES]


def _cache_kv(cache_t, wl2k, wv):
    nb, _, _, past = cache_t.shape

    def kv(n):
        return (jax.ShapeDtypeStruct((DEPTH, nb, n, past, LANES), jnp.bfloat16),
                pl.BlockSpec((None, None, n, past, LANES), lambda l, b: (l, b, 0, 0, 0)))

    (k_shape, k_spec), (v_shape, v_spec) = kv(N_HEADS), kv(N_PAIRS)
    return pl.pallas_call(
        _cache_kv_kernel,
        grid=(DEPTH, nb),
        in_specs=[
            pl.BlockSpec((None, None, MLA_LATENT, past), lambda l, b: (b, l, 0, 0)),
            pl.BlockSpec((None, _LAT_EXT, HP), lambda l, b: (l, 0, 0)),
            pl.BlockSpec((None, KV_LORA, N_HEADS * V_HEAD), lambda l, b: (l, 0, 0)),
        ],
        out_specs=[k_spec, v_spec],
        out_shape=[k_shape, v_shape],
        compiler_params=_params(2),
        name="cache_kv",
    )(cache_t, wl2k, wv)


class _Pass:
    def __init__(self, x, tile, first_step, first_cond_row, shared_cond):
        self.nb, self.seq, _ = x.shape
        self.tm = min(tile, self.seq)
        assert self.seq % self.tm == 0
        self.nt = self.seq // self.tm
        self.steps = self.nb * self.nt
        self.first = first_step
        self.first_cond_row = first_cond_row
        self.shared_cond = shared_cond

    def local(self, i):
        return jnp.clip(i - self.first, 0, self.steps - 1)

    def batch(self, i):
        return self.local(i) // self.nt

    def tile(self, i):
        return self.local(i) % self.nt

    def cond_row(self, i):
        return self.first_cond_row if self.shared_cond else self.first_cond_row + self.batch(i)


def _passes(x_prompt, x_sample, tile):
    ctx = _Pass(x_prompt, tile, 0, 0, True)
    dec = _Pass(x_sample, tile, ctx.steps, 1, False)
    return ctx, dec


def _const_spec(l, shape):
    return pl.BlockSpec((None,) + shape, lambda i: (l,) + (0,) * len(shape), pipeline_mode=pl.Buffered(1))


_N_FRONT_WEIGHTS = 11


def _front_tile(seq_len, j, x_ref, xp_ref, xn_ref, qtab_ref, kcos_ref, ksin_ref, mod_row, weights, outs, scratch):
    gpre_ref, w_ref, convw_ref, convb_ref, gq_ref, wuq_ref, gkv_ref, wl2k_ref, wv_ref, poolw_ref, pscale_ref = weights
    q_out, k_out, v_out, ya_out, yc_out, sbz_out = outs[:6]
    u_scr, cu_scr = scratch
    tm = x_ref.shape[0]
    ext_rows = slice(0, tm + 2 * HALO)
    last_j = seq_len // tm - 1

    x_ext = jnp.concatenate([xp_ref[...], x_ref[...], xn_ref[...]], axis=0)
    hn_ext = _modulated_norm(x_ext, gpre_ref[...], mod_row).astype(jnp.bfloat16)
    hn = hn_ext[HALO:HALO + tm]

    acx = _dot_nt(hn_ext, w_ref[_R_ACX:_R_ACX + 2 * BRANCH_DIM, :])
    row = lax.broadcasted_iota(jnp.int32, (tm + 2 * HALO, 1), 0)
    first_valid = jnp.where(j > 0, 0, HALO)
    end_valid = jnp.where(j < last_j, tm + 2 * HALO, tm + HALO)
    valid = jnp.logical_and(row >= first_valid, row < end_valid)
    u_scr[ext_rows, :] = jnp.where(valid, acx[:, 0:BRANCH_DIM] * acx[:, BRANCH_DIM:2 * BRANCH_DIM], 0.0)
    cu_scr[ext_rows, :] = jnp.where(valid, _dot_nt(hn_ext, w_ref[_R_CU:_R_CU + BRANCH_DIM, :]), 0.0)

    pm = _dot_nt(hn, w_ref[_R_MAIN:_R_MAIN + _MAIN_ROWS, :])
    conv = (u_scr[HALO - 1:HALO - 1 + tm, :] * convw_ref[0:1, :]
            + u_scr[HALO:HALO + tm, :] * convw_ref[1:2, :]
            + u_scr[HALO + 1:HALO + 1 + tm, :] * convw_ref[2:3, :]
            + convb_ref[...])
    ya = _silu(pm[:, 0:BRANCH_DIM]) * (_dot_nt(hn, w_ref[_R_AB:_R_AB + BRANCH_DIM, :]) * conv)
    ya_out[...] = ya.astype(jnp.bfloat16)

    t = j * tm + lax.broadcasted_iota(jnp.int32, (tm, POOL_GROUP), 0)
    mixed = []
    for gi, win in enumerate(POOL_WINDOWS):
        half = win // 2
        cols = slice(gi * POOL_GROUP, (gi + 1) * POOL_GROUP)
        total = cu_scr[HALO - half:HALO - half + tm, cols]
        for k in range(-half + 1, half):
            total = total + cu_scr[HALO + k:HALO + k + tm, cols]
        count = jnp.minimum(t + half, seq_len) - jnp.maximum(t - half, 0)
        pooled = total / count.astype(jnp.float32) - cu_scr[HALO:HALO + tm, cols]
        mixed.append(_dot(pooled.astype(jnp.bfloat16), poolw_ref[gi]))
    mixed = jnp.concatenate(mixed, axis=1) * pscale_ref[...]
    c_z = _dot_nt(hn, w_ref[_R_CZ:_R_CZ + BRANCH_DIM, :])
    yc_out[...] = (_silu(c_z) * mixed).astype(jnp.bfloat16)

    qn = _rms(pm[:, _QD0:_QD0 + Q_LORA], gq_ref[...]).astype(jnp.bfloat16)
    q = _dot(qn, wuq_ref[...])
    qtab = qtab_ref[...]
    ckv = _rms(pm[:, _CKV0:_CKV0 + KV_LORA], gkv_ref[...])
    kr = pm[:, _KR0:_KR0 + LANES]
    if len(outs) > 6:
        lat_out = outs[6]
        lat_out[0:KV_LORA, :] = ckv.T
        lat_out[KV_LORA:MLA_LATENT, :] = kr.T[0:QK_ROPE, :]
    lane = lax.broadcasted_iota(jnp.int32, (tm, LANES), 1)
    quarter = QK_ROPE // 4
    partner = jnp.where(lane % (2 * quarter) < quarter,
                        pltpu.roll(kr, LANES - quarter, 1), pltpu.roll(kr, quarter, 1))
    kr_rot = jnp.where(lane < QK_ROPE, kr * kcos_ref[...] + partner * ksin_ref[...], 0.0)
    ext = jnp.concatenate([ckv, kr_rot], axis=1).astype(jnp.bfloat16)
    k = _dot(ext, wl2k_ref[...]).astype(jnp.bfloat16)
    v = _dot(ext[:, 0:KV_LORA], wv_ref[...]).astype(jnp.bfloat16)
    for h in range(N_HEADS):
        cols = slice(h * HEAD_PAD, (h + 1) * HEAD_PAD)
        q_out[h] = (q[:, cols] * qtab).astype(jnp.bfloat16)
        k_out[h] = k[:, cols]
    for p in range(N_PAIRS):
        v_out[p] = v[:, p * LANES:(p + 1) * LANES]
    sbz_out[...] = _silu(_dot_nt(hn, w_ref[_R_BZ:_R_BZ + BRANCH_DIM, :])).astype(jnp.bfloat16)


def _front_kernel(ctx, dec, *refs):
    n_in = 6
    ctx_in, dec_in = refs[0:n_in], refs[n_in:2 * n_in]
    mod_ref = refs[2 * n_in]
    weights = refs[2 * n_in + 1:2 * n_in + 1 + _N_FRONT_WEIGHTS]
    outs = refs[2 * n_in + 1 + _N_FRONT_WEIGHTS:-2]
    ctx_out, dec_out = outs[0:7], outs[7:13]
    scratch = refs[-2:]
    i = pl.program_id(0)

    @pl.when(i < dec.first)
    def _():
        _front_tile(ctx.seq, ctx.tile(i), *ctx_in, mod_ref[...], weights, ctx_out, scratch)

    @pl.when(i >= dec.first)
    def _():
        _front_tile(dec.seq, dec.tile(i), *dec_in, mod_ref[...], weights, dec_out, scratch)


def _front(l, x_prompt, x_sample, mod_l, w, ctx_tabs, dec_tabs):
    ctx, dec = _passes(x_prompt, x_sample, FRONT_TILE)

    def pass_in_specs(p):
        tm = p.tm
        hb = tm // HALO
        n_hblk = p.seq // HALO
        tab = pl.BlockSpec((tm, LANES), lambda i: (p.tile(i), 0))
        return [
            pl.BlockSpec((None, tm, D_MODEL), lambda i: (p.batch(i), p.tile(i), 0)),
            pl.BlockSpec((None, HALO, D_MODEL), lambda i: (p.batch(i), jnp.maximum(p.tile(i) * hb - 1, 0), 0)),
            pl.BlockSpec((None, HALO, D_MODEL),
                         lambda i: (p.batch(i), jnp.minimum((p.tile(i) + 1) * hb, n_hblk - 1), 0)),
            tab, tab, tab,
        ]

    def pass_out(p, with_latent):
        tm = p.tm

        def blocks(n):
            return (jax.ShapeDtypeStruct((p.nb, n, p.seq, LANES), jnp.bfloat16),
                    pl.BlockSpec((None, n, tm, LANES), lambda i: (p.batch(i), 0, p.tile(i), 0)))

        (head_shape, head_spec), (pair_shape, pair_spec) = blocks(N_HEADS), blocks(N_PAIRS)
        br_shape = jax.ShapeDtypeStruct((p.nb, p.seq, BRANCH_DIM), jnp.bfloat16)
        br_spec = pl.BlockSpec((None, tm, BRANCH_DIM), lambda i: (p.batch(i), p.tile(i), 0))
        shapes = [head_shape, head_shape, pair_shape] + [br_shape] * 3
        specs = [head_spec, head_spec, pair_spec] + [br_spec] * 3
        if with_latent:
            shapes.append(jax.ShapeDtypeStruct((p.nb, MLA_LATENT, p.seq), jnp.float32))
            specs.append(pl.BlockSpec((None, MLA_LATENT, tm), lambda i: (p.batch(i), 0, p.tile(i))))
        return shapes, specs

    def cond_row(i):
        return jnp.where(i < dec.first, ctx.cond_row(i), dec.cond_row(i))

    weight_specs = [
        _const_spec(l, (1, D_MODEL)),
        _const_spec(l, (_FRONT_ROWS, D_MODEL)),
        _const_spec(l, (3, BRANCH_DIM)),
        _const_spec(l, (1, BRANCH_DIM)),
        _const_spec(l, (1, Q_LORA)),
        _const_spec(l, (Q_LORA, HP)),
        _const_spec(l, (1, KV_LORA)),
        _const_spec(l, (_LAT_EXT, HP)),
        _const_spec(l, (KV_LORA, N_HEADS * V_HEAD)),
        _const_spec(l, (len(POOL_WINDOWS), POOL_GROUP, POOL_GROUP)),
        _const_spec(l, (1, BRANCH_DIM)),
    ]
    assert len(weight_specs) == _N_FRONT_WEIGHTS
    ctx_shapes, ctx_specs = pass_out(ctx, True)
    dec_shapes, dec_specs = pass_out(dec, False)
    outs = pl.pallas_call(
        functools.partial(_front_kernel, ctx, dec),
        grid=(ctx.steps + dec.steps,),
        in_specs=(pass_in_specs(ctx) + pass_in_specs(dec)
                  + [pl.BlockSpec((None, 1, 3 * D_MODEL), lambda i: (cond_row(i), 0, 0))] + weight_specs),
        out_specs=ctx_specs + dec_specs,
        out_shape=ctx_shapes + dec_shapes,
        scratch_shapes=[pltpu.VMEM((max(ctx.tm, dec.tm) + 2 * HALO, BRANCH_DIM), jnp.float32)] * 2,
        compiler_params=_params(),
        name="front",
    )(x_prompt, x_prompt, x_prompt, *ctx_tabs, x_sample, x_sample, x_sample, *dec_tabs, mod_l,
      w["g_pre"], w["w_in_t"], w["conv_w"], w["conv_b"], w["g_q"], w["wuq"], w["g_kv"],
      w["wl2k"], w["wv"], w["pool_w"], w["pool_scale"])
    return outs[:7], outs[7:]


_N_BACK_WEIGHTS = 5


def _back_tile(x_ref, q_ref, k_ref, v_ref, cache, ya_ref, yc_ref, sbz_ref, mod_row, weights, out_ref):
    gpre_ref, wg_ref, wbr_ref, wo_ref, gpost_ref = weights
    x = x_ref[...]

    low_half = lax.broadcasted_iota(jnp.int32, (x.shape[0], LANES), 1) < V_HEAD
    pairs = []
    for p in range(N_PAIRS):
        halves = []
        for h in (2 * p, 2 * p + 1):
            q = q_ref[h]
            s = _dot_nt(q, k_ref[h])
            m = jnp.max(s, axis=-1, keepdims=True)
            if cache is not None:
                kc_ref, vc_ref = cache
                sc = _dot_nt(q, kc_ref[h])
                m = jnp.maximum(m, jnp.max(sc, axis=-1, keepdims=True))
            e = jnp.exp(s - m)
            denom = jnp.sum(e, axis=-1, keepdims=True)
            o = _dot(e.astype(jnp.bfloat16), v_ref[p])
            if cache is not None:
                ec = jnp.exp(sc - m)
                denom = denom + jnp.sum(ec, axis=-1, keepdims=True)
                o = o + _dot(ec.astype(jnp.bfloat16), vc_ref[p])
            halves.append(o / denom)
        pairs.append(jnp.where(low_half, halves[0], halves[1]))
    attn = jnp.concatenate(pairs, axis=1)
    yb = (sbz_ref[...].astype(jnp.float32) * attn).astype(jnp.bfloat16)

    hn = _modulated_norm(x, gpre_ref[...], mod_row).astype(jnp.bfloat16)
    merged = None
    for n, y in enumerate((ya_ref[...], yb, yc_ref[...])):
        r0 = _R_MERGE - _BACK_ROWS + n * D_MODEL
        gate = jax.nn.sigmoid(_dot_nt(hn, wg_ref[r0:r0 + D_MODEL, :]))
        term = gate * _dot(y, wbr_ref[n])
        merged = term if merged is None else merged + term
    out = _rms(_dot(merged.astype(jnp.bfloat16), wo_ref[...]), gpost_ref[...])
    out_ref[...] = x + mod_row[:, 2 * D_MODEL:3 * D_MODEL] * out


def _back_kernel(dec_first, *refs):
    x_c, q_c, k_c, v_c, ya_c, yc_c, sbz_c = refs[0:7]
    x_d, q_d, k_d, v_d, kc_ref, vc_ref, ya_d, yc_d, sbz_d = refs[7:16]
    mod_ref = refs[16]
    weights = refs[17:17 + _N_BACK_WEIGHTS]
    out_c, out_d = refs[17 + _N_BACK_WEIGHTS:]
    i = pl.program_id(0)

    @pl.when(i < dec_first)
    def _():
        _back_tile(x_c, q_c, k_c, v_c, None, ya_c, yc_c, sbz_c, mod_ref[...], weights, out_c)

    @pl.when(i >= dec_first)
    def _():
        _back_tile(x_d, q_d, k_d, v_d, (kc_ref, vc_ref), ya_d, yc_d, sbz_d, mod_ref[...], weights, out_d)


def _back(l, x_prompt, x_sample, mod_l, w, ctx_fr, dec_fr, kc, vc):
    ctx, dec = _passes(x_prompt, x_sample, BACK_TILE)
    past = kc.shape[3]

    def pass_specs(p):
        tq = p.tm
        tile = pl.BlockSpec((None, tq, D_MODEL), lambda i: (p.batch(i), p.tile(i), 0))
        q = pl.BlockSpec((None, N_HEADS, tq, HEAD_PAD), lambda i: (p.batch(i), 0, p.tile(i), 0))
        k = pl.BlockSpec((None, N_HEADS, p.seq, LANES), lambda i: (p.batch(i), 0, 0, 0))
        v = pl.BlockSpec((None, N_PAIRS, p.seq, LANES), lambda i: (p.batch(i), 0, 0, 0))
        br = pl.BlockSpec((None, tq, BRANCH_DIM), lambda i: (p.batch(i), p.tile(i), 0))
        return tile, q, k, v, br

    def cond_row(i):
        return jnp.where(i < dec.first, ctx.cond_row(i), dec.cond_row(i))

    tile_c, q_c, k_c, v_c, br_c = pass_specs(ctx)
    tile_d, q_d, k_d, v_d, br_d = pass_specs(dec)
    kc_spec = pl.BlockSpec((None, None, N_HEADS, past, LANES), lambda i: (l, dec.batch(i), 0, 0, 0))
    vc_spec = pl.BlockSpec((None, None, N_PAIRS, past, LANES), lambda i: (l, dec.batch(i), 0, 0, 0))
    in_specs = ([tile_c, q_c, k_c, v_c, br_c, br_c, br_c]
                + [tile_d, q_d, k_d, v_d, kc_spec, vc_spec, br_d, br_d, br_d]
                + [pl.BlockSpec((None, 1, 3 * D_MODEL), lambda i: (cond_row(i), 0, 0)),
                   _const_spec(l, (1, D_MODEL)),
                   pl.BlockSpec((None, _BACK_ROWS, D_MODEL), lambda i: (l, 1, 0),
                                pipeline_mode=pl.Buffered(1)),
                   _const_spec(l, (3, BRANCH_DIM, D_MODEL)),
                   _const_spec(l, (D_MODEL, D_MODEL)),
                   _const_spec(l, (1, D_MODEL))])
    q_cx, k_cx, v_cx, ya_cx, yc_cx, sbz_cx = ctx_fr
    q_dx, k_dx, v_dx, ya_dx, yc_dx, sbz_dx = dec_fr
    return pl.pallas_call(
        functools.partial(_back_kernel, dec.first),
        grid=(ctx.steps + dec.steps,),
        in_specs=in_specs,
        out_specs=[tile_c, tile_d],
        out_shape=[jax.ShapeDtypeStruct(x_prompt.shape, jnp.float32),
                   jax.ShapeDtypeStruct(x_sample.shape, jnp.float32)],
        compiler_params=_params(),
        name="back",
    )(x_prompt, q_cx, k_cx, v_cx, ya_cx, yc_cx, sbz_cx,
      x_sample, q_dx, k_dx, v_dx, kc, vc, ya_dx, yc_dx, sbz_dx,
      mod_l, w["g_pre"], w["w_in_t"], w["w_branch"], w["w_o"], w["g_post"])


def _rope_swap_perm():
    quarter = QK_ROPE // 4
    idx = np.arange(QK_ROPE).reshape(2, 2, quarter)
    return idx[:, ::-1, :].reshape(-1)


def _rope_tables(n_tokens, rotate):
    if rotate:
        t = np.arange(n_tokens)
        pos = np.stack([t // GRID_W, t % GRID_W], axis=1).astype(np.float32)
        axis_dim = QK_ROPE // 2
        inv = (1.0 / (ROPE_THETA ** (jnp.arange(0, axis_dim, 2, dtype=jnp.float32) / axis_dim)))
        ang = jnp.asarray(pos)[:, :, None] * inv
        cos, sin = jnp.cos(ang), jnp.sin(ang)
        cfull = jnp.stack([cos, cos], axis=2).reshape(n_tokens, QK_ROPE)
        sfull = jnp.stack([-sin, sin], axis=2).reshape(n_tokens, QK_ROPE)
    else:
        cfull = jnp.ones((n_tokens, QK_ROPE), jnp.float32)
        sfull = jnp.zeros((n_tokens, QK_ROPE), jnp.float32)
    scale = (QK_NOPE + QK_ROPE) ** -0.5
    qtab = jnp.concatenate([jnp.full((n_tokens, QK_NOPE), scale, jnp.float32), cfull * scale, sfull * scale], axis=1)
    pad = jnp.zeros((n_tokens, LANES - QK_ROPE), jnp.float32)
    return qtab, jnp.concatenate([cfull, pad], axis=1), jnp.concatenate([sfull, pad], axis=1)


def _prepare_weights(g_pre, g_post, w_in, conv_w, conv_b, g_q, w_uq, g_kv, w_ukv, pool_w, pool_scale, w_branch, w_o):
    bf = jnp.bfloat16
    perm = _rope_swap_perm()
    assert w_in.shape == (DEPTH, D_MODEL, W_IN_COLS)
    w_in_t = _cast_w_in(jnp.swapaxes(w_in, 1, 2))

    uq = w_uq.reshape(DEPTH, Q_LORA, N_HEADS, QK_NOPE + QK_ROPE)
    rope_q = uq[..., QK_NOPE:]
    wuq = jnp.concatenate([uq, rope_q[..., perm]], axis=-1).reshape(DEPTH, Q_LORA, HP).astype(bf)

    ukv = w_ukv.reshape(DEPTH, KV_LORA, N_HEADS, QK_NOPE + V_HEAD)
    k_part = jnp.concatenate([ukv[..., :QK_NOPE], jnp.zeros_like(ukv[..., :HEAD_PAD - QK_NOPE])], axis=-1)
    place = np.zeros((LANES, N_HEADS, HEAD_PAD), np.float32)
    for d in range(QK_ROPE):
        place[d, :, QK_NOPE + d] = 1.0
        place[d, :, QK_NOPE + QK_ROPE + d] = 1.0
    place = jnp.broadcast_to(jnp.asarray(place), (DEPTH, LANES, N_HEADS, HEAD_PAD))
    wl2k = jnp.concatenate([k_part, place], axis=1).reshape(DEPTH, _LAT_EXT, HP).astype(bf)

    wv = ukv[..., QK_NOPE:].reshape(DEPTH, KV_LORA, N_HEADS * V_HEAD).astype(bf)

    return {
        "g_pre": g_pre.reshape(DEPTH, 1, D_MODEL), "g_post": g_post.reshape(DEPTH, 1, D_MODEL),
        "w_in_t": w_in_t, "conv_w": conv_w, "conv_b": conv_b.reshape(DEPTH, 1, BRANCH_DIM),
        "g_q": g_q.reshape(DEPTH, 1, Q_LORA), "wuq": wuq, "g_kv": g_kv.reshape(DEPTH, 1, KV_LORA),
        "wl2k": wl2k, "wv": wv, "pool_w": pool_w.astype(bf),
        "pool_scale": pool_scale.reshape(DEPTH, 1, BRANCH_DIM),
        "w_branch": w_branch.astype(bf), "w_o": w_o.astype(bf),
    }


def kernel(x_prompt, x_sample, cache_mla_latent, c, c_ctx, w_mod, b_mod, g_pre, g_post, w_in, conv_w, conv_b,
           g_q, w_uq, g_kv, w_ukv, pool_w, pool_scale, w_branch, w_o):
    n_dec = x_sample.shape[0]
    assert 1 + n_dec <= COND_ROWS
    w = _prepare_weights(g_pre, g_post, w_in, conv_w, conv_b, g_q, w_uq, g_kv, w_ukv, pool_w, pool_scale,
                         w_branch, w_o)
    cond = jnp.concatenate([c_ctx[None, :], c, jnp.zeros((COND_ROWS - 1 - n_dec, D_MODEL), jnp.float32)], axis=0)
    mod = _modulation(cond, w_mod, b_mod).reshape(DEPTH, COND_ROWS, 1, 3 * D_MODEL)
    kc, vc = _cache_kv(jnp.swapaxes(cache_mla_latent, 2, 3), w["wl2k"], w["wv"])
    ctx_tabs = _rope_tables(x_prompt.shape[1], rotate=False)
    dec_tabs = _rope_tables(x_sample.shape[1], rotate=True)

    h, hs = x_prompt, x_sample
    latents = []
    for l in range(DEPTH):
        ctx_fr, dec_fr = _front(l, h, hs, mod[l], w, ctx_tabs, dec_tabs)
        latents.append(ctx_fr[6])
        h, hs = _back(l, h, hs, mod[l], w, ctx_fr[:6], dec_fr, kc, vc)
    state = jnp.swapaxes(jnp.stack(latents, axis=1), 2, 3)
    return (h, hs, state)
```

```python
import functools

import numpy as np
import jax
import jax.numpy as jnp
from jax import lax
from jax.experimental import pallas as pl
from jax.experimental.pallas import tpu as pltpu

D_MODEL = 1024
DEPTH = 2
GRID_W = 64
EPS = 1e-6
BRANCH_DIM = 512
N_HEADS = 8
QK_NOPE = 64
QK_ROPE = 32
V_HEAD = 64
Q_LORA = 384
KV_LORA = 256
MLA_LATENT = KV_LORA + QK_ROPE
ROPE_THETA = 10000.0
POOL_WINDOWS = (2, 4, 8, 16)
POOL_GROUP = 128

LANES = 128
HEAD_PAD = LANES
HP = N_HEADS * HEAD_PAD
N_PAIRS = N_HEADS * V_HEAD // LANES
FRONT_TILE = 512
BACK_TILE = 256
HALO = 16
COND_ROWS = 8
VMEM_LIMIT = 56 * 1024 * 1024

_R_AB = 0
_R_ACX = BRANCH_DIM
_R_MAIN = 3 * BRANCH_DIM
_MAIN_ROWS = BRANCH_DIM + Q_LORA + KV_LORA + LANES
_R_BZ = 4 * BRANCH_DIM + Q_LORA + MLA_LATENT
_R_CU = _R_BZ + BRANCH_DIM
_R_CZ = _R_CU + BRANCH_DIM
_R_MERGE = _R_CZ + BRANCH_DIM
W_IN_COLS = _R_MERGE + 3 * D_MODEL
_FRONT_ROWS = 4352
_BACK_ROWS = W_IN_COLS // 2
_QD0 = BRANCH_DIM
_CKV0 = _QD0 + Q_LORA
_KR0 = _CKV0 + KV_LORA
_LAT_EXT = KV_LORA + LANES
assert _FRONT_ROWS >= _R_MERGE and _BACK_ROWS <= _R_MERGE and _BACK_ROWS % 16 == 0


def _dot(a, b):
    return jnp.dot(a, b, preferred_element_type=jnp.float32)


def _dot_nt(a, b):
    return lax.dot_general(a, b, (((1,), (1,)), ((), ())), preferred_element_type=jnp.float32)


def _rms(x, g):
    return x * lax.rsqrt(jnp.mean(x * x, axis=-1, keepdims=True) + EPS) * g


def _silu(x):
    return x * jax.nn.sigmoid(x)


def _modulated_norm(x, g_pre, mod_row):
    shift = mod_row[:, 0:D_MODEL]
    scale = mod_row[:, D_MODEL:2 * D_MODEL]
    return _rms(x, g_pre) * (1.0 + scale) + shift


def _params(n_grid_dims=1):
    return pltpu.CompilerParams(dimension_semantics=("arbitrary",) * n_grid_dims, vmem_limit_bytes=VMEM_LIMIT)


def _mod_kernel(cond_ref, w_ref, b_ref, out_ref):
    h = _silu(cond_ref[...]).astype(jnp.bfloat16)
    out_ref[...] = _dot(h, w_ref[...].astype(jnp.bfloat16)) + b_ref[...]


def _modulation(cond, w_mod, b_mod):
    col_tile = D_MODEL
    return pl.pallas_call(
        _mod_kernel,
        grid=(DEPTH, 3 * D_MODEL // col_tile),
        in_specs=[
            pl.BlockSpec((COND_ROWS, D_MODEL), lambda l, n: (0, 0)),
            pl.BlockSpec((None, D_MODEL, col_tile), lambda l, n: (l, 0, n)),
            pl.BlockSpec((None, 1, col_tile), lambda l, n: (l, 0, n)),
        ],
        out_specs=pl.BlockSpec((None, COND_ROWS, col_tile), lambda l, n: (l, 0, n)),
        out_shape=jax.ShapeDtypeStruct((DEPTH, COND_ROWS, 3 * D_MODEL), jnp.float32),
        compiler_params=_params(2),
        name="modulation",
    )(cond, w_mod, b_mod.reshape(DEPTH, 1, 3 * D_MODEL))


def _cast_kernel(w_ref, out_ref):
    out_ref[...] = w_ref[...].astype(jnp.bfloat16)


def _cast_w_in(w_in_t):
    rows = 512
    spec = pl.BlockSpec((None, rows, D_MODEL), lambda l, r: (l, r, 0))
    return pl.pallas_call(
        _cast_kernel,
        grid=(DEPTH, pl.cdiv(W_IN_COLS, rows)),
        in_specs=[spec],
        out_specs=spec,
        out_shape=jax.ShapeDtypeStruct(w_in_t.shape, jnp.bfloat16),
        compiler_params=_params(2),
        name="cast_w_in",
    )(w_in_t)


def _cache_kv_kernel(lat_t_ref, wl2k_ref, wv_ref, k_ref, v_ref):
    lat_t = lat_t_ref[...]
    pad = jnp.zeros((_LAT_EXT - MLA_LATENT, lat_t.shape[1]), jnp.float32)
    lat = jnp.concatenate([lat_t, pad], axis=0).T.astype(jnp.bfloat16)
    k = _dot(lat, wl2k_ref[...]).astype(jnp.bfloat16)
    v = _dot(lat[:, 0:KV_LORA], wv_ref[...]).astype(jnp.bfloat16)
    for h in range(N_HEADS):
        k_ref[h] = k[:, h * HEAD_PAD:(h + 1) * HEAD_PAD]
    for p in range(N_PAIRS):
        v_ref[p] = v[:, p * LANES:(p + 1) * LANES]


def _cache_kv(cache_t, wl2k, wv):
    nb, _, _, past = cache_t.shape

    def kv(n):
        return (jax.ShapeDtypeStruct((DEPTH, nb, n, past, LANES), jnp.bfloat16),
                pl.BlockSpec((None, None, n, past, LANES), lambda l, b: (l, b, 0, 0, 0)))

    (k_shape, k_spec), (v_shape, v_spec) = kv(N_HEADS), kv(N_PAIRS)
    return pl.pallas_call(
        _cache_kv_kernel,
        grid=(DEPTH, nb),
        in_specs=[
            pl.BlockSpec((None, None, MLA_LATENT, past), lambda l, b: (b, l, 0, 0)),
            pl.BlockSpec((None, _LAT_EXT, HP), lambda l, b: (l, 0, 0)),
            pl.BlockSpec((None, KV_LORA, N_HEADS * V_HEAD), lambda l, b: (l, 0, 0)),
        ],
        out_specs=[k_spec, v_spec],
        out_shape=[k_shape, v_shape],
        compiler_params=_params(2),
        name="cache_kv",
    )(cache_t, wl2k, wv)


class _Pass:
    def __init__(self, x, tile, first_step, first_cond_row, shared_cond):
        self.nb, self.seq, _ = x.shape
        self.tm = min(tile, self.seq)
        assert self.seq % self.tm == 0
        self.nt = self.seq // self.tm
        self.steps = self.nb * self.nt
        self.first = first_step
        self.first_cond_row = first_cond_row
        self.shared_cond = shared_cond

    def local(self, i):
        return jnp.clip(i - self.first, 0, self.steps - 1)

    def batch(self, i):
        return self.local(i) // self.nt

    def tile(self, i):
        return self.local(i) % self.nt

    def cond_row(self, i):
        return self.first_cond_row if self.shared_cond else self.first_cond_row + self.batch(i)


def _passes(x_prompt, x_sample, tile):
    ctx = _Pass(x_prompt, tile, 0, 0, True)
    dec = _Pass(x_sample, tile, ctx.steps, 1, False)
    return ctx, dec


def _const_spec(l, shape):
    return pl.BlockSpec((None,) + shape, lambda i: (l,) + (0,) * len(shape), pipeline_mode=pl.Buffered(1))


_N_FRONT_WEIGHTS = 11


def _front_tile(seq_len, j, x_ref, xp_ref, xn_ref, qtab_ref, kcos_ref, ksin_ref, mod_row, weights, outs, scratch):
    gpre_ref, w_ref, convw_ref, convb_ref, gq_ref, wuq_ref, gkv_ref, wl2k_ref, wv_ref, poolw_ref, pscale_ref = weights
    q_out, k_out, v_out, ya_out, yc_out, sbz_out = outs[:6]
    u_scr, cu_scr = scratch
    tm = x_ref.shape[0]
    ext_rows = slice(0, tm + 2 * HALO)
    last_j = seq_len // tm - 1

    x_ext = jnp.concatenate([xp_ref[...], x_ref[...], xn_ref[...]], axis=0)
    hn_ext = _modulated_norm(x_ext, gpre_ref[...], mod_row).astype(jnp.bfloat16)
    hn = hn_ext[HALO:HALO + tm]

    acx = _dot_nt(hn_ext, w_ref[_R_ACX:_R_ACX + 2 * BRANCH_DIM, :])
    row = lax.broadcasted_iota(jnp.int32, (tm + 2 * HALO, 1), 0)
    first_valid = jnp.where(j > 0, 0, HALO)
    end_valid = jnp.where(j < last_j, tm + 2 * HALO, tm + HALO)
    valid = jnp.logical_and(row >= first_valid, row < end_valid)
    u_scr[ext_rows, :] = jnp.where(valid, acx[:, 0:BRANCH_DIM] * acx[:, BRANCH_DIM:2 * BRANCH_DIM], 0.0)
    cu_scr[ext_rows, :] = jnp.where(valid, _dot_nt(hn_ext, w_ref[_R_CU:_R_CU + BRANCH_DIM, :]), 0.0)

    pm = _dot_nt(hn, w_ref[_R_MAIN:_R_MAIN + _MAIN_ROWS, :])
    conv = (u_scr[HALO - 1:HALO - 1 + tm, :] * convw_ref[0:1, :]
            + u_scr[HALO:HALO + tm, :] * convw_ref[1:2, :]
            + u_scr[HALO + 1:HALO + 1 + tm, :] * convw_ref[2:3, :]
            + convb_ref[...])
    ya = _silu(pm[:, 0:BRANCH_DIM]) * (_dot_nt(hn, w_ref[_R_AB:_R_AB + BRANCH_DIM, :]) * conv)
    ya_out[...] = ya.astype(jnp.bfloat16)

    t = j * tm + lax.broadcasted_iota(jnp.int32, (tm, POOL_GROUP), 0)
    mixed = []
    for gi, win in enumerate(POOL_WINDOWS):
        half = win // 2
        cols = slice(gi * POOL_GROUP, (gi + 1) * POOL_GROUP)
        total = cu_scr[HALO - half:HALO - half + tm, cols]
        for k in range(-half + 1, half):
            total = total + cu_scr[HALO + k:HALO + k + tm, cols]
        count = jnp.minimum(t + half, seq_len) - jnp.maximum(t - half, 0)
        pooled = total / count.astype(jnp.float32) - cu_scr[HALO:HALO + tm, cols]
        mixed.append(_dot(pooled.astype(jnp.bfloat16), poolw_ref[gi]))
    mixed = jnp.concatenate(mixed, axis=1) * pscale_ref[...]
    c_z = _dot_nt(hn, w_ref[_R_CZ:_R_CZ + BRANCH_DIM, :])
    yc_out[...] = (_silu(c_z) * mixed).astype(jnp.bfloat16)

    qn = _rms(pm[:, _QD0:_QD0 + Q_LORA], gq_ref[...]).astype(jnp.bfloat16)
    q = _dot(qn, wuq_ref[...])
    qtab = qtab_ref[...]
    ckv = _rms(pm[:, _CKV0:_CKV0 + KV_LORA], gkv_ref[...])
    kr = pm[:, _KR0:_KR0 + LANES]
    if len(outs) > 6:
        lat_out = outs[6]
        lat_out[0:KV_LORA, :] = ckv.T
        lat_out[KV_LORA:MLA_LATENT, :] = kr.T[0:QK_ROPE, :]
    lane = lax.broadcasted_iota(jnp.int32, (tm, LANES), 1)
    quarter = QK_ROPE // 4
    partner = jnp.where(lane % (2 * quarter) < quarter,
                        pltpu.roll(kr, LANES - quarter, 1), pltpu.roll(kr, quarter, 1))
    kr_rot = jnp.where(lane < QK_ROPE, kr * kcos_ref[...] + partner * ksin_ref[...], 0.0)
    ext = jnp.concatenate([ckv, kr_rot], axis=1).astype(jnp.bfloat16)
    k = _dot(ext, wl2k_ref[...]).astype(jnp.bfloat16)
    v = _dot(ext[:, 0:KV_LORA], wv_ref[...]).astype(jnp.bfloat16)
    for h in range(N_HEADS):
        cols = slice(h * HEAD_PAD, (h + 1) * HEAD_PAD)
        q_out[h] = (q[:, cols] * qtab).astype(jnp.bfloat16)
        k_out[h] = k[:, cols]
    for p in range(N_PAIRS):
        v_out[p] = v[:, p * LANES:(p + 1) * LANES]
    sbz_out[...] = _silu(_dot_nt(hn, w_ref[_R_BZ:_R_BZ + BRANCH_DIM, :])).astype(jnp.bfloat16)


def _front_kernel(ctx, dec, *refs):
    n_in = 6
    ctx_in, dec_in = refs[0:n_in], refs[n_in:2 * n_in]
    mod_ref = refs[2 * n_in]
    weights = refs[2 * n_in + 1:2 * n_in + 1 + _N_FRONT_WEIGHTS]
    outs = refs[2 * n_in + 1 + _N_FRONT_WEIGHTS:-2]
    ctx_out, dec_out = outs[0:7], outs[7:13]
    scratch = refs[-2:]
    i = pl.program_id(0)

    @pl.when(i < dec.first)
    def _():
        _front_tile(ctx.seq, ctx.tile(i), *ctx_in, mod_ref[...], weights, ctx_out, scratch)

    @pl.when(i >= dec.first)
    def _():
        _front_tile(dec.seq, dec.tile(i), *dec_in, mod_ref[...], weights, dec_out, scratch)


def _front(l, x_prompt, x_sample, mod_l, w, ctx_tabs, dec_tabs):
    ctx, dec = _passes(x_prompt, x_sample, FRONT_TILE)

    def pass_in_specs(p):
        tm = p.tm
        hb = tm // HALO
        n_hblk = p.seq // HALO
        tab = pl.BlockSpec((tm, LANES), lambda i: (p.tile(i), 0))
        return [
            pl.BlockSpec((None, tm, D_MODEL), lambda i: (p.batch(i), p.tile(i), 0)),
            pl.BlockSpec((None, HALO, D_MODEL), lambda i: (p.batch(i), jnp.maximum(p.tile(i) * hb - 1, 0), 0)),
            pl.BlockSpec((None, HALO, D_MODEL),
                         lambda i: (p.batch(i), jnp.minimum((p.tile(i) + 1) * hb, n_hblk - 1), 0)),
            tab, tab, tab,
        ]

    def pass_out(p, with_latent):
        tm = p.tm

        def blocks(n):
            return (jax.ShapeDtypeStruct((p.nb, n, p.seq, LANES), jnp.bfloat16),
                    pl.BlockSpec((None, n, tm, LANES), lambda i: (p.batch(i), 0, p.tile(i), 0)))

        (head_shape, head_spec), (pair_shape, pair_spec) = blocks(N_HEADS), blocks(N_PAIRS)
        br_shape = jax.ShapeDtypeStruct((p.nb, p.seq, BRANCH_DIM), jnp.bfloat16)
        br_spec = pl.BlockSpec((None, tm, BRANCH_DIM), lambda i: (p.batch(i), p.tile(i), 0))
        shapes = [head_shape, head_shape, pair_shape] + [br_shape] * 3
        specs = [head_spec, head_spec, pair_spec] + [br_spec] * 3
        if with_latent:
            shapes.append(jax.ShapeDtypeStruct((p.nb, MLA_LATENT, p.seq), jnp.float32))
            specs.append(pl.BlockSpec((None, MLA_LATENT, tm), lambda i: (p.batch(i), 0, p.tile(i))))
        return shapes, specs

    def cond_row(i):
        return jnp.where(i < dec.first, ctx.cond_row(i), dec.cond_row(i))

    weight_specs = [
        _const_spec(l, (1, D_MODEL)),
        _const_spec(l, (_FRONT_ROWS, D_MODEL)),
        _const_spec(l, (3, BRANCH_DIM)),
        _const_spec(l, (1, BRANCH_DIM)),
        _const_spec(l, (1, Q_LORA)),
        _const_spec(l, (Q_LORA, HP)),
        _const_spec(l, (1, KV_LORA)),
        _const_spec(l, (_LAT_EXT, HP)),
        _const_spec(l, (KV_LORA, N_HEADS * V_HEAD)),
        _const_spec(l, (len(POOL_WINDOWS), POOL_GROUP, POOL_GROUP)),
        _const_spec(l, (1, BRANCH_DIM)),
    ]
    assert len(weight_specs) == _N_FRONT_WEIGHTS
    ctx_shapes, ctx_specs = pass_out(ctx, True)
    dec_shapes, dec_specs = pass_out(dec, False)
    outs = pl.pallas_call(
        functools.partial(_front_kernel, ctx, dec),
        grid=(ctx.steps + dec.steps,),
        in_specs=(pass_in_specs(ctx) + pass_in_specs(dec)
                  + [pl.BlockSpec((None, 1, 3 * D_MODEL), lambda i: (cond_row(i), 0, 0))] + weight_specs),
        out_specs=ctx_specs + dec_specs,
        out_shape=ctx_shapes + dec_shapes,
        scratch_shapes=[pltpu.VMEM((max(ctx.tm, dec.tm) + 2 * HALO, BRANCH_DIM), jnp.float32)] * 2,
        compiler_params=_params(),
        name="front",
    )(x_prompt, x_prompt, x_prompt, *ctx_tabs, x_sample, x_sample, x_sample, *dec_tabs, mod_l,
      w["g_pre"], w["w_in_t"], w["conv_w"], w["conv_b"], w["g_q"], w["wuq"], w["g_kv"],
      w["wl2k"], w["wv"], w["pool_w"], w["pool_scale"])
    return outs[:7], outs[7:]


_N_BACK_WEIGHTS = 5


def _back_tile(x_ref, q_ref, k_ref, v_ref, cache, ya_ref, yc_ref, sbz_ref, mod_row, weights, out_ref):
    gpre_ref, wg_ref, wbr_ref, wo_ref, gpost_ref = weights
    x = x_ref[...]

    def with_ones(v):
        return jnp.concatenate([v, jnp.ones_like(v)], axis=1)

    low_half = lax.broadcasted_iota(jnp.int32, (x.shape[0], LANES), 1) < V_HEAD
    pairs = []
    for p in range(N_PAIRS):
        v_ext = with_ones(v_ref[p])
        if cache is not None:
            kc_ref, vc_ref = cache
            vc_ext = with_ones(vc_ref[p])
        halves = []
        for h in (2 * p, 2 * p + 1):
            q = q_ref[h]
            s = _dot_nt(q, k_ref[h])
            m = jnp.max(s, axis=-1, keepdims=True)
            if cache is not None:
                sc = _dot_nt(q, kc_ref[h])
                m = jnp.maximum(m, jnp.max(sc, axis=-1, keepdims=True))
            o = _dot(jnp.exp2((s - m).astype(jnp.bfloat16)), v_ext)
            if cache is not None:
                o = o + _dot(jnp.exp2((sc - m).astype(jnp.bfloat16)), vc_ext)
            halves.append(o[:, 0:LANES] / o[:, LANES:2 * LANES])
        pairs.append(jnp.where(low_half, halves[0], halves[1]))
    attn = jnp.concatenate(pairs, axis=1)
    yb = (sbz_ref[...].astype(jnp.float32) * attn).astype(jnp.bfloat16)

    hn = _modulated_norm(x, gpre_ref[...], mod_row).astype(jnp.bfloat16)
    merged = None
    for n, y in enumerate((ya_ref[...], yb, yc_ref[...])):
        r0 = _R_MERGE - _BACK_ROWS + n * D_MODEL
        gate = jax.nn.sigmoid(_dot_nt(hn, wg_ref[r0:r0 + D_MODEL, :]))
        term = gate * _dot(y, wbr_ref[n])
        merged = term if merged is None else merged + term
    out = _rms(_dot(merged.astype(jnp.bfloat16), wo_ref[...]), gpost_ref[...])
    out_ref[...] = x + mod_row[:, 2 * D_MODEL:3 * D_MODEL] * out


def _back_kernel(dec_first, *refs):
    x_c, q_c, k_c, v_c, ya_c, yc_c, sbz_c = refs[0:7]
    x_d, q_d, k_d, v_d, kc_ref, vc_ref, ya_d, yc_d, sbz_d = refs[7:16]
    mod_ref = refs[16]
    weights = refs[17:17 + _N_BACK_WEIGHTS]
    out_c, out_d = refs[17 + _N_BACK_WEIGHTS:]
    i = pl.program_id(0)

    @pl.when(i < dec_first)
    def _():
        _back_tile(x_c, q_c, k_c, v_c, None, ya_c, yc_c, sbz_c, mod_ref[...], weights, out_c)

    @pl.when(i >= dec_first)
    def _():
        _back_tile(x_d, q_d, k_d, v_d, (kc_ref, vc_ref), ya_d, yc_d, sbz_d, mod_ref[...], weights, out_d)


def _back(l, x_prompt, x_sample, mod_l, w, ctx_fr, dec_fr, kc, vc):
    ctx, dec = _passes(x_prompt, x_sample, BACK_TILE)
    past = kc.shape[3]

    def pass_specs(p):
        tq = p.tm
        tile = pl.BlockSpec((None, tq, D_MODEL), lambda i: (p.batch(i), p.tile(i), 0))
        q = pl.BlockSpec((None, N_HEADS, tq, HEAD_PAD), lambda i: (p.batch(i), 0, p.tile(i), 0))
        k = pl.BlockSpec((None, N_HEADS, p.seq, LANES), lambda i: (p.batch(i), 0, 0, 0))
        v = pl.BlockSpec((None, N_PAIRS, p.seq, LANES), lambda i: (p.batch(i), 0, 0, 0))
        br = pl.BlockSpec((None, tq, BRANCH_DIM), lambda i: (p.batch(i), p.tile(i), 0))
        return tile, q, k, v, br

    def cond_row(i):
        return jnp.where(i < dec.first, ctx.cond_row(i), dec.cond_row(i))

    tile_c, q_c, k_c, v_c, br_c = pass_specs(ctx)
    tile_d, q_d, k_d, v_d, br_d = pass_specs(dec)
    kc_spec = pl.BlockSpec((None, None, N_HEADS, past, LANES), lambda i: (l, dec.batch(i), 0, 0, 0))
    vc_spec = pl.BlockSpec((None, None, N_PAIRS, past, LANES), lambda i: (l, dec.batch(i), 0, 0, 0))
    in_specs = ([tile_c, q_c, k_c, v_c, br_c, br_c, br_c]
                + [tile_d, q_d, k_d, v_d, kc_spec, vc_spec, br_d, br_d, br_d]
                + [pl.BlockSpec((None, 1, 3 * D_MODEL), lambda i: (cond_row(i), 0, 0)),
                   _const_spec(l, (1, D_MODEL)),
                   pl.BlockSpec((None, _BACK_ROWS, D_MODEL), lambda i: (l, 1, 0),
                                pipeline_mode=pl.Buffered(1)),
                   _const_spec(l, (3, BRANCH_DIM, D_MODEL)),
                   _const_spec(l, (D_MODEL, D_MODEL)),
                   _const_spec(l, (1, D_MODEL))])
    q_cx, k_cx, v_cx, ya_cx, yc_cx, sbz_cx = ctx_fr
    q_dx, k_dx, v_dx, ya_dx, yc_dx, sbz_dx = dec_fr
    return pl.pallas_call(
        functools.partial(_back_kernel, dec.first),
        grid=(ctx.steps + dec.steps,),
        in_specs=in_specs,
        out_specs=[tile_c, tile_d],
        out_shape=[jax.ShapeDtypeStruct(x_prompt.shape, jnp.float32),
                   jax.ShapeDtypeStruct(x_sample.shape, jnp.float32)],
        compiler_params=_params(),
        name="back",
    )(x_prompt, q_cx, k_cx, v_cx, ya_cx, yc_cx, sbz_cx,
      x_sample, q_dx, k_dx, v_dx, kc, vc, ya_dx, yc_dx, sbz_dx,
      mod_l, w["g_pre"], w["w_in_t"], w["w_branch"], w["w_o"], w["g_post"])


def _rope_swap_perm():
    quarter = QK_ROPE // 4
    idx = np.arange(QK_ROPE).reshape(2, 2, quarter)
    return idx[:, ::-1, :].reshape(-1)


def _rope_tables(n_tokens, rotate):
    if rotate:
        t = np.arange(n_tokens)
        pos = np.stack([t // GRID_W, t % GRID_W], axis=1).astype(np.float32)
        axis_dim = QK_ROPE // 2
        inv = (1.0 / (ROPE_THETA ** (jnp.arange(0, axis_dim, 2, dtype=jnp.float32) / axis_dim)))
        ang = jnp.asarray(pos)[:, :, None] * inv
        cos, sin = jnp.cos(ang), jnp.sin(ang)
        cfull = jnp.stack([cos, cos], axis=2).reshape(n_tokens, QK_ROPE)
        sfull = jnp.stack([-sin, sin], axis=2).reshape(n_tokens, QK_ROPE)
    else:
        cfull = jnp.ones((n_tokens, QK_ROPE), jnp.float32)
        sfull = jnp.zeros((n_tokens, QK_ROPE), jnp.float32)
    scale = (QK_NOPE + QK_ROPE) ** -0.5 * np.log2(np.e)
    qtab = jnp.concatenate([jnp.full((n_tokens, QK_NOPE), scale, jnp.float32), cfull * scale, sfull * scale], axis=1)
    pad = jnp.zeros((n_tokens, LANES - QK_ROPE), jnp.float32)
    return qtab, jnp.concatenate([cfull, pad], axis=1), jnp.concatenate([sfull, pad], axis=1)


def _prepare_weights(g_pre, g_post, w_in, conv_w, conv_b, g_q, w_uq, g_kv, w_ukv, pool_w, pool_scale, w_branch, w_o):
    bf = jnp.bfloat16
    perm = _rope_swap_perm()
    assert w_in.shape == (DEPTH, D_MODEL, W_IN_COLS)
    w_in_t = _cast_w_in(jnp.swapaxes(w_in, 1, 2))

    uq = w_uq.reshape(DEPTH, Q_LORA, N_HEADS, QK_NOPE + QK_ROPE)
    rope_q = uq[..., QK_NOPE:]
    wuq = jnp.concatenate([uq, rope_q[..., perm]], axis=-1).reshape(DEPTH, Q_LORA, HP).astype(bf)

    ukv = w_ukv.reshape(DEPTH, KV_LORA, N_HEADS, QK_NOPE + V_HEAD)
    k_part = jnp.concatenate([ukv[..., :QK_NOPE], jnp.zeros_like(ukv[..., :HEAD_PAD - QK_NOPE])], axis=-1)
    place = np.zeros((LANES, N_HEADS, HEAD_PAD), np.float32)
    for d in range(QK_ROPE):
        place[d, :, QK_NOPE + d] = 1.0
        place[d, :, QK_NOPE + QK_ROPE + d] = 1.0
    place = jnp.broadcast_to(jnp.asarray(place), (DEPTH, LANES, N_HEADS, HEAD_PAD))
    wl2k = jnp.concatenate([k_part, place], axis=1).reshape(DEPTH, _LAT_EXT, HP).astype(bf)

    wv = ukv[..., QK_NOPE:].reshape(DEPTH, KV_LORA, N_HEADS * V_HEAD).astype(bf)

    return {
        "g_pre": g_pre.reshape(DEPTH, 1, D_MODEL), "g_post": g_post.reshape(DEPTH, 1, D_MODEL),
        "w_in_t": w_in_t, "conv_w": conv_w, "conv_b": conv_b.reshape(DEPTH, 1, BRANCH_DIM),
        "g_q": g_q.reshape(DEPTH, 1, Q_LORA), "wuq": wuq, "g_kv": g_kv.reshape(DEPTH, 1, KV_LORA),
        "wl2k": wl2k, "wv": wv, "pool_w": pool_w.astype(bf),
        "pool_scale": pool_scale.reshape(DEPTH, 1, BRANCH_DIM),
        "w_branch": w_branch.astype(bf), "w_o": w_o.astype(bf),
    }


def kernel(x_prompt, x_sample, cache_mla_latent, c, c_ctx, w_mod, b_mod, g_pre, g_post, w_in, conv_w, conv_b,
           g_q, w_uq, g_kv, w_ukv, pool_w, pool_scale, w_branch, w_o):
    n_dec = x_sample.shape[0]
    assert 1 + n_dec <= COND_ROWS
    w = _prepare_weights(g_pre, g_post, w_in, conv_w, conv_b, g_q, w_uq, g_kv, w_ukv, pool_w, pool_scale,
                         w_branch, w_o)
    cond = jnp.concatenate([c_ctx[None, :], c, jnp.zeros((COND_ROWS - 1 - n_dec, D_MODEL), jnp.float32)], axis=0)
    mod = _modulation(cond, w_mod, b_mod).reshape(DEPTH, COND_ROWS, 1, 3 * D_MODEL)
    kc, vc = _cache_kv(jnp.swapaxes(cache_mla_latent, 2, 3), w["wl2k"], w["wv"])
    ctx_tabs = _rope_tables(x_prompt.shape[1], rotate=False)
    dec_tabs = _rope_tables(x_sample.shape[1], rotate=True)

    h, hs = x_prompt, x_sample
    latents = []
    for l in range(DEPTH):
        ctx_fr, dec_fr = _front(l, h, hs, mod[l], w, ctx_tabs, dec_tabs)
        latents.append(ctx_fr[6])
        h, hs = _back(l, h, hs, mod[l], w, ctx_fr[:6], dec_fr, kc, vc)
    state = jnp.swapaxes(jnp.stack(latents, axis=1), 2, 3)
    return (h, hs, state)
```

```python
import functools

import numpy as np
import jax
import jax.numpy as jnp
from jax import lax
from jax.experimental import pallas as pl
from jax.experimental.pallas import tpu as pltpu

D_MODEL = 1024
DEPTH = 2
GRID_W = 64
EPS = 1e-6
BRANCH_DIM = 512
N_HEADS = 8
QK_NOPE = 64
QK_ROPE = 32
V_HEAD = 64
Q_LORA = 384
KV_LORA = 256
MLA_LATENT = KV_LORA + QK_ROPE
ROPE_THETA = 10000.0
POOL_WINDOWS = (2, 4, 8, 16)
POOL_GROUP = 128

LANES = 128
HEAD_PAD = LANES
HP = N_HEADS * HEAD_PAD
N_PAIRS = N_HEADS * V_HEAD // LANES
FRONT_TILE = 512
BACK_TILE = 256
HALO = 16
COND_ROWS = 8
VMEM_LIMIT = 56 * 1024 * 1024

_R_AB = 0
_R_ACX = BRANCH_DIM
_R_MAIN = 3 * BRANCH_DIM
_MAIN_ROWS = BRANCH_DIM + Q_LORA + KV_LORA + LANES
_R_BZ = 4 * BRANCH_DIM + Q_LORA + MLA_LATENT
_R_CU = _R_BZ + BRANCH_DIM
_R_CZ = _R_CU + BRANCH_DIM
_R_MERGE = _R_CZ + BRANCH_DIM
W_IN_COLS = _R_MERGE + 3 * D_MODEL
_FRONT_ROWS = -(-_R_MERGE // LANES) * LANES
_BACK_ROWS = W_IN_COLS // 2
_QD0 = BRANCH_DIM
_CKV0 = _QD0 + Q_LORA
_KR0 = _CKV0 + KV_LORA
_LAT_EXT = KV_LORA + LANES
assert _FRONT_ROWS >= _R_MERGE and _BACK_ROWS <= _R_MERGE and _BACK_ROWS % 16 == 0


def _dot(a, b):
    return jnp.dot(a, b, preferred_element_type=jnp.float32)


def _dot_nt(a, b):
    return lax.dot_general(a, b, (((1,), (1,)), ((), ())), preferred_element_type=jnp.float32)


def _rms(x, g):
    return x * lax.rsqrt(jnp.mean(x * x, axis=-1, keepdims=True) + EPS) * g


def _silu(x):
    return x * jax.nn.sigmoid(x)


def _modulated_norm(x, g_pre, mod_row):
    shift = mod_row[:, 0:D_MODEL]
    scale = mod_row[:, D_MODEL:2 * D_MODEL]
    return _rms(x, g_pre) * (1.0 + scale) + shift


def _params(n_grid_dims=1):
    return pltpu.CompilerParams(dimension_semantics=("arbitrary",) * n_grid_dims, vmem_limit_bytes=VMEM_LIMIT)


def _mod_kernel(cond_ref, w_ref, b_ref, out_ref):
    h = _silu(cond_ref[...]).astype(jnp.bfloat16)
    out_ref[...] = _dot(h, w_ref[...].astype(jnp.bfloat16)) + b_ref[...]


def _modulation(cond, w_mod, b_mod):
    col_tile = D_MODEL
    return pl.pallas_call(
        _mod_kernel,
        grid=(DEPTH, 3 * D_MODEL // col_tile),
        in_specs=[
            pl.BlockSpec((COND_ROWS, D_MODEL), lambda l, n: (0, 0)),
            pl.BlockSpec((None, D_MODEL, col_tile), lambda l, n: (l, 0, n)),
            pl.BlockSpec((None, 1, col_tile), lambda l, n: (l, 0, n)),
        ],
        out_specs=pl.BlockSpec((None, COND_ROWS, col_tile), lambda l, n: (l, 0, n)),
        out_shape=jax.ShapeDtypeStruct((DEPTH, COND_ROWS, 3 * D_MODEL), jnp.float32),
        compiler_params=_params(2),
        name="modulation",
    )(cond, w_mod, b_mod.reshape(DEPTH, 1, 3 * D_MODEL))


def _cast_kernel(w_ref, out_ref):
    out_ref[...] = w_ref[...].astype(jnp.bfloat16)


def _cast_w_in(w_in_t):
    rows = W_IN_COLS // 2
    spec = pl.BlockSpec((None, rows, D_MODEL), lambda l, r: (l, r, 0))
    return pl.pallas_call(
        _cast_kernel,
        grid=(DEPTH, pl.cdiv(W_IN_COLS, rows)),
        in_specs=[spec],
        out_specs=spec,
        out_shape=jax.ShapeDtypeStruct(w_in_t.shape, jnp.bfloat16),
        compiler_params=_params(2),
        name="cast_w_in",
    )(w_in_t)


def _rope_key_block(kr):
    return pltpu.roll(kr, QK_NOPE, 1) + pltpu.roll(kr, QK_NOPE + QK_ROPE, 1)


def _cache_kv_kernel(lat_t_ref, wk_ref, wv_ref, k_ref, v_ref):
    lat_t = lat_t_ref[...]
    pad = jnp.zeros((_LAT_EXT - MLA_LATENT, lat_t.shape[1]), jnp.float32)
    lat = jnp.concatenate([lat_t, pad], axis=0).T
    ckv_bf = lat[:, 0:KV_LORA].astype(jnp.bfloat16)
    k_nope = _dot(ckv_bf, wk_ref[...])
    k_rope = _rope_key_block(lat[:, KV_LORA:_LAT_EXT])
    v = _dot(ckv_bf, wv_ref[...]).astype(jnp.bfloat16)
    for h in range(N_HEADS):
        k_ref[h] = (k_nope[:, h * HEAD_PAD:(h + 1) * HEAD_PAD] + k_rope).astype(jnp.bfloat16)
    for p in range(N_PAIRS):
        v_ref[p] = v[:, p * LANES:(p + 1) * LANES]


def _cache_kv(cache_t, wk, wv):
    nb, _, _, past = cache_t.shape

    def kv(n):
        return (jax.ShapeDtypeStruct((DEPTH, nb, n, past, LANES), jnp.bfloat16),
                pl.BlockSpec((None, None, n, past, LANES), lambda l, b: (l, b, 0, 0, 0)))

    (k_shape, k_spec), (v_shape, v_spec) = kv(N_HEADS), kv(N_PAIRS)
    return pl.pallas_call(
        _cache_kv_kernel,
        grid=(DEPTH, nb),
        in_specs=[
            pl.BlockSpec((None, None, MLA_LATENT, past), lambda l, b: (b, l, 0, 0)),
            pl.BlockSpec((None, KV_LORA, HP), lambda l, b: (l, 0, 0)),
            pl.BlockSpec((None, KV_LORA, N_HEADS * V_HEAD), lambda l, b: (l, 0, 0)),
        ],
        out_specs=[k_spec, v_spec],
        out_shape=[k_shape, v_shape],
        compiler_params=_params(2),
        name="cache_kv",
    )(cache_t, wk, wv)


class _Pass:
    def __init__(self, x, tile, first_step, first_cond_row, shared_cond):
        self.nb, self.seq, _ = x.shape
        self.tm = min(tile, self.seq)
        assert self.seq % self.tm == 0
        self.nt = self.seq // self.tm
        self.steps = self.nb * self.nt
        self.first = first_step
        self.first_cond_row = first_cond_row
        self.shared_cond = shared_cond

    def local(self, i):
        return jnp.clip(i - self.first, 0, self.steps - 1)

    def batch(self, i):
        return self.local(i) // self.nt

    def tile(self, i):
        return self.local(i) % self.nt

    def cond_row(self, i):
        return self.first_cond_row if self.shared_cond else self.first_cond_row + self.batch(i)


def _passes(x_prompt, x_sample, tile):
    ctx = _Pass(x_prompt, tile, 0, 0, True)
    dec = _Pass(x_sample, tile, ctx.steps, 1, False)
    return ctx, dec


def _const_spec(l, shape):
    return pl.BlockSpec((None,) + shape, lambda i: (l,) + (0,) * len(shape), pipeline_mode=pl.Buffered(1))


_N_FRONT_WEIGHTS = 11


def _front_tile(seq_len, j, x_ref, xp_ref, xn_ref, qtab_ref, kcos_ref, ksin_ref, mod_row, weights, outs, scratch):
    gpre_ref, w_ref, convw_ref, convb_ref, gq_ref, wuq_ref, gkv_ref, wk_ref, wv_ref, poolw_ref, pscale_ref = weights
    q_out, k_out, v_out, ya_out, yc_out, sbz_out = outs[:6]
    u_scr, cu_scr = scratch
    tm = x_ref.shape[0]
    last_j = seq_len // tm - 1

    if last_j == 0:
        hn = _modulated_norm(x_ref[...], gpre_ref[...], mod_row).astype(jnp.bfloat16)
        acx = _dot_nt(hn, w_ref[_R_ACX:_R_ACX + 2 * BRANCH_DIM, :])
        zeros = jnp.zeros((HALO, BRANCH_DIM), jnp.float32)
        for scr in (u_scr, cu_scr):
            scr[0:HALO, :] = zeros
            scr[HALO + tm:2 * HALO + tm, :] = zeros
        u_scr[HALO:HALO + tm, :] = acx[:, 0:BRANCH_DIM] * acx[:, BRANCH_DIM:2 * BRANCH_DIM]
        cu_scr[HALO:HALO + tm, :] = _dot_nt(hn, w_ref[_R_CU:_R_CU + BRANCH_DIM, :])
    else:
        x_ext = jnp.concatenate([xp_ref[...], x_ref[...], xn_ref[...]], axis=0)
        hn_ext = _modulated_norm(x_ext, gpre_ref[...], mod_row).astype(jnp.bfloat16)
        hn = hn_ext[HALO:HALO + tm]
        acx = _dot_nt(hn_ext, w_ref[_R_ACX:_R_ACX + 2 * BRANCH_DIM, :])
        row = lax.broadcasted_iota(jnp.int32, (tm + 2 * HALO, 1), 0)
        first_valid = jnp.where(j > 0, 0, HALO)
        end_valid = jnp.where(j < last_j, tm + 2 * HALO, tm + HALO)
        valid = jnp.logical_and(row >= first_valid, row < end_valid)
        ext_rows = slice(0, tm + 2 * HALO)
        u_scr[ext_rows, :] = jnp.where(valid, acx[:, 0:BRANCH_DIM] * acx[:, BRANCH_DIM:2 * BRANCH_DIM], 0.0)
        cu_scr[ext_rows, :] = jnp.where(valid, _dot_nt(hn_ext, w_ref[_R_CU:_R_CU + BRANCH_DIM, :]), 0.0)

    pm = _dot_nt(hn, w_ref[_R_MAIN:_R_MAIN + _MAIN_ROWS, :])
    conv = (u_scr[HALO - 1:HALO - 1 + tm, :] * convw_ref[0:1, :]
            + u_scr[HALO:HALO + tm, :] * convw_ref[1:2, :]
            + u_scr[HALO + 1:HALO + 1 + tm, :] * convw_ref[2:3, :]
            + convb_ref[...])
    ya = _silu(pm[:, 0:BRANCH_DIM]) * (_dot_nt(hn, w_ref[_R_AB:_R_AB + BRANCH_DIM, :]) * conv)
    ya_out[...] = ya.astype(jnp.bfloat16)

    t = j * tm + lax.broadcasted_iota(jnp.int32, (tm, POOL_GROUP), 0)
    pooled = []
    for gi, win in enumerate(POOL_WINDOWS):
        half = win // 2
        cols = slice(gi * POOL_GROUP, (gi + 1) * POOL_GROUP)
        total = cu_scr[HALO - half:HALO - half + tm, cols]
        for k in range(-half + 1, half):
            total = total + cu_scr[HALO + k:HALO + k + tm, cols]
        count = jnp.minimum(t + half, seq_len) - jnp.maximum(t - half, 0)
        pooled.append((total / count.astype(jnp.float32) - cu_scr[HALO:HALO + tm, cols]).astype(jnp.bfloat16))
    mixed = [_dot(jnp.concatenate(pooled[2 * n:2 * n + 2], axis=1), poolw_ref[n]) for n in range(len(pooled) // 2)]
    mixed = jnp.concatenate(mixed, axis=1) * pscale_ref[...]
    c_z = _dot_nt(hn, w_ref[_R_CZ:_R_CZ + BRANCH_DIM, :])
    yc_out[...] = (_silu(c_z) * mixed).astype(jnp.bfloat16)

    qn = _rms(pm[:, _QD0:_QD0 + Q_LORA], gq_ref[...]).astype(jnp.bfloat16)
    q = _dot(qn, wuq_ref[...])
    qtab = qtab_ref[...]
    ckv = _rms(pm[:, _CKV0:_CKV0 + KV_LORA], gkv_ref[...])
    kr = pm[:, _KR0:_KR0 + LANES]
    if len(outs) > 6:
        lat_out = outs[6]
        lat_out[0:KV_LORA, :] = ckv.T
        lat_out[KV_LORA:MLA_LATENT, :] = kr.T[0:QK_ROPE, :]
    lane = lax.broadcasted_iota(jnp.int32, (tm, LANES), 1)
    quarter = QK_ROPE // 4
    partner = jnp.where(lane % (2 * quarter) < quarter,
                        pltpu.roll(kr, LANES - quarter, 1), pltpu.roll(kr, quarter, 1))
    kr_rot = jnp.where(lane < QK_ROPE, kr * kcos_ref[...] + partner * ksin_ref[...], 0.0)
    ckv_bf = ckv.astype(jnp.bfloat16)
    k_nope = _dot(ckv_bf, wk_ref[...])
    k_rope = _rope_key_block(kr_rot)
    v = _dot(ckv_bf, wv_ref[...]).astype(jnp.bfloat16)
    for h in range(N_HEADS):
        cols = slice(h * HEAD_PAD, (h + 1) * HEAD_PAD)
        q_out[h] = (q[:, cols] * qtab).astype(jnp.bfloat16)
        k_out[h] = (k_nope[:, cols] + k_rope).astype(jnp.bfloat16)
    for p in range(N_PAIRS):
        v_out[p] = v[:, p * LANES:(p + 1) * LANES]
    sbz_out[...] = _silu(_dot_nt(hn, w_ref[_R_BZ:_R_BZ + BRANCH_DIM, :])).astype(jnp.bfloat16)


def _front_kernel(ctx, dec, *refs):
    n_in = 6
    ctx_in, dec_in = refs[0:n_in], refs[n_in:2 * n_in]
    mod_ref = refs[2 * n_in]
    weights = refs[2 * n_in + 1:2 * n_in + 1 + _N_FRONT_WEIGHTS]
    outs = refs[2 * n_in + 1 + _N_FRONT_WEIGHTS:-2]
    ctx_out, dec_out = outs[0:7], outs[7:13]
    scratch = refs[-2:]
    i = pl.program_id(0)

    @pl.when(i < dec.first)
    def _():
        _front_tile(ctx.seq, ctx.tile(i), *ctx_in, mod_ref[...], weights, ctx_out, scratch)

    @pl.when(i >= dec.first)
    def _():
        _front_tile(dec.seq, dec.tile(i), *dec_in, mod_ref[...], weights, dec_out, scratch)


def _front(l, x_prompt, x_sample, mod_l, w, ctx_tabs, dec_tabs):
    ctx, dec = _passes(x_prompt, x_sample, FRONT_TILE)

    def pass_in_specs(p):
        tm = p.tm
        hb = tm // HALO
        n_hblk = p.seq // HALO
        tab = pl.BlockSpec((tm, LANES), lambda i: (p.tile(i), 0))
        return [
            pl.BlockSpec((None, tm, D_MODEL), lambda i: (p.batch(i), p.tile(i), 0)),
            pl.BlockSpec((None, HALO, D_MODEL), lambda i: (p.batch(i), jnp.maximum(p.tile(i) * hb - 1, 0), 0)),
            pl.BlockSpec((None, HALO, D_MODEL),
                         lambda i: (p.batch(i), jnp.minimum((p.tile(i) + 1) * hb, n_hblk - 1), 0)),
            tab, tab, tab,
        ]

    def pass_out(p, with_latent):
        tm = p.tm

        def blocks(n):
            return (jax.ShapeDtypeStruct((p.nb, n, p.seq, LANES), jnp.bfloat16),
                    pl.BlockSpec((None, n, tm, LANES), lambda i: (p.batch(i), 0, p.tile(i), 0)))

        (head_shape, head_spec), (pair_shape, pair_spec) = blocks(N_HEADS), blocks(N_PAIRS)
        br_shape = jax.ShapeDtypeStruct((p.nb, p.seq, BRANCH_DIM), jnp.bfloat16)
        br_spec = pl.BlockSpec((None, tm, BRANCH_DIM), lambda i: (p.batch(i), p.tile(i), 0))
        shapes = [head_shape, head_shape, pair_shape] + [br_shape] * 3
        specs = [head_spec, head_spec, pair_spec] + [br_spec] * 3
        if with_latent:
            shapes.append(jax.ShapeDtypeStruct((p.nb, MLA_LATENT, p.seq), jnp.float32))
            specs.append(pl.BlockSpec((None, MLA_LATENT, tm), lambda i: (p.batch(i), 0, p.tile(i))))
        return shapes, specs

    def cond_row(i):
        return jnp.where(i < dec.first, ctx.cond_row(i), dec.cond_row(i))

    weight_specs = [
        _const_spec(l, (1, D_MODEL)),
        _const_spec(l, (_FRONT_ROWS, D_MODEL)),
        _const_spec(l, (3, BRANCH_DIM)),
        _const_spec(l, (1, BRANCH_DIM)),
        _const_spec(l, (1, Q_LORA)),
        _const_spec(l, (Q_LORA, HP)),
        _const_spec(l, (1, KV_LORA)),
        _const_spec(l, (KV_LORA, HP)),
        _const_spec(l, (KV_LORA, N_HEADS * V_HEAD)),
        _const_spec(l, (len(POOL_WINDOWS) // 2, 2 * POOL_GROUP, 2 * POOL_GROUP)),
        _const_spec(l, (1, BRANCH_DIM)),
    ]
    assert len(weight_specs) == _N_FRONT_WEIGHTS
    ctx_shapes, ctx_specs = pass_out(ctx, True)
    dec_shapes, dec_specs = pass_out(dec, False)
    outs = pl.pallas_call(
        functools.partial(_front_kernel, ctx, dec),
        grid=(ctx.steps + dec.steps,),
        in_specs=(pass_in_specs(ctx) + pass_in_specs(dec)
                  + [pl.BlockSpec((None, 1, 3 * D_MODEL), lambda i: (cond_row(i), 0, 0))] + weight_specs),
        out_specs=ctx_specs + dec_specs,
        out_shape=ctx_shapes + dec_shapes,
        scratch_shapes=[pltpu.VMEM((max(ctx.tm, dec.tm) + 2 * HALO, BRANCH_DIM), jnp.float32)] * 2,
        compiler_params=_params(),
        name="front",
    )(x_prompt, x_prompt, x_prompt, *ctx_tabs, x_sample, x_sample, x_sample, *dec_tabs, mod_l,
      w["g_pre"], w["w_in_t"], w["conv_w"], w["conv_b"], w["g_q"], w["wuq"], w["g_kv"],
      w["wk"], w["wv"], w["pool_w"], w["pool_scale"])
    return outs[:7], outs[7:]


_N_BACK_WEIGHTS = 5


def _back_tile(x_ref, q_ref, k_ref, v_ref, cache, ya_ref, yc_ref, sbz_ref, mod_row, weights, out_ref):
    gpre_ref, wg_ref, wbr_ref, wo_ref, gpost_ref = weights
    x = x_ref[...]

    def with_ones(v):
        return jnp.concatenate([v, jnp.ones_like(v)], axis=1)

    low_half = lax.broadcasted_iota(jnp.int32, (x.shape[0], LANES), 1) < V_HEAD
    pairs = []
    for p in range(N_PAIRS):
        v_ext = with_ones(v_ref[p])
        if cache is not None:
            kc_ref, vc_ref = cache
            vc_ext = with_ones(vc_ref[p])
        halves = []
        for h in (2 * p, 2 * p + 1):
            q = q_ref[h]
            s = _dot_nt(q, k_ref[h])
            m = jnp.max(s, axis=-1, keepdims=True)
            if cache is not None:
                sc = _dot_nt(q, kc_ref[h])
                m = jnp.maximum(m, jnp.max(sc, axis=-1, keepdims=True))
            o = _dot(jnp.exp2((s - m).astype(jnp.bfloat16)), v_ext)
            if cache is not None:
                o = o + _dot(jnp.exp2((sc - m).astype(jnp.bfloat16)), vc_ext)
            halves.append(o[:, 0:LANES] / o[:, LANES:2 * LANES])
        pairs.append(jnp.where(low_half, halves[0], halves[1]))
    attn = jnp.concatenate(pairs, axis=1)
    yb = (sbz_ref[...].astype(jnp.float32) * attn).astype(jnp.bfloat16)

    hn = _modulated_norm(x, gpre_ref[...], mod_row).astype(jnp.bfloat16)
    merged = None
    for n, y in enumerate((ya_ref[...], yb, yc_ref[...])):
        r0 = _R_MERGE - _BACK_ROWS + n * D_MODEL
        gate = jax.nn.sigmoid(_dot_nt(hn, wg_ref[r0:r0 + D_MODEL, :]))
        term = gate * _dot(y, wbr_ref[n])
        merged = term if merged is None else merged + term
    out = _rms(_dot(merged.astype(jnp.bfloat16), wo_ref[...]), gpost_ref[...])
    out_ref[...] = x + mod_row[:, 2 * D_MODEL:3 * D_MODEL] * out


def _back_kernel(dec_first, *refs):
    x_c, q_c, k_c, v_c, ya_c, yc_c, sbz_c = refs[0:7]
    x_d, q_d, k_d, v_d, kc_ref, vc_ref, ya_d, yc_d, sbz_d = refs[7:16]
    mod_ref = refs[16]
    weights = refs[17:17 + _N_BACK_WEIGHTS]
    out_c, out_d = refs[17 + _N_BACK_WEIGHTS:]
    i = pl.program_id(0)

    @pl.when(i < dec_first)
    def _():
        _back_tile(x_c, q_c, k_c, v_c, None, ya_c, yc_c, sbz_c, mod_ref[...], weights, out_c)

    @pl.when(i >= dec_first)
    def _():
        _back_tile(x_d, q_d, k_d, v_d, (kc_ref, vc_ref), ya_d, yc_d, sbz_d, mod_ref[...], weights, out_d)


def _back(l, x_prompt, x_sample, mod_l, w, ctx_fr, dec_fr, kc, vc):
    ctx, dec = _passes(x_prompt, x_sample, BACK_TILE)
    past = kc.shape[3]

    def pass_specs(p):
        tq = p.tm
        tile = pl.BlockSpec((None, tq, D_MODEL), lambda i: (p.batch(i), p.tile(i), 0))
        q = pl.BlockSpec((None, N_HEADS, tq, HEAD_PAD), lambda i: (p.batch(i), 0, p.tile(i), 0))
        k = pl.BlockSpec((None, N_HEADS, p.seq, LANES), lambda i: (p.batch(i), 0, 0, 0))
        v = pl.BlockSpec((None, N_PAIRS, p.seq, LANES), lambda i: (p.batch(i), 0, 0, 0))
        br = pl.BlockSpec((None, tq, BRANCH_DIM), lambda i: (p.batch(i), p.tile(i), 0))
        return tile, q, k, v, br

    def cond_row(i):
        return jnp.where(i < dec.first, ctx.cond_row(i), dec.cond_row(i))

    tile_c, q_c, k_c, v_c, br_c = pass_specs(ctx)
    tile_d, q_d, k_d, v_d, br_d = pass_specs(dec)
    kc_spec = pl.BlockSpec((None, None, N_HEADS, past, LANES), lambda i: (l, dec.batch(i), 0, 0, 0))
    vc_spec = pl.BlockSpec((None, None, N_PAIRS, past, LANES), lambda i: (l, dec.batch(i), 0, 0, 0))
    in_specs = ([tile_c, q_c, k_c, v_c, br_c, br_c, br_c]
                + [tile_d, q_d, k_d, v_d, kc_spec, vc_spec, br_d, br_d, br_d]
                + [pl.BlockSpec((None, 1, 3 * D_MODEL), lambda i: (cond_row(i), 0, 0)),
                   _const_spec(l, (1, D_MODEL)),
                   pl.BlockSpec((None, _BACK_ROWS, D_MODEL), lambda i: (l, 1, 0),
                                pipeline_mode=pl.Buffered(1)),
                   _const_spec(l, (3, BRANCH_DIM, D_MODEL)),
                   _const_spec(l, (D_MODEL, D_MODEL)),
                   _const_spec(l, (1, D_MODEL))])
    q_cx, k_cx, v_cx, ya_cx, yc_cx, sbz_cx = ctx_fr
    q_dx, k_dx, v_dx, ya_dx, yc_dx, sbz_dx = dec_fr
    return pl.pallas_call(
        functools.partial(_back_kernel, dec.first),
        grid=(ctx.steps + dec.steps,),
        in_specs=in_specs,
        out_specs=[tile_c, tile_d],
        out_shape=[jax.ShapeDtypeStruct(x_prompt.shape, jnp.float32),
                   jax.ShapeDtypeStruct(x_sample.shape, jnp.float32)],
        compiler_params=_params(),
        name="back",
    )(x_prompt, q_cx, k_cx, v_cx, ya_cx, yc_cx, sbz_cx,
      x_sample, q_dx, k_dx, v_dx, kc, vc, ya_dx, yc_dx, sbz_dx,
      mod_l, w["g_pre"], w["w_in_t"], w["w_branch"], w["w_o"], w["g_post"])


def _rope_swap_perm():
    quarter = QK_ROPE // 4
    idx = np.arange(QK_ROPE).reshape(2, 2, quarter)
    return idx[:, ::-1, :].reshape(-1)


def _rope_tables(n_tokens, rotate):
    if rotate:
        t = np.arange(n_tokens)
        pos = np.stack([t // GRID_W, t % GRID_W], axis=1).astype(np.float64)
        axis_dim = QK_ROPE // 2
        inv = 1.0 / (ROPE_THETA ** (np.arange(0, axis_dim, 2, dtype=np.float64) / axis_dim))
        ang = pos[:, :, None] * inv
        cos, sin = np.cos(ang), np.sin(ang)
        cfull = np.stack([cos, cos], axis=2).reshape(n_tokens, QK_ROPE)
        sfull = np.stack([-sin, sin], axis=2).reshape(n_tokens, QK_ROPE)
    else:
        cfull = np.ones((n_tokens, QK_ROPE))
        sfull = np.zeros((n_tokens, QK_ROPE))
    scale = (QK_NOPE + QK_ROPE) ** -0.5 * np.log2(np.e)
    qtab = np.concatenate([np.full((n_tokens, QK_NOPE), scale), cfull * scale, sfull * scale], axis=1)
    pad = np.zeros((n_tokens, LANES - QK_ROPE))
    tabs = (qtab, np.concatenate([cfull, pad], axis=1), np.concatenate([sfull, pad], axis=1))
    return tuple(jnp.asarray(tab, jnp.float32) for tab in tabs)


def _prepare_weights(g_pre, g_post, w_in, conv_w, conv_b, g_q, w_uq, g_kv, w_ukv, pool_w, pool_scale, w_branch, w_o):
    bf = jnp.bfloat16
    perm = _rope_swap_perm()
    assert w_in.shape == (DEPTH, D_MODEL, W_IN_COLS)
    w_in_t = _cast_w_in(jnp.swapaxes(w_in, 1, 2))

    uq = w_uq.reshape(DEPTH, Q_LORA, N_HEADS, QK_NOPE + QK_ROPE)
    rope_q = uq[..., QK_NOPE:]
    wuq = jnp.concatenate([uq, rope_q[..., perm]], axis=-1).reshape(DEPTH, Q_LORA, HP).astype(bf)

    ukv = w_ukv.reshape(DEPTH, KV_LORA, N_HEADS, QK_NOPE + V_HEAD)
    wk = jnp.pad(ukv[..., :QK_NOPE], ((0, 0), (0, 0), (0, 0), (0, HEAD_PAD - QK_NOPE)))
    wk = wk.reshape(DEPTH, KV_LORA, HP).astype(bf)
    wv = ukv[..., QK_NOPE:].reshape(DEPTH, KV_LORA, N_HEADS * V_HEAD).astype(bf)

    pw = pool_w.astype(bf).reshape(DEPTH, len(POOL_WINDOWS) // 2, 2, POOL_GROUP, POOL_GROUP)
    zero = jnp.zeros_like(pw[:, :, 0])
    pool_bd = jnp.concatenate([jnp.concatenate([pw[:, :, 0], zero], axis=-1),
                               jnp.concatenate([zero, pw[:, :, 1]], axis=-1)], axis=-2)

    return {
        "g_pre": g_pre.reshape(DEPTH, 1, D_MODEL), "g_post": g_post.reshape(DEPTH, 1, D_MODEL),
        "w_in_t": w_in_t, "conv_w": conv_w, "conv_b": conv_b.reshape(DEPTH, 1, BRANCH_DIM),
        "g_q": g_q.reshape(DEPTH, 1, Q_LORA), "wuq": wuq, "g_kv": g_kv.reshape(DEPTH, 1, KV_LORA),
        "wk": wk, "wv": wv, "pool_w": pool_bd,
        "pool_scale": pool_scale.reshape(DEPTH, 1, BRANCH_DIM),
        "w_branch": w_branch.astype(bf), "w_o": w_o.astype(bf),
    }


def kernel(x_prompt, x_sample, cache_mla_latent, c, c_ctx, w_mod, b_mod, g_pre, g_post, w_in, conv_w, conv_b,
           g_q, w_uq, g_kv, w_ukv, pool_w, pool_scale, w_branch, w_o):
    n_dec = x_sample.shape[0]
    assert 1 + n_dec <= COND_ROWS
    w = _prepare_weights(g_pre, g_post, w_in, conv_w, conv_b, g_q, w_uq, g_kv, w_ukv, pool_w, pool_scale,
                         w_branch, w_o)
    cond = jnp.concatenate([c_ctx[None, :], c, jnp.zeros((COND_ROWS - 1 - n_dec, D_MODEL), jnp.float32)], axis=0)
    mod = _modulation(cond, w_mod, b_mod).reshape(DEPTH, COND_ROWS, 1, 3 * D_MODEL)
    kc, vc = _cache_kv(jnp.swapaxes(cache_mla_latent, 2, 3), w["wk"], w["wv"])
    ctx_tabs = _rope_tables(x_prompt.shape[1], rotate=False)
    dec_tabs = _rope_tables(x_sample.shape[1], rotate=True)

    h, hs = x_prompt, x_sample
    latents = []
    for l in range(DEPTH):
        ctx_fr, dec_fr = _front(l, h, hs, mod[l], w, ctx_tabs, dec_tabs)
        latents.append(ctx_fr[6])
        h, hs = _back(l, h, hs, mod[l], w, ctx_fr[:6], dec_fr, kc, vc)
    state = jnp.swapaxes(jnp.stack(latents, axis=1), 2, 3)
    return (h, hs, state)
```

```python
import functools

import numpy as np
import jax
import jax.numpy as jnp
from jax import lax
from jax.experimental import pallas as pl
from jax.experimental.pallas import tpu as pltpu

D_MODEL = 1024
DEPTH = 2
GRID_W = 64
EPS = 1e-6
BRANCH_DIM = 512
N_HEADS = 8
QK_NOPE = 64
QK_ROPE = 32
V_HEAD = 64
Q_LORA = 384
KV_LORA = 256
MLA_LATENT = KV_LORA + QK_ROPE
ROPE_THETA = 10000.0
POOL_WINDOWS = (2, 4, 8, 16)
POOL_GROUP = 128

LANES = 128
HEAD_PAD = LANES
HP = N_HEADS * HEAD_PAD
N_PAIRS = N_HEADS * V_HEAD // LANES
FRONT_TILE = 512
BACK_TILE = 256
HALO = 16
COND_ROWS = 8
VMEM_LIMIT = 56 * 1024 * 1024

_R_AB = 0
_R_ACX = BRANCH_DIM
_R_MAIN = 3 * BRANCH_DIM
_MAIN_ROWS = BRANCH_DIM + Q_LORA + KV_LORA + LANES
_R_BZ = 4 * BRANCH_DIM + Q_LORA + MLA_LATENT
_R_CU = _R_BZ + BRANCH_DIM
_R_CZ = _R_CU + BRANCH_DIM
_R_MERGE = _R_CZ + BRANCH_DIM
W_IN_COLS = _R_MERGE + 3 * D_MODEL
_FRONT_ROWS = -(-_R_MERGE // LANES) * LANES
_BACK_ROWS = W_IN_COLS // 2
_QD0 = BRANCH_DIM
_CKV0 = _QD0 + Q_LORA
_KR0 = _CKV0 + KV_LORA
_LAT_EXT = KV_LORA + LANES
assert _FRONT_ROWS >= _R_MERGE and _BACK_ROWS <= _R_MERGE and _BACK_ROWS % 16 == 0


def _dot(a, b):
    return jnp.dot(a, b, preferred_element_type=jnp.float32)


def _dot_nt(a, b):
    return lax.dot_general(a, b, (((1,), (1,)), ((), ())), preferred_element_type=jnp.float32)


def _rms(x, g):
    return x * lax.rsqrt(jnp.mean(x * x, axis=-1, keepdims=True) + EPS) * g


def _silu(x):
    return x * jax.nn.sigmoid(x)


def _modulated_norm(x, g_pre, mod_row):
    shift = mod_row[:, 0:D_MODEL]
    scale = mod_row[:, D_MODEL:2 * D_MODEL]
    return _rms(x, g_pre) * (1.0 + scale) + shift


def _params(n_grid_dims=1):
    return pltpu.CompilerParams(dimension_semantics=("arbitrary",) * n_grid_dims, vmem_limit_bytes=VMEM_LIMIT)


def _mod_kernel(cond_ref, w_ref, b_ref, out_ref):
    h = _silu(cond_ref[...]).astype(jnp.bfloat16)
    out_ref[...] = _dot(h, w_ref[...].astype(jnp.bfloat16)) + b_ref[...]


def _modulation(cond, w_mod, b_mod):
    col_tile = D_MODEL
    return pl.pallas_call(
        _mod_kernel,
        grid=(DEPTH, 3 * D_MODEL // col_tile),
        in_specs=[
            pl.BlockSpec((COND_ROWS, D_MODEL), lambda l, n: (0, 0)),
            pl.BlockSpec((None, D_MODEL, col_tile), lambda l, n: (l, 0, n)),
            pl.BlockSpec((None, 1, col_tile), lambda l, n: (l, 0, n)),
        ],
        out_specs=pl.BlockSpec((None, COND_ROWS, col_tile), lambda l, n: (l, 0, n)),
        out_shape=jax.ShapeDtypeStruct((DEPTH, COND_ROWS, 3 * D_MODEL), jnp.float32),
        compiler_params=_params(2),
        name="modulation",
    )(cond, w_mod, b_mod.reshape(DEPTH, 1, 3 * D_MODEL))


_BIG_WEIGHT_ROWS = (W_IN_COLS, 3 * BRANCH_DIM, D_MODEL)


def _chunk_rows(rows, n_chunks):
    per = -(-rows // n_chunks)
    return -(-per // 16) * 16


def _cast_specs(layer, n_chunks):
    in_specs, out_specs, out_shapes = [], [], []
    for rows in _BIG_WEIGHT_ROWS:
        per = _chunk_rows(rows, n_chunks)
        last = pl.cdiv(rows, per) - 1
        in_specs.append(pl.BlockSpec((None, per, D_MODEL), lambda i, last=last: (layer, jnp.minimum(i, last), 0)))
        out_specs.append(pl.BlockSpec((per, D_MODEL), lambda i, last=last: (jnp.minimum(i, last), 0)))
        out_shapes.append(jax.ShapeDtypeStruct((rows, D_MODEL), jnp.bfloat16))
    return in_specs, out_specs, out_shapes


def _cast_chunks(in_refs, out_refs):
    for src, dst in zip(in_refs, out_refs):
        dst[...] = src[...].astype(jnp.bfloat16)


def _cast_kernel(*refs):
    n = len(_BIG_WEIGHT_ROWS)
    _cast_chunks(refs[:n], refs[n:])


def _cast_layer(layer, big_f32):
    n_chunks = 8
    in_specs, out_specs, out_shapes = _cast_specs(layer, n_chunks)
    return pl.pallas_call(
        _cast_kernel,
        grid=(n_chunks,),
        in_specs=in_specs,
        out_specs=out_specs,
        out_shape=out_shapes,
        compiler_params=_params(),
        name="cast_weights",
    )(*big_f32)


def _rope_key_block(kr):
    return pltpu.roll(kr, QK_NOPE, 1) + pltpu.roll(kr, QK_NOPE + QK_ROPE, 1)


def _cache_kv_kernel(lat_t_ref, wk_ref, wv_ref, k_ref, v_ref):
    lat_t = lat_t_ref[...]
    pad = jnp.zeros((_LAT_EXT - MLA_LATENT, lat_t.shape[1]), jnp.float32)
    lat = jnp.concatenate([lat_t, pad], axis=0).T
    ckv_bf = lat[:, 0:KV_LORA].astype(jnp.bfloat16)
    k_nope = _dot(ckv_bf, wk_ref[...])
    k_rope = _rope_key_block(lat[:, KV_LORA:_LAT_EXT])
    v = _dot(ckv_bf, wv_ref[...]).astype(jnp.bfloat16)
    for h in range(N_HEADS):
        k_ref[h] = (k_nope[:, h * HEAD_PAD:(h + 1) * HEAD_PAD] + k_rope).astype(jnp.bfloat16)
    for p in range(N_PAIRS):
        v_ref[p] = v[:, p * LANES:(p + 1) * LANES]


def _cache_kv(cache_t, wk, wv):
    nb, _, _, past = cache_t.shape

    def kv(n):
        return (jax.ShapeDtypeStruct((DEPTH, nb, n, past, LANES), jnp.bfloat16),
                pl.BlockSpec((None, None, n, past, LANES), lambda l, b: (l, b, 0, 0, 0)))

    (k_shape, k_spec), (v_shape, v_spec) = kv(N_HEADS), kv(N_PAIRS)
    return pl.pallas_call(
        _cache_kv_kernel,
        grid=(DEPTH, nb),
        in_specs=[
            pl.BlockSpec((None, None, MLA_LATENT, past), lambda l, b: (b, l, 0, 0)),
            pl.BlockSpec((None, KV_LORA, HP), lambda l, b: (l, 0, 0)),
            pl.BlockSpec((None, KV_LORA, N_HEADS * V_HEAD), lambda l, b: (l, 0, 0)),
        ],
        out_specs=[k_spec, v_spec],
        out_shape=[k_shape, v_shape],
        compiler_params=_params(2),
        name="cache_kv",
    )(cache_t, wk, wv)


class _Pass:
    def __init__(self, x, tile, first_step, first_cond_row, shared_cond):
        self.nb, self.seq, _ = x.shape
        self.tm = min(tile, self.seq)
        assert self.seq % self.tm == 0
        self.nt = self.seq // self.tm
        self.steps = self.nb * self.nt
        self.first = first_step
        self.first_cond_row = first_cond_row
        self.shared_cond = shared_cond

    def local(self, i):
        return jnp.clip(i - self.first, 0, self.steps - 1)

    def batch(self, i):
        return self.local(i) // self.nt

    def tile(self, i):
        return self.local(i) % self.nt

    def cond_row(self, i):
        return self.first_cond_row if self.shared_cond else self.first_cond_row + self.batch(i)


def _passes(x_prompt, x_sample, tile):
    ctx = _Pass(x_prompt, tile, 0, 0, True)
    dec = _Pass(x_sample, tile, ctx.steps, 1, False)
    return ctx, dec


def _const_spec(l, shape):
    return pl.BlockSpec((None,) + shape, lambda i: (l,) + (0,) * len(shape), pipeline_mode=pl.Buffered(1))


def _row_block_spec(rows, block_index):
    return pl.BlockSpec((rows, D_MODEL), lambda i: (block_index, 0), pipeline_mode=pl.Buffered(1))


_N_FRONT_WEIGHTS = 11


def _front_tile(seq_len, j, x_ref, xp_ref, xn_ref, qtab_ref, kcos_ref, ksin_ref, mod_row, weights, outs, scratch):
    gpre_ref, w_ref, convw_ref, convb_ref, gq_ref, wuq_ref, gkv_ref, wk_ref, wv_ref, poolw_ref, pscale_ref = weights
    q_out, k_out, v_out, ya_out, yc_out, sbz_out = outs[:6]
    u_scr, cu_scr = scratch
    tm = x_ref.shape[0]
    last_j = seq_len // tm - 1

    if last_j == 0:
        hn = _modulated_norm(x_ref[...], gpre_ref[...], mod_row).astype(jnp.bfloat16)
        acx = _dot_nt(hn, w_ref[_R_ACX:_R_ACX + 2 * BRANCH_DIM, :])
        zeros = jnp.zeros((HALO, BRANCH_DIM), jnp.float32)
        for scr in (u_scr, cu_scr):
            scr[0:HALO, :] = zeros
            scr[HALO + tm:2 * HALO + tm, :] = zeros
        u_scr[HALO:HALO + tm, :] = acx[:, 0:BRANCH_DIM] * acx[:, BRANCH_DIM:2 * BRANCH_DIM]
        cu_scr[HALO:HALO + tm, :] = _dot_nt(hn, w_ref[_R_CU:_R_CU + BRANCH_DIM, :])
    else:
        x_ext = jnp.concatenate([xp_ref[...], x_ref[...], xn_ref[...]], axis=0)
        hn_ext = _modulated_norm(x_ext, gpre_ref[...], mod_row).astype(jnp.bfloat16)
        hn = hn_ext[HALO:HALO + tm]
        acx = _dot_nt(hn_ext, w_ref[_R_ACX:_R_ACX + 2 * BRANCH_DIM, :])
        row = lax.broadcasted_iota(jnp.int32, (tm + 2 * HALO, 1), 0)
        first_valid = jnp.where(j > 0, 0, HALO)
        end_valid = jnp.where(j < last_j, tm + 2 * HALO, tm + HALO)
        valid = jnp.logical_and(row >= first_valid, row < end_valid)
        ext_rows = slice(0, tm + 2 * HALO)
        u_scr[ext_rows, :] = jnp.where(valid, acx[:, 0:BRANCH_DIM] * acx[:, BRANCH_DIM:2 * BRANCH_DIM], 0.0)
        cu_scr[ext_rows, :] = jnp.where(valid, _dot_nt(hn_ext, w_ref[_R_CU:_R_CU + BRANCH_DIM, :]), 0.0)

    pm = _dot_nt(hn, w_ref[_R_MAIN:_R_MAIN + _MAIN_ROWS, :])
    conv = (u_scr[HALO - 1:HALO - 1 + tm, :] * convw_ref[0:1, :]
            + u_scr[HALO:HALO + tm, :] * convw_ref[1:2, :]
            + u_scr[HALO + 1:HALO + 1 + tm, :] * convw_ref[2:3, :]
            + convb_ref[...])
    ya = _silu(pm[:, 0:BRANCH_DIM]) * (_dot_nt(hn, w_ref[_R_AB:_R_AB + BRANCH_DIM, :]) * conv)
    ya_out[...] = ya.astype(jnp.bfloat16)

    t = j * tm + lax.broadcasted_iota(jnp.int32, (tm, POOL_GROUP), 0)
    pooled = []
    for gi, win in enumerate(POOL_WINDOWS):
        half = win // 2
        cols = slice(gi * POOL_GROUP, (gi + 1) * POOL_GROUP)
        total = cu_scr[HALO - half:HALO - half + tm, cols]
        for k in range(-half + 1, half):
            total = total + cu_scr[HALO + k:HALO + k + tm, cols]
        count = jnp.minimum(t + half, seq_len) - jnp.maximum(t - half, 0)
        pooled.append((total / count.astype(jnp.float32) - cu_scr[HALO:HALO + tm, cols]).astype(jnp.bfloat16))
    mixed = [_dot(jnp.concatenate(pooled[2 * n:2 * n + 2], axis=1), poolw_ref[n]) for n in range(len(pooled) // 2)]
    mixed = jnp.concatenate(mixed, axis=1) * pscale_ref[...]
    c_z = _dot_nt(hn, w_ref[_R_CZ:_R_CZ + BRANCH_DIM, :])
    yc_out[...] = (_silu(c_z) * mixed).astype(jnp.bfloat16)

    qn = _rms(pm[:, _QD0:_QD0 + Q_LORA], gq_ref[...]).astype(jnp.bfloat16)
    q = _dot(qn, wuq_ref[...])
    qtab = qtab_ref[...]
    ckv = _rms(pm[:, _CKV0:_CKV0 + KV_LORA], gkv_ref[...])
    kr = pm[:, _KR0:_KR0 + LANES]
    if len(outs) > 6:
        lat_out = outs[6]
        if len(lat_out.shape) == 3:
            for later in range(1, lat_out.shape[0]):
                lat_out[later] = jnp.zeros(lat_out.shape[1:], jnp.float32)
            lat_out = lat_out.at[0]
        lat_out[0:KV_LORA, :] = ckv.T
        lat_out[KV_LORA:MLA_LATENT, :] = kr.T[0:QK_ROPE, :]
    lane = lax.broadcasted_iota(jnp.int32, (tm, LANES), 1)
    quarter = QK_ROPE // 4
    partner = jnp.where(lane % (2 * quarter) < quarter,
                        pltpu.roll(kr, LANES - quarter, 1), pltpu.roll(kr, quarter, 1))
    kr_rot = jnp.where(lane < QK_ROPE, kr * kcos_ref[...] + partner * ksin_ref[...], 0.0)
    ckv_bf = ckv.astype(jnp.bfloat16)
    k_nope = _dot(ckv_bf, wk_ref[...])
    k_rope = _rope_key_block(kr_rot)
    v = _dot(ckv_bf, wv_ref[...]).astype(jnp.bfloat16)
    for h in range(N_HEADS):
        cols = slice(h * HEAD_PAD, (h + 1) * HEAD_PAD)
        q_out[h] = (q[:, cols] * qtab).astype(jnp.bfloat16)
        k_out[h] = (k_nope[:, cols] + k_rope).astype(jnp.bfloat16)
    for p in range(N_PAIRS):
        v_out[p] = v[:, p * LANES:(p + 1) * LANES]
    sbz_out[...] = _silu(_dot_nt(hn, w_ref[_R_BZ:_R_BZ + BRANCH_DIM, :])).astype(jnp.bfloat16)


def _front_kernel(ctx, dec, n_aliased, *refs):
    n_in = 6
    ctx_in, dec_in = refs[0:n_in], refs[n_in:2 * n_in]
    mod_ref = refs[2 * n_in]
    weights = refs[2 * n_in + 1:2 * n_in + 1 + _N_FRONT_WEIGHTS]
    outs = refs[2 * n_in + 1 + _N_FRONT_WEIGHTS + n_aliased:-2]
    ctx_out, dec_out = outs[0:7], outs[7:13]
    scratch = refs[-2:]
    i = pl.program_id(0)

    @pl.when(i < dec.first)
    def _():
        _front_tile(ctx.seq, ctx.tile(i), *ctx_in, mod_ref[...], weights, ctx_out, scratch)

    @pl.when(i >= dec.first)
    def _():
        _front_tile(dec.seq, dec.tile(i), *dec_in, mod_ref[...], weights, dec_out, scratch)


def _front(l, x_prompt, x_sample, mod_l, w, w_in_t, ctx_tabs, dec_tabs, state_t):
    ctx, dec = _passes(x_prompt, x_sample, FRONT_TILE)

    def pass_in_specs(p):
        tm = p.tm
        hb = tm // HALO
        n_hblk = p.seq // HALO
        tab = pl.BlockSpec((tm, LANES), lambda i: (p.tile(i), 0))
        return [
            pl.BlockSpec((None, tm, D_MODEL), lambda i: (p.batch(i), p.tile(i), 0)),
            pl.BlockSpec((None, HALO, D_MODEL), lambda i: (p.batch(i), jnp.maximum(p.tile(i) * hb - 1, 0), 0)),
            pl.BlockSpec((None, HALO, D_MODEL),
                         lambda i: (p.batch(i), jnp.minimum((p.tile(i) + 1) * hb, n_hblk - 1), 0)),
            tab, tab, tab,
        ]

    def pass_out(p, with_latent):
        tm = p.tm

        def blocks(n):
            return (jax.ShapeDtypeStruct((p.nb, n, p.seq, LANES), jnp.bfloat16),
                    pl.BlockSpec((None, n, tm, LANES), lambda i: (p.batch(i), 0, p.tile(i), 0)))

        (head_shape, head_spec), (pair_shape, pair_spec) = blocks(N_HEADS), blocks(N_PAIRS)
        br_shape = jax.ShapeDtypeStruct((p.nb, p.seq, BRANCH_DIM), jnp.bfloat16)
        br_spec = pl.BlockSpec((None, tm, BRANCH_DIM), lambda i: (p.batch(i), p.tile(i), 0))
        shapes = [head_shape, head_shape, pair_shape] + [br_shape] * 3
        specs = [head_spec, head_spec, pair_spec] + [br_spec] * 3
        if with_latent:
            shapes.append(jax.ShapeDtypeStruct((p.nb, DEPTH, MLA_LATENT, p.seq), jnp.float32))
            if state_t is None:
                specs.append(pl.BlockSpec((None, DEPTH, MLA_LATENT, tm), lambda i: (p.batch(i), 0, 0, p.tile(i))))
            else:
                specs.append(pl.BlockSpec((None, None, MLA_LATENT, tm), lambda i: (p.batch(i), l, 0, p.tile(i))))
        return shapes, specs

    def cond_row(i):
        return jnp.where(i < dec.first, ctx.cond_row(i), dec.cond_row(i))

    weight_specs = [
        _const_spec(l, (1, D_MODEL)),
        _row_block_spec(_FRONT_ROWS, 0),
        _const_spec(l, (3, BRANCH_DIM)),
        _const_spec(l, (1, BRANCH_DIM)),
        _const_spec(l, (1, Q_LORA)),
        _const_spec(l, (Q_LORA, HP)),
        _const_spec(l, (1, KV_LORA)),
        _const_spec(l, (KV_LORA, HP)),
        _const_spec(l, (KV_LORA, N_HEADS * V_HEAD)),
        _const_spec(l, (len(POOL_WINDOWS) // 2, 2 * POOL_GROUP, 2 * POOL_GROUP)),
        _const_spec(l, (1, BRANCH_DIM)),
    ]
    assert len(weight_specs) == _N_FRONT_WEIGHTS
    ctx_shapes, ctx_specs = pass_out(ctx, True)
    dec_shapes, dec_specs = pass_out(dec, False)
    in_specs = (pass_in_specs(ctx) + pass_in_specs(dec)
                + [pl.BlockSpec((None, 1, 3 * D_MODEL), lambda i: (cond_row(i), 0, 0))] + weight_specs)
    args = [x_prompt, x_prompt, x_prompt, *ctx_tabs, x_sample, x_sample, x_sample, *dec_tabs, mod_l,
            w["g_pre"], w_in_t, w["conv_w"], w["conv_b"], w["g_q"], w["wuq"], w["g_kv"],
            w["wk"], w["wv"], w["pool_w"], w["pool_scale"]]
    aliases = {}
    if state_t is not None:
        aliases = {len(args): len(ctx_shapes) - 1}
        in_specs.append(pl.BlockSpec(memory_space=pl.ANY))
        args.append(state_t)
    outs = pl.pallas_call(
        functools.partial(_front_kernel, ctx, dec, len(aliases)),
        grid=(ctx.steps + dec.steps,),
        in_specs=in_specs,
        out_specs=ctx_specs + dec_specs,
        out_shape=ctx_shapes + dec_shapes,
        scratch_shapes=[pltpu.VMEM((max(ctx.tm, dec.tm) + 2 * HALO, BRANCH_DIM), jnp.float32)] * 2,
        input_output_aliases=aliases,
        compiler_params=_params(),
        name="front",
    )(*args)
    return outs[:7], outs[7:]


_N_BACK_WEIGHTS = 5


def _back_tile(x_ref, q_ref, k_ref, v_ref, cache, ya_ref, yc_ref, sbz_ref, mod_row, weights, out_ref):
    gpre_ref, wg_ref, wbr_ref, wo_ref, gpost_ref = weights
    x = x_ref[...]

    def with_ones(v):
        return jnp.concatenate([v, jnp.ones_like(v)], axis=1)

    low_half = lax.broadcasted_iota(jnp.int32, (x.shape[0], LANES), 1) < V_HEAD
    pairs = []
    for p in range(N_PAIRS):
        v_ext = with_ones(v_ref[p])
        if cache is not None:
            kc_ref, vc_ref = cache
            vc_ext = with_ones(vc_ref[p])
        halves = []
        for h in (2 * p, 2 * p + 1):
            q = q_ref[h]
            s = _dot_nt(q, k_ref[h])
            m = jnp.max(s, axis=-1, keepdims=True)
            if cache is not None:
                sc = _dot_nt(q, kc_ref[h])
                m = jnp.maximum(m, jnp.max(sc, axis=-1, keepdims=True))
            o = _dot(jnp.exp2((s - m).astype(jnp.bfloat16)), v_ext)
            if cache is not None:
                o = o + _dot(jnp.exp2((sc - m).astype(jnp.bfloat16)), vc_ext)
            halves.append(o[:, 0:LANES] / o[:, LANES:2 * LANES])
        pairs.append(jnp.where(low_half, halves[0], halves[1]))
    attn = jnp.concatenate(pairs, axis=1)
    yb = (sbz_ref[...].astype(jnp.float32) * attn).astype(jnp.bfloat16)

    hn = _modulated_norm(x, gpre_ref[...], mod_row).astype(jnp.bfloat16)
    merged = None
    for n, y in enumerate((ya_ref[...], yb, yc_ref[...])):
        r0 = _R_MERGE - _BACK_ROWS + n * D_MODEL
        gate = jax.nn.sigmoid(_dot_nt(hn, wg_ref[r0:r0 + D_MODEL, :]))
        term = gate * _dot(y, wbr_ref[n * BRANCH_DIM:(n + 1) * BRANCH_DIM, :])
        merged = term if merged is None else merged + term
    out = _rms(_dot(merged.astype(jnp.bfloat16), wo_ref[...]), gpost_ref[...])
    out_ref[...] = x + mod_row[:, 2 * D_MODEL:3 * D_MODEL] * out


def _back_kernel(dec_first, cast_next, *refs):
    x_c, q_c, k_c, v_c, ya_c, yc_c, sbz_c = refs[0:7]
    x_d, q_d, k_d, v_d, kc_ref, vc_ref, ya_d, yc_d, sbz_d = refs[7:16]
    mod_ref = refs[16]
    n_w = 17 + _N_BACK_WEIGHTS
    weights = refs[17:n_w]
    n_cast = len(_BIG_WEIGHT_ROWS) if cast_next else 0
    out_c, out_d = refs[n_w + n_cast:n_w + n_cast + 2]
    i = pl.program_id(0)
    if cast_next:
        _cast_chunks(refs[n_w:n_w + n_cast], refs[n_w + n_cast + 2:])

    @pl.when(i < dec_first)
    def _():
        _back_tile(x_c, q_c, k_c, v_c, None, ya_c, yc_c, sbz_c, mod_ref[...], weights, out_c)

    @pl.when(i >= dec_first)
    def _():
        _back_tile(x_d, q_d, k_d, v_d, (kc_ref, vc_ref), ya_d, yc_d, sbz_d, mod_ref[...], weights, out_d)


def _back(l, x_prompt, x_sample, mod_l, w, big_bf, ctx_fr, dec_fr, kc, vc, next_big_f32):
    ctx, dec = _passes(x_prompt, x_sample, BACK_TILE)
    past = kc.shape[3]
    n_steps = ctx.steps + dec.steps

    def pass_specs(p):
        tq = p.tm
        tile = pl.BlockSpec((None, tq, D_MODEL), lambda i: (p.batch(i), p.tile(i), 0))
        q = pl.BlockSpec((None, N_HEADS, tq, HEAD_PAD), lambda i: (p.batch(i), 0, p.tile(i), 0))
        k = pl.BlockSpec((None, N_HEADS, p.seq, LANES), lambda i: (p.batch(i), 0, 0, 0))
        v = pl.BlockSpec((None, N_PAIRS, p.seq, LANES), lambda i: (p.batch(i), 0, 0, 0))
        br = pl.BlockSpec((None, tq, BRANCH_DIM), lambda i: (p.batch(i), p.tile(i), 0))
        return tile, q, k, v, br

    def cond_row(i):
        return jnp.where(i < dec.first, ctx.cond_row(i), dec.cond_row(i))

    tile_c, q_c, k_c, v_c, br_c = pass_specs(ctx)
    tile_d, q_d, k_d, v_d, br_d = pass_specs(dec)
    kc_spec = pl.BlockSpec((None, None, N_HEADS, past, LANES), lambda i: (l, dec.batch(i), 0, 0, 0))
    vc_spec = pl.BlockSpec((None, None, N_PAIRS, past, LANES), lambda i: (l, dec.batch(i), 0, 0, 0))
    in_specs = ([tile_c, q_c, k_c, v_c, br_c, br_c, br_c]
                + [tile_d, q_d, k_d, v_d, kc_spec, vc_spec, br_d, br_d, br_d]
                + [pl.BlockSpec((None, 1, 3 * D_MODEL), lambda i: (cond_row(i), 0, 0)),
                   _const_spec(l, (1, D_MODEL)),
                   _row_block_spec(_BACK_ROWS, 1),
                   _row_block_spec(3 * BRANCH_DIM, 0),
                   _row_block_spec(D_MODEL, 0),
                   _const_spec(l, (1, D_MODEL))])
    q_cx, k_cx, v_cx, ya_cx, yc_cx, sbz_cx = ctx_fr
    q_dx, k_dx, v_dx, ya_dx, yc_dx, sbz_dx = dec_fr
    args = [x_prompt, q_cx, k_cx, v_cx, ya_cx, yc_cx, sbz_cx,
            x_sample, q_dx, k_dx, v_dx, kc, vc, ya_dx, yc_dx, sbz_dx,
            mod_l, w["g_pre"], *big_bf, w["g_post"]]
    out_specs = [tile_c, tile_d]
    out_shape = [jax.ShapeDtypeStruct(x_prompt.shape, jnp.float32), jax.ShapeDtypeStruct(x_sample.shape, jnp.float32)]
    if next_big_f32 is not None:
        cast_in, cast_out, cast_shapes = _cast_specs(l + 1, n_steps)
        in_specs += cast_in
        args += list(next_big_f32)
        out_specs += cast_out
        out_shape += cast_shapes
    outs = pl.pallas_call(
        functools.partial(_back_kernel, dec.first, next_big_f32 is not None),
        grid=(n_steps,),
        in_specs=in_specs,
        out_specs=out_specs,
        out_shape=out_shape,
        compiler_params=_params(),
        name="back",
    )(*args)
    return outs[0], outs[1], tuple(outs[2:])


def _rope_swap_perm():
    quarter = QK_ROPE // 4
    idx = np.arange(QK_ROPE).reshape(2, 2, quarter)
    return idx[:, ::-1, :].reshape(-1)


def _rope_tables(n_tokens, rotate):
    if rotate:
        t = np.arange(n_tokens)
        pos = np.stack([t // GRID_W, t % GRID_W], axis=1).astype(np.float64)
        axis_dim = QK_ROPE // 2
        inv = 1.0 / (ROPE_THETA ** (np.arange(0, axis_dim, 2, dtype=np.float64) / axis_dim))
        ang = pos[:, :, None] * inv
        cos, sin = np.cos(ang), np.sin(ang)
        cfull = np.stack([cos, cos], axis=2).reshape(n_tokens, QK_ROPE)
        sfull = np.stack([-sin, sin], axis=2).reshape(n_tokens, QK_ROPE)
    else:
        cfull = np.ones((n_tokens, QK_ROPE))
        sfull = np.zeros((n_tokens, QK_ROPE))
    scale = (QK_NOPE + QK_ROPE) ** -0.5 * np.log2(np.e)
    qtab = np.concatenate([np.full((n_tokens, QK_NOPE), scale), cfull * scale, sfull * scale], axis=1)
    pad = np.zeros((n_tokens, LANES - QK_ROPE))
    tabs = (qtab, np.concatenate([cfull, pad], axis=1), np.concatenate([sfull, pad], axis=1))
    return tuple(jnp.asarray(tab, jnp.float32) for tab in tabs)


def _prepare_weights(g_pre, g_post, conv_w, conv_b, g_q, w_uq, g_kv, w_ukv, pool_w, pool_scale):
    bf = jnp.bfloat16
    perm = _rope_swap_perm()

    uq = w_uq.reshape(DEPTH, Q_LORA, N_HEADS, QK_NOPE + QK_ROPE)
    rope_q = uq[..., QK_NOPE:]
    wuq = jnp.concatenate([uq, rope_q[..., perm]], axis=-1).reshape(DEPTH, Q_LORA, HP).astype(bf)

    ukv = w_ukv.reshape(DEPTH, KV_LORA, N_HEADS, QK_NOPE + V_HEAD)
    wk = jnp.pad(ukv[..., :QK_NOPE], ((0, 0), (0, 0), (0, 0), (0, HEAD_PAD - QK_NOPE)))
    wk = wk.reshape(DEPTH, KV_LORA, HP).astype(bf)
    wv = ukv[..., QK_NOPE:].reshape(DEPTH, KV_LORA, N_HEADS * V_HEAD).astype(bf)

    pw = pool_w.astype(bf).reshape(DEPTH, len(POOL_WINDOWS) // 2, 2, POOL_GROUP, POOL_GROUP)
    zero = jnp.zeros_like(pw[:, :, 0])
    pool_bd = jnp.concatenate([jnp.concatenate([pw[:, :, 0], zero], axis=-1),
                               jnp.concatenate([zero, pw[:, :, 1]], axis=-1)], axis=-2)

    return {
        "g_pre": g_pre.reshape(DEPTH, 1, D_MODEL), "g_post": g_post.reshape(DEPTH, 1, D_MODEL),
        "conv_w": conv_w, "conv_b": conv_b.reshape(DEPTH, 1, BRANCH_DIM),
        "g_q": g_q.reshape(DEPTH, 1, Q_LORA), "wuq": wuq, "g_kv": g_kv.reshape(DEPTH, 1, KV_LORA),
        "wk": wk, "wv": wv, "pool_w": pool_bd,
        "pool_scale": pool_scale.reshape(DEPTH, 1, BRANCH_DIM),
    }


def kernel(x_prompt, x_sample, cache_mla_latent, c, c_ctx, w_mod, b_mod, g_pre, g_post, w_in, conv_w, conv_b,
           g_q, w_uq, g_kv, w_ukv, pool_w, pool_scale, w_branch, w_o):
    n_dec = x_sample.shape[0]
    assert 1 + n_dec <= COND_ROWS
    assert w_in.shape == (DEPTH, D_MODEL, W_IN_COLS)
    w = _prepare_weights(g_pre, g_post, conv_w, conv_b, g_q, w_uq, g_kv, w_ukv, pool_w, pool_scale)
    big_f32 = (jnp.swapaxes(w_in, 1, 2), w_branch.reshape(DEPTH, 3 * BRANCH_DIM, D_MODEL), w_o)
    big_bf = _cast_layer(0, big_f32)
    cond = jnp.concatenate([c_ctx[None, :], c, jnp.zeros((COND_ROWS - 1 - n_dec, D_MODEL), jnp.float32)], axis=0)
    mod = _modulation(cond, w_mod, b_mod).reshape(DEPTH, COND_ROWS, 1, 3 * D_MODEL)
    kc, vc = _cache_kv(jnp.swapaxes(cache_mla_latent, 2, 3), w["wk"], w["wv"])
    ctx_tabs = _rope_tables(x_prompt.shape[1], rotate=False)
    dec_tabs = _rope_tables(x_sample.shape[1], rotate=True)

    h, hs, state_t = x_prompt, x_sample, None
    for l in range(DEPTH):
        ctx_fr, dec_fr = _front(l, h, hs, mod[l], w, big_bf[0], ctx_tabs, dec_tabs, state_t)
        state_t = ctx_fr[6]
        h, hs, big_bf = _back(l, h, hs, mod[l], w, big_bf, ctx_fr[:6], dec_fr, kc, vc,
                              big_f32 if l + 1 < DEPTH else None)
    return (h, hs, jnp.swapaxes(state_t, 2, 3))
```

```python
import functools

import numpy as np
import jax
import jax.numpy as jnp
from jax import lax
from jax.experimental import pallas as pl
from jax.experimental.pallas import tpu as pltpu

D_MODEL = 1024
DEPTH = 2
GRID_W = 64
EPS = 1e-6
BRANCH_DIM = 512
N_HEADS = 8
QK_NOPE = 64
QK_ROPE = 32
V_HEAD = 64
Q_LORA = 384
KV_LORA = 256
MLA_LATENT = KV_LORA + QK_ROPE
ROPE_THETA = 10000.0
POOL_WINDOWS = (2, 4, 8, 16)
POOL_GROUP = 128

LANES = 128
HEAD_PAD = LANES
HP = N_HEADS * HEAD_PAD
N_PAIRS = N_HEADS * V_HEAD // LANES
FRONT_TILE = 512
BACK_TILE = 256
HALO = 16
COND_ROWS = 8
VMEM_LIMIT = 56 * 1024 * 1024

_R_AB = 0
_R_ACX = BRANCH_DIM
_R_MAIN = 3 * BRANCH_DIM
_MAIN_ROWS = BRANCH_DIM + Q_LORA + KV_LORA + LANES
_R_BZ = 4 * BRANCH_DIM + Q_LORA + MLA_LATENT
_R_CU = _R_BZ + BRANCH_DIM
_R_CZ = _R_CU + BRANCH_DIM
_R_MERGE = _R_CZ + BRANCH_DIM
W_IN_COLS = _R_MERGE + 3 * D_MODEL
_FRONT_ROWS = -(-_R_MERGE // LANES) * LANES
_BACK_ROWS = W_IN_COLS // 2
_QD0 = BRANCH_DIM
_CKV0 = _QD0 + Q_LORA
_KR0 = _CKV0 + KV_LORA
_LAT_EXT = KV_LORA + LANES
assert _FRONT_ROWS >= _R_MERGE and _BACK_ROWS <= _R_MERGE and _BACK_ROWS % 16 == 0


def _dot(a, b):
    return jnp.dot(a, b, preferred_element_type=jnp.float32)


def _dot_nt(a, b):
    return lax.dot_general(a, b, (((1,), (1,)), ((), ())), preferred_element_type=jnp.float32)


def _rms(x, g):
    return x * lax.rsqrt(jnp.mean(x * x, axis=-1, keepdims=True) + EPS) * g


def _silu(x):
    return x * jax.nn.sigmoid(x)


def _modulated_norm(x, g_pre, mod_row):
    shift = mod_row[:, 0:D_MODEL]
    scale = mod_row[:, D_MODEL:2 * D_MODEL]
    return _rms(x, g_pre) * (1.0 + scale) + shift


def _params(n_grid_dims=1):
    return pltpu.CompilerParams(dimension_semantics=("arbitrary",) * n_grid_dims, vmem_limit_bytes=VMEM_LIMIT)


def _mod_kernel(cond_ref, w_ref, b_ref, out_ref):
    h = _silu(cond_ref[...]).astype(jnp.bfloat16)
    out_ref[...] = _dot(h, w_ref[...].astype(jnp.bfloat16)) + b_ref[pl.ds(pl.program_id(0), 1), :]


def _modulation(cond, w_mod, b_mod):
    col_tile = D_MODEL
    return pl.pallas_call(
        _mod_kernel,
        grid=(DEPTH, 3 * D_MODEL // col_tile),
        in_specs=[
            pl.BlockSpec((COND_ROWS, D_MODEL), lambda l, n: (0, 0)),
            pl.BlockSpec((None, D_MODEL, col_tile), lambda l, n: (l, 0, n)),
            pl.BlockSpec((DEPTH, col_tile), lambda l, n: (0, n)),
        ],
        out_specs=pl.BlockSpec((None, COND_ROWS, col_tile), lambda l, n: (l, 0, n)),
        out_shape=jax.ShapeDtypeStruct((DEPTH, COND_ROWS, 3 * D_MODEL), jnp.float32),
        compiler_params=_params(2),
        name="modulation",
    )(cond, w_mod, b_mod)


_BIG_WEIGHT_ROWS = (W_IN_COLS, 3 * BRANCH_DIM, D_MODEL)


def _chunk_rows(rows, n_chunks):
    per = -(-rows // n_chunks)
    return -(-per // 16) * 16


def _cast_specs(layer, n_chunks):
    in_specs, out_specs, out_shapes = [], [], []
    for rows in _BIG_WEIGHT_ROWS:
        per = _chunk_rows(rows, n_chunks)
        last = pl.cdiv(rows, per) - 1
        in_specs.append(pl.BlockSpec((None, per, D_MODEL), lambda i, last=last: (layer, jnp.minimum(i, last), 0)))
        out_specs.append(pl.BlockSpec((per, D_MODEL), lambda i, last=last: (jnp.minimum(i, last), 0)))
        out_shapes.append(jax.ShapeDtypeStruct((rows, D_MODEL), jnp.bfloat16))
    return in_specs, out_specs, out_shapes


def _cast_chunks(in_refs, out_refs):
    for src, dst in zip(in_refs, out_refs):
        dst[...] = src[...].astype(jnp.bfloat16)


def _cast_kernel(*refs):
    n = len(_BIG_WEIGHT_ROWS)
    _cast_chunks(refs[:n], refs[n:])


def _cast_layer(layer, big_f32):
    n_chunks = 8
    in_specs, out_specs, out_shapes = _cast_specs(layer, n_chunks)
    return pl.pallas_call(
        _cast_kernel,
        grid=(n_chunks,),
        in_specs=in_specs,
        out_specs=out_specs,
        out_shape=out_shapes,
        compiler_params=_params(),
        name="cast_weights",
    )(*big_f32)


def _rope_key_block(kr):
    return pltpu.roll(kr, QK_NOPE, 1) + pltpu.roll(kr, QK_NOPE + QK_ROPE, 1)


def _cache_kv_kernel(lat_t_ref, wk_ref, wv_ref, k_ref, v_ref):
    lat_t = lat_t_ref[...]
    pad = jnp.zeros((_LAT_EXT - MLA_LATENT, lat_t.shape[1]), jnp.float32)
    lat = jnp.concatenate([lat_t, pad], axis=0).T
    ckv_bf = lat[:, 0:KV_LORA].astype(jnp.bfloat16)
    k_nope = _dot(ckv_bf, wk_ref[...])
    k_rope = _rope_key_block(lat[:, KV_LORA:_LAT_EXT])
    v = _dot(ckv_bf, wv_ref[...]).astype(jnp.bfloat16)
    for h in range(N_HEADS):
        k_ref[h] = (k_nope[:, h * HEAD_PAD:(h + 1) * HEAD_PAD] + k_rope).astype(jnp.bfloat16)
    for p in range(N_PAIRS):
        v_ref[p] = v[:, p * LANES:(p + 1) * LANES]


def _cache_kv(cache_t, wk, wv):
    nb, _, _, past = cache_t.shape

    def kv(n):
        return (jax.ShapeDtypeStruct((DEPTH, nb, n, past, LANES), jnp.bfloat16),
                pl.BlockSpec((None, None, n, past, LANES), lambda l, b: (l, b, 0, 0, 0)))

    (k_shape, k_spec), (v_shape, v_spec) = kv(N_HEADS), kv(N_PAIRS)
    return pl.pallas_call(
        _cache_kv_kernel,
        grid=(DEPTH, nb),
        in_specs=[
            pl.BlockSpec((None, None, MLA_LATENT, past), lambda l, b: (b, l, 0, 0)),
            pl.BlockSpec((None, KV_LORA, HP), lambda l, b: (l, 0, 0)),
            pl.BlockSpec((None, KV_LORA, N_HEADS * V_HEAD), lambda l, b: (l, 0, 0)),
        ],
        out_specs=[k_spec, v_spec],
        out_shape=[k_shape, v_shape],
        compiler_params=_params(2),
        name="cache_kv",
    )(cache_t, wk, wv)


class _Pass:
    def __init__(self, x, tile, first_step, first_cond_row, shared_cond):
        self.nb, self.seq, _ = x.shape
        self.tm = min(tile, self.seq)
        assert self.seq % self.tm == 0
        self.nt = self.seq // self.tm
        self.steps = self.nb * self.nt
        self.first = first_step
        self.first_cond_row = first_cond_row
        self.shared_cond = shared_cond

    def local(self, i):
        return jnp.clip(i - self.first, 0, self.steps - 1)

    def batch(self, i):
        return self.local(i) // self.nt

    def tile(self, i):
        return self.local(i) % self.nt

    def cond_row(self, i):
        return self.first_cond_row if self.shared_cond else self.first_cond_row + self.batch(i)


def _passes(x_prompt, x_sample, tile):
    ctx = _Pass(x_prompt, tile, 0, 0, True)
    dec = _Pass(x_sample, tile, ctx.steps, 1, False)
    return ctx, dec


def _const_spec(l, shape):
    return pl.BlockSpec((None,) + shape, lambda i: (l,) + (0,) * len(shape), pipeline_mode=pl.Buffered(1))


def _whole_spec(arr):
    return pl.BlockSpec(arr.shape, lambda i: (0,) * arr.ndim, pipeline_mode=pl.Buffered(1))


def _row_block_spec(rows, block_index):
    return pl.BlockSpec((rows, D_MODEL), lambda i: (block_index, 0), pipeline_mode=pl.Buffered(1))


_N_FRONT_WEIGHTS = 11


def _front_tile(l, seq_len, j, x_ref, xp_ref, xn_ref, qtab_ref, kcos_ref, ksin_ref, mod_row, weights, outs, scratch):
    gpre_ref, w_ref, convw_ref, convb_ref, gq_ref, wuq_ref, gkv_ref, wk_ref, wv_ref, poolw_ref, pscale_ref = weights
    g_pre, conv_b, g_q, g_kv, p_scale = (r[l:l + 1, :] for r in (gpre_ref, convb_ref, gq_ref, gkv_ref, pscale_ref))
    conv_w = [convw_ref[tap, l:l + 1, :] for tap in range(3)]
    q_out, k_out, v_out, ya_out, yc_out, sbz_out = outs[:6]
    u_scr, cu_scr = scratch
    tm = x_ref.shape[0]
    last_j = seq_len // tm - 1

    if last_j == 0:
        hn = _modulated_norm(x_ref[...], g_pre, mod_row).astype(jnp.bfloat16)
        acx = _dot_nt(hn, w_ref[_R_ACX:_R_ACX + 2 * BRANCH_DIM, :])
        zeros = jnp.zeros((HALO, BRANCH_DIM), jnp.float32)
        for scr in (u_scr, cu_scr):
            scr[0:HALO, :] = zeros
            scr[HALO + tm:2 * HALO + tm, :] = zeros
        u_scr[HALO:HALO + tm, :] = acx[:, 0:BRANCH_DIM] * acx[:, BRANCH_DIM:2 * BRANCH_DIM]
        cu_scr[HALO:HALO + tm, :] = _dot_nt(hn, w_ref[_R_CU:_R_CU + BRANCH_DIM, :])
    else:
        x_ext = jnp.concatenate([xp_ref[...], x_ref[...], xn_ref[...]], axis=0)
        hn_ext = _modulated_norm(x_ext, g_pre, mod_row).astype(jnp.bfloat16)
        hn = hn_ext[HALO:HALO + tm]
        acx = _dot_nt(hn_ext, w_ref[_R_ACX:_R_ACX + 2 * BRANCH_DIM, :])
        row = lax.broadcasted_iota(jnp.int32, (tm + 2 * HALO, 1), 0)
        first_valid = jnp.where(j > 0, 0, HALO)
        end_valid = jnp.where(j < last_j, tm + 2 * HALO, tm + HALO)
        valid = jnp.logical_and(row >= first_valid, row < end_valid)
        ext_rows = slice(0, tm + 2 * HALO)
        u_scr[ext_rows, :] = jnp.where(valid, acx[:, 0:BRANCH_DIM] * acx[:, BRANCH_DIM:2 * BRANCH_DIM], 0.0)
        cu_scr[ext_rows, :] = jnp.where(valid, _dot_nt(hn_ext, w_ref[_R_CU:_R_CU + BRANCH_DIM, :]), 0.0)

    pm = _dot_nt(hn, w_ref[_R_MAIN:_R_MAIN + _MAIN_ROWS, :])
    conv = (u_scr[HALO - 1:HALO - 1 + tm, :] * conv_w[0]
            + u_scr[HALO:HALO + tm, :] * conv_w[1]
            + u_scr[HALO + 1:HALO + 1 + tm, :] * conv_w[2]
            + conv_b)
    ya = _silu(pm[:, 0:BRANCH_DIM]) * (_dot_nt(hn, w_ref[_R_AB:_R_AB + BRANCH_DIM, :]) * conv)
    ya_out[...] = ya.astype(jnp.bfloat16)

    t = j * tm + lax.broadcasted_iota(jnp.int32, (tm, POOL_GROUP), 0)
    pooled = []
    for gi, win in enumerate(POOL_WINDOWS):
        half = win // 2
        cols = slice(gi * POOL_GROUP, (gi + 1) * POOL_GROUP)
        total = cu_scr[HALO - half:HALO - half + tm, cols]
        for k in range(-half + 1, half):
            total = total + cu_scr[HALO + k:HALO + k + tm, cols]
        count = jnp.minimum(t + half, seq_len) - jnp.maximum(t - half, 0)
        pooled.append((total / count.astype(jnp.float32) - cu_scr[HALO:HALO + tm, cols]).astype(jnp.bfloat16))
    mixed = [_dot(jnp.concatenate(pooled[2 * n:2 * n + 2], axis=1), poolw_ref[n]) for n in range(len(pooled) // 2)]
    mixed = jnp.concatenate(mixed, axis=1) * p_scale
    c_z = _dot_nt(hn, w_ref[_R_CZ:_R_CZ + BRANCH_DIM, :])
    yc_out[...] = (_silu(c_z) * mixed).astype(jnp.bfloat16)

    qn = _rms(pm[:, _QD0:_QD0 + Q_LORA], g_q).astype(jnp.bfloat16)
    q = _dot(qn, wuq_ref[...])
    qtab = qtab_ref[...]
    ckv = _rms(pm[:, _CKV0:_CKV0 + KV_LORA], g_kv)
    kr = pm[:, _KR0:_KR0 + LANES]
    if len(outs) > 6:
        lat_out = outs[6]
        if len(lat_out.shape) == 3:
            for later in range(1, lat_out.shape[0]):
                lat_out[later] = jnp.zeros(lat_out.shape[1:], jnp.float32)
            lat_out = lat_out.at[0]
        lat_out[0:KV_LORA, :] = ckv.T
        lat_out[KV_LORA:MLA_LATENT, :] = kr.T[0:QK_ROPE, :]
    lane = lax.broadcasted_iota(jnp.int32, (tm, LANES), 1)
    quarter = QK_ROPE // 4
    partner = jnp.where(lane % (2 * quarter) < quarter,
                        pltpu.roll(kr, LANES - quarter, 1), pltpu.roll(kr, quarter, 1))
    kr_rot = jnp.where(lane < QK_ROPE, kr * kcos_ref[...] + partner * ksin_ref[...], 0.0)
    ckv_bf = ckv.astype(jnp.bfloat16)
    k_nope = _dot(ckv_bf, wk_ref[...])
    k_rope = _rope_key_block(kr_rot)
    v = _dot(ckv_bf, wv_ref[...]).astype(jnp.bfloat16)
    for h in range(N_HEADS):
        cols = slice(h * HEAD_PAD, (h + 1) * HEAD_PAD)
        q_out[h] = (q[:, cols] * qtab).astype(jnp.bfloat16)
        k_out[h] = (k_nope[:, cols] + k_rope).astype(jnp.bfloat16)
    for p in range(N_PAIRS):
        v_out[p] = v[:, p * LANES:(p + 1) * LANES]
    sbz_out[...] = _silu(_dot_nt(hn, w_ref[_R_BZ:_R_BZ + BRANCH_DIM, :])).astype(jnp.bfloat16)


def _mod_row(mod_ref, ctx, dec, i):
    row = jnp.where(i < dec.first, ctx.cond_row(i), dec.cond_row(i))
    return mod_ref[pl.ds(row, 1), :]


def _front_kernel(l, ctx, dec, n_aliased, *refs):
    n_in = 6
    ctx_in, dec_in = refs[0:n_in], refs[n_in:2 * n_in]
    mod_ref = refs[2 * n_in]
    weights = refs[2 * n_in + 1:2 * n_in + 1 + _N_FRONT_WEIGHTS]
    outs = refs[2 * n_in + 1 + _N_FRONT_WEIGHTS + n_aliased:-2]
    ctx_out, dec_out = outs[0:7], outs[7:13]
    scratch = refs[-2:]
    i = pl.program_id(0)

    @pl.when(i < dec.first)
    def _():
        _front_tile(l, ctx.seq, ctx.tile(i), *ctx_in, _mod_row(mod_ref, ctx, dec, i), weights, ctx_out, scratch)

    @pl.when(i >= dec.first)
    def _():
        _front_tile(l, dec.seq, dec.tile(i), *dec_in, _mod_row(mod_ref, ctx, dec, i), weights, dec_out, scratch)


def _front(l, x_prompt, x_sample, mod, w, w_in_t, ctx_tabs, dec_tabs, state_t):
    ctx, dec = _passes(x_prompt, x_sample, FRONT_TILE)

    def pass_in_specs(p):
        tm = p.tm
        hb = tm // HALO
        n_hblk = p.seq // HALO
        tab = pl.BlockSpec((tm, LANES), lambda i: (p.tile(i), 0))
        return [
            pl.BlockSpec((None, tm, D_MODEL), lambda i: (p.batch(i), p.tile(i), 0)),
            pl.BlockSpec((None, HALO, D_MODEL), lambda i: (p.batch(i), jnp.maximum(p.tile(i) * hb - 1, 0), 0)),
            pl.BlockSpec((None, HALO, D_MODEL),
                         lambda i: (p.batch(i), jnp.minimum((p.tile(i) + 1) * hb, n_hblk - 1), 0)),
            tab, tab, tab,
        ]

    def pass_out(p, with_latent):
        tm = p.tm

        def blocks(n):
            return (jax.ShapeDtypeStruct((p.nb, n, p.seq, LANES), jnp.bfloat16),
                    pl.BlockSpec((None, n, tm, LANES), lambda i: (p.batch(i), 0, p.tile(i), 0)))

        (head_shape, head_spec), (pair_shape, pair_spec) = blocks(N_HEADS), blocks(N_PAIRS)
        br_shape = jax.ShapeDtypeStruct((p.nb, p.seq, BRANCH_DIM), jnp.bfloat16)
        br_spec = pl.BlockSpec((None, tm, BRANCH_DIM), lambda i: (p.batch(i), p.tile(i), 0))
        shapes = [head_shape, head_shape, pair_shape] + [br_shape] * 3
        specs = [head_spec, head_spec, pair_spec] + [br_spec] * 3
        if with_latent:
            shapes.append(jax.ShapeDtypeStruct((p.nb, DEPTH, MLA_LATENT, p.seq), jnp.float32))
            if state_t is None:
                specs.append(pl.BlockSpec((None, DEPTH, MLA_LATENT, tm), lambda i: (p.batch(i), 0, 0, p.tile(i))))
            else:
                specs.append(pl.BlockSpec((None, None, MLA_LATENT, tm), lambda i: (p.batch(i), l, 0, p.tile(i))))
        return shapes, specs

    weight_specs = [
        _whole_spec(w["g_pre"]),
        _row_block_spec(_FRONT_ROWS, 0),
        _whole_spec(w["conv_w"]),
        _whole_spec(w["conv_b"]),
        _whole_spec(w["g_q"]),
        _const_spec(l, (Q_LORA, HP)),
        _whole_spec(w["g_kv"]),
        _const_spec(l, (KV_LORA, HP)),
        _const_spec(l, (KV_LORA, N_HEADS * V_HEAD)),
        _const_spec(l, (len(POOL_WINDOWS) // 2, 2 * POOL_GROUP, 2 * POOL_GROUP)),
        _whole_spec(w["pool_scale"]),
    ]
    assert len(weight_specs) == _N_FRONT_WEIGHTS
    ctx_shapes, ctx_specs = pass_out(ctx, True)
    dec_shapes, dec_specs = pass_out(dec, False)
    in_specs = pass_in_specs(ctx) + pass_in_specs(dec) + [_const_spec(l, mod.shape[1:])] + weight_specs
    args = [x_prompt, x_prompt, x_prompt, *ctx_tabs, x_sample, x_sample, x_sample, *dec_tabs, mod,
            w["g_pre"], w_in_t, w["conv_w"], w["conv_b"], w["g_q"], w["wuq"], w["g_kv"],
            w["wk"], w["wv"], w["pool_w"], w["pool_scale"]]
    aliases = {}
    if state_t is not None:
        aliases = {len(args): len(ctx_shapes) - 1}
        in_specs.append(pl.BlockSpec(memory_space=pl.ANY))
        args.append(state_t)
    outs = pl.pallas_call(
        functools.partial(_front_kernel, l, ctx, dec, len(aliases)),
        grid=(ctx.steps + dec.steps,),
        in_specs=in_specs,
        out_specs=ctx_specs + dec_specs,
        out_shape=ctx_shapes + dec_shapes,
        scratch_shapes=[pltpu.VMEM((max(ctx.tm, dec.tm) + 2 * HALO, BRANCH_DIM), jnp.float32)] * 2,
        input_output_aliases=aliases,
        compiler_params=_params(),
        name="front",
    )(*args)
    return outs[:7], outs[7:]


_N_BACK_WEIGHTS = 5


def _back_tile(l, x_ref, q_ref, k_ref, v_ref, cache, ya_ref, yc_ref, sbz_ref, mod_row, weights, out_ref):
    gpre_ref, wg_ref, wbr_ref, wo_ref, gpost_ref = weights
    g_pre, g_post = gpre_ref[l:l + 1, :], gpost_ref[l:l + 1, :]
    x = x_ref[...]

    def with_ones(v):
        return jnp.concatenate([v, jnp.ones_like(v)], axis=1)

    low_half = lax.broadcasted_iota(jnp.int32, (x.shape[0], LANES), 1) < V_HEAD
    pairs = []
    for p in range(N_PAIRS):
        v_ext = with_ones(v_ref[p])
        if cache is not None:
            kc_ref, vc_ref = cache
            vc_ext = with_ones(vc_ref[p])
        halves = []
        for h in (2 * p, 2 * p + 1):
            q = q_ref[h]
            s = _dot_nt(q, k_ref[h])
            m = jnp.max(s, axis=-1, keepdims=True)
            if cache is not None:
                sc = _dot_nt(q, kc_ref[h])
                m = jnp.maximum(m, jnp.max(sc, axis=-1, keepdims=True))
            o = _dot(jnp.exp2((s - m).astype(jnp.bfloat16)), v_ext)
            if cache is not None:
                o = o + _dot(jnp.exp2((sc - m).astype(jnp.bfloat16)), vc_ext)
            halves.append(o[:, 0:LANES] / o[:, LANES:2 * LANES])
        pairs.append(jnp.where(low_half, halves[0], halves[1]))
    attn = jnp.concatenate(pairs, axis=1)
    yb = (sbz_ref[...].astype(jnp.float32) * attn).astype(jnp.bfloat16)

    hn = _modulated_norm(x, g_pre, mod_row).astype(jnp.bfloat16)
    merged = None
    for n, y in enumerate((ya_ref[...], yb, yc_ref[...])):
        r0 = _R_MERGE - _BACK_ROWS + n * D_MODEL
        gate = jax.nn.sigmoid(_dot_nt(hn, wg_ref[r0:r0 + D_MODEL, :]))
        term = gate * _dot(y, wbr_ref[n * BRANCH_DIM:(n + 1) * BRANCH_DIM, :])
        merged = term if merged is None else merged + term
    out = _rms(_dot(merged.astype(jnp.bfloat16), wo_ref[...]), g_post)
    out_ref[...] = x + mod_row[:, 2 * D_MODEL:3 * D_MODEL] * out


def _back_kernel(l, ctx, dec, cast_next, *refs):
    x_c, q_c, k_c, v_c, ya_c, yc_c, sbz_c = refs[0:7]
    x_d, q_d, k_d, v_d, kc_ref, vc_ref, ya_d, yc_d, sbz_d = refs[7:16]
    mod_ref = refs[16]
    n_w = 17 + _N_BACK_WEIGHTS
    weights = refs[17:n_w]
    n_cast = len(_BIG_WEIGHT_ROWS) if cast_next else 0
    out_c, out_d = refs[n_w + n_cast:n_w + n_cast + 2]
    i = pl.program_id(0)
    if cast_next:
        _cast_chunks(refs[n_w:n_w + n_cast], refs[n_w + n_cast + 2:])

    @pl.when(i < dec.first)
    def _():
        _back_tile(l, x_c, q_c, k_c, v_c, None, ya_c, yc_c, sbz_c, _mod_row(mod_ref, ctx, dec, i), weights, out_c)

    @pl.when(i >= dec.first)
    def _():
        _back_tile(l, x_d, q_d, k_d, v_d, (kc_ref, vc_ref), ya_d, yc_d, sbz_d, _mod_row(mod_ref, ctx, dec, i),
                   weights, out_d)


def _back(l, x_prompt, x_sample, mod, w, big_bf, ctx_fr, dec_fr, kc, vc, next_big_f32):
    ctx, dec = _passes(x_prompt, x_sample, BACK_TILE)
    past = kc.shape[3]
    n_steps = ctx.steps + dec.steps

    def pass_specs(p):
        tq = p.tm
        tile = pl.BlockSpec((None, tq, D_MODEL), lambda i: (p.batch(i), p.tile(i), 0))
        q = pl.BlockSpec((None, N_HEADS, tq, HEAD_PAD), lambda i: (p.batch(i), 0, p.tile(i), 0))
        k = pl.BlockSpec((None, N_HEADS, p.seq, LANES), lambda i: (p.batch(i), 0, 0, 0))
        v = pl.BlockSpec((None, N_PAIRS, p.seq, LANES), lambda i: (p.batch(i), 0, 0, 0))
        br = pl.BlockSpec((None, tq, BRANCH_DIM), lambda i: (p.batch(i), p.tile(i), 0))
        return tile, q, k, v, br

    tile_c, q_c, k_c, v_c, br_c = pass_specs(ctx)
    tile_d, q_d, k_d, v_d, br_d = pass_specs(dec)
    kc_spec = pl.BlockSpec((None, None, N_HEADS, past, LANES), lambda i: (l, dec.batch(i), 0, 0, 0))
    vc_spec = pl.BlockSpec((None, None, N_PAIRS, past, LANES), lambda i: (l, dec.batch(i), 0, 0, 0))
    in_specs = ([tile_c, q_c, k_c, v_c, br_c, br_c, br_c]
                + [tile_d, q_d, k_d, v_d, kc_spec, vc_spec, br_d, br_d, br_d]
                + [_const_spec(l, mod.shape[1:]),
                   _whole_spec(w["g_pre"]),
                   _row_block_spec(_BACK_ROWS, 1),
                   _row_block_spec(3 * BRANCH_DIM, 0),
                   _row_block_spec(D_MODEL, 0),
                   _whole_spec(w["g_post"])])
    q_cx, k_cx, v_cx, ya_cx, yc_cx, sbz_cx = ctx_fr
    q_dx, k_dx, v_dx, ya_dx, yc_dx, sbz_dx = dec_fr
    args = [x_prompt, q_cx, k_cx, v_cx, ya_cx, yc_cx, sbz_cx,
            x_sample, q_dx, k_dx, v_dx, kc, vc, ya_dx, yc_dx, sbz_dx,
            mod, w["g_pre"], *big_bf, w["g_post"]]
    out_specs = [tile_c, tile_d]
    out_shape = [jax.ShapeDtypeStruct(x_prompt.shape, jnp.float32), jax.ShapeDtypeStruct(x_sample.shape, jnp.float32)]
    if next_big_f32 is not None:
        cast_in, cast_out, cast_shapes = _cast_specs(l + 1, n_steps)
        in_specs += cast_in
        args += list(next_big_f32)
        out_specs += cast_out
        out_shape += cast_shapes
    outs = pl.pallas_call(
        functools.partial(_back_kernel, l, ctx, dec, next_big_f32 is not None),
        grid=(n_steps,),
        in_specs=in_specs,
        out_specs=out_specs,
        out_shape=out_shape,
        compiler_params=_params(),
        name="back",
    )(*args)
    return outs[0], outs[1], tuple(outs[2:])


def _rope_swap_perm():
    quarter = QK_ROPE // 4
    idx = np.arange(QK_ROPE).reshape(2, 2, quarter)
    return idx[:, ::-1, :].reshape(-1)


def _rope_tables(n_tokens, rotate):
    if rotate:
        t = np.arange(n_tokens)
        pos = np.stack([t // GRID_W, t % GRID_W], axis=1).astype(np.float64)
        axis_dim = QK_ROPE // 2
        inv = 1.0 / (ROPE_THETA ** (np.arange(0, axis_dim, 2, dtype=np.float64) / axis_dim))
        ang = pos[:, :, None] * inv
        cos, sin = np.cos(ang), np.sin(ang)
        cfull = np.stack([cos, cos], axis=2).reshape(n_tokens, QK_ROPE)
        sfull = np.stack([-sin, sin], axis=2).reshape(n_tokens, QK_ROPE)
    else:
        cfull = np.ones((n_tokens, QK_ROPE))
        sfull = np.zeros((n_tokens, QK_ROPE))
    scale = (QK_NOPE + QK_ROPE) ** -0.5 * np.log2(np.e)
    qtab = np.concatenate([np.full((n_tokens, QK_NOPE), scale), cfull * scale, sfull * scale], axis=1)
    pad = np.zeros((n_tokens, LANES - QK_ROPE))
    tabs = (qtab, np.concatenate([cfull, pad], axis=1), np.concatenate([sfull, pad], axis=1))
    return tuple(jnp.asarray(tab, jnp.float32) for tab in tabs)


def _prepare_weights(g_pre, g_post, conv_w, conv_b, g_q, w_uq, g_kv, w_ukv, pool_w, pool_scale):
    bf = jnp.bfloat16
    perm = _rope_swap_perm()

    uq = w_uq.reshape(DEPTH, Q_LORA, N_HEADS, QK_NOPE + QK_ROPE)
    rope_q = uq[..., QK_NOPE:]
    wuq = jnp.concatenate([uq, rope_q[..., perm]], axis=-1).reshape(DEPTH, Q_LORA, HP).astype(bf)

    ukv = w_ukv.reshape(DEPTH, KV_LORA, N_HEADS, QK_NOPE + V_HEAD)
    wk = jnp.pad(ukv[..., :QK_NOPE], ((0, 0), (0, 0), (0, 0), (0, HEAD_PAD - QK_NOPE)))
    wk = wk.reshape(DEPTH, KV_LORA, HP).astype(bf)
    wv = ukv[..., QK_NOPE:].reshape(DEPTH, KV_LORA, N_HEADS * V_HEAD).astype(bf)

    pw = pool_w.astype(bf).reshape(DEPTH, len(POOL_WINDOWS) // 2, 2, POOL_GROUP, POOL_GROUP)
    zero = jnp.zeros_like(pw[:, :, 0])
    pool_bd = jnp.concatenate([jnp.concatenate([pw[:, :, 0], zero], axis=-1),
                               jnp.concatenate([zero, pw[:, :, 1]], axis=-1)], axis=-2)

    return {
        "g_pre": g_pre, "g_post": g_post, "conv_w": jnp.swapaxes(conv_w, 0, 1), "conv_b": conv_b,
        "g_q": g_q, "g_kv": g_kv, "pool_scale": pool_scale,
        "wuq": wuq, "wk": wk, "wv": wv, "pool_w": pool_bd,
    }


def kernel(x_prompt, x_sample, cache_mla_latent, c, c_ctx, w_mod, b_mod, g_pre, g_post, w_in, conv_w, conv_b,
           g_q, w_uq, g_kv, w_ukv, pool_w, pool_scale, w_branch, w_o):
    n_dec = x_sample.shape[0]
    assert 1 + n_dec <= COND_ROWS
    assert w_in.shape == (DEPTH, D_MODEL, W_IN_COLS)
    w = _prepare_weights(g_pre, g_post, conv_w, conv_b, g_q, w_uq, g_kv, w_ukv, pool_w, pool_scale)
    big_f32 = (jnp.swapaxes(w_in, 1, 2), w_branch.reshape(DEPTH, 3 * BRANCH_DIM, D_MODEL), w_o)
    big_bf = _cast_layer(0, big_f32)
    cond = jnp.concatenate([c_ctx[None, :], c, jnp.zeros((COND_ROWS - 1 - n_dec, D_MODEL), jnp.float32)], axis=0)
    mod = _modulation(cond, w_mod, b_mod)
    kc, vc = _cache_kv(jnp.swapaxes(cache_mla_latent, 2, 3), w["wk"], w["wv"])
    ctx_tabs = _rope_tables(x_prompt.shape[1], rotate=False)
    dec_tabs = _rope_tables(x_sample.shape[1], rotate=True)

    h, hs, state_t = x_prompt, x_sample, None
    for l in range(DEPTH):
        ctx_fr, dec_fr = _front(l, h, hs, mod, w, big_bf[0], ctx_tabs, dec_tabs, state_t)
        state_t = ctx_fr[6]
        h, hs, big_bf = _back(l, h, hs, mod, w, big_bf, ctx_fr[:6], dec_fr, kc, vc,
                              big_f32 if l + 1 < DEPTH else None)
    return (h, hs, jnp.swapaxes(state_t, 2, 3))
```

```python
import functools

import numpy as np
import jax
import jax.numpy as jnp
from jax import lax
from jax.experimental import pallas as pl
from jax.experimental.pallas import tpu as pltpu

D_MODEL = 1024
DEPTH = 2
GRID_W = 64
EPS = 1e-6
BRANCH_DIM = 512
N_HEADS = 8
QK_NOPE = 64
QK_ROPE = 32
V_HEAD = 64
Q_LORA = 384
KV_LORA = 256
MLA_LATENT = KV_LORA + QK_ROPE
ROPE_THETA = 10000.0
POOL_WINDOWS = (2, 4, 8, 16)
POOL_GROUP = 128

LANES = 128
HEAD_PAD = LANES
HP = N_HEADS * HEAD_PAD
N_PAIRS = N_HEADS * V_HEAD // LANES
FRONT_TILE = 512
BACK_TILE = 256
HALO = 16
COND_ROWS = 8
VMEM_LIMIT = 56 * 1024 * 1024

_R_AB = 0
_R_ACX = BRANCH_DIM
_R_MAIN = 3 * BRANCH_DIM
_MAIN_ROWS = BRANCH_DIM + Q_LORA + KV_LORA + LANES
_R_BZ = 4 * BRANCH_DIM + Q_LORA + MLA_LATENT
_R_CU = _R_BZ + BRANCH_DIM
_R_CZ = _R_CU + BRANCH_DIM
_R_MERGE = _R_CZ + BRANCH_DIM
W_IN_COLS = _R_MERGE + 3 * D_MODEL
_FRONT_ROWS = -(-_R_MERGE // LANES) * LANES
_QD0 = BRANCH_DIM
_CKV0 = _QD0 + Q_LORA
_KR0 = _CKV0 + KV_LORA
_LAT_EXT = KV_LORA + LANES


def _dot(a, b):
    return jnp.dot(a, b, preferred_element_type=jnp.float32)


def _dot_nt(a, b):
    return lax.dot_general(a, b, (((1,), (1,)), ((), ())), preferred_element_type=jnp.float32)


def _rms(x, g):
    return x * lax.rsqrt(jnp.mean(x * x, axis=-1, keepdims=True) + EPS) * g


def _silu(x):
    return x * jax.nn.sigmoid(x)


def _modulated_norm(x, g_pre, mod_row):
    shift = mod_row[:, 0:D_MODEL]
    scale = mod_row[:, D_MODEL:2 * D_MODEL]
    return _rms(x, g_pre) * (1.0 + scale) + shift


def _params(n_grid_dims=1):
    return pltpu.CompilerParams(dimension_semantics=("arbitrary",) * n_grid_dims, vmem_limit_bytes=VMEM_LIMIT)


def _mod_kernel(cond_ref, w_ref, b_ref, out_ref):
    h = _silu(cond_ref[...]).astype(jnp.bfloat16)
    out_ref[...] = _dot(h, w_ref[...].astype(jnp.bfloat16)) + b_ref[pl.ds(pl.program_id(0), 1), :]


def _modulation(cond, w_mod, b_mod):
    col_tile = D_MODEL
    return pl.pallas_call(
        _mod_kernel,
        grid=(DEPTH, 3 * D_MODEL // col_tile),
        in_specs=[
            pl.BlockSpec((COND_ROWS, D_MODEL), lambda l, n: (0, 0)),
            pl.BlockSpec((None, D_MODEL, col_tile), lambda l, n: (l, 0, n)),
            pl.BlockSpec((DEPTH, col_tile), lambda l, n: (0, n)),
        ],
        out_specs=pl.BlockSpec((None, COND_ROWS, col_tile), lambda l, n: (l, 0, n)),
        out_shape=jax.ShapeDtypeStruct((DEPTH, COND_ROWS, 3 * D_MODEL), jnp.float32),
        compiler_params=_params(2),
        name="modulation",
    )(cond, w_mod, b_mod)


_SRC_W_IN_T, _SRC_W_BRANCH, _SRC_W_O = 0, 1, 2
_BACK_START = (_R_MERGE // 1024) * 1024
_W_FRONT = (_SRC_W_IN_T, 0, _FRONT_ROWS)
_W_BACK = (_SRC_W_IN_T, _BACK_START, W_IN_COLS - _BACK_START)
_W_BRANCH = (_SRC_W_BRANCH, 0, 3 * BRANCH_DIM)
_W_OUT = (_SRC_W_O, 0, D_MODEL)
_BACK_ROWS = -(-_W_BACK[2] // LANES) * LANES


def _chunk_rows(first_row, rows, n_steps):
    if first_row == 0:
        return pl.cdiv(pl.cdiv(rows, n_steps), 16) * 16
    per = LANES
    while first_row % per or pl.cdiv(rows, per) > n_steps:
        per *= 2
        assert per <= first_row
    return per


def _cast_specs(layer, windows, n_steps):
    in_specs, out_specs, out_shapes = [], [], []
    for _, first_row, rows in windows:
        per = _chunk_rows(first_row, rows, n_steps)
        n_blocks = pl.cdiv(rows, per)
        first_blk = first_row // per

        def in_map(i, first_blk=first_blk, last=n_blocks - 1):
            return (layer, first_blk + jnp.minimum(i, last), 0)

        def out_map(i, last=n_blocks - 1):
            return (jnp.minimum(i, last), 0)

        in_specs.append(pl.BlockSpec((None, per, D_MODEL), in_map))
        out_specs.append(pl.BlockSpec((per, D_MODEL), out_map))
        out_shapes.append(jax.ShapeDtypeStruct((n_blocks * per, D_MODEL), jnp.bfloat16))
    return in_specs, out_specs, out_shapes


def _cast_chunks(in_refs, out_refs):
    for src, dst in zip(in_refs, out_refs):
        dst[...] = src[...].astype(jnp.bfloat16)


def _cast_kernel(src, dst):
    _cast_chunks([src], [dst])


def _cast_front_window(layer, big_f32):
    n_steps = 4
    in_specs, out_specs, out_shapes = _cast_specs(layer, [_W_FRONT], n_steps)
    return pl.pallas_call(
        _cast_kernel,
        grid=(n_steps,),
        in_specs=in_specs,
        out_specs=out_specs[0],
        out_shape=out_shapes[0],
        compiler_params=_params(),
        name="cast_weights",
    )(big_f32[_SRC_W_IN_T])


def _rope_key_block(kr):
    return pltpu.roll(kr, QK_NOPE, 1) + pltpu.roll(kr, QK_NOPE + QK_ROPE, 1)


def _cache_kv_kernel(lat_t_ref, wk_ref, wv_ref, k_ref, v_ref):
    lat_t = lat_t_ref[...]
    pad = jnp.zeros((_LAT_EXT - MLA_LATENT, lat_t.shape[1]), jnp.float32)
    lat = jnp.concatenate([lat_t, pad], axis=0).T
    ckv_bf = lat[:, 0:KV_LORA].astype(jnp.bfloat16)
    k_nope = _dot(ckv_bf, wk_ref[...])
    k_rope = _rope_key_block(lat[:, KV_LORA:_LAT_EXT])
    v = _dot(ckv_bf, wv_ref[...]).astype(jnp.bfloat16)
    for h in range(N_HEADS):
        k_ref[h] = (k_nope[:, h * HEAD_PAD:(h + 1) * HEAD_PAD] + k_rope).astype(jnp.bfloat16)
    for p in range(N_PAIRS):
        v_ref[p] = v[:, p * LANES:(p + 1) * LANES]


def _cache_kv(cache_t, wk, wv):
    nb, _, _, past = cache_t.shape

    def kv(n):
        return (jax.ShapeDtypeStruct((DEPTH, nb, n, past, LANES), jnp.bfloat16),
                pl.BlockSpec((None, None, n, past, LANES), lambda l, b: (l, b, 0, 0, 0)))

    (k_shape, k_spec), (v_shape, v_spec) = kv(N_HEADS), kv(N_PAIRS)
    return pl.pallas_call(
        _cache_kv_kernel,
        grid=(DEPTH, nb),
        in_specs=[
            pl.BlockSpec((None, None, MLA_LATENT, past), lambda l, b: (b, l, 0, 0)),
            pl.BlockSpec((None, KV_LORA, HP), lambda l, b: (l, 0, 0)),
            pl.BlockSpec((None, KV_LORA, N_HEADS * V_HEAD), lambda l, b: (l, 0, 0)),
        ],
        out_specs=[k_spec, v_spec],
        out_shape=[k_shape, v_shape],
        compiler_params=_params(2),
        name="cache_kv",
    )(cache_t, wk, wv)


class _Pass:
    def __init__(self, x, tile, first_step, first_cond_row, shared_cond):
        self.nb, self.seq, _ = x.shape
        self.tm = min(tile, self.seq)
        assert self.seq % self.tm == 0
        self.nt = self.seq // self.tm
        self.steps = self.nb * self.nt
        self.first = first_step
        self.first_cond_row = first_cond_row
        self.shared_cond = shared_cond

    def local(self, i):
        return jnp.clip(i - self.first, 0, self.steps - 1)

    def batch(self, i):
        return self.local(i) // self.nt

    def tile(self, i):
        return self.local(i) % self.nt

    def cond_row(self, i):
        return self.first_cond_row if self.shared_cond else self.first_cond_row + self.batch(i)


def _passes(x_prompt, x_sample, tile):
    ctx = _Pass(x_prompt, tile, 0, 0, True)
    dec = _Pass(x_sample, tile, ctx.steps, 1, False)
    return ctx, dec


def _const_spec(l, shape):
    return pl.BlockSpec((None,) + shape, lambda i: (l,) + (0,) * len(shape), pipeline_mode=pl.Buffered(1))


def _whole_spec(arr):
    return pl.BlockSpec(arr.shape, lambda i: (0,) * arr.ndim, pipeline_mode=pl.Buffered(1))


def _row_block_spec(rows, block_index):
    return pl.BlockSpec((rows, D_MODEL), lambda i: (block_index, 0), pipeline_mode=pl.Buffered(1))


_N_FRONT_WEIGHTS = 11


def _front_tile(l, seq_len, j, x_ref, xp_ref, xn_ref, qtab_ref, kcos_ref, ksin_ref, mod_row, weights, outs, scratch):
    gpre_ref, w_ref, convw_ref, convb_ref, gq_ref, wuq_ref, gkv_ref, wk_ref, wv_ref, poolw_ref, pscale_ref = weights
    g_pre, conv_b, g_q, g_kv, p_scale = (r[l:l + 1, :] for r in (gpre_ref, convb_ref, gq_ref, gkv_ref, pscale_ref))
    conv_w = [convw_ref[tap, l:l + 1, :] for tap in range(3)]
    q_out, k_out, v_out, ya_out, yc_out, sbz_out = outs[:6]
    u_scr, cu_scr = scratch
    tm = x_ref.shape[0]
    last_j = seq_len // tm - 1

    if last_j == 0:
        hn = _modulated_norm(x_ref[...], g_pre, mod_row).astype(jnp.bfloat16)
        acx = _dot_nt(hn, w_ref[_R_ACX:_R_ACX + 2 * BRANCH_DIM, :])
        zeros = jnp.zeros((HALO, BRANCH_DIM), jnp.float32)
        for scr in (u_scr, cu_scr):
            scr[0:HALO, :] = zeros
            scr[HALO + tm:2 * HALO + tm, :] = zeros
        u_scr[HALO:HALO + tm, :] = acx[:, 0:BRANCH_DIM] * acx[:, BRANCH_DIM:2 * BRANCH_DIM]
        cu_scr[HALO:HALO + tm, :] = _dot_nt(hn, w_ref[_R_CU:_R_CU + BRANCH_DIM, :])
    else:
        x_ext = jnp.concatenate([xp_ref[...], x_ref[...], xn_ref[...]], axis=0)
        hn_ext = _modulated_norm(x_ext, g_pre, mod_row).astype(jnp.bfloat16)
        hn = hn_ext[HALO:HALO + tm]
        acx = _dot_nt(hn_ext, w_ref[_R_ACX:_R_ACX + 2 * BRANCH_DIM, :])
        row = lax.broadcasted_iota(jnp.int32, (tm + 2 * HALO, 1), 0)
        first_valid = jnp.where(j > 0, 0, HALO)
        end_valid = jnp.where(j < last_j, tm + 2 * HALO, tm + HALO)
        valid = jnp.logical_and(row >= first_valid, row < end_valid)
        ext_rows = slice(0, tm + 2 * HALO)
        u_scr[ext_rows, :] = jnp.where(valid, acx[:, 0:BRANCH_DIM] * acx[:, BRANCH_DIM:2 * BRANCH_DIM], 0.0)
        cu_scr[ext_rows, :] = jnp.where(valid, _dot_nt(hn_ext, w_ref[_R_CU:_R_CU + BRANCH_DIM, :]), 0.0)

    pm = _dot_nt(hn, w_ref[_R_MAIN:_R_MAIN + _MAIN_ROWS, :])
    conv = (u_scr[HALO - 1:HALO - 1 + tm, :] * conv_w[0]
            + u_scr[HALO:HALO + tm, :] * conv_w[1]
            + u_scr[HALO + 1:HALO + 1 + tm, :] * conv_w[2]
            + conv_b)
    ya = _silu(pm[:, 0:BRANCH_DIM]) * (_dot_nt(hn, w_ref[_R_AB:_R_AB + BRANCH_DIM, :]) * conv)
    ya_out[...] = ya.astype(jnp.bfloat16)

    t = j * tm + lax.broadcasted_iota(jnp.int32, (tm, POOL_GROUP), 0)
    pooled = []
    for gi, win in enumerate(POOL_WINDOWS):
        half = win // 2
        cols = slice(gi * POOL_GROUP, (gi + 1) * POOL_GROUP)
        total = cu_scr[HALO - half:HALO - half + tm, cols]
        for k in range(-half + 1, half):
            total = total + cu_scr[HALO + k:HALO + k + tm, cols]
        count = jnp.minimum(t + half, seq_len) - jnp.maximum(t - half, 0)
        pooled.append((total / count.astype(jnp.float32) - cu_scr[HALO:HALO + tm, cols]).astype(jnp.bfloat16))
    mixed = [_dot(jnp.concatenate(pooled[2 * n:2 * n + 2], axis=1), poolw_ref[n]) for n in range(len(pooled) // 2)]
    mixed = jnp.concatenate(mixed, axis=1) * p_scale
    c_z = _dot_nt(hn, w_ref[_R_CZ:_R_CZ + BRANCH_DIM, :])
    yc_out[...] = (_silu(c_z) * mixed).astype(jnp.bfloat16)

    qn = _rms(pm[:, _QD0:_QD0 + Q_LORA], g_q).astype(jnp.bfloat16)
    q = _dot(qn, wuq_ref[...])
    qtab = qtab_ref[...]
    ckv = _rms(pm[:, _CKV0:_CKV0 + KV_LORA], g_kv)
    kr = pm[:, _KR0:_KR0 + LANES]
    if len(outs) > 6:
        lat_out = outs[6]
        if len(lat_out.shape) == 3:
            for later in range(1, lat_out.shape[0]):
                lat_out[later] = jnp.zeros(lat_out.shape[1:], jnp.float32)
            lat_out = lat_out.at[0]
        lat_out[0:KV_LORA, :] = ckv.T
        lat_out[KV_LORA:MLA_LATENT, :] = kr.T[0:QK_ROPE, :]
    lane = lax.broadcasted_iota(jnp.int32, (tm, LANES), 1)
    quarter = QK_ROPE // 4
    partner = jnp.where(lane % (2 * quarter) < quarter,
                        pltpu.roll(kr, LANES - quarter, 1), pltpu.roll(kr, quarter, 1))
    kr_rot = jnp.where(lane < QK_ROPE, kr * kcos_ref[...] + partner * ksin_ref[...], 0.0)
    ckv_bf = ckv.astype(jnp.bfloat16)
    k_nope = _dot(ckv_bf, wk_ref[...])
    k_rope = _rope_key_block(kr_rot)
    v = _dot(ckv_bf, wv_ref[...]).astype(jnp.bfloat16)
    for h in range(N_HEADS):
        cols = slice(h * HEAD_PAD, (h + 1) * HEAD_PAD)
        q_out[h] = (q[:, cols] * qtab).astype(jnp.bfloat16)
        k_out[h] = (k_nope[:, cols] + k_rope).astype(jnp.bfloat16)
    for p in range(N_PAIRS):
        v_out[p] = v[:, p * LANES:(p + 1) * LANES]
    sbz_out[...] = _silu(_dot_nt(hn, w_ref[_R_BZ:_R_BZ + BRANCH_DIM, :])).astype(jnp.bfloat16)


def _mod_row(mod_ref, ctx, dec, i):
    row = jnp.where(i < dec.first, ctx.cond_row(i), dec.cond_row(i))
    return mod_ref[pl.ds(row, 1), :]


def _front_kernel(l, ctx, dec, n_cast, n_aliased, *refs):
    n_in = 6
    ctx_in, dec_in = refs[0:n_in], refs[n_in:2 * n_in]
    mod_ref = refs[2 * n_in]
    n_w = 2 * n_in + 1 + _N_FRONT_WEIGHTS
    weights = refs[2 * n_in + 1:n_w]
    outs = refs[n_w + n_cast + n_aliased:-2]
    ctx_out, dec_out = outs[0:7], outs[7:13]
    scratch = refs[-2:]
    i = pl.program_id(0)
    _cast_chunks(refs[n_w:n_w + n_cast], outs[13:])

    @pl.when(i < dec.first)
    def _():
        _front_tile(l, ctx.seq, ctx.tile(i), *ctx_in, _mod_row(mod_ref, ctx, dec, i), weights, ctx_out, scratch)

    @pl.when(i >= dec.first)
    def _():
        _front_tile(l, dec.seq, dec.tile(i), *dec_in, _mod_row(mod_ref, ctx, dec, i), weights, dec_out, scratch)


def _front(l, x_prompt, x_sample, mod, w, w_front, ctx_tabs, dec_tabs, state_t, big_f32, cast_windows):
    ctx, dec = _passes(x_prompt, x_sample, FRONT_TILE)
    n_steps = ctx.steps + dec.steps

    def pass_in_specs(p):
        tm = p.tm
        hb = tm // HALO
        n_hblk = p.seq // HALO
        tab = pl.BlockSpec((tm, LANES), lambda i: (p.tile(i), 0))
        return [
            pl.BlockSpec((None, tm, D_MODEL), lambda i: (p.batch(i), p.tile(i), 0)),
            pl.BlockSpec((None, HALO, D_MODEL), lambda i: (p.batch(i), jnp.maximum(p.tile(i) * hb - 1, 0), 0)),
            pl.BlockSpec((None, HALO, D_MODEL),
                         lambda i: (p.batch(i), jnp.minimum((p.tile(i) + 1) * hb, n_hblk - 1), 0)),
            tab, tab, tab,
        ]

    def pass_out(p, with_latent):
        tm = p.tm

        def blocks(n):
            return (jax.ShapeDtypeStruct((p.nb, n, p.seq, LANES), jnp.bfloat16),
                    pl.BlockSpec((None, n, tm, LANES), lambda i: (p.batch(i), 0, p.tile(i), 0)))

        (head_shape, head_spec), (pair_shape, pair_spec) = blocks(N_HEADS), blocks(N_PAIRS)
        br_shape = jax.ShapeDtypeStruct((p.nb, p.seq, BRANCH_DIM), jnp.bfloat16)
        br_spec = pl.BlockSpec((None, tm, BRANCH_DIM), lambda i: (p.batch(i), p.tile(i), 0))
        shapes = [head_shape, head_shape, pair_shape] + [br_shape] * 3
        specs = [head_spec, head_spec, pair_spec] + [br_spec] * 3
        if with_latent:
            shapes.append(jax.ShapeDtypeStruct((p.nb, DEPTH, MLA_LATENT, p.seq), jnp.float32))
            if state_t is None:
                specs.append(pl.BlockSpec((None, DEPTH, MLA_LATENT, tm), lambda i: (p.batch(i), 0, 0, p.tile(i))))
            else:
                specs.append(pl.BlockSpec((None, None, MLA_LATENT, tm), lambda i: (p.batch(i), l, 0, p.tile(i))))
        return shapes, specs

    weight_specs = [
        _whole_spec(w["g_pre"]),
        _row_block_spec(_FRONT_ROWS, 0),
        _whole_spec(w["conv_w"]),
        _whole_spec(w["conv_b"]),
        _whole_spec(w["g_q"]),
        _const_spec(l, (Q_LORA, HP)),
        _whole_spec(w["g_kv"]),
        _const_spec(l, (KV_LORA, HP)),
        _const_spec(l, (KV_LORA, N_HEADS * V_HEAD)),
        _const_spec(l, (len(POOL_WINDOWS) // 2, 2 * POOL_GROUP, 2 * POOL_GROUP)),
        _whole_spec(w["pool_scale"]),
    ]
    assert len(weight_specs) == _N_FRONT_WEIGHTS
    ctx_shapes, ctx_specs = pass_out(ctx, True)
    dec_shapes, dec_specs = pass_out(dec, False)
    in_specs = pass_in_specs(ctx) + pass_in_specs(dec) + [_const_spec(l, mod.shape[1:])] + weight_specs
    args = [x_prompt, x_prompt, x_prompt, *ctx_tabs, x_sample, x_sample, x_sample, *dec_tabs, mod,
            w["g_pre"], w_front, w["conv_w"], w["conv_b"], w["g_q"], w["wuq"], w["g_kv"],
            w["wk"], w["wv"], w["pool_w"], w["pool_scale"]]
    cast_in, cast_out, cast_shapes = _cast_specs(l, cast_windows, n_steps)
    in_specs += cast_in
    args += [big_f32[src] for src, _, _ in cast_windows]
    aliases = {}
    if state_t is not None:
        aliases = {len(args): len(ctx_shapes) - 1}
        in_specs.append(pl.BlockSpec(memory_space=pl.ANY))
        args.append(state_t)
    outs = pl.pallas_call(
        functools.partial(_front_kernel, l, ctx, dec, len(cast_windows), len(aliases)),
        grid=(n_steps,),
        in_specs=in_specs,
        out_specs=ctx_specs + dec_specs + cast_out,
        out_shape=ctx_shapes + dec_shapes + cast_shapes,
        scratch_shapes=[pltpu.VMEM((max(ctx.tm, dec.tm) + 2 * HALO, BRANCH_DIM), jnp.float32)] * 2,
        input_output_aliases=aliases,
        compiler_params=_params(),
        name="front",
    )(*args)
    return outs[:7], outs[7:13], tuple(outs[13:])


_N_BACK_WEIGHTS = 5


def _back_tile(l, x_ref, q_ref, k_ref, v_ref, cache, ya_ref, yc_ref, sbz_ref, mod_row, weights, out_ref):
    gpre_ref, wg_ref, wbr_ref, wo_ref, gpost_ref = weights
    g_pre, g_post = gpre_ref[l:l + 1, :], gpost_ref[l:l + 1, :]
    x = x_ref[...]

    def with_ones(v):
        return jnp.concatenate([v, jnp.ones_like(v)], axis=1)

    low_half = lax.broadcasted_iota(jnp.int32, (x.shape[0], LANES), 1) < V_HEAD
    pairs = []
    for p in range(N_PAIRS):
        v_ext = with_ones(v_ref[p])
        if cache is not None:
            kc_ref, vc_ref = cache
            vc_ext = with_ones(vc_ref[p])
        halves = []
        for h in (2 * p, 2 * p + 1):
            q = q_ref[h]
            s = _dot_nt(q, k_ref[h])
            m = jnp.max(s, axis=-1, keepdims=True)
            if cache is not None:
                sc = _dot_nt(q, kc_ref[h])
                m = jnp.maximum(m, jnp.max(sc, axis=-1, keepdims=True))
            o = _dot(jnp.exp2((s - m).astype(jnp.bfloat16)), v_ext)
            if cache is not None:
                o = o + _dot(jnp.exp2((sc - m).astype(jnp.bfloat16)), vc_ext)
            halves.append(o[:, 0:LANES] / o[:, LANES:2 * LANES])
        pairs.append(jnp.where(low_half, halves[0], halves[1]))
    attn = jnp.concatenate(pairs, axis=1)
    yb = (sbz_ref[...].astype(jnp.float32) * attn).astype(jnp.bfloat16)

    hn = _modulated_norm(x, g_pre, mod_row).astype(jnp.bfloat16)
    merged = None
    for n, y in enumerate((ya_ref[...], yb, yc_ref[...])):
        r0 = _R_MERGE - _BACK_START + n * D_MODEL
        gate = jax.nn.sigmoid(_dot_nt(hn, wg_ref[r0:r0 + D_MODEL, :]))
        term = gate * _dot(y, wbr_ref[n * BRANCH_DIM:(n + 1) * BRANCH_DIM, :])
        merged = term if merged is None else merged + term
    out = _rms(_dot(merged.astype(jnp.bfloat16), wo_ref[...]), g_post)
    out_ref[...] = x + mod_row[:, 2 * D_MODEL:3 * D_MODEL] * out


def _back_kernel(l, ctx, dec, n_cast, *refs):
    x_c, q_c, k_c, v_c, ya_c, yc_c, sbz_c = refs[0:7]
    x_d, q_d, k_d, v_d, kc_ref, vc_ref, ya_d, yc_d, sbz_d = refs[7:16]
    mod_ref = refs[16]
    n_w = 17 + _N_BACK_WEIGHTS
    weights = refs[17:n_w]
    out_c, out_d = refs[n_w + n_cast:n_w + n_cast + 2]
    i = pl.program_id(0)
    _cast_chunks(refs[n_w:n_w + n_cast], refs[n_w + n_cast + 2:])

    @pl.when(i < dec.first)
    def _():
        _back_tile(l, x_c, q_c, k_c, v_c, None, ya_c, yc_c, sbz_c, _mod_row(mod_ref, ctx, dec, i), weights, out_c)

    @pl.when(i >= dec.first)
    def _():
        _back_tile(l, x_d, q_d, k_d, v_d, (kc_ref, vc_ref), ya_d, yc_d, sbz_d, _mod_row(mod_ref, ctx, dec, i),
                   weights, out_d)


def _back(l, x_prompt, x_sample, mod, w, big_bf, ctx_fr, dec_fr, kc, vc, big_f32, cast_windows):
    ctx, dec = _passes(x_prompt, x_sample, BACK_TILE)
    past = kc.shape[3]
    n_steps = ctx.steps + dec.steps

    def pass_specs(p):
        tq = p.tm
        tile = pl.BlockSpec((None, tq, D_MODEL), lambda i: (p.batch(i), p.tile(i), 0))
        q = pl.BlockSpec((None, N_HEADS, tq, HEAD_PAD), lambda i: (p.batch(i), 0, p.tile(i), 0))
        k = pl.BlockSpec((None, N_HEADS, p.seq, LANES), lambda i: (p.batch(i), 0, 0, 0))
        v = pl.BlockSpec((None, N_PAIRS, p.seq, LANES), lambda i: (p.batch(i), 0, 0, 0))
        br = pl.BlockSpec((None, tq, BRANCH_DIM), lambda i: (p.batch(i), p.tile(i), 0))
        return tile, q, k, v, br

    tile_c, q_c, k_c, v_c, br_c = pass_specs(ctx)
    tile_d, q_d, k_d, v_d, br_d = pass_specs(dec)
    kc_spec = pl.BlockSpec((None, None, N_HEADS, past, LANES), lambda i: (l, dec.batch(i), 0, 0, 0))
    vc_spec = pl.BlockSpec((None, None, N_PAIRS, past, LANES), lambda i: (l, dec.batch(i), 0, 0, 0))
    in_specs = ([tile_c, q_c, k_c, v_c, br_c, br_c, br_c]
                + [tile_d, q_d, k_d, v_d, kc_spec, vc_spec, br_d, br_d, br_d]
                + [_const_spec(l, mod.shape[1:]),
                   _whole_spec(w["g_pre"]),
                   _row_block_spec(_BACK_ROWS, 0),
                   _row_block_spec(3 * BRANCH_DIM, 0),
                   _row_block_spec(D_MODEL, 0),
                   _whole_spec(w["g_post"])])
    q_cx, k_cx, v_cx, ya_cx, yc_cx, sbz_cx = ctx_fr
    q_dx, k_dx, v_dx, ya_dx, yc_dx, sbz_dx = dec_fr
    args = [x_prompt, q_cx, k_cx, v_cx, ya_cx, yc_cx, sbz_cx,
            x_sample, q_dx, k_dx, v_dx, kc, vc, ya_dx, yc_dx, sbz_dx,
            mod, w["g_pre"], *big_bf, w["g_post"]]
    out_specs = [tile_c, tile_d]
    out_shape = [jax.ShapeDtypeStruct(x_prompt.shape, jnp.float32), jax.ShapeDtypeStruct(x_sample.shape, jnp.float32)]
    cast_in, cast_out, cast_shapes = _cast_specs(l + 1, cast_windows, n_steps)
    in_specs += cast_in
    args += [big_f32[src] for src, _, _ in cast_windows]
    out_specs += cast_out
    out_shape += cast_shapes
    outs = pl.pallas_call(
        functools.partial(_back_kernel, l, ctx, dec, len(cast_windows)),
        grid=(n_steps,),
        in_specs=in_specs,
        out_specs=out_specs,
        out_shape=out_shape,
        compiler_params=_params(),
        name="back",
    )(*args)
    return outs[0], outs[1], tuple(outs[2:])


def _rope_swap_perm():
    quarter = QK_ROPE // 4
    idx = np.arange(QK_ROPE).reshape(2, 2, quarter)
    return idx[:, ::-1, :].reshape(-1)


def _rope_tables(n_tokens, rotate):
    if rotate:
        t = np.arange(n_tokens)
        pos = np.stack([t // GRID_W, t % GRID_W], axis=1).astype(np.float64)
        axis_dim = QK_ROPE // 2
        inv = 1.0 / (ROPE_THETA ** (np.arange(0, axis_dim, 2, dtype=np.float64) / axis_dim))
        ang = pos[:, :, None] * inv
        cos, sin = np.cos(ang), np.sin(ang)
        cfull = np.stack([cos, cos], axis=2).reshape(n_tokens, QK_ROPE)
        sfull = np.stack([-sin, sin], axis=2).reshape(n_tokens, QK_ROPE)
    else:
        cfull = np.ones((n_tokens, QK_ROPE))
        sfull = np.zeros((n_tokens, QK_ROPE))
    scale = (QK_NOPE + QK_ROPE) ** -0.5 * np.log2(np.e)
    qtab = np.concatenate([np.full((n_tokens, QK_NOPE), scale), cfull * scale, sfull * scale], axis=1)
    pad = np.zeros((n_tokens, LANES - QK_ROPE))
    tabs = (qtab, np.concatenate([cfull, pad], axis=1), np.concatenate([sfull, pad], axis=1))
    return tuple(jnp.asarray(tab, jnp.float32) for tab in tabs)


def _prepare_weights(g_pre, g_post, conv_w, conv_b, g_q, w_uq, g_kv, w_ukv, pool_w, pool_scale):
    bf = jnp.bfloat16
    perm = _rope_swap_perm()

    uq = w_uq.reshape(DEPTH, Q_LORA, N_HEADS, QK_NOPE + QK_ROPE)
    rope_q = uq[..., QK_NOPE:]
    wuq = jnp.concatenate([uq, rope_q[..., perm]], axis=-1).reshape(DEPTH, Q_LORA, HP).astype(bf)

    ukv = w_ukv.reshape(DEPTH, KV_LORA, N_HEADS, QK_NOPE + V_HEAD)
    wk = jnp.pad(ukv[..., :QK_NOPE], ((0, 0), (0, 0), (0, 0), (0, HEAD_PAD - QK_NOPE)))
    wk = wk.reshape(DEPTH, KV_LORA, HP).astype(bf)
    wv = ukv[..., QK_NOPE:].reshape(DEPTH, KV_LORA, N_HEADS * V_HEAD).astype(bf)

    pw = pool_w.astype(bf).reshape(DEPTH, len(POOL_WINDOWS) // 2, 2, POOL_GROUP, POOL_GROUP)
    zero = jnp.zeros_like(pw[:, :, 0])
    pool_bd = jnp.concatenate([jnp.concatenate([pw[:, :, 0], zero], axis=-1),
                               jnp.concatenate([zero, pw[:, :, 1]], axis=-1)], axis=-2)

    return {
        "g_pre": g_pre, "g_post": g_post, "conv_w": jnp.swapaxes(conv_w, 0, 1), "conv_b": conv_b,
        "g_q": g_q, "g_kv": g_kv, "pool_scale": pool_scale,
        "wuq": wuq, "wk": wk, "wv": wv, "pool_w": pool_bd,
    }


def kernel(x_prompt, x_sample, cache_mla_latent, c, c_ctx, w_mod, b_mod, g_pre, g_post, w_in, conv_w, conv_b,
           g_q, w_uq, g_kv, w_ukv, pool_w, pool_scale, w_branch, w_o):
    n_dec = x_sample.shape[0]
    assert 1 + n_dec <= COND_ROWS
    assert w_in.shape == (DEPTH, D_MODEL, W_IN_COLS)
    w = _prepare_weights(g_pre, g_post, conv_w, conv_b, g_q, w_uq, g_kv, w_ukv, pool_w, pool_scale)
    big_f32 = (jnp.swapaxes(w_in, 1, 2), w_branch.reshape(DEPTH, 3 * BRANCH_DIM, D_MODEL), w_o)
    w_front = _cast_front_window(0, big_f32)
    cond =jnp.concatenate([c_ctx[None, :], c, jnp.zeros((COND_ROWS - 1 - n_dec, D_MODEL), jnp.float32)], axis=0)
    mod = _modulation(cond, w_mod, b_mod)
    kc, vc = _cache_kv(jnp.swapaxes(cache_mla_latent, 2, 3), w["wk"], w["wv"])
    ctx_tabs = _rope_tables(x_prompt.shape[1], rotate=False)
    dec_tabs = _rope_tables(x_sample.shape[1], rotate=True)

    back_windows = (_W_BACK, _W_BRANCH, _W_OUT)
    h, hs, state_t, big_bf = x_prompt, x_sample, None, None
    for l in range(DEPTH):
        ctx_fr, dec_fr, cast = _front(l, h, hs, mod, w, w_front, ctx_tabs, dec_tabs, state_t, big_f32,
                                      back_windows if big_bf is None else ())
        state_t = ctx_fr[6]
        big_bf = cast if big_bf is None else big_bf
        next_windows = (_W_FRONT,) + back_windows if l + 1 < DEPTH else ()
        h, hs, cast = _back(l, h, hs, mod, w, big_bf, ctx_fr[:6], dec_fr, kc, vc, big_f32, next_windows)
        if next_windows:
            w_front, big_bf = cast[0], cast[1:]
    return (h, hs, jnp.swapaxes(state_t, 2, 3))
```

```python
import functools

import numpy as np
import jax
import jax.numpy as jnp
from jax import lax
from jax.experimental import pallas as pl
from jax.experimental.pallas import tpu as pltpu

D_MODEL = 1024
DEPTH = 2
GRID_W = 64
EPS = 1e-6
BRANCH_DIM = 512
N_HEADS = 8
QK_NOPE = 64
QK_ROPE = 32
V_HEAD = 64
Q_LORA = 384
KV_LORA = 256
MLA_LATENT = KV_LORA + QK_ROPE
ROPE_THETA = 10000.0
POOL_WINDOWS = (2, 4, 8, 16)
POOL_GROUP = 128

LANES = 128
HEAD_PAD = LANES
HP = N_HEADS * HEAD_PAD
N_PAIRS = N_HEADS * V_HEAD // LANES
FRONT_TILE = 512
BACK_TILE = 512
HALO = 16
COND_ROWS = 8
VMEM_LIMIT = 56 * 1024 * 1024

_R_AB = 0
_R_ACX = BRANCH_DIM
_R_MAIN = 3 * BRANCH_DIM
_MAIN_ROWS = BRANCH_DIM + Q_LORA + KV_LORA + LANES
_R_BZ = 4 * BRANCH_DIM + Q_LORA + MLA_LATENT
_R_CU = _R_BZ + BRANCH_DIM
_R_CZ = _R_CU + BRANCH_DIM
_R_MERGE = _R_CZ + BRANCH_DIM
W_IN_COLS = _R_MERGE + 3 * D_MODEL
_FRONT_ROWS = -(-_R_MERGE // LANES) * LANES
_QD0 = BRANCH_DIM
_CKV0 = _QD0 + Q_LORA
_KR0 = _CKV0 + KV_LORA
_LAT_EXT = KV_LORA + LANES


def _dot(a, b):
    return jnp.dot(a, b, preferred_element_type=jnp.float32)


def _dot_nt(a, b):
    return lax.dot_general(a, b, (((1,), (1,)), ((), ())), preferred_element_type=jnp.float32)


def _rms(x, g):
    return x * lax.rsqrt(jnp.mean(x * x, axis=-1, keepdims=True) + EPS) * g


def _silu(x):
    return x * jax.nn.sigmoid(x)


def _modulated_norm(x, g_pre, mod_row):
    shift = mod_row[:, 0:D_MODEL]
    scale = mod_row[:, D_MODEL:2 * D_MODEL]
    return _rms(x, g_pre) * (1.0 + scale) + shift


def _params(n_grid_dims=1):
    return pltpu.CompilerParams(dimension_semantics=("arbitrary",) * n_grid_dims, vmem_limit_bytes=VMEM_LIMIT)


def _mod_kernel(cond_ref, w_ref, b_ref, out_ref):
    h = _silu(cond_ref[...]).astype(jnp.bfloat16)
    out_ref[...] = _dot(h, w_ref[...].astype(jnp.bfloat16)) + b_ref[pl.ds(pl.program_id(0), 1), :]


def _modulation(cond, w_mod, b_mod):
    col_tile = D_MODEL
    return pl.pallas_call(
        _mod_kernel,
        grid=(DEPTH, 3 * D_MODEL // col_tile),
        in_specs=[
            pl.BlockSpec((COND_ROWS, D_MODEL), lambda l, n: (0, 0)),
            pl.BlockSpec((None, D_MODEL, col_tile), lambda l, n: (l, 0, n)),
            pl.BlockSpec((DEPTH, col_tile), lambda l, n: (0, n)),
        ],
        out_specs=pl.BlockSpec((None, COND_ROWS, col_tile), lambda l, n: (l, 0, n)),
        out_shape=jax.ShapeDtypeStruct((DEPTH, COND_ROWS, 3 * D_MODEL), jnp.float32),
        compiler_params=_params(2),
        name="modulation",
    )(cond, w_mod, b_mod)


_SRC_W_IN_T, _SRC_W_BRANCH, _SRC_W_O = 0, 1, 2
_BACK_START = (_R_MERGE // 1024) * 1024
_W_FRONT = (_SRC_W_IN_T, 0, _FRONT_ROWS)
_W_BACK = (_SRC_W_IN_T, _BACK_START, W_IN_COLS - _BACK_START)
_W_BRANCH = (_SRC_W_BRANCH, 0, 3 * BRANCH_DIM)
_W_OUT = (_SRC_W_O, 0, D_MODEL)
_BACK_ROWS = -(-_W_BACK[2] // LANES) * LANES


def _chunk_rows(first_row, rows, n_steps):
    if first_row == 0:
        return pl.cdiv(pl.cdiv(rows, n_steps), 16) * 16
    per = LANES
    while first_row % per or pl.cdiv(rows, per) > n_steps:
        per *= 2
        assert per <= first_row
    return per


def _cast_specs(layer, windows, n_steps):
    in_specs, out_specs, out_shapes = [], [], []
    for _, first_row, rows in windows:
        per = _chunk_rows(first_row, rows, n_steps)
        n_blocks = pl.cdiv(rows, per)
        first_blk = first_row // per

        def in_map(i, first_blk=first_blk, last=n_blocks - 1):
            return (layer, first_blk + jnp.minimum(i, last), 0)

        def out_map(i, last=n_blocks - 1):
            return (jnp.minimum(i, last), 0)

        in_specs.append(pl.BlockSpec((None, per, D_MODEL), in_map))
        out_specs.append(pl.BlockSpec((per, D_MODEL), out_map))
        out_shapes.append(jax.ShapeDtypeStruct((n_blocks * per, D_MODEL), jnp.bfloat16))
    return in_specs, out_specs, out_shapes


def _cast_chunks(in_refs, out_refs):
    for src, dst in zip(in_refs, out_refs):
        dst[...] = src[...].astype(jnp.bfloat16)


def _cast_kernel(src, dst):
    _cast_chunks([src], [dst])


def _cast_front_window(layer, big_f32):
    n_steps = 4
    in_specs, out_specs, out_shapes = _cast_specs(layer, [_W_FRONT], n_steps)
    return pl.pallas_call(
        _cast_kernel,
        grid=(n_steps,),
        in_specs=in_specs,
        out_specs=out_specs[0],
        out_shape=out_shapes[0],
        compiler_params=_params(),
        name="cast_weights",
    )(big_f32[_SRC_W_IN_T])


def _rope_key_block(kr):
    return pltpu.roll(kr, QK_NOPE, 1) + pltpu.roll(kr, QK_NOPE + QK_ROPE, 1)


def _cache_kv_kernel(lat_t_ref, wk_ref, wv_ref, k_ref, v_ref):
    lat_t = lat_t_ref[...]
    pad = jnp.zeros((_LAT_EXT - MLA_LATENT, lat_t.shape[1]), jnp.float32)
    lat = jnp.concatenate([lat_t, pad], axis=0).T
    ckv_bf = lat[:, 0:KV_LORA].astype(jnp.bfloat16)
    k_nope = _dot(ckv_bf, wk_ref[...])
    k_rope = _rope_key_block(lat[:, KV_LORA:_LAT_EXT])
    v = _dot(ckv_bf, wv_ref[...]).astype(jnp.bfloat16)
    for h in range(N_HEADS):
        k_ref[h] = (k_nope[:, h * HEAD_PAD:(h + 1) * HEAD_PAD] + k_rope).astype(jnp.bfloat16)
    for p in range(N_PAIRS):
        v_ref[p] = v[:, p * LANES:(p + 1) * LANES]


def _cache_kv(cache_t, wk, wv):
    nb, _, _, past = cache_t.shape

    def kv(n):
        return (jax.ShapeDtypeStruct((DEPTH, nb, n, past, LANES), jnp.bfloat16),
                pl.BlockSpec((None, None, n, past, LANES), lambda l, b: (l, b, 0, 0, 0)))

    (k_shape, k_spec), (v_shape, v_spec) = kv(N_HEADS), kv(N_PAIRS)
    return pl.pallas_call(
        _cache_kv_kernel,
        grid=(DEPTH, nb),
        in_specs=[
            pl.BlockSpec((None, None, MLA_LATENT, past), lambda l, b: (b, l, 0, 0)),
            pl.BlockSpec((None, KV_LORA, HP), lambda l, b: (l, 0, 0)),
            pl.BlockSpec((None, KV_LORA, N_HEADS * V_HEAD), lambda l, b: (l, 0, 0)),
        ],
        out_specs=[k_spec, v_spec],
        out_shape=[k_shape, v_shape],
        compiler_params=_params(2),
        name="cache_kv",
    )(cache_t, wk, wv)


class _Pass:
    def __init__(self, x, tile, first_step, first_cond_row, shared_cond):
        self.nb, self.seq, _ = x.shape
        self.tm = min(tile, self.seq)
        assert self.seq % self.tm == 0
        self.nt = self.seq // self.tm
        self.steps = self.nb * self.nt
        self.first = first_step
        self.first_cond_row = first_cond_row
        self.shared_cond = shared_cond

    def local(self, i):
        return jnp.clip(i - self.first, 0, self.steps - 1)

    def batch(self, i):
        return self.local(i) // self.nt

    def tile(self, i):
        return self.local(i) % self.nt

    def cond_row(self, i):
        return self.first_cond_row if self.shared_cond else self.first_cond_row + self.batch(i)


def _passes(x_prompt, x_sample, tile):
    ctx = _Pass(x_prompt, tile, 0, 0, True)
    dec = _Pass(x_sample, tile, ctx.steps, 1, False)
    return ctx, dec


def _const_spec(l, shape):
    return pl.BlockSpec((None,) + shape, lambda i: (l,) + (0,) * len(shape), pipeline_mode=pl.Buffered(1))


def _whole_spec(arr):
    return pl.BlockSpec(arr.shape, lambda i: (0,) * arr.ndim, pipeline_mode=pl.Buffered(1))


def _row_block_spec(rows, block_index):
    return pl.BlockSpec((rows, D_MODEL), lambda i: (block_index, 0), pipeline_mode=pl.Buffered(1))


_N_FRONT_WEIGHTS = 11


def _front_tile(l, seq_len, j, x_ref, xp_ref, xn_ref, qtab_ref, kcos_ref, ksin_ref, mod_row, weights, outs, scratch):
    gpre_ref, w_ref, convw_ref, convb_ref, gq_ref, wuq_ref, gkv_ref, wk_ref, wv_ref, poolw_ref, pscale_ref = weights
    g_pre, conv_b, g_q, g_kv, p_scale = (r[l:l + 1, :] for r in (gpre_ref, convb_ref, gq_ref, gkv_ref, pscale_ref))
    conv_w = [convw_ref[tap, l:l + 1, :] for tap in range(3)]
    q_out, k_out, v_out, ya_out, yc_out, sbz_out = outs[:6]
    u_scr, cu_scr = scratch
    tm = x_ref.shape[0]
    last_j = seq_len // tm - 1

    if last_j == 0:
        hn = _modulated_norm(x_ref[...], g_pre, mod_row).astype(jnp.bfloat16)
        acx = _dot_nt(hn, w_ref[_R_ACX:_R_ACX + 2 * BRANCH_DIM, :])
        zeros = jnp.zeros((HALO, BRANCH_DIM), jnp.float32)
        for scr in (u_scr, cu_scr):
            scr[0:HALO, :] = zeros
            scr[HALO + tm:2 * HALO + tm, :] = zeros
        u_scr[HALO:HALO + tm, :] = acx[:, 0:BRANCH_DIM] * acx[:, BRANCH_DIM:2 * BRANCH_DIM]
        cu_scr[HALO:HALO + tm, :] = _dot_nt(hn, w_ref[_R_CU:_R_CU + BRANCH_DIM, :])
    else:
        x_ext = jnp.concatenate([xp_ref[...], x_ref[...], xn_ref[...]], axis=0)
        hn_ext = _modulated_norm(x_ext, g_pre, mod_row).astype(jnp.bfloat16)
        hn = hn_ext[HALO:HALO + tm]
        acx = _dot_nt(hn_ext, w_ref[_R_ACX:_R_ACX + 2 * BRANCH_DIM, :])
        row = lax.broadcasted_iota(jnp.int32, (tm + 2 * HALO, 1), 0)
        first_valid = jnp.where(j > 0, 0, HALO)
        end_valid = jnp.where(j < last_j, tm + 2 * HALO, tm + HALO)
        valid = jnp.logical_and(row >= first_valid, row < end_valid)
        ext_rows = slice(0, tm + 2 * HALO)
        u_scr[ext_rows, :] = jnp.where(valid, acx[:, 0:BRANCH_DIM] * acx[:, BRANCH_DIM:2 * BRANCH_DIM], 0.0)
        cu_scr[ext_rows, :] = jnp.where(valid, _dot_nt(hn_ext, w_ref[_R_CU:_R_CU + BRANCH_DIM, :]), 0.0)

    pm = _dot_nt(hn, w_ref[_R_MAIN:_R_MAIN + _MAIN_ROWS, :])
    conv = (u_scr[HALO - 1:HALO - 1 + tm, :] * conv_w[0]
            + u_scr[HALO:HALO + tm, :] * conv_w[1]
            + u_scr[HALO + 1:HALO + 1 + tm, :] * conv_w[2]
            + conv_b)
    ya = _silu(pm[:, 0:BRANCH_DIM]) * (_dot_nt(hn, w_ref[_R_AB:_R_AB + BRANCH_DIM, :]) * conv)
    ya_out[...] = ya.astype(jnp.bfloat16)

    t = j * tm + lax.broadcasted_iota(jnp.int32, (tm, POOL_GROUP), 0)
    pooled = []
    for gi, win in enumerate(POOL_WINDOWS):
        half = win // 2
        cols = slice(gi * POOL_GROUP, (gi + 1) * POOL_GROUP)
        total = cu_scr[HALO - half:HALO - half + tm, cols]
        for k in range(-half + 1, half):
            total = total + cu_scr[HALO + k:HALO + k + tm, cols]
        count = jnp.minimum(t + half, seq_len) - jnp.maximum(t - half, 0)
        pooled.append((total / count.astype(jnp.float32) - cu_scr[HALO:HALO + tm, cols]).astype(jnp.bfloat16))
    mixed = [_dot(jnp.concatenate(pooled[2 * n:2 * n + 2], axis=1), poolw_ref[n]) for n in range(len(pooled) // 2)]
    mixed = jnp.concatenate(mixed, axis=1) * p_scale
    c_z = _dot_nt(hn, w_ref[_R_CZ:_R_CZ + BRANCH_DIM, :])
    yc_out[...] = (_silu(c_z) * mixed).astype(jnp.bfloat16)

    qn = _rms(pm[:, _QD0:_QD0 + Q_LORA], g_q).astype(jnp.bfloat16)
    q = _dot(qn, wuq_ref[...])
    qtab = qtab_ref[...]
    ckv = _rms(pm[:, _CKV0:_CKV0 + KV_LORA], g_kv)
    kr = pm[:, _KR0:_KR0 + LANES]
    if len(outs) > 6:
        lat_out = outs[6]
        if len(lat_out.shape) == 3:
            for later in range(1, lat_out.shape[0]):
                lat_out[later] = jnp.zeros(lat_out.shape[1:], jnp.float32)
            lat_out = lat_out.at[0]
        lat_out[0:KV_LORA, :] = ckv.T
        lat_out[KV_LORA:MLA_LATENT, :] = kr.T[0:QK_ROPE, :]
    lane = lax.broadcasted_iota(jnp.int32, (tm, LANES), 1)
    quarter = QK_ROPE // 4
    partner = jnp.where(lane % (2 * quarter) < quarter,
                        pltpu.roll(kr, LANES - quarter, 1), pltpu.roll(kr, quarter, 1))
    kr_rot = jnp.where(lane < QK_ROPE, kr * kcos_ref[...] + partner * ksin_ref[...], 0.0)
    ckv_bf = ckv.astype(jnp.bfloat16)
    k_nope = _dot(ckv_bf, wk_ref[...])
    k_rope = _rope_key_block(kr_rot)
    v = _dot(ckv_bf, wv_ref[...]).astype(jnp.bfloat16)
    for h in range(N_HEADS):
        cols = slice(h * HEAD_PAD, (h + 1) * HEAD_PAD)
        q_out[h] = (q[:, cols] * qtab).astype(jnp.bfloat16)
        k_out[h] = (k_nope[:, cols] + k_rope).astype(jnp.bfloat16)
    for p in range(N_PAIRS):
        v_out[p] = v[:, p * LANES:(p + 1) * LANES]
    sbz_out[...] = _silu(_dot_nt(hn, w_ref[_R_BZ:_R_BZ + BRANCH_DIM, :])).astype(jnp.bfloat16)


def _mod_row(mod_ref, ctx, dec, i):
    row = jnp.where(i < dec.first, ctx.cond_row(i), dec.cond_row(i))
    return mod_ref[pl.ds(row, 1), :]


def _front_kernel(l, ctx, dec, n_cast, n_aliased, *refs):
    n_in = 6
    ctx_in, dec_in = refs[0:n_in], refs[n_in:2 * n_in]
    mod_ref = refs[2 * n_in]
    n_w = 2 * n_in + 1 + _N_FRONT_WEIGHTS
    weights = refs[2 * n_in + 1:n_w]
    outs = refs[n_w + n_cast + n_aliased:-2]
    ctx_out, dec_out = outs[0:7], outs[7:13]
    scratch = refs[-2:]
    i = pl.program_id(0)
    _cast_chunks(refs[n_w:n_w + n_cast], outs[13:])

    @pl.when(i < dec.first)
    def _():
        _front_tile(l, ctx.seq, ctx.tile(i), *ctx_in, _mod_row(mod_ref, ctx, dec, i), weights, ctx_out, scratch)

    @pl.when(i >= dec.first)
    def _():
        _front_tile(l, dec.seq, dec.tile(i), *dec_in, _mod_row(mod_ref, ctx, dec, i), weights, dec_out, scratch)


def _front(l, x_prompt, x_sample, mod, w, w_front, ctx_tabs, dec_tabs, state_t, big_f32, cast_windows):
    ctx, dec = _passes(x_prompt, x_sample, FRONT_TILE)
    n_steps = ctx.steps + dec.steps

    def pass_in_specs(p):
        tm = p.tm
        hb = tm // HALO
        n_hblk = p.seq // HALO
        tab = pl.BlockSpec((tm, LANES), lambda i: (p.tile(i), 0))
        return [
            pl.BlockSpec((None, tm, D_MODEL), lambda i: (p.batch(i), p.tile(i), 0)),
            pl.BlockSpec((None, HALO, D_MODEL), lambda i: (p.batch(i), jnp.maximum(p.tile(i) * hb - 1, 0), 0)),
            pl.BlockSpec((None, HALO, D_MODEL),
                         lambda i: (p.batch(i), jnp.minimum((p.tile(i) + 1) * hb, n_hblk - 1), 0)),
            tab, tab, tab,
        ]

    def pass_out(p, with_latent):
        tm = p.tm

        def blocks(n):
            return (jax.ShapeDtypeStruct((p.nb, n, p.seq, LANES), jnp.bfloat16),
                    pl.BlockSpec((None, n, tm, LANES), lambda i: (p.batch(i), 0, p.tile(i), 0)))

        (head_shape, head_spec), (pair_shape, pair_spec) = blocks(N_HEADS), blocks(N_PAIRS)
        br_shape = jax.ShapeDtypeStruct((p.nb, p.seq, BRANCH_DIM), jnp.bfloat16)
        br_spec = pl.BlockSpec((None, tm, BRANCH_DIM), lambda i: (p.batch(i), p.tile(i), 0))
        shapes = [head_shape, head_shape, pair_shape] + [br_shape] * 3
        specs = [head_spec, head_spec, pair_spec] + [br_spec] * 3
        if with_latent:
            shapes.append(jax.ShapeDtypeStruct((p.nb, DEPTH, MLA_LATENT, p.seq), jnp.float32))
            if state_t is None:
                specs.append(pl.BlockSpec((None, DEPTH, MLA_LATENT, tm), lambda i: (p.batch(i), 0, 0, p.tile(i))))
            else:
                specs.append(pl.BlockSpec((None, None, MLA_LATENT, tm), lambda i: (p.batch(i), l, 0, p.tile(i))))
        return shapes, specs

    weight_specs = [
        _whole_spec(w["g_pre"]),
        _row_block_spec(_FRONT_ROWS, 0),
        _whole_spec(w["conv_w"]),
        _whole_spec(w["conv_b"]),
        _whole_spec(w["g_q"]),
        _const_spec(l, (Q_LORA, HP)),
        _whole_spec(w["g_kv"]),
        _const_spec(l, (KV_LORA, HP)),
        _const_spec(l, (KV_LORA, N_HEADS * V_HEAD)),
        _const_spec(l, (len(POOL_WINDOWS) // 2, 2 * POOL_GROUP, 2 * POOL_GROUP)),
        _whole_spec(w["pool_scale"]),
    ]
    assert len(weight_specs) == _N_FRONT_WEIGHTS
    ctx_shapes, ctx_specs = pass_out(ctx, True)
    dec_shapes, dec_specs = pass_out(dec, False)
    in_specs = pass_in_specs(ctx) + pass_in_specs(dec) + [_const_spec(l, mod.shape[1:])] + weight_specs
    args = [x_prompt, x_prompt, x_prompt, *ctx_tabs, x_sample, x_sample, x_sample, *dec_tabs, mod,
            w["g_pre"], w_front, w["conv_w"], w["conv_b"], w["g_q"], w["wuq"], w["g_kv"],
            w["wk"], w["wv"], w["pool_w"], w["pool_scale"]]
    cast_in, cast_out, cast_shapes = _cast_specs(l, cast_windows, n_steps)
    in_specs += cast_in
    args += [big_f32[src] for src, _, _ in cast_windows]
    aliases = {}
    if state_t is not None:
        aliases = {len(args): len(ctx_shapes) - 1}
        in_specs.append(pl.BlockSpec(memory_space=pl.ANY))
        args.append(state_t)
    outs = pl.pallas_call(
        functools.partial(_front_kernel, l, ctx, dec, len(cast_windows), len(aliases)),
        grid=(n_steps,),
        in_specs=in_specs,
        out_specs=ctx_specs + dec_specs + cast_out,
        out_shape=ctx_shapes + dec_shapes + cast_shapes,
        scratch_shapes=[pltpu.VMEM((max(ctx.tm, dec.tm) + 2 * HALO, BRANCH_DIM), jnp.float32)] * 2,
        input_output_aliases=aliases,
        compiler_params=_params(),
        name="front",
    )(*args)
    return outs[:7], outs[7:13], tuple(outs[13:])


_N_BACK_WEIGHTS = 5


def _back_tile(l, x_ref, q_ref, k_ref, v_ref, cache, ya_ref, yc_ref, sbz_ref, mod_row, weights, out_ref):
    gpre_ref, wg_ref, wbr_ref, wo_ref, gpost_ref = weights
    g_pre, g_post = gpre_ref[l:l + 1, :], gpost_ref[l:l + 1, :]
    x = x_ref[...]

    def with_ones(v):
        return jnp.concatenate([v, jnp.ones_like(v)], axis=1)

    low_half = lax.broadcasted_iota(jnp.int32, (x.shape[0], LANES), 1) < V_HEAD
    pairs = []
    for p in range(N_PAIRS):
        v_ext = with_ones(v_ref[p])
        if cache is not None:
            kc_ref, vc_ref = cache
            vc_ext = with_ones(vc_ref[p])
        halves = []
        for h in (2 * p, 2 * p + 1):
            q = q_ref[h]
            s = _dot_nt(q, k_ref[h])
            m = jnp.max(s, axis=-1, keepdims=True)
            if cache is not None:
                sc = _dot_nt(q, kc_ref[h])
                m = jnp.maximum(m, jnp.max(sc, axis=-1, keepdims=True))
            o = _dot(jnp.exp2((s - m).astype(jnp.bfloat16)), v_ext)
            if cache is not None:
                o = o + _dot(jnp.exp2((sc - m).astype(jnp.bfloat16)), vc_ext)
            halves.append(o[:, 0:LANES] / o[:, LANES:2 * LANES])
        pairs.append(jnp.where(low_half, halves[0], halves[1]))
    attn = jnp.concatenate(pairs, axis=1)
    yb = (sbz_ref[...].astype(jnp.float32) * attn).astype(jnp.bfloat16)

    hn = _modulated_norm(x, g_pre, mod_row).astype(jnp.bfloat16)
    merged = None
    for n, y in enumerate((ya_ref[...], yb, yc_ref[...])):
        r0 = _R_MERGE - _BACK_START + n * D_MODEL
        gate = jax.nn.sigmoid(_dot_nt(hn, wg_ref[r0:r0 + D_MODEL, :]))
        term = gate * _dot(y, wbr_ref[n * BRANCH_DIM:(n + 1) * BRANCH_DIM, :])
        merged = term if merged is None else merged + term
    out = _rms(_dot(merged.astype(jnp.bfloat16), wo_ref[...]), g_post)
    out_ref[...] = x + mod_row[:, 2 * D_MODEL:3 * D_MODEL] * out


def _back_kernel(l, ctx, dec, n_cast, *refs):
    x_c, q_c, k_c, v_c, ya_c, yc_c, sbz_c = refs[0:7]
    x_d, q_d, k_d, v_d, kc_ref, vc_ref, ya_d, yc_d, sbz_d = refs[7:16]
    mod_ref = refs[16]
    n_w = 17 + _N_BACK_WEIGHTS
    weights = refs[17:n_w]
    out_c, out_d = refs[n_w + n_cast:n_w + n_cast + 2]
    i = pl.program_id(0)
    _cast_chunks(refs[n_w:n_w + n_cast], refs[n_w + n_cast + 2:])

    @pl.when(i < dec.first)
    def _():
        _back_tile(l, x_c, q_c, k_c, v_c, None, ya_c, yc_c, sbz_c, _mod_row(mod_ref, ctx, dec, i), weights, out_c)

    @pl.when(i >= dec.first)
    def _():
        _back_tile(l, x_d, q_d, k_d, v_d, (kc_ref, vc_ref), ya_d, yc_d, sbz_d, _mod_row(mod_ref, ctx, dec, i),
                   weights, out_d)


def _back(l, x_prompt, x_sample, mod, w, big_bf, ctx_fr, dec_fr, kc, vc, big_f32, cast_windows):
    ctx, dec = _passes(x_prompt, x_sample, BACK_TILE)
    past = kc.shape[3]
    n_steps = ctx.steps + dec.steps

    def pass_specs(p):
        tq = p.tm
        tile = pl.BlockSpec((None, tq, D_MODEL), lambda i: (p.batch(i), p.tile(i), 0))
        q = pl.BlockSpec((None, N_HEADS, tq, HEAD_PAD), lambda i: (p.batch(i), 0, p.tile(i), 0))
        seq_mode = pl.Buffered(1) if p.nt > 1 else None
        k = pl.BlockSpec((None, N_HEADS, p.seq, LANES), lambda i: (p.batch(i), 0, 0, 0), pipeline_mode=seq_mode)
        v = pl.BlockSpec((None, N_PAIRS, p.seq, LANES), lambda i: (p.batch(i), 0, 0, 0), pipeline_mode=seq_mode)
        br = pl.BlockSpec((None, tq, BRANCH_DIM), lambda i: (p.batch(i), p.tile(i), 0))
        return tile, q, k, v, br

    tile_c, q_c, k_c, v_c, br_c = pass_specs(ctx)
    tile_d, q_d, k_d, v_d, br_d = pass_specs(dec)
    kc_spec = pl.BlockSpec((None, None, N_HEADS, past, LANES), lambda i: (l, dec.batch(i), 0, 0, 0))
    vc_spec = pl.BlockSpec((None, None, N_PAIRS, past, LANES), lambda i: (l, dec.batch(i), 0, 0, 0))
    in_specs = ([tile_c, q_c, k_c, v_c, br_c, br_c, br_c]
                + [tile_d, q_d, k_d, v_d, kc_spec, vc_spec, br_d, br_d, br_d]
                + [_const_spec(l, mod.shape[1:]),
                   _whole_spec(w["g_pre"]),
                   _row_block_spec(_BACK_ROWS, 0),
                   _row_block_spec(3 * BRANCH_DIM, 0),
                   _row_block_spec(D_MODEL, 0),
                   _whole_spec(w["g_post"])])
    q_cx, k_cx, v_cx, ya_cx, yc_cx, sbz_cx = ctx_fr
    q_dx, k_dx, v_dx, ya_dx, yc_dx, sbz_dx = dec_fr
    args = [x_prompt, q_cx, k_cx, v_cx, ya_cx, yc_cx, sbz_cx,
            x_sample, q_dx, k_dx, v_dx, kc, vc, ya_dx, yc_dx, sbz_dx,
            mod, w["g_pre"], *big_bf, w["g_post"]]
    out_specs = [tile_c, tile_d]
    out_shape = [jax.ShapeDtypeStruct(x_prompt.shape, jnp.float32), jax.ShapeDtypeStruct(x_sample.shape, jnp.float32)]
    cast_in, cast_out, cast_shapes = _cast_specs(l + 1, cast_windows, n_steps)
    in_specs += cast_in
    args += [big_f32[src] for src, _, _ in cast_windows]
    out_specs += cast_out
    out_shape += cast_shapes
    outs = pl.pallas_call(
        functools.partial(_back_kernel, l, ctx, dec, len(cast_windows)),
        grid=(n_steps,),
        in_specs=in_specs,
        out_specs=out_specs,
        out_shape=out_shape,
        compiler_params=_params(),
        name="back",
    )(*args)
    return outs[0], outs[1], tuple(outs[2:])


def _rope_swap_perm():
    quarter = QK_ROPE // 4
    idx = np.arange(QK_ROPE).reshape(2, 2, quarter)
    return idx[:, ::-1, :].reshape(-1)


def _rope_tables(n_tokens, rotate):
    if rotate:
        t = np.arange(n_tokens)
        pos = np.stack([t // GRID_W, t % GRID_W], axis=1).astype(np.float64)
        axis_dim = QK_ROPE // 2
        inv = 1.0 / (ROPE_THETA ** (np.arange(0, axis_dim, 2, dtype=np.float64) / axis_dim))
        ang = pos[:, :, None] * inv
        cos, sin = np.cos(ang), np.sin(ang)
        cfull = np.stack([cos, cos], axis=2).reshape(n_tokens, QK_ROPE)
        sfull = np.stack([-sin, sin], axis=2).reshape(n_tokens, QK_ROPE)
    else:
        cfull = np.ones((n_tokens, QK_ROPE))
        sfull = np.zeros((n_tokens, QK_ROPE))
    scale = (QK_NOPE + QK_ROPE) ** -0.5 * np.log2(np.e)
    qtab = np.concatenate([np.full((n_tokens, QK_NOPE), scale), cfull * scale, sfull * scale], axis=1)
    pad = np.zeros((n_tokens, LANES - QK_ROPE))
    tabs = (qtab, np.concatenate([cfull, pad], axis=1), np.concatenate([sfull, pad], axis=1))
    return tuple(jnp.asarray(tab, jnp.float32) for tab in tabs)


def _prepare_weights(g_pre, g_post, conv_w, conv_b, g_q, w_uq, g_kv, w_ukv, pool_w, pool_scale):
    bf = jnp.bfloat16
    perm = _rope_swap_perm()

    uq = w_uq.reshape(DEPTH, Q_LORA, N_HEADS, QK_NOPE + QK_ROPE)
    rope_q = uq[..., QK_NOPE:]
    wuq = jnp.concatenate([uq, rope_q[..., perm]], axis=-1).reshape(DEPTH, Q_LORA, HP).astype(bf)

    ukv = w_ukv.reshape(DEPTH, KV_LORA, N_HEADS, QK_NOPE + V_HEAD)
    wk = jnp.pad(ukv[..., :QK_NOPE], ((0, 0), (0, 0), (0, 0), (0, HEAD_PAD - QK_NOPE)))
    wk = wk.reshape(DEPTH, KV_LORA, HP).astype(bf)
    wv = ukv[..., QK_NOPE:].reshape(DEPTH, KV_LORA, N_HEADS * V_HEAD).astype(bf)

    pw = pool_w.astype(bf).reshape(DEPTH, len(POOL_WINDOWS) // 2, 2, POOL_GROUP, POOL_GROUP)
    zero = jnp.zeros_like(pw[:, :, 0])
    pool_bd = jnp.concatenate([jnp.concatenate([pw[:, :, 0], zero], axis=-1),
                               jnp.concatenate([zero, pw[:, :, 1]], axis=-1)], axis=-2)

    return {
        "g_pre": g_pre, "g_post": g_post, "conv_w": jnp.swapaxes(conv_w, 0, 1), "conv_b": conv_b,
        "g_q": g_q, "g_kv": g_kv, "pool_scale": pool_scale,
        "wuq": wuq, "wk": wk, "wv": wv, "pool_w": pool_bd,
    }


def kernel(x_prompt, x_sample, cache_mla_latent, c, c_ctx, w_mod, b_mod, g_pre, g_post, w_in, conv_w, conv_b,
           g_q, w_uq, g_kv, w_ukv, pool_w, pool_scale, w_branch, w_o):
    n_dec = x_sample.shape[0]
    assert 1 + n_dec <= COND_ROWS
    assert w_in.shape == (DEPTH, D_MODEL, W_IN_COLS)
    w = _prepare_weights(g_pre, g_post, conv_w, conv_b, g_q, w_uq, g_kv, w_ukv, pool_w, pool_scale)
    big_f32 = (jnp.swapaxes(w_in, 1, 2), w_branch.reshape(DEPTH, 3 * BRANCH_DIM, D_MODEL), w_o)
    w_front = _cast_front_window(0, big_f32)
    cond =jnp.concatenate([c_ctx[None, :], c, jnp.zeros((COND_ROWS - 1 - n_dec, D_MODEL), jnp.float32)], axis=0)
    mod = _modulation(cond, w_mod, b_mod)
    kc, vc = _cache_kv(jnp.swapaxes(cache_mla_latent, 2, 3), w["wk"], w["wv"])
    ctx_tabs = _rope_tables(x_prompt.shape[1], rotate=False)
    dec_tabs = _rope_tables(x_sample.shape[1], rotate=True)

    back_windows = (_W_BACK, _W_BRANCH, _W_OUT)
    h, hs, state_t, big_bf = x_prompt, x_sample, None, None
    for l in range(DEPTH):
        ctx_fr, dec_fr, cast = _front(l, h, hs, mod, w, w_front, ctx_tabs, dec_tabs, state_t, big_f32,
                                      back_windows if big_bf is None else ())
        state_t = ctx_fr[6]
        big_bf = cast if big_bf is None else big_bf
        next_windows = (_W_FRONT,) + back_windows if l + 1 < DEPTH else ()
        h, hs, cast = _back(l, h, hs, mod, w, big_bf, ctx_fr[:6], dec_fr, kc, vc, big_f32, next_windows)
        if next_windows:
            w_front, big_bf = cast[0], cast[1:]
    return (h, hs, jnp.swapaxes(state_t, 2, 3))
```

```python
import functools

import numpy as np
import jax
import jax.numpy as jnp
from jax import lax
from jax.experimental import pallas as pl
from jax.experimental.pallas import tpu as pltpu

D_MODEL = 1024
DEPTH = 2
GRID_W = 64
EPS = 1e-6
BRANCH_DIM = 512
N_HEADS = 8
QK_NOPE = 64
QK_ROPE = 32
V_HEAD = 64
Q_LORA = 384
KV_LORA = 256
MLA_LATENT = KV_LORA + QK_ROPE
ROPE_THETA = 10000.0
POOL_WINDOWS = (2, 4, 8, 16)
POOL_GROUP = 128

LANES = 128
HEAD_PAD = LANES
HP = N_HEADS * HEAD_PAD
N_PAIRS = N_HEADS * V_HEAD // LANES
FRONT_TILE = 512
BACK_TILE = 256
HALO = 16
COND_ROWS = 8
VMEM_LIMIT = 56 * 1024 * 1024

_R_AB = 0
_R_ACX = BRANCH_DIM
_R_MAIN = 3 * BRANCH_DIM
_MAIN_ROWS = BRANCH_DIM + Q_LORA + KV_LORA + LANES
_R_BZ = 4 * BRANCH_DIM + Q_LORA + MLA_LATENT
_R_CU = _R_BZ + BRANCH_DIM
_R_CZ = _R_CU + BRANCH_DIM
_R_MERGE = _R_CZ + BRANCH_DIM
W_IN_COLS = _R_MERGE + 3 * D_MODEL
_FRONT_ROWS = -(-_R_MERGE // LANES) * LANES
_QD0 = BRANCH_DIM
_CKV0 = _QD0 + Q_LORA
_KR0 = _CKV0 + KV_LORA
_LAT_EXT = KV_LORA + LANES


def _dot(a, b):
    return jnp.dot(a, b, preferred_element_type=jnp.float32)


def _dot_nt(a, b):
    return lax.dot_general(a, b, (((1,), (1,)), ((), ())), preferred_element_type=jnp.float32)


def _rms(x, g):
    return x * lax.rsqrt(jnp.mean(x * x, axis=-1, keepdims=True) + EPS) * g


def _silu(x):
    return x * jax.nn.sigmoid(x)


def _modulated_norm(x, g_pre, mod_row):
    shift = mod_row[:, 0:D_MODEL]
    scale = mod_row[:, D_MODEL:2 * D_MODEL]
    return _rms(x, g_pre) * (1.0 + scale) + shift


def _params(n_grid_dims=1):
    return pltpu.CompilerParams(dimension_semantics=("arbitrary",) * n_grid_dims, vmem_limit_bytes=VMEM_LIMIT)


def _mod_kernel(cond_ref, w_ref, b_ref, out_ref):
    h = _silu(cond_ref[...]).astype(jnp.bfloat16)
    out_ref[...] = _dot(h, w_ref[...].astype(jnp.bfloat16)) + b_ref[pl.ds(pl.program_id(0), 1), :]


def _modulation(cond, w_mod, b_mod):
    col_tile = 3 * D_MODEL // 2
    return pl.pallas_call(
        _mod_kernel,
        grid=(DEPTH, 3 * D_MODEL // col_tile),
        in_specs=[
            pl.BlockSpec((COND_ROWS, D_MODEL), lambda l, n: (0, 0)),
            pl.BlockSpec((None, D_MODEL, col_tile), lambda l, n: (l, 0, n)),
            pl.BlockSpec((DEPTH, col_tile), lambda l, n: (0, n)),
        ],
        out_specs=pl.BlockSpec((None, COND_ROWS, col_tile), lambda l, n: (l, 0, n)),
        out_shape=jax.ShapeDtypeStruct((DEPTH, COND_ROWS, 3 * D_MODEL), jnp.float32),
        compiler_params=_params(2),
        name="modulation",
    )(cond, w_mod, b_mod)


_SRC_W_IN_T, _SRC_W_BRANCH, _SRC_W_O = 0, 1, 2
_BACK_START = (_R_MERGE // 1024) * 1024
_W_FRONT = (_SRC_W_IN_T, 0, _FRONT_ROWS)
_W_BACK = (_SRC_W_IN_T, _BACK_START, W_IN_COLS - _BACK_START)
_W_BRANCH = (_SRC_W_BRANCH, 0, 3 * BRANCH_DIM)
_W_OUT = (_SRC_W_O, 0, D_MODEL)
_BACK_ROWS = _W_BACK[2]


def _chunk_rows(first_row, rows, n_steps):
    if first_row == 0:
        return pl.cdiv(pl.cdiv(rows, n_steps), 16) * 16
    per = LANES
    while first_row % per or pl.cdiv(rows, per) > n_steps:
        per *= 2
        assert per <= first_row
    return per


def _cast_specs(layer, windows, n_steps):
    in_specs, out_specs, out_shapes = [], [], []
    for _, first_row, rows in windows:
        per = _chunk_rows(first_row, rows, n_steps)
        n_blocks = pl.cdiv(rows, per)
        first_blk = first_row // per

        def in_map(i, first_blk=first_blk, last=n_blocks - 1):
            return (layer, first_blk + jnp.minimum(i, last), 0)

        def out_map(i, last=n_blocks - 1):
            return (jnp.minimum(i, last), 0)

        in_specs.append(pl.BlockSpec((None, per, D_MODEL), in_map))
        out_specs.append(pl.BlockSpec((per, D_MODEL), out_map))
        out_shapes.append(jax.ShapeDtypeStruct((rows, D_MODEL), jnp.bfloat16))
    return in_specs, out_specs, out_shapes


def _cast_chunks(in_refs, out_refs):
    for src, dst in zip(in_refs, out_refs):
        dst[...] = src[...].astype(jnp.bfloat16)


def _cast_kernel(src, dst):
    _cast_chunks([src], [dst])


def _cast_front_window(layer, big_f32):
    n_steps = 4
    in_specs, out_specs, out_shapes = _cast_specs(layer, [_W_FRONT], n_steps)
    return pl.pallas_call(
        _cast_kernel,
        grid=(n_steps,),
        in_specs=in_specs,
        out_specs=out_specs[0],
        out_shape=out_shapes[0],
        compiler_params=_params(),
        name="cast_weights",
    )(big_f32[_SRC_W_IN_T])


def _rope_key_block(kr):
    return pltpu.roll(kr, QK_NOPE, 1) + pltpu.roll(kr, QK_NOPE + QK_ROPE, 1)


def _cache_kv_kernel(lat_t_ref, wk_ref, wv_ref, k_ref, v_ref):
    lat_t = lat_t_ref[...]
    pad = jnp.zeros((_LAT_EXT - MLA_LATENT, lat_t.shape[1]), jnp.float32)
    lat = jnp.concatenate([lat_t, pad], axis=0).T
    ckv_bf = lat[:, 0:KV_LORA].astype(jnp.bfloat16)
    k_nope = _dot(ckv_bf, wk_ref[...])
    k_rope = _rope_key_block(lat[:, KV_LORA:_LAT_EXT])
    v = _dot(ckv_bf, wv_ref[...]).astype(jnp.bfloat16)
    for h in range(N_HEADS):
        k_ref[h] = (k_nope[:, h * HEAD_PAD:(h + 1) * HEAD_PAD] + k_rope).astype(jnp.bfloat16)
    for p in range(N_PAIRS):
        v_ref[p] = v[:, p * LANES:(p + 1) * LANES]


def _cache_kv(cache_t, wk, wv):
    nb, _, _, past = cache_t.shape

    def kv(n):
        return (jax.ShapeDtypeStruct((DEPTH, nb, n, past, LANES), jnp.bfloat16),
                pl.BlockSpec((None, None, n, past, LANES), lambda l, b: (l, b, 0, 0, 0)))

    (k_shape, k_spec), (v_shape, v_spec) = kv(N_HEADS), kv(N_PAIRS)
    return pl.pallas_call(
        _cache_kv_kernel,
        grid=(DEPTH, nb),
        in_specs=[
            pl.BlockSpec((None, None, MLA_LATENT, past), lambda l, b: (b, l, 0, 0)),
            pl.BlockSpec((None, KV_LORA, HP), lambda l, b: (l, 0, 0)),
            pl.BlockSpec((None, KV_LORA, N_HEADS * V_HEAD), lambda l, b: (l, 0, 0)),
        ],
        out_specs=[k_spec, v_spec],
        out_shape=[k_shape, v_shape],
        compiler_params=_params(2),
        name="cache_kv",
    )(cache_t, wk, wv)


class _Pass:
    def __init__(self, x, tile, first_step, first_cond_row, shared_cond):
        self.nb, self.seq, _ = x.shape
        self.tm = min(tile, self.seq)
        assert self.seq % self.tm == 0
        self.nt = self.seq // self.tm
        self.steps = self.nb * self.nt
        self.first = first_step
        self.first_cond_row = first_cond_row
        self.shared_cond = shared_cond

    def local(self, i):
        return jnp.clip(i - self.first, 0, self.steps - 1)

    def batch(self, i):
        return self.local(i) // self.nt

    def tile(self, i):
        return self.local(i) % self.nt

    def cond_row(self, i):
        return self.first_cond_row if self.shared_cond else self.first_cond_row + self.batch(i)


def _passes(x_prompt, x_sample, tile):
    ctx = _Pass(x_prompt, tile, 0, 0, True)
    dec = _Pass(x_sample, tile, ctx.steps, 1, False)
    return ctx, dec


def _const_spec(l, shape):
    return pl.BlockSpec((None,) + shape, lambda i: (l,) + (0,) * len(shape), pipeline_mode=pl.Buffered(1))


def _whole_spec(arr):
    return pl.BlockSpec(arr.shape, lambda i: (0,) * arr.ndim, pipeline_mode=pl.Buffered(1))


def _row_block_spec(rows, block_index):
    return pl.BlockSpec((rows, D_MODEL), lambda i: (block_index, 0), pipeline_mode=pl.Buffered(1))


_N_FRONT_WEIGHTS = 11


def _front_tile(l, seq_len, j, x_ref, xp_ref, xn_ref, qtab_ref, kcos_ref, ksin_ref, mod_row, weights, outs, scratch):
    gpre_ref, w_ref, convw_ref, convb_ref, gq_ref, wuq_ref, gkv_ref, wk_ref, wv_ref, poolw_ref, pscale_ref = weights
    g_pre, conv_b, g_q, g_kv, p_scale = (r[l:l + 1, :] for r in (gpre_ref, convb_ref, gq_ref, gkv_ref, pscale_ref))
    conv_w = [convw_ref[tap, l:l + 1, :] for tap in range(3)]
    q_out, k_out, v_out, ya_out, yc_out, sbz_out = outs[:6]
    u_scr, cu_scr = scratch
    tm = x_ref.shape[0]
    last_j = seq_len // tm - 1

    if last_j == 0:
        hn = _modulated_norm(x_ref[...], g_pre, mod_row).astype(jnp.bfloat16)
        acx = _dot_nt(hn, w_ref[_R_ACX:_R_ACX + 2 * BRANCH_DIM, :])
        zeros = jnp.zeros((HALO, BRANCH_DIM), jnp.float32)
        for scr in (u_scr, cu_scr):
            scr[0:HALO, :] = zeros
            scr[HALO + tm:2 * HALO + tm, :] = zeros
        u_scr[HALO:HALO + tm, :] = acx[:, 0:BRANCH_DIM] * acx[:, BRANCH_DIM:2 * BRANCH_DIM]
        cu_scr[HALO:HALO + tm, :] = _dot_nt(hn, w_ref[_R_CU:_R_CU + BRANCH_DIM, :])
    else:
        x_ext = jnp.concatenate([xp_ref[...], x_ref[...], xn_ref[...]], axis=0)
        hn_ext = _modulated_norm(x_ext, g_pre, mod_row).astype(jnp.bfloat16)
        hn = hn_ext[HALO:HALO + tm]
        acx = _dot_nt(hn_ext, w_ref[_R_ACX:_R_ACX + 2 * BRANCH_DIM, :])
        row = lax.broadcasted_iota(jnp.int32, (tm + 2 * HALO, 1), 0)
        first_valid = jnp.where(j > 0, 0, HALO)
        end_valid = jnp.where(j < last_j, tm + 2 * HALO, tm + HALO)
        valid = jnp.logical_and(row >= first_valid, row < end_valid)
        ext_rows = slice(0, tm + 2 * HALO)
        u_scr[ext_rows, :] = jnp.where(valid, acx[:, 0:BRANCH_DIM] * acx[:, BRANCH_DIM:2 * BRANCH_DIM], 0.0)
        cu_scr[ext_rows, :] = jnp.where(valid, _dot_nt(hn_ext, w_ref[_R_CU:_R_CU + BRANCH_DIM, :]), 0.0)

    pm = _dot_nt(hn, w_ref[_R_MAIN:_R_MAIN + _MAIN_ROWS, :])
    conv = (u_scr[HALO - 1:HALO - 1 + tm, :] * conv_w[0]
            + u_scr[HALO:HALO + tm, :] * conv_w[1]
            + u_scr[HALO + 1:HALO + 1 + tm, :] * conv_w[2]
            + conv_b)
    ya = _silu(pm[:, 0:BRANCH_DIM]) * (_dot_nt(hn, w_ref[_R_AB:_R_AB + BRANCH_DIM, :]) * conv)
    ya_out[...] = ya.astype(jnp.bfloat16)

    t = j * tm + lax.broadcasted_iota(jnp.int32, (tm, POOL_GROUP), 0)
    pooled = []
    for gi, win in enumerate(POOL_WINDOWS):
        half = win // 2
        cols = slice(gi * POOL_GROUP, (gi + 1) * POOL_GROUP)
        total = cu_scr[HALO - half:HALO - half + tm, cols]
        for k in range(-half + 1, half):
            total = total + cu_scr[HALO + k:HALO + k + tm, cols]
        count = jnp.minimum(t + half, seq_len) - jnp.maximum(t - half, 0)
        pooled.append((total / count.astype(jnp.float32) - cu_scr[HALO:HALO + tm, cols]).astype(jnp.bfloat16))
    mixed = [_dot(jnp.concatenate(pooled[2 * n:2 * n + 2], axis=1), poolw_ref[n]) for n in range(len(pooled) // 2)]
    mixed = jnp.concatenate(mixed, axis=1) * p_scale
    c_z = _dot_nt(hn, w_ref[_R_CZ:_R_CZ + BRANCH_DIM, :])
    yc_out[...] = (_silu(c_z) * mixed).astype(jnp.bfloat16)

    qn = _rms(pm[:, _QD0:_QD0 + Q_LORA], g_q).astype(jnp.bfloat16)
    q = _dot(qn, wuq_ref[...])
    qtab = qtab_ref[...]
    ckv = _rms(pm[:, _CKV0:_CKV0 + KV_LORA], g_kv)
    kr = pm[:, _KR0:_KR0 + LANES]
    if len(outs) > 6:
        lat_out = outs[6]
        if len(lat_out.shape) == 3:
            for later in range(1, lat_out.shape[0]):
                lat_out[later] = jnp.zeros(lat_out.shape[1:], jnp.float32)
            lat_out = lat_out.at[0]
        lat_out[0:KV_LORA, :] = ckv.T
        lat_out[KV_LORA:MLA_LATENT, :] = kr.T[0:QK_ROPE, :]
    lane = lax.broadcasted_iota(jnp.int32, (tm, LANES), 1)
    quarter = QK_ROPE // 4
    partner = jnp.where(lane % (2 * quarter) < quarter,
                        pltpu.roll(kr, LANES - quarter, 1), pltpu.roll(kr, quarter, 1))
    kr_rot = jnp.where(lane < QK_ROPE, kr * kcos_ref[...] + partner * ksin_ref[...], 0.0)
    ckv_bf = ckv.astype(jnp.bfloat16)
    k_nope = _dot(ckv_bf, wk_ref[...])
    k_rope = _rope_key_block(kr_rot)
    v = _dot(ckv_bf, wv_ref[...]).astype(jnp.bfloat16)
    for h in range(N_HEADS):
        cols = slice(h * HEAD_PAD, (h + 1) * HEAD_PAD)
        q_out[h] = (q[:, cols] * qtab).astype(jnp.bfloat16)
        k_out[h] = (k_nope[:, cols] + k_rope).astype(jnp.bfloat16)
    for p in range(N_PAIRS):
        v_out[p] = v[:, p * LANES:(p + 1) * LANES]
    sbz_out[...] = _silu(_dot_nt(hn, w_ref[_R_BZ:_R_BZ + BRANCH_DIM, :])).astype(jnp.bfloat16)


def _mod_row(mod_ref, ctx, dec, i):
    row = jnp.where(i < dec.first, ctx.cond_row(i), dec.cond_row(i))
    return mod_ref[pl.ds(row, 1), :]


def _front_kernel(l, ctx, dec, n_cast, n_aliased, *refs):
    n_in = 6
    ctx_in, dec_in = refs[0:n_in], refs[n_in:2 * n_in]
    mod_ref = refs[2 * n_in]
    n_w = 2 * n_in + 1 + _N_FRONT_WEIGHTS
    weights = refs[2 * n_in + 1:n_w]
    outs = refs[n_w + n_cast + n_aliased:-2]
    ctx_out, dec_out = outs[0:7], outs[7:13]
    scratch = refs[-2:]
    i = pl.program_id(0)
    _cast_chunks(refs[n_w:n_w + n_cast], outs[13:])

    @pl.when(i < dec.first)
    def _():
        _front_tile(l, ctx.seq, ctx.tile(i), *ctx_in, _mod_row(mod_ref, ctx, dec, i), weights, ctx_out, scratch)

    @pl.when(i >= dec.first)
    def _():
        _front_tile(l, dec.seq, dec.tile(i), *dec_in, _mod_row(mod_ref, ctx, dec, i), weights, dec_out, scratch)


def _front(l, x_prompt, x_sample, mod, w, w_front, ctx_tabs, dec_tabs, state_t, big_f32, cast_windows):
    ctx, dec = _passes(x_prompt, x_sample, FRONT_TILE)
    n_steps = ctx.steps + dec.steps

    def pass_in_specs(p):
        tm = p.tm
        hb = tm // HALO
        n_hblk = p.seq // HALO
        tab = pl.BlockSpec((tm, LANES), lambda i: (p.tile(i), 0))
        return [
            pl.BlockSpec((None, tm, D_MODEL), lambda i: (p.batch(i), p.tile(i), 0)),
            pl.BlockSpec((None, HALO, D_MODEL), lambda i: (p.batch(i), jnp.maximum(p.tile(i) * hb - 1, 0), 0)),
            pl.BlockSpec((None, HALO, D_MODEL),
                         lambda i: (p.batch(i), jnp.minimum((p.tile(i) + 1) * hb, n_hblk - 1), 0)),
            tab, tab, tab,
        ]

    def pass_out(p, with_latent):
        tm = p.tm

        def blocks(n):
            return (jax.ShapeDtypeStruct((p.nb, n, p.seq, LANES), jnp.bfloat16),
                    pl.BlockSpec((None, n, tm, LANES), lambda i: (p.batch(i), 0, p.tile(i), 0)))

        (head_shape, head_spec), (pair_shape, pair_spec) = blocks(N_HEADS), blocks(N_PAIRS)
        br_shape = jax.ShapeDtypeStruct((p.nb, p.seq, BRANCH_DIM), jnp.bfloat16)
        br_spec = pl.BlockSpec((None, tm, BRANCH_DIM), lambda i: (p.batch(i), p.tile(i), 0))
        shapes = [head_shape, head_shape, pair_shape] + [br_shape] * 3
        specs = [head_spec, head_spec, pair_spec] + [br_spec] * 3
        if with_latent:
            shapes.append(jax.ShapeDtypeStruct((p.nb, DEPTH, MLA_LATENT, p.seq), jnp.float32))
            if state_t is None:
                specs.append(pl.BlockSpec((None, DEPTH, MLA_LATENT, tm), lambda i: (p.batch(i), 0, 0, p.tile(i))))
            else:
                specs.append(pl.BlockSpec((None, None, MLA_LATENT, tm), lambda i: (p.batch(i), l, 0, p.tile(i))))
        return shapes, specs

    weight_specs = [
        _whole_spec(w["g_pre"]),
        _row_block_spec(_FRONT_ROWS, 0),
        _whole_spec(w["conv_w"]),
        _whole_spec(w["conv_b"]),
        _whole_spec(w["g_q"]),
        _const_spec(l, (Q_LORA, HP)),
        _whole_spec(w["g_kv"]),
        _const_spec(l, (KV_LORA, HP)),
        _const_spec(l, (KV_LORA, N_HEADS * V_HEAD)),
        _const_spec(l, (len(POOL_WINDOWS) // 2, 2 * POOL_GROUP, 2 * POOL_GROUP)),
        _whole_spec(w["pool_scale"]),
    ]
    assert len(weight_specs) == _N_FRONT_WEIGHTS
    ctx_shapes, ctx_specs = pass_out(ctx, True)
    dec_shapes, dec_specs = pass_out(dec, False)
    in_specs = pass_in_specs(ctx) + pass_in_specs(dec) + [_const_spec(l, mod.shape[1:])] + weight_specs
    args = [x_prompt, x_prompt, x_prompt, *ctx_tabs, x_sample, x_sample, x_sample, *dec_tabs, mod,
            w["g_pre"], w_front, w["conv_w"], w["conv_b"], w["g_q"], w["wuq"], w["g_kv"],
            w["wk"], w["wv"], w["pool_w"], w["pool_scale"]]
    cast_in, cast_out, cast_shapes = _cast_specs(l, cast_windows, n_steps)
    in_specs += cast_in
    args += [big_f32[src] for src, _, _ in cast_windows]
    aliases = {}
    if state_t is not None:
        aliases = {len(args): len(ctx_shapes) - 1}
        in_specs.append(pl.BlockSpec(memory_space=pl.ANY))
        args.append(state_t)
    outs = pl.pallas_call(
        functools.partial(_front_kernel, l, ctx, dec, len(cast_windows), len(aliases)),
        grid=(n_steps,),
        in_specs=in_specs,
        out_specs=ctx_specs + dec_specs + cast_out,
        out_shape=ctx_shapes + dec_shapes + cast_shapes,
        scratch_shapes=[pltpu.VMEM((max(ctx.tm, dec.tm) + 2 * HALO, BRANCH_DIM), jnp.float32)] * 2,
        input_output_aliases=aliases,
        compiler_params=_params(),
        name="front",
    )(*args)
    return outs[:7], outs[7:13], tuple(outs[13:])


_N_BACK_WEIGHTS = 5


def _back_tile(l, x_ref, q_ref, k_ref, v_ref, cache, ya_ref, yc_ref, sbz_ref, mod_row, weights, out_ref):
    gpre_ref, wg_ref, wbr_ref, wo_ref, gpost_ref = weights
    g_pre, g_post = gpre_ref[l:l + 1, :], gpost_ref[l:l + 1, :]
    x = x_ref[...]

    def with_ones(v):
        return jnp.concatenate([v, jnp.ones_like(v)], axis=1)

    low_half = lax.broadcasted_iota(jnp.int32, (x.shape[0], LANES), 1) < V_HEAD
    pairs = []
    for p in range(N_PAIRS):
        v_ext = with_ones(v_ref[p])
        if cache is not None:
            kc_ref, vc_ref = cache
            vc_ext = with_ones(vc_ref[p])
        halves = []
        for h in (2 * p, 2 * p + 1):
            q = q_ref[h]
            s = _dot_nt(q, k_ref[h])
            m = jnp.max(s, axis=-1, keepdims=True)
            if cache is not None:
                sc = _dot_nt(q, kc_ref[h])
                m = jnp.maximum(m, jnp.max(sc, axis=-1, keepdims=True))
            o = _dot(jnp.exp2((s - m).astype(jnp.bfloat16)), v_ext)
            if cache is not None:
                o = o + _dot(jnp.exp2((sc - m).astype(jnp.bfloat16)), vc_ext)
            halves.append(o[:, 0:LANES] / o[:, LANES:2 * LANES])
        pairs.append(jnp.where(low_half, halves[0], halves[1]))
    attn = jnp.concatenate(pairs, axis=1)
    yb = (sbz_ref[...].astype(jnp.float32) * attn).astype(jnp.bfloat16)

    hn = _modulated_norm(x, g_pre, mod_row).astype(jnp.bfloat16)
    merged = None
    for n, y in enumerate((ya_ref[...], yb, yc_ref[...])):
        r0 = _R_MERGE - _BACK_START + n * D_MODEL
        gate = jax.nn.sigmoid(_dot_nt(hn, wg_ref[r0:r0 + D_MODEL, :]))
        term = gate * _dot(y, wbr_ref[n * BRANCH_DIM:(n + 1) * BRANCH_DIM, :])
        merged = term if merged is None else merged + term
    out = _rms(_dot(merged.astype(jnp.bfloat16), wo_ref[...]), g_post)
    out_ref[...] = x + mod_row[:, 2 * D_MODEL:3 * D_MODEL] * out


def _back_kernel(l, ctx, dec, n_cast, *refs):
    x_c, q_c, k_c, v_c, ya_c, yc_c, sbz_c = refs[0:7]
    x_d, q_d, k_d, v_d, kc_ref, vc_ref, ya_d, yc_d, sbz_d = refs[7:16]
    mod_ref = refs[16]
    n_w = 17 + _N_BACK_WEIGHTS
    weights = refs[17:n_w]
    out_c, out_d = refs[n_w + n_cast:n_w + n_cast + 2]
    i = pl.program_id(0)
    _cast_chunks(refs[n_w:n_w + n_cast], refs[n_w + n_cast + 2:])

    @pl.when(i < dec.first)
    def _():
        _back_tile(l, x_c, q_c, k_c, v_c, None, ya_c, yc_c, sbz_c, _mod_row(mod_ref, ctx, dec, i), weights, out_c)

    @pl.when(i >= dec.first)
    def _():
        _back_tile(l, x_d, q_d, k_d, v_d, (kc_ref, vc_ref), ya_d, yc_d, sbz_d, _mod_row(mod_ref, ctx, dec, i),
                   weights, out_d)


def _back(l, x_prompt, x_sample, mod, w, big_bf, ctx_fr, dec_fr, kc, vc, big_f32, cast_windows):
    ctx, dec = _passes(x_prompt, x_sample, BACK_TILE)
    past = kc.shape[3]
    n_steps = ctx.steps + dec.steps

    def pass_specs(p):
        tq = p.tm
        tile = pl.BlockSpec((None, tq, D_MODEL), lambda i: (p.batch(i), p.tile(i), 0))
        q = pl.BlockSpec((None, N_HEADS, tq, HEAD_PAD), lambda i: (p.batch(i), 0, p.tile(i), 0))
        k = pl.BlockSpec((None, N_HEADS, p.seq, LANES), lambda i: (p.batch(i), 0, 0, 0))
        v = pl.BlockSpec((None, N_PAIRS, p.seq, LANES), lambda i: (p.batch(i), 0, 0, 0))
        br = pl.BlockSpec((None, tq, BRANCH_DIM), lambda i: (p.batch(i), p.tile(i), 0))
        return tile, q, k, v, br

    tile_c, q_c, k_c, v_c, br_c = pass_specs(ctx)
    tile_d, q_d, k_d, v_d, br_d = pass_specs(dec)
    kc_spec = pl.BlockSpec((None, None, N_HEADS, past, LANES), lambda i: (l, dec.batch(i), 0, 0, 0))
    vc_spec = pl.BlockSpec((None, None, N_PAIRS, past, LANES), lambda i: (l, dec.batch(i), 0, 0, 0))
    in_specs = ([tile_c, q_c, k_c, v_c, br_c, br_c, br_c]
                + [tile_d, q_d, k_d, v_d, kc_spec, vc_spec, br_d, br_d, br_d]
                + [_const_spec(l, mod.shape[1:]),
                   _whole_spec(w["g_pre"]),
                   _row_block_spec(_BACK_ROWS, 0),
                   _row_block_spec(3 * BRANCH_DIM, 0),
                   _row_block_spec(D_MODEL, 0),
                   _whole_spec(w["g_post"])])
    q_cx, k_cx, v_cx, ya_cx, yc_cx, sbz_cx = ctx_fr
    q_dx, k_dx, v_dx, ya_dx, yc_dx, sbz_dx = dec_fr
    args = [x_prompt, q_cx, k_cx, v_cx, ya_cx, yc_cx, sbz_cx,
            x_sample, q_dx, k_dx, v_dx, kc, vc, ya_dx, yc_dx, sbz_dx,
            mod, w["g_pre"], *big_bf, w["g_post"]]
    out_specs = [tile_c, tile_d]
    out_shape = [jax.ShapeDtypeStruct(x_prompt.shape, jnp.float32), jax.ShapeDtypeStruct(x_sample.shape, jnp.float32)]
    cast_in, cast_out, cast_shapes = _cast_specs(l + 1, cast_windows, n_steps)
    in_specs += cast_in
    args += [big_f32[src] for src, _, _ in cast_windows]
    out_specs += cast_out
    out_shape += cast_shapes
    outs = pl.pallas_call(
        functools.partial(_back_kernel, l, ctx, dec, len(cast_windows)),
        grid=(n_steps,),
        in_specs=in_specs,
        out_specs=out_specs,
        out_shape=out_shape,
        compiler_params=_params(),
        name="back",
    )(*args)
    return outs[0], outs[1], tuple(outs[2:])


def _rope_swap_perm():
    quarter = QK_ROPE // 4
    idx = np.arange(QK_ROPE).reshape(2, 2, quarter)
    return idx[:, ::-1, :].reshape(-1)


def _rope_tables(n_tokens, rotate):
    if rotate:
        t = np.arange(n_tokens)
        pos = np.stack([t // GRID_W, t % GRID_W], axis=1).astype(np.float64)
        axis_dim = QK_ROPE // 2
        inv = 1.0 / (ROPE_THETA ** (np.arange(0, axis_dim, 2, dtype=np.float64) / axis_dim))
        ang = pos[:, :, None] * inv
        cos, sin = np.cos(ang), np.sin(ang)
        cfull = np.stack([cos, cos], axis=2).reshape(n_tokens, QK_ROPE)
        sfull = np.stack([-sin, sin], axis=2).reshape(n_tokens, QK_ROPE)
    else:
        cfull = np.ones((n_tokens, QK_ROPE))
        sfull = np.zeros((n_tokens, QK_ROPE))
    scale = (QK_NOPE + QK_ROPE) ** -0.5 * np.log2(np.e)
    qtab = np.concatenate([np.full((n_tokens, QK_NOPE), scale), cfull * scale, sfull * scale], axis=1)
    pad = np.zeros((n_tokens, LANES - QK_ROPE))
    tabs = (qtab, np.concatenate([cfull, pad], axis=1), np.concatenate([sfull, pad], axis=1))
    return tuple(jnp.asarray(tab, jnp.float32) for tab in tabs)


def _prepare_weights(g_pre, g_post, conv_w, conv_b, g_q, w_uq, g_kv, w_ukv, pool_w, pool_scale):
    bf = jnp.bfloat16
    perm = _rope_swap_perm()

    uq = w_uq.reshape(DEPTH, Q_LORA, N_HEADS, QK_NOPE + QK_ROPE)
    rope_q = uq[..., QK_NOPE:]
    wuq = jnp.concatenate([uq, rope_q[..., perm]], axis=-1).reshape(DEPTH, Q_LORA, HP).astype(bf)

    ukv = w_ukv.reshape(DEPTH, KV_LORA, N_HEADS, QK_NOPE + V_HEAD)
    wk = jnp.pad(ukv[..., :QK_NOPE], ((0, 0), (0, 0), (0, 0), (0, HEAD_PAD - QK_NOPE)))
    wk = wk.reshape(DEPTH, KV_LORA, HP).astype(bf)
    wv = ukv[..., QK_NOPE:].reshape(DEPTH, KV_LORA, N_HEADS * V_HEAD).astype(bf)

    pw = pool_w.astype(bf).reshape(DEPTH, len(POOL_WINDOWS) // 2, 2, POOL_GROUP, POOL_GROUP)
    zero = jnp.zeros_like(pw[:, :, 0])
    pool_bd = jnp.concatenate([jnp.concatenate([pw[:, :, 0], zero], axis=-1),
                               jnp.concatenate([zero, pw[:, :, 1]], axis=-1)], axis=-2)

    return {
        "g_pre": g_pre, "g_post": g_post, "conv_w": jnp.swapaxes(conv_w, 0, 1), "conv_b": conv_b,
        "g_q": g_q, "g_kv": g_kv, "pool_scale": pool_scale,
        "wuq": wuq, "wk": wk, "wv": wv, "pool_w": pool_bd,
    }


def kernel(x_prompt, x_sample, cache_mla_latent, c, c_ctx, w_mod, b_mod, g_pre, g_post, w_in, conv_w, conv_b,
           g_q, w_uq, g_kv, w_ukv, pool_w, pool_scale, w_branch, w_o):
    n_dec = x_sample.shape[0]
    assert 1 + n_dec <= COND_ROWS
    assert w_in.shape == (DEPTH, D_MODEL, W_IN_COLS)
    w = _prepare_weights(g_pre, g_post, conv_w, conv_b, g_q, w_uq, g_kv, w_ukv, pool_w, pool_scale)
    big_f32 = (jnp.swapaxes(w_in, 1, 2), w_branch.reshape(DEPTH, 3 * BRANCH_DIM, D_MODEL), w_o)
    w_front = _cast_front_window(0, big_f32)
    cond =jnp.concatenate([c_ctx[None, :], c, jnp.zeros((COND_ROWS - 1 - n_dec, D_MODEL), jnp.float32)], axis=0)
    mod = _modulation(cond, w_mod, b_mod)
    kc, vc = _cache_kv(jnp.swapaxes(cache_mla_latent, 2, 3), w["wk"], w["wv"])
    ctx_tabs = _rope_tables(x_prompt.shape[1], rotate=False)
    dec_tabs = _rope_tables(x_sample.shape[1], rotate=True)

    back_windows = (_W_BACK, _W_BRANCH, _W_OUT)
    h, hs, state_t, big_bf = x_prompt, x_sample, None, None
    for l in range(DEPTH):
        ctx_fr, dec_fr, cast = _front(l, h, hs, mod, w, w_front, ctx_tabs, dec_tabs, state_t, big_f32,
                                      back_windows if big_bf is None else ())
        state_t = ctx_fr[6]
        big_bf = cast if big_bf is None else big_bf
        next_windows = (_W_FRONT,) + back_windows if l + 1 < DEPTH else ()
        h, hs, cast = _back(l, h, hs, mod, w, big_bf, ctx_fr[:6], dec_fr, kc, vc, big_f32, next_windows)
        if next_windows:
            w_front, big_bf = cast[0], cast[1:]
    return (h, hs, jnp.swapaxes(state_t, 2, 3))
```

```python
import functools

import numpy as np
import jax
import jax.numpy as jnp
from jax import lax
from jax.experimental import pallas as pl
from jax.experimental.pallas import tpu as pltpu

D_MODEL = 1024
DEPTH = 2
GRID_W = 64
EPS = 1e-6
BRANCH_DIM = 512
N_HEADS = 8
QK_NOPE = 64
QK_ROPE = 32
V_HEAD = 64
Q_LORA = 384
KV_LORA = 256
MLA_LATENT = KV_LORA + QK_ROPE
ROPE_THETA = 10000.0
POOL_WINDOWS = (2, 4, 8, 16)
POOL_GROUP = 128

LANES = 128
HEAD_PAD = LANES
HP = N_HEADS * HEAD_PAD
N_PAIRS = N_HEADS * V_HEAD // LANES
FRONT_TILE = 512
BACK_TILE = 256
HALO = 16
COND_ROWS = 8
VMEM_LIMIT = 56 * 1024 * 1024

_R_AB = 0
_R_ACX = BRANCH_DIM
_R_MAIN = 3 * BRANCH_DIM
_MAIN_ROWS = BRANCH_DIM + Q_LORA + KV_LORA + LANES
_R_BZ = 4 * BRANCH_DIM + Q_LORA + MLA_LATENT
_R_CU = _R_BZ + BRANCH_DIM
_R_CZ = _R_CU + BRANCH_DIM
_R_MERGE = _R_CZ + BRANCH_DIM
W_IN_COLS = _R_MERGE + 3 * D_MODEL
_FRONT_ROWS = -(-_R_MERGE // LANES) * LANES
_QD0 = BRANCH_DIM
_CKV0 = _QD0 + Q_LORA
_KR0 = _CKV0 + KV_LORA
_LAT_EXT = KV_LORA + LANES


def _dot(a, b):
    return jnp.dot(a, b, preferred_element_type=jnp.float32)


def _dot_nt(a, b):
    return lax.dot_general(a, b, (((1,), (1,)), ((), ())), preferred_element_type=jnp.float32)


def _rms(x, g):
    return x * lax.rsqrt(jnp.mean(x * x, axis=-1, keepdims=True) + EPS) * g


def _silu(x):
    return x * jax.nn.sigmoid(x)


def _modulated_norm(x, g_pre, mod_row):
    shift = mod_row[:, 0:D_MODEL]
    scale = mod_row[:, D_MODEL:2 * D_MODEL]
    return _rms(x, g_pre) * (1.0 + scale) + shift


def _params(n_grid_dims=1):
    return pltpu.CompilerParams(dimension_semantics=("arbitrary",) * n_grid_dims, vmem_limit_bytes=VMEM_LIMIT)


def _mod_kernel(cond_ref, w_ref, b_ref, out_ref):
    h = _silu(cond_ref[...]).astype(jnp.bfloat16)
    out_ref[...] = _dot(h, w_ref[...].astype(jnp.bfloat16)) + b_ref[pl.ds(pl.program_id(0), 1), :]


def _modulation(cond, w_mod, b_mod):
    col_tile = 3 * D_MODEL // 2
    return pl.pallas_call(
        _mod_kernel,
        grid=(DEPTH, 3 * D_MODEL // col_tile),
        in_specs=[
            pl.BlockSpec((COND_ROWS, D_MODEL), lambda l, n: (0, 0)),
            pl.BlockSpec((None, D_MODEL, col_tile), lambda l, n: (l, 0, n)),
            pl.BlockSpec((DEPTH, col_tile), lambda l, n: (0, n)),
        ],
        out_specs=pl.BlockSpec((None, COND_ROWS, col_tile), lambda l, n: (l, 0, n)),
        out_shape=jax.ShapeDtypeStruct((DEPTH, COND_ROWS, 3 * D_MODEL), jnp.float32),
        compiler_params=_params(2),
        name="modulation",
    )(cond, w_mod, b_mod)


_SRC_W_IN_T, _SRC_W_BRANCH, _SRC_W_O = 0, 1, 2
_BACK_START = (_R_MERGE // 1024) * 1024
_W_FRONT = (_SRC_W_IN_T, 0, _FRONT_ROWS)
_W_BACK = (_SRC_W_IN_T, _BACK_START, W_IN_COLS - _BACK_START)
_W_BRANCH = (_SRC_W_BRANCH, 0, 3 * BRANCH_DIM)
_W_OUT = (_SRC_W_O, 0, D_MODEL)
_BACK_ROWS = _W_BACK[2]


def _chunk_rows(first_row, rows, n_steps):
    if first_row == 0:
        return pl.cdiv(pl.cdiv(rows, n_steps), 16) * 16
    per = LANES
    while first_row % per or pl.cdiv(rows, per) > n_steps:
        per *= 2
        assert per <= first_row
    return per


def _cast_specs(layer, windows, n_steps):
    in_specs, out_specs, out_shapes = [], [], []
    for _, first_row, rows in windows:
        per = _chunk_rows(first_row, rows, n_steps)
        n_blocks = pl.cdiv(rows, per)
        first_blk = first_row // per

        def in_map(i, first_blk=first_blk, last=n_blocks - 1):
            return (layer, first_blk + jnp.minimum(i, last), 0)

        def out_map(i, last=n_blocks - 1):
            return (jnp.minimum(i, last), 0)

        in_specs.append(pl.BlockSpec((None, per, D_MODEL), in_map))
        out_specs.append(pl.BlockSpec((per, D_MODEL), out_map))
        out_shapes.append(jax.ShapeDtypeStruct((rows, D_MODEL), jnp.bfloat16))
    return in_specs, out_specs, out_shapes


def _cast_chunks(in_refs, out_refs):
    for src, dst in zip(in_refs, out_refs):
        dst[...] = src[...].astype(jnp.bfloat16)


def _cast_kernel(src, dst):
    _cast_chunks([src], [dst])


def _cast_front_window(layer, big_f32):
    n_steps = 4
    in_specs, out_specs, out_shapes = _cast_specs(layer, [_W_FRONT], n_steps)
    return pl.pallas_call(
        _cast_kernel,
        grid=(n_steps,),
        in_specs=in_specs,
        out_specs=out_specs[0],
        out_shape=out_shapes[0],
        compiler_params=_params(),
        name="cast_weights",
    )(big_f32[_SRC_W_IN_T])


def _store_keys(k_ref, k_nope, kr):
    k_rope = pltpu.roll(kr, QK_NOPE, 1) + pltpu.roll(kr, QK_NOPE + QK_ROPE, 1)
    nope_lanes = lax.broadcasted_iota(jnp.int32, kr.shape, 1) < QK_NOPE
    for h in range(N_HEADS):
        pair = k_nope[:, (h // 2) * LANES:(h // 2 + 1) * LANES]
        if h % 2:
            pair = pltpu.roll(pair, LANES - QK_NOPE, 1)
        k_ref[h] = jnp.where(nope_lanes, pair, k_rope).astype(jnp.bfloat16)


def _cache_kv_kernel(lat_t_ref, wk_ref, wv_ref, k_ref, v_ref):
    lat_t = lat_t_ref[...]
    pad = jnp.zeros((_LAT_EXT - MLA_LATENT, lat_t.shape[1]), jnp.float32)
    lat = jnp.concatenate([lat_t, pad], axis=0).T
    ckv_bf = lat[:, 0:KV_LORA].astype(jnp.bfloat16)
    _store_keys(k_ref, _dot(ckv_bf, wk_ref[...]), lat[:, KV_LORA:_LAT_EXT])
    v = _dot(ckv_bf, wv_ref[...]).astype(jnp.bfloat16)
    for p in range(N_PAIRS):
        v_ref[p] = v[:, p * LANES:(p + 1) * LANES]


def _cache_kv(cache_t, wk, wv):
    nb, _, _, past = cache_t.shape

    def kv(n):
        return (jax.ShapeDtypeStruct((DEPTH, nb, n, past, LANES), jnp.bfloat16),
                pl.BlockSpec((None, None, n, past, LANES), lambda l, b: (l, b, 0, 0, 0)))

    (k_shape, k_spec), (v_shape, v_spec) = kv(N_HEADS), kv(N_PAIRS)
    return pl.pallas_call(
        _cache_kv_kernel,
        grid=(DEPTH, nb),
        in_specs=[
            pl.BlockSpec((None, None, MLA_LATENT, past), lambda l, b: (b, l, 0, 0)),
            pl.BlockSpec((None, KV_LORA, N_HEADS * QK_NOPE), lambda l, b: (l, 0, 0)),
            pl.BlockSpec((None, KV_LORA, N_HEADS * V_HEAD), lambda l, b: (l, 0, 0)),
        ],
        out_specs=[k_spec, v_spec],
        out_shape=[k_shape, v_shape],
        compiler_params=_params(2),
        name="cache_kv",
    )(cache_t, wk, wv)


class _Pass:
    def __init__(self, x, tile, first_step, first_cond_row, shared_cond):
        self.nb, self.seq, _ = x.shape
        self.tm = min(tile, self.seq)
        assert self.seq % self.tm == 0
        self.nt = self.seq // self.tm
        self.steps = self.nb * self.nt
        self.first = first_step
        self.first_cond_row = first_cond_row
        self.shared_cond = shared_cond

    def local(self, i):
        return jnp.clip(i - self.first, 0, self.steps - 1)

    def batch(self, i):
        return self.local(i) // self.nt

    def tile(self, i):
        return self.local(i) % self.nt

    def cond_row(self, i):
        return self.first_cond_row if self.shared_cond else self.first_cond_row + self.batch(i)


def _passes(x_prompt, x_sample, tile):
    ctx = _Pass(x_prompt, tile, 0, 0, True)
    dec = _Pass(x_sample, tile, ctx.steps, 1, False)
    return ctx, dec


def _const_spec(l, shape):
    return pl.BlockSpec((None,) + shape, lambda i: (l,) + (0,) * len(shape), pipeline_mode=pl.Buffered(1))


def _whole_spec(arr):
    return pl.BlockSpec(arr.shape, lambda i: (0,) * arr.ndim, pipeline_mode=pl.Buffered(1))


def _row_block_spec(rows, block_index):
    return pl.BlockSpec((rows, D_MODEL), lambda i: (block_index, 0), pipeline_mode=pl.Buffered(1))


_N_FRONT_WEIGHTS = 11


def _front_tile(l, seq_len, j, x_ref, xp_ref, xn_ref, qtab_ref, kcos_ref, ksin_ref, mod_row, weights, outs, scratch):
    gpre_ref, w_ref, convw_ref, convb_ref, gq_ref, wuq_ref, gkv_ref, wk_ref, wv_ref, poolw_ref, pscale_ref = weights
    g_pre, conv_b, g_q, g_kv, p_scale = (r[l:l + 1, :] for r in (gpre_ref, convb_ref, gq_ref, gkv_ref, pscale_ref))
    conv_w = [convw_ref[tap, l:l + 1, :] for tap in range(3)]
    q_out, k_out, v_out, ya_out, yc_out, sbz_out = outs[:6]
    u_scr, cu_scr = scratch
    tm = x_ref.shape[0]
    last_j = seq_len // tm - 1

    if last_j == 0:
        hn = _modulated_norm(x_ref[...], g_pre, mod_row).astype(jnp.bfloat16)
        acx = _dot_nt(hn, w_ref[_R_ACX:_R_ACX + 2 * BRANCH_DIM, :])
        zeros = jnp.zeros((HALO, BRANCH_DIM), jnp.float32)
        for scr in (u_scr, cu_scr):
            scr[0:HALO, :] = zeros
            scr[HALO + tm:2 * HALO + tm, :] = zeros
        u_scr[HALO:HALO + tm, :] = acx[:, 0:BRANCH_DIM] * acx[:, BRANCH_DIM:2 * BRANCH_DIM]
        cu_scr[HALO:HALO + tm, :] = _dot_nt(hn, w_ref[_R_CU:_R_CU + BRANCH_DIM, :])
    else:
        x_ext = jnp.concatenate([xp_ref[...], x_ref[...], xn_ref[...]], axis=0)
        hn_ext = _modulated_norm(x_ext, g_pre, mod_row).astype(jnp.bfloat16)
        hn = hn_ext[HALO:HALO + tm]
        acx = _dot_nt(hn_ext, w_ref[_R_ACX:_R_ACX + 2 * BRANCH_DIM, :])
        row = lax.broadcasted_iota(jnp.int32, (tm + 2 * HALO, 1), 0)
        first_valid = jnp.where(j > 0, 0, HALO)
        end_valid = jnp.where(j < last_j, tm + 2 * HALO, tm + HALO)
        valid = jnp.logical_and(row >= first_valid, row < end_valid)
        ext_rows = slice(0, tm + 2 * HALO)
        u_scr[ext_rows, :] = jnp.where(valid, acx[:, 0:BRANCH_DIM] * acx[:, BRANCH_DIM:2 * BRANCH_DIM], 0.0)
        cu_scr[ext_rows, :] = jnp.where(valid, _dot_nt(hn_ext, w_ref[_R_CU:_R_CU + BRANCH_DIM, :]), 0.0)

    pm = _dot_nt(hn, w_ref[_R_MAIN:_R_MAIN + _MAIN_ROWS, :])
    conv = (u_scr[HALO - 1:HALO - 1 + tm, :] * conv_w[0]
            + u_scr[HALO:HALO + tm, :] * conv_w[1]
            + u_scr[HALO + 1:HALO + 1 + tm, :] * conv_w[2]
            + conv_b)
    ya = _silu(pm[:, 0:BRANCH_DIM]) * (_dot_nt(hn, w_ref[_R_AB:_R_AB + BRANCH_DIM, :]) * conv)
    ya_out[...] = ya.astype(jnp.bfloat16)

    t = j * tm + lax.broadcasted_iota(jnp.int32, (tm, POOL_GROUP), 0)
    pooled = []
    for gi, win in enumerate(POOL_WINDOWS):
        half = win // 2
        cols = slice(gi * POOL_GROUP, (gi + 1) * POOL_GROUP)
        total = cu_scr[HALO - half:HALO - half + tm, cols]
        for k in range(-half + 1, half):
            total = total + cu_scr[HALO + k:HALO + k + tm, cols]
        count = jnp.minimum(t + half, seq_len) - jnp.maximum(t - half, 0)
        pooled.append((total / count.astype(jnp.float32) - cu_scr[HALO:HALO + tm, cols]).astype(jnp.bfloat16))
    mixed = [_dot(jnp.concatenate(pooled[2 * n:2 * n + 2], axis=1), poolw_ref[n]) for n in range(len(pooled) // 2)]
    mixed = jnp.concatenate(mixed, axis=1) * p_scale
    c_z = _dot_nt(hn, w_ref[_R_CZ:_R_CZ + BRANCH_DIM, :])
    yc_out[...] = (_silu(c_z) * mixed).astype(jnp.bfloat16)

    qn = _rms(pm[:, _QD0:_QD0 + Q_LORA], g_q).astype(jnp.bfloat16)
    q = _dot(qn, wuq_ref[...])
    qtab = qtab_ref[...]
    ckv = _rms(pm[:, _CKV0:_CKV0 + KV_LORA], g_kv)
    kr = pm[:, _KR0:_KR0 + LANES]
    if len(outs) > 6:
        lat_out = outs[6]
        if len(lat_out.shape) == 3:
            for later in range(1, lat_out.shape[0]):
                lat_out[later] = jnp.zeros(lat_out.shape[1:], jnp.float32)
            lat_out = lat_out.at[0]
        lat_out[0:KV_LORA, :] = ckv.T
        lat_out[KV_LORA:MLA_LATENT, :] = kr.T[0:QK_ROPE, :]
    lane = lax.broadcasted_iota(jnp.int32, (tm, LANES), 1)
    quarter = QK_ROPE // 4
    partner = jnp.where(lane % (2 * quarter) < quarter,
                        pltpu.roll(kr, LANES - quarter, 1), pltpu.roll(kr, quarter, 1))
    kr_rot = jnp.where(lane < QK_ROPE, kr * kcos_ref[...] + partner * ksin_ref[...], 0.0)
    ckv_bf = ckv.astype(jnp.bfloat16)
    _store_keys(k_out, _dot(ckv_bf, wk_ref[...]), kr_rot)
    v = _dot(ckv_bf, wv_ref[...]).astype(jnp.bfloat16)
    for h in range(N_HEADS):
        q_out[h] = (q[:, h * HEAD_PAD:(h + 1) * HEAD_PAD] * qtab).astype(jnp.bfloat16)
    for p in range(N_PAIRS):
        v_out[p] = v[:, p * LANES:(p + 1) * LANES]
    sbz_out[...] = _silu(_dot_nt(hn, w_ref[_R_BZ:_R_BZ + BRANCH_DIM, :])).astype(jnp.bfloat16)


def _mod_row(mod_ref, ctx, dec, i):
    row = jnp.where(i < dec.first, ctx.cond_row(i), dec.cond_row(i))
    return mod_ref[pl.ds(row, 1), :]


def _front_kernel(l, ctx, dec, n_cast, n_aliased, *refs):
    n_in = 6
    ctx_in, dec_in = refs[0:n_in], refs[n_in:2 * n_in]
    mod_ref = refs[2 * n_in]
    n_w = 2 * n_in + 1 + _N_FRONT_WEIGHTS
    weights = refs[2 * n_in + 1:n_w]
    outs = refs[n_w + n_cast + n_aliased:-2]
    ctx_out, dec_out = outs[0:7], outs[7:13]
    scratch = refs[-2:]
    i = pl.program_id(0)
    _cast_chunks(refs[n_w:n_w + n_cast], outs[13:])

    @pl.when(i < dec.first)
    def _():
        _front_tile(l, ctx.seq, ctx.tile(i), *ctx_in, _mod_row(mod_ref, ctx, dec, i), weights, ctx_out, scratch)

    @pl.when(i >= dec.first)
    def _():
        _front_tile(l, dec.seq, dec.tile(i), *dec_in, _mod_row(mod_ref, ctx, dec, i), weights, dec_out, scratch)


def _front(l, x_prompt, x_sample, mod, w, w_front, ctx_tabs, dec_tabs, state_t, big_f32, cast_windows):
    ctx, dec = _passes(x_prompt, x_sample, FRONT_TILE)
    n_steps = ctx.steps + dec.steps

    def pass_in_specs(p):
        tm = p.tm
        hb = tm // HALO
        n_hblk = p.seq // HALO
        tab = pl.BlockSpec((tm, LANES), lambda i: (p.tile(i), 0))
        return [
            pl.BlockSpec((None, tm, D_MODEL), lambda i: (p.batch(i), p.tile(i), 0)),
            pl.BlockSpec((None, HALO, D_MODEL), lambda i: (p.batch(i), jnp.maximum(p.tile(i) * hb - 1, 0), 0)),
            pl.BlockSpec((None, HALO, D_MODEL),
                         lambda i: (p.batch(i), jnp.minimum((p.tile(i) + 1) * hb, n_hblk - 1), 0)),
            tab, tab, tab,
        ]

    def pass_out(p, with_latent):
        tm = p.tm

        def blocks(n):
            return (jax.ShapeDtypeStruct((p.nb, n, p.seq, LANES), jnp.bfloat16),
                    pl.BlockSpec((None, n, tm, LANES), lambda i: (p.batch(i), 0, p.tile(i), 0)))

        (head_shape, head_spec), (pair_shape, pair_spec) = blocks(N_HEADS), blocks(N_PAIRS)
        br_shape = jax.ShapeDtypeStruct((p.nb, p.seq, BRANCH_DIM), jnp.bfloat16)
        br_spec = pl.BlockSpec((None, tm, BRANCH_DIM), lambda i: (p.batch(i), p.tile(i), 0))
        shapes = [head_shape, head_shape, pair_shape] + [br_shape] * 3
        specs = [head_spec, head_spec, pair_spec] + [br_spec] * 3
        if with_latent:
            shapes.append(jax.ShapeDtypeStruct((p.nb, DEPTH, MLA_LATENT, p.seq), jnp.float32))
            if state_t is None:
                specs.append(pl.BlockSpec((None, DEPTH, MLA_LATENT, tm), lambda i: (p.batch(i), 0, 0, p.tile(i))))
            else:
                specs.append(pl.BlockSpec((None, None, MLA_LATENT, tm), lambda i: (p.batch(i), l, 0, p.tile(i))))
        return shapes, specs

    weight_specs = [
        _whole_spec(w["g_pre"]),
        _row_block_spec(_FRONT_ROWS, 0),
        _whole_spec(w["conv_w"]),
        _whole_spec(w["conv_b"]),
        _whole_spec(w["g_q"]),
        _const_spec(l, (Q_LORA, HP)),
        _whole_spec(w["g_kv"]),
        _const_spec(l, (KV_LORA, N_HEADS * QK_NOPE)),
        _const_spec(l, (KV_LORA, N_HEADS * V_HEAD)),
        _const_spec(l, (len(POOL_WINDOWS) // 2, 2 * POOL_GROUP, 2 * POOL_GROUP)),
        _whole_spec(w["pool_scale"]),
    ]
    assert len(weight_specs) == _N_FRONT_WEIGHTS
    ctx_shapes, ctx_specs = pass_out(ctx, True)
    dec_shapes, dec_specs = pass_out(dec, False)
    in_specs = pass_in_specs(ctx) + pass_in_specs(dec) + [_const_spec(l, mod.shape[1:])] + weight_specs
    args = [x_prompt, x_prompt, x_prompt, *ctx_tabs, x_sample, x_sample, x_sample, *dec_tabs, mod,
            w["g_pre"], w_front, w["conv_w"], w["conv_b"], w["g_q"], w["wuq"], w["g_kv"],
            w["wk"], w["wv"], w["pool_w"], w["pool_scale"]]
    cast_in, cast_out, cast_shapes = _cast_specs(l, cast_windows, n_steps)
    in_specs += cast_in
    args += [big_f32[src] for src, _, _ in cast_windows]
    aliases = {}
    if state_t is not None:
        aliases = {len(args): len(ctx_shapes) - 1}
        in_specs.append(pl.BlockSpec(memory_space=pl.ANY))
        args.append(state_t)
    outs = pl.pallas_call(
        functools.partial(_front_kernel, l, ctx, dec, len(cast_windows), len(aliases)),
        grid=(n_steps,),
        in_specs=in_specs,
        out_specs=ctx_specs + dec_specs + cast_out,
        out_shape=ctx_shapes + dec_shapes + cast_shapes,
        scratch_shapes=[pltpu.VMEM((max(ctx.tm, dec.tm) + 2 * HALO, BRANCH_DIM), jnp.float32)] * 2,
        input_output_aliases=aliases,
        compiler_params=_params(),
        name="front",
    )(*args)
    return outs[:7], outs[7:13], tuple(outs[13:])


_N_BACK_WEIGHTS = 5


def _back_tile(l, x_ref, q_ref, k_ref, v_ref, cache, ya_ref, yc_ref, sbz_ref, mod_row, weights, out_ref):
    gpre_ref, wg_ref, wbr_ref, wo_ref, gpost_ref = weights
    g_pre, g_post = gpre_ref[l:l + 1, :], gpost_ref[l:l + 1, :]
    x = x_ref[...]

    def with_ones(v):
        return jnp.concatenate([v, jnp.ones_like(v)], axis=1)

    low_half = lax.broadcasted_iota(jnp.int32, (x.shape[0], LANES), 1) < V_HEAD
    pairs = []
    for p in range(N_PAIRS):
        v_ext = with_ones(v_ref[p])
        if cache is not None:
            kc_ref, vc_ref = cache
            vc_ext = with_ones(vc_ref[p])
        halves = []
        for h in (2 * p, 2 * p + 1):
            q = q_ref[h]
            s = _dot_nt(q, k_ref[h])
            m = jnp.max(s, axis=-1, keepdims=True)
            if cache is not None:
                sc = _dot_nt(q, kc_ref[h])
                m = jnp.maximum(m, jnp.max(sc, axis=-1, keepdims=True))
            o = _dot(jnp.exp2((s - m).astype(jnp.bfloat16)), v_ext)
            if cache is not None:
                o = o + _dot(jnp.exp2((sc - m).astype(jnp.bfloat16)), vc_ext)
            halves.append(o[:, 0:LANES] / o[:, LANES:2 * LANES])
        pairs.append(jnp.where(low_half, halves[0], halves[1]))
    attn = jnp.concatenate(pairs, axis=1)
    yb = (sbz_ref[...].astype(jnp.float32) * attn).astype(jnp.bfloat16)

    hn = _modulated_norm(x, g_pre, mod_row).astype(jnp.bfloat16)
    merged = None
    for n, y in enumerate((ya_ref[...], yb, yc_ref[...])):
        r0 = _R_MERGE - _BACK_START + n * D_MODEL
        gate = jax.nn.sigmoid(_dot_nt(hn, wg_ref[r0:r0 + D_MODEL, :]))
        term = gate * _dot(y, wbr_ref[n * BRANCH_DIM:(n + 1) * BRANCH_DIM, :])
        merged = term if merged is None else merged + term
    out = _rms(_dot(merged.astype(jnp.bfloat16), wo_ref[...]), g_post)
    out_ref[...] = x + mod_row[:, 2 * D_MODEL:3 * D_MODEL] * out


def _back_kernel(l, ctx, dec, n_cast, *refs):
    x_c, q_c, k_c, v_c, ya_c, yc_c, sbz_c = refs[0:7]
    x_d, q_d, k_d, v_d, kc_ref, vc_ref, ya_d, yc_d, sbz_d = refs[7:16]
    mod_ref = refs[16]
    n_w = 17 + _N_BACK_WEIGHTS
    weights = refs[17:n_w]
    out_c, out_d = refs[n_w + n_cast:n_w + n_cast + 2]
    i = pl.program_id(0)
    _cast_chunks(refs[n_w:n_w + n_cast], refs[n_w + n_cast + 2:])

    @pl.when(i < dec.first)
    def _():
        _back_tile(l, x_c, q_c, k_c, v_c, None, ya_c, yc_c, sbz_c, _mod_row(mod_ref, ctx, dec, i), weights, out_c)

    @pl.when(i >= dec.first)
    def _():
        _back_tile(l, x_d, q_d, k_d, v_d, (kc_ref, vc_ref), ya_d, yc_d, sbz_d, _mod_row(mod_ref, ctx, dec, i),
                   weights, out_d)


def _back(l, x_prompt, x_sample, mod, w, big_bf, ctx_fr, dec_fr, kc, vc, big_f32, cast_windows):
    ctx, dec = _passes(x_prompt, x_sample, BACK_TILE)
    past = kc.shape[3]
    n_steps = ctx.steps + dec.steps

    def pass_specs(p):
        tq = p.tm
        tile = pl.BlockSpec((None, tq, D_MODEL), lambda i: (p.batch(i), p.tile(i), 0))
        q = pl.BlockSpec((None, N_HEADS, tq, HEAD_PAD), lambda i: (p.batch(i), 0, p.tile(i), 0))
        k = pl.BlockSpec((None, N_HEADS, p.seq, LANES), lambda i: (p.batch(i), 0, 0, 0))
        v = pl.BlockSpec((None, N_PAIRS, p.seq, LANES), lambda i: (p.batch(i), 0, 0, 0))
        br = pl.BlockSpec((None, tq, BRANCH_DIM), lambda i: (p.batch(i), p.tile(i), 0))
        return tile, q, k, v, br

    tile_c, q_c, k_c, v_c, br_c = pass_specs(ctx)
    tile_d, q_d, k_d, v_d, br_d = pass_specs(dec)
    kc_spec = pl.BlockSpec((None, None, N_HEADS, past, LANES), lambda i: (l, dec.batch(i), 0, 0, 0))
    vc_spec = pl.BlockSpec((None, None, N_PAIRS, past, LANES), lambda i: (l, dec.batch(i), 0, 0, 0))
    in_specs = ([tile_c, q_c, k_c, v_c, br_c, br_c, br_c]
                + [tile_d, q_d, k_d, v_d, kc_spec, vc_spec, br_d, br_d, br_d]
                + [_const_spec(l, mod.shape[1:]),
                   _whole_spec(w["g_pre"]),
                   _row_block_spec(_BACK_ROWS, 0),
                   _row_block_spec(3 * BRANCH_DIM, 0),
                   _row_block_spec(D_MODEL, 0),
                   _whole_spec(w["g_post"])])
    q_cx, k_cx, v_cx, ya_cx, yc_cx, sbz_cx = ctx_fr
    q_dx, k_dx, v_dx, ya_dx, yc_dx, sbz_dx = dec_fr
    args = [x_prompt, q_cx, k_cx, v_cx, ya_cx, yc_cx, sbz_cx,
            x_sample, q_dx, k_dx, v_dx, kc, vc, ya_dx, yc_dx, sbz_dx,
            mod, w["g_pre"], *big_bf, w["g_post"]]
    out_specs = [tile_c, tile_d]
    out_shape = [jax.ShapeDtypeStruct(x_prompt.shape, jnp.float32), jax.ShapeDtypeStruct(x_sample.shape, jnp.float32)]
    cast_in, cast_out, cast_shapes = _cast_specs(l + 1, cast_windows, n_steps)
    in_specs += cast_in
    args += [big_f32[src] for src, _, _ in cast_windows]
    out_specs += cast_out
    out_shape += cast_shapes
    outs = pl.pallas_call(
        functools.partial(_back_kernel, l, ctx, dec, len(cast_windows)),
        grid=(n_steps,),
        in_specs=in_specs,
        out_specs=out_specs,
        out_shape=out_shape,
        compiler_params=_params(),
        name="back",
    )(*args)
    return outs[0], outs[1], tuple(outs[2:])


def _rope_swap_perm():
    quarter = QK_ROPE // 4
    idx = np.arange(QK_ROPE).reshape(2, 2, quarter)
    return idx[:, ::-1, :].reshape(-1)


def _rope_tables(n_tokens, rotate):
    if rotate:
        t = np.arange(n_tokens)
        pos = np.stack([t // GRID_W, t % GRID_W], axis=1).astype(np.float64)
        axis_dim = QK_ROPE // 2
        inv = 1.0 / (ROPE_THETA ** (np.arange(0, axis_dim, 2, dtype=np.float64) / axis_dim))
        ang = pos[:, :, None] * inv
        cos, sin = np.cos(ang), np.sin(ang)
        cfull = np.stack([cos, cos], axis=2).reshape(n_tokens, QK_ROPE)
        sfull = np.stack([-sin, sin], axis=2).reshape(n_tokens, QK_ROPE)
    else:
        cfull = np.ones((n_tokens, QK_ROPE))
        sfull = np.zeros((n_tokens, QK_ROPE))
    scale = (QK_NOPE + QK_ROPE) ** -0.5 * np.log2(np.e)
    qtab = np.concatenate([np.full((n_tokens, QK_NOPE), scale), cfull * scale, sfull * scale], axis=1)
    pad = np.zeros((n_tokens, LANES - QK_ROPE))
    tabs = (qtab, np.concatenate([cfull, pad], axis=1), np.concatenate([sfull, pad], axis=1))
    return tuple(jnp.asarray(tab, jnp.float32) for tab in tabs)


def _prepare_weights(g_pre, g_post, conv_w, conv_b, g_q, w_uq, g_kv, w_ukv, pool_w, pool_scale):
    bf = jnp.bfloat16
    perm = _rope_swap_perm()

    uq = w_uq.reshape(DEPTH, Q_LORA, N_HEADS, QK_NOPE + QK_ROPE)
    rope_q = uq[..., QK_NOPE:]
    wuq = jnp.concatenate([uq, rope_q[..., perm]], axis=-1).reshape(DEPTH, Q_LORA, HP).astype(bf)

    ukv = w_ukv.reshape(DEPTH, KV_LORA, N_HEADS, QK_NOPE + V_HEAD)
    wk = ukv[..., :QK_NOPE].reshape(DEPTH, KV_LORA, N_HEADS * QK_NOPE).astype(bf)
    wv = ukv[..., QK_NOPE:].reshape(DEPTH, KV_LORA, N_HEADS * V_HEAD).astype(bf)

    pw = pool_w.astype(bf).reshape(DEPTH, len(POOL_WINDOWS) // 2, 2, POOL_GROUP, POOL_GROUP)
    zero = jnp.zeros_like(pw[:, :, 0])
    pool_bd = jnp.concatenate([jnp.concatenate([pw[:, :, 0], zero], axis=-1),
                               jnp.concatenate([zero, pw[:, :, 1]], axis=-1)], axis=-2)

    return {
        "g_pre": g_pre, "g_post": g_post, "conv_w": jnp.swapaxes(conv_w, 0, 1), "conv_b": conv_b,
        "g_q": g_q, "g_kv": g_kv, "pool_scale": pool_scale,
        "wuq": wuq, "wk": wk, "wv": wv, "pool_w": pool_bd,
    }


def kernel(x_prompt, x_sample, cache_mla_latent, c, c_ctx, w_mod, b_mod, g_pre, g_post, w_in, conv_w, conv_b,
           g_q, w_uq, g_kv, w_ukv, pool_w, pool_scale, w_branch, w_o):
    n_dec = x_sample.shape[0]
    assert 1 + n_dec <= COND_ROWS
    assert w_in.shape == (DEPTH, D_MODEL, W_IN_COLS)
    w = _prepare_weights(g_pre, g_post, conv_w, conv_b, g_q, w_uq, g_kv, w_ukv, pool_w, pool_scale)
    big_f32 = (jnp.swapaxes(w_in, 1, 2), w_branch.reshape(DEPTH, 3 * BRANCH_DIM, D_MODEL), w_o)
    w_front = _cast_front_window(0, big_f32)
    cond = jnp.concatenate([c_ctx[None, :], c, jnp.zeros((COND_ROWS - 1 - n_dec, D_MODEL), jnp.float32)], axis=0)
    mod = _modulation(cond, w_mod, b_mod)
    kc, vc = _cache_kv(jnp.swapaxes(cache_mla_latent, 2, 3), w["wk"], w["wv"])
    ctx_tabs = _rope_tables(x_prompt.shape[1], rotate=False)
    dec_tabs = _rope_tables(x_sample.shape[1], rotate=True)

    back_windows = (_W_BACK, _W_BRANCH, _W_OUT)
    h, hs, state_t, big_bf = x_prompt, x_sample, None, None
    for l in range(DEPTH):
        ctx_fr, dec_fr, cast = _front(l, h, hs, mod, w, w_front, ctx_tabs, dec_tabs, state_t, big_f32,
                                      back_windows if big_bf is None else ())
        state_t = ctx_fr[6]
        big_bf = cast if big_bf is None else big_bf
        next_windows = (_W_FRONT,) + back_windows if l + 1 < DEPTH else ()
        h, hs, cast = _back(l, h, hs, mod, w, big_bf, ctx_fr[:6], dec_fr, kc, vc, big_f32, next_windows)
        if next_windows:
            w_front, big_bf = cast[0], cast[1:]
    return (h, hs, jnp.swapaxes(state_t, 2, 3))
```

```python
import functools

import numpy as np
import jax
import jax.numpy as jnp
from jax import lax
from jax.experimental import pallas as pl
from jax.experimental.pallas import tpu as pltpu

D_MODEL = 1024
DEPTH = 2
GRID_W = 64
EPS = 1e-6
BRANCH_DIM = 512
N_HEADS = 8
QK_NOPE = 64
QK_ROPE = 32
V_HEAD = 64
Q_LORA = 384
KV_LORA = 256
MLA_LATENT = KV_LORA + QK_ROPE
ROPE_THETA = 10000.0
POOL_WINDOWS = (2, 4, 8, 16)
POOL_GROUP = 128

LANES = 128
HEAD_PAD = LANES
HP = N_HEADS * HEAD_PAD
N_PAIRS = N_HEADS * V_HEAD // LANES
FRONT_TILE = 512
BACK_TILE = 256
HALO = 16
COND_ROWS = 8
VMEM_LIMIT = 56 * 1024 * 1024

_R_AB = 0
_R_ACX = BRANCH_DIM
_R_MAIN = 3 * BRANCH_DIM
_MAIN_ROWS = BRANCH_DIM + Q_LORA + KV_LORA + LANES
_R_BZ = 4 * BRANCH_DIM + Q_LORA + MLA_LATENT
_R_CU = _R_BZ + BRANCH_DIM
_R_CZ = _R_CU + BRANCH_DIM
_R_MERGE = _R_CZ + BRANCH_DIM
W_IN_COLS = _R_MERGE + 3 * D_MODEL
_FRONT_ROWS = -(-_R_MERGE // LANES) * LANES
_QD0 = BRANCH_DIM
_CKV0 = _QD0 + Q_LORA
_KR0 = _CKV0 + KV_LORA
_LAT_EXT = KV_LORA + LANES


def _dot(a, b):
    return jnp.dot(a, b, preferred_element_type=jnp.float32)


def _dot_nt(a, b):
    return lax.dot_general(a, b, (((1,), (1,)), ((), ())), preferred_element_type=jnp.float32)


def _rms(x, g):
    return x * lax.rsqrt(jnp.mean(x * x, axis=-1, keepdims=True) + EPS) * g


def _silu(x):
    return x * jax.nn.sigmoid(x)


def _modulated_norm(x, g_pre, mod_row):
    shift = mod_row[:, 0:D_MODEL]
    scale = mod_row[:, D_MODEL:2 * D_MODEL]
    return _rms(x, g_pre) * (1.0 + scale) + shift


def _params(n_grid_dims=1):
    return pltpu.CompilerParams(dimension_semantics=("arbitrary",) * n_grid_dims, vmem_limit_bytes=VMEM_LIMIT)


def _mod_chunk(layer, cond_ref, w_ref, b_ref, out_ref):
    h = _silu(cond_ref[...]).astype(jnp.bfloat16)
    out_ref[...] = _dot(h, w_ref[...].astype(jnp.bfloat16)) + b_ref[layer:layer + 1, :]


def _mod_specs(layer, col_tile):
    last = 3 * D_MODEL // col_tile - 1
    in_specs = [
        pl.BlockSpec((COND_ROWS, D_MODEL), lambda i: (0, 0), pipeline_mode=pl.Buffered(1)),
        pl.BlockSpec((None, D_MODEL, col_tile), lambda i: (layer, 0, jnp.minimum(i, last))),
        pl.BlockSpec((DEPTH, col_tile), lambda i: (0, jnp.minimum(i, last))),
    ]
    out_spec = pl.BlockSpec((COND_ROWS, col_tile), lambda i: (0, jnp.minimum(i, last)))
    return in_specs, out_spec, jax.ShapeDtypeStruct((COND_ROWS, 3 * D_MODEL), jnp.float32), last + 1


def _modulation(layer, cond, w_mod, b_mod):
    col_tile = 3 * D_MODEL // 2
    in_specs, out_spec, out_shape, n_steps = _mod_specs(layer, col_tile)
    return pl.pallas_call(
        functools.partial(_mod_chunk, layer),
        grid=(n_steps,),
        in_specs=in_specs,
        out_specs=out_spec,
        out_shape=out_shape,
        compiler_params=_params(),
        name="modulation",
    )(cond, w_mod, b_mod)


_SRC_W_IN_T, _SRC_W_BRANCH, _SRC_W_O = 0, 1, 2
_BACK_START = (_R_MERGE // 1024) * 1024
_W_FRONT = (_SRC_W_IN_T, 0, _FRONT_ROWS)
_W_BACK = (_SRC_W_IN_T, _BACK_START, W_IN_COLS - _BACK_START)
_W_BRANCH = (_SRC_W_BRANCH, 0, 3 * BRANCH_DIM)
_W_OUT = (_SRC_W_O, 0, D_MODEL)
_BACK_ROWS = _W_BACK[2]


def _chunk_rows(first_row, rows, n_steps):
    if first_row == 0:
        return pl.cdiv(pl.cdiv(rows, n_steps), 16) * 16
    per = LANES
    while first_row % per or pl.cdiv(rows, per) > n_steps:
        per *= 2
        assert per <= first_row
    return per


def _cast_specs(layer, windows, n_steps):
    in_specs, out_specs, out_shapes = [], [], []
    for _, first_row, rows in windows:
        per = _chunk_rows(first_row, rows, n_steps)
        n_blocks = pl.cdiv(rows, per)
        first_blk = first_row // per

        def in_map(i, first_blk=first_blk, last=n_blocks - 1):
            return (layer, first_blk + jnp.minimum(i, last), 0)

        def out_map(i, last=n_blocks - 1):
            return (jnp.minimum(i, last), 0)

        in_specs.append(pl.BlockSpec((None, per, D_MODEL), in_map))
        out_specs.append(pl.BlockSpec((per, D_MODEL), out_map))
        out_shapes.append(jax.ShapeDtypeStruct((rows, D_MODEL), jnp.bfloat16))
    return in_specs, out_specs, out_shapes


def _cast_chunks(in_refs, out_refs):
    for src, dst in zip(in_refs, out_refs):
        dst[...] = src[...].astype(jnp.bfloat16)


def _cast_kernel(src, dst):
    _cast_chunks([src], [dst])


def _cast_front_window(layer, big_f32):
    n_steps = 4
    in_specs, out_specs, out_shapes = _cast_specs(layer, [_W_FRONT], n_steps)
    return pl.pallas_call(
        _cast_kernel,
        grid=(n_steps,),
        in_specs=in_specs,
        out_specs=out_specs[0],
        out_shape=out_shapes[0],
        compiler_params=_params(),
        name="cast_weights",
    )(big_f32[_SRC_W_IN_T])


def _store_keys(k_ref, k_nope, kr):
    k_rope = pltpu.roll(kr, QK_NOPE, 1) + pltpu.roll(kr, QK_NOPE + QK_ROPE, 1)
    nope_lanes = lax.broadcasted_iota(jnp.int32, kr.shape, 1) < QK_NOPE
    for h in range(N_HEADS):
        pair = k_nope[:, (h // 2) * LANES:(h // 2 + 1) * LANES]
        if h % 2:
            pair = pltpu.roll(pair, LANES - QK_NOPE, 1)
        k_ref[h] = jnp.where(nope_lanes, pair, k_rope).astype(jnp.bfloat16)


def _cache_kv_kernel(lat_t_ref, wk_ref, wv_ref, k_ref, v_ref):
    lat_t = lat_t_ref[...]
    pad = jnp.zeros((_LAT_EXT - MLA_LATENT, lat_t.shape[1]), jnp.float32)
    lat = jnp.concatenate([lat_t, pad], axis=0).T
    ckv_bf = lat[:, 0:KV_LORA].astype(jnp.bfloat16)
    _store_keys(k_ref, _dot(ckv_bf, wk_ref[...]), lat[:, KV_LORA:_LAT_EXT])
    v = _dot(ckv_bf, wv_ref[...]).astype(jnp.bfloat16)
    for p in range(N_PAIRS):
        v_ref[p] = v[:, p * LANES:(p + 1) * LANES]


def _cache_kv_specs(cache_t):
    nb, _, _, past = cache_t.shape
    n_jobs = DEPTH * nb

    def layer(i):
        return jnp.minimum(i, n_jobs - 1) // nb

    def request(i):
        return jnp.minimum(i, n_jobs - 1) % nb

    def kv(n):
        return (jax.ShapeDtypeStruct((DEPTH, nb, n, past, LANES), jnp.bfloat16),
                pl.BlockSpec((None, None, n, past, LANES), lambda i: (layer(i), request(i), 0, 0, 0)))

    (k_shape, k_spec), (v_shape, v_spec) = kv(N_HEADS), kv(N_PAIRS)
    in_specs = [
        pl.BlockSpec((None, None, MLA_LATENT, past), lambda i: (request(i), layer(i), 0, 0)),
        pl.BlockSpec((None, KV_LORA, N_HEADS * QK_NOPE), lambda i: (layer(i), 0, 0)),
        pl.BlockSpec((None, KV_LORA, N_HEADS * V_HEAD), lambda i: (layer(i), 0, 0)),
    ]
    return in_specs, [k_spec, v_spec], [k_shape, v_shape], n_jobs


class _Pass:
    def __init__(self, x, tile, first_step, first_cond_row, shared_cond):
        self.nb, self.seq, _ = x.shape
        self.tm = min(tile, self.seq)
        assert self.seq % self.tm == 0
        self.nt = self.seq // self.tm
        self.steps = self.nb * self.nt
        self.first = first_step
        self.first_cond_row = first_cond_row
        self.shared_cond = shared_cond

    def local(self, i):
        return jnp.clip(i - self.first, 0, self.steps - 1)

    def batch(self, i):
        return self.local(i) // self.nt

    def tile(self, i):
        return self.local(i) % self.nt

    def cond_row(self, i):
        return self.first_cond_row if self.shared_cond else self.first_cond_row + self.batch(i)


def _passes(x_prompt, x_sample, tile):
    ctx = _Pass(x_prompt, tile, 0, 0, True)
    dec = _Pass(x_sample, tile, ctx.steps, 1, False)
    return ctx, dec


def _const_spec(l, shape):
    return pl.BlockSpec((None,) + shape, lambda i: (l,) + (0,) * len(shape), pipeline_mode=pl.Buffered(1))


def _whole_spec(arr):
    return pl.BlockSpec(arr.shape, lambda i: (0,) * arr.ndim, pipeline_mode=pl.Buffered(1))


def _row_block_spec(rows, block_index):
    return pl.BlockSpec((rows, D_MODEL), lambda i: (block_index, 0), pipeline_mode=pl.Buffered(1))


_N_FRONT_WEIGHTS = 11


def _front_tile(l, seq_len, j, x_ref, xp_ref, xn_ref, qtab_ref, kcos_ref, ksin_ref, mod_row, weights, outs, scratch):
    gpre_ref, w_ref, convw_ref, convb_ref, gq_ref, wuq_ref, gkv_ref, wk_ref, wv_ref, poolw_ref, pscale_ref = weights
    g_pre, conv_b, g_q, g_kv, p_scale = (r[l:l + 1, :] for r in (gpre_ref, convb_ref, gq_ref, gkv_ref, pscale_ref))
    conv_w = [convw_ref[tap, l:l + 1, :] for tap in range(3)]
    q_out, k_out, v_out, ya_out, yc_out, sbz_out = outs[:6]
    u_scr, cu_scr = scratch
    tm = x_ref.shape[0]
    last_j = seq_len // tm - 1

    if last_j == 0:
        hn = _modulated_norm(x_ref[...], g_pre, mod_row).astype(jnp.bfloat16)
        acx = _dot_nt(hn, w_ref[_R_ACX:_R_ACX + 2 * BRANCH_DIM, :])
        zeros = jnp.zeros((HALO, BRANCH_DIM), jnp.float32)
        for scr in (u_scr, cu_scr):
            scr[0:HALO, :] = zeros
            scr[HALO + tm:2 * HALO + tm, :] = zeros
        u_scr[HALO:HALO + tm, :] = acx[:, 0:BRANCH_DIM] * acx[:, BRANCH_DIM:2 * BRANCH_DIM]
        cu_scr[HALO:HALO + tm, :] = _dot_nt(hn, w_ref[_R_CU:_R_CU + BRANCH_DIM, :])
    else:
        x_ext = jnp.concatenate([xp_ref[...], x_ref[...], xn_ref[...]], axis=0)
        hn_ext = _modulated_norm(x_ext, g_pre, mod_row).astype(jnp.bfloat16)
        hn = hn_ext[HALO:HALO + tm]
        acx = _dot_nt(hn_ext, w_ref[_R_ACX:_R_ACX + 2 * BRANCH_DIM, :])
        row = lax.broadcasted_iota(jnp.int32, (tm + 2 * HALO, 1), 0)
        first_valid = jnp.where(j > 0, 0, HALO)
        end_valid = jnp.where(j < last_j, tm + 2 * HALO, tm + HALO)
        valid = jnp.logical_and(row >= first_valid, row < end_valid)
        ext_rows = slice(0, tm + 2 * HALO)
        u_scr[ext_rows, :] = jnp.where(valid, acx[:, 0:BRANCH_DIM] * acx[:, BRANCH_DIM:2 * BRANCH_DIM], 0.0)
        cu_scr[ext_rows, :] = jnp.where(valid, _dot_nt(hn_ext, w_ref[_R_CU:_R_CU + BRANCH_DIM, :]), 0.0)

    pm = _dot_nt(hn, w_ref[_R_MAIN:_R_MAIN + _MAIN_ROWS, :])
    conv = (u_scr[HALO - 1:HALO - 1 + tm, :] * conv_w[0]
            + u_scr[HALO:HALO + tm, :] * conv_w[1]
            + u_scr[HALO + 1:HALO + 1 + tm, :] * conv_w[2]
            + conv_b)
    ya = _silu(pm[:, 0:BRANCH_DIM]) * (_dot_nt(hn, w_ref[_R_AB:_R_AB + BRANCH_DIM, :]) * conv)
    ya_out[...] = ya.astype(jnp.bfloat16)

    t = j * tm + lax.broadcasted_iota(jnp.int32, (tm, POOL_GROUP), 0)
    pooled = []
    for gi, win in enumerate(POOL_WINDOWS):
        half = win // 2
        cols = slice(gi * POOL_GROUP, (gi + 1) * POOL_GROUP)
        total = cu_scr[HALO - half:HALO - half + tm, cols]
        for k in range(-half + 1, half):
            total = total + cu_scr[HALO + k:HALO + k + tm, cols]
        count = jnp.minimum(t + half, seq_len) - jnp.maximum(t - half, 0)
        pooled.append((total / count.astype(jnp.float32) - cu_scr[HALO:HALO + tm, cols]).astype(jnp.bfloat16))
    mixed = [_dot(jnp.concatenate(pooled[2 * n:2 * n + 2], axis=1), poolw_ref[n]) for n in range(len(pooled) // 2)]
    mixed = jnp.concatenate(mixed, axis=1) * p_scale
    c_z = _dot_nt(hn, w_ref[_R_CZ:_R_CZ + BRANCH_DIM, :])
    yc_out[...] = (_silu(c_z) * mixed).astype(jnp.bfloat16)

    qn = _rms(pm[:, _QD0:_QD0 + Q_LORA], g_q).astype(jnp.bfloat16)
    q = _dot(qn, wuq_ref[...])
    qtab = qtab_ref[...]
    ckv = _rms(pm[:, _CKV0:_CKV0 + KV_LORA], g_kv)
    kr = pm[:, _KR0:_KR0 + LANES]
    if len(outs) > 6:
        lat_out = outs[6]
        if len(lat_out.shape) == 3:
            for later in range(1, lat_out.shape[0]):
                lat_out[later] = jnp.zeros(lat_out.shape[1:], jnp.float32)
            lat_out = lat_out.at[0]
        lat_out[0:KV_LORA, :] = ckv.T
        lat_out[KV_LORA:MLA_LATENT, :] = kr.T[0:QK_ROPE, :]
    lane = lax.broadcasted_iota(jnp.int32, (tm, LANES), 1)
    quarter = QK_ROPE // 4
    partner = jnp.where(lane % (2 * quarter) < quarter,
                        pltpu.roll(kr, LANES - quarter, 1), pltpu.roll(kr, quarter, 1))
    kr_rot = jnp.where(lane < QK_ROPE, kr * kcos_ref[...] + partner * ksin_ref[...], 0.0)
    ckv_bf = ckv.astype(jnp.bfloat16)
    _store_keys(k_out, _dot(ckv_bf, wk_ref[...]), kr_rot)
    v = _dot(ckv_bf, wv_ref[...]).astype(jnp.bfloat16)
    for h in range(N_HEADS):
        q_out[h] = (q[:, h * HEAD_PAD:(h + 1) * HEAD_PAD] * qtab).astype(jnp.bfloat16)
    for p in range(N_PAIRS):
        v_out[p] = v[:, p * LANES:(p + 1) * LANES]
    sbz_out[...] = _silu(_dot_nt(hn, w_ref[_R_BZ:_R_BZ + BRANCH_DIM, :])).astype(jnp.bfloat16)


def _mod_row(mod_ref, ctx, dec, i):
    row = jnp.where(i < dec.first, ctx.cond_row(i), dec.cond_row(i))
    return mod_ref[pl.ds(row, 1), :]


def _front_kernel(l, ctx, dec, n_cast, n_cache_jobs, n_aliased, *refs):
    n_in = 6
    ctx_in, dec_in = refs[0:n_in], refs[n_in:2 * n_in]
    mod_ref = refs[2 * n_in]
    n_w = 2 * n_in + 1 + _N_FRONT_WEIGHTS
    weights = refs[2 * n_in + 1:n_w]
    n_cache_in = 3 if n_cache_jobs else 0
    cache_in = refs[n_w + n_cast:n_w + n_cast + n_cache_in]
    outs = refs[n_w + n_cast + n_cache_in + n_aliased:-2]
    ctx_out, dec_out = outs[0:7], outs[7:13]
    scratch = refs[-2:]
    i = pl.program_id(0)
    _cast_chunks(refs[n_w:n_w + n_cast], outs[13:13 + n_cast])

    if n_cache_jobs:
        @pl.when(i < n_cache_jobs)
        def _():
            _cache_kv_kernel(*cache_in, *outs[13 + n_cast:])

    @pl.when(i < dec.first)
    def _():
        _front_tile(l, ctx.seq, ctx.tile(i), *ctx_in, _mod_row(mod_ref, ctx, dec, i), weights, ctx_out, scratch)

    @pl.when(i >= dec.first)
    def _():
        _front_tile(l, dec.seq, dec.tile(i), *dec_in, _mod_row(mod_ref, ctx, dec, i), weights, dec_out, scratch)


def _front(l, x_prompt, x_sample, mod, w, w_front, ctx_tabs, dec_tabs, state_t, big_f32, cast_windows, cache_t):
    ctx, dec = _passes(x_prompt, x_sample, FRONT_TILE)
    n_steps = ctx.steps + dec.steps

    def pass_in_specs(p):
        tm = p.tm
        hb = tm // HALO
        n_hblk = p.seq // HALO
        tab = pl.BlockSpec((tm, LANES), lambda i: (p.tile(i), 0))
        return [
            pl.BlockSpec((None, tm, D_MODEL), lambda i: (p.batch(i), p.tile(i), 0)),
            pl.BlockSpec((None, HALO, D_MODEL), lambda i: (p.batch(i), jnp.maximum(p.tile(i) * hb - 1, 0), 0)),
            pl.BlockSpec((None, HALO, D_MODEL),
                         lambda i: (p.batch(i), jnp.minimum((p.tile(i) + 1) * hb, n_hblk - 1), 0)),
            tab, tab, tab,
        ]

    def pass_out(p, with_latent):
        tm = p.tm

        def blocks(n):
            return (jax.ShapeDtypeStruct((p.nb, n, p.seq, LANES), jnp.bfloat16),
                    pl.BlockSpec((None, n, tm, LANES), lambda i: (p.batch(i), 0, p.tile(i), 0)))

        (head_shape, head_spec), (pair_shape, pair_spec) = blocks(N_HEADS), blocks(N_PAIRS)
        br_shape = jax.ShapeDtypeStruct((p.nb, p.seq, BRANCH_DIM), jnp.bfloat16)
        br_spec = pl.BlockSpec((None, tm, BRANCH_DIM), lambda i: (p.batch(i), p.tile(i), 0))
        shapes = [head_shape, head_shape, pair_shape] + [br_shape] * 3
        specs = [head_spec, head_spec, pair_spec] + [br_spec] * 3
        if with_latent:
            shapes.append(jax.ShapeDtypeStruct((p.nb, DEPTH, MLA_LATENT, p.seq), jnp.float32))
            if state_t is None:
                specs.append(pl.BlockSpec((None, DEPTH, MLA_LATENT, tm), lambda i: (p.batch(i), 0, 0, p.tile(i))))
            else:
                specs.append(pl.BlockSpec((None, None, MLA_LATENT, tm), lambda i: (p.batch(i), l, 0, p.tile(i))))
        return shapes, specs

    weight_specs = [
        _whole_spec(w["g_pre"]),
        _row_block_spec(_FRONT_ROWS, 0),
        _whole_spec(w["conv_w"]),
        _whole_spec(w["conv_b"]),
        _whole_spec(w["g_q"]),
        _const_spec(l, (Q_LORA, HP)),
        _whole_spec(w["g_kv"]),
        _const_spec(l, (KV_LORA, N_HEADS * QK_NOPE)),
        _const_spec(l, (KV_LORA, N_HEADS * V_HEAD)),
        _const_spec(l, (len(POOL_WINDOWS) // 2, 2 * POOL_GROUP, 2 * POOL_GROUP)),
        _whole_spec(w["pool_scale"]),
    ]
    assert len(weight_specs) == _N_FRONT_WEIGHTS
    ctx_shapes, ctx_specs = pass_out(ctx, True)
    dec_shapes, dec_specs = pass_out(dec, False)
    in_specs = pass_in_specs(ctx) + pass_in_specs(dec) + [_whole_spec(mod)] + weight_specs
    args = [x_prompt, x_prompt, x_prompt, *ctx_tabs, x_sample, x_sample, x_sample, *dec_tabs, mod,
            w["g_pre"], w_front, w["conv_w"], w["conv_b"], w["g_q"], w["wuq"], w["g_kv"],
            w["wk"], w["wv"], w["pool_w"], w["pool_scale"]]
    cast_in, cast_out, cast_shapes = _cast_specs(l, cast_windows, n_steps)
    in_specs += cast_in
    args += [big_f32[src] for src, _, _ in cast_windows]
    cache_out, cache_shapes, n_cache_jobs = [], [], 0
    if cache_t is not None:
        cache_in, cache_out, cache_shapes, n_cache_jobs = _cache_kv_specs(cache_t)
        assert n_cache_jobs <= n_steps
        in_specs += cache_in
        args += [cache_t, w["wk"], w["wv"]]
    aliases = {}
    if state_t is not None:
        aliases = {len(args): len(ctx_shapes) - 1}
        in_specs.append(pl.BlockSpec(memory_space=pl.ANY))
        args.append(state_t)
    outs = pl.pallas_call(
        functools.partial(_front_kernel, l, ctx, dec, len(cast_windows), n_cache_jobs, len(aliases)),
        grid=(n_steps,),
        in_specs=in_specs,
        out_specs=ctx_specs + dec_specs + cast_out + cache_out,
        out_shape=ctx_shapes + dec_shapes + cast_shapes + cache_shapes,
        scratch_shapes=[pltpu.VMEM((max(ctx.tm, dec.tm) + 2 * HALO, BRANCH_DIM), jnp.float32)] * 2,
        input_output_aliases=aliases,
        compiler_params=_params(),
        name="front",
    )(*args)
    n_cast = len(cast_windows)
    return outs[:7], outs[7:13], tuple(outs[13:13 + n_cast]), tuple(outs[13 + n_cast:])


_N_BACK_WEIGHTS = 5


def _back_tile(l, x_ref, q_ref, k_ref, v_ref, cache, ya_ref, yc_ref, sbz_ref, mod_row, weights, out_ref):
    gpre_ref, wg_ref, wbr_ref, wo_ref, gpost_ref = weights
    g_pre, g_post = gpre_ref[l:l + 1, :], gpost_ref[l:l + 1, :]
    x = x_ref[...]

    def with_ones(v):
        return jnp.concatenate([v, jnp.ones_like(v)], axis=1)

    low_half = lax.broadcasted_iota(jnp.int32, (x.shape[0], LANES), 1) < V_HEAD
    pairs = []
    for p in range(N_PAIRS):
        v_ext = with_ones(v_ref[p])
        if cache is not None:
            kc_ref, vc_ref = cache
            vc_ext = with_ones(vc_ref[p])
        halves = []
        for h in (2 * p, 2 * p + 1):
            q = q_ref[h]
            s = _dot_nt(q, k_ref[h])
            m = jnp.max(s, axis=-1, keepdims=True)
            if cache is not None:
                sc = _dot_nt(q, kc_ref[h])
                m = jnp.maximum(m, jnp.max(sc, axis=-1, keepdims=True))
            o = _dot(jnp.exp2((s - m).astype(jnp.bfloat16)), v_ext)
            if cache is not None:
                o = o + _dot(jnp.exp2((sc - m).astype(jnp.bfloat16)), vc_ext)
            halves.append(o[:, 0:LANES] / o[:, LANES:2 * LANES])
        pairs.append(jnp.where(low_half, halves[0], halves[1]))
    attn = jnp.concatenate(pairs, axis=1)
    yb = (sbz_ref[...].astype(jnp.float32) * attn).astype(jnp.bfloat16)

    hn = _modulated_norm(x, g_pre, mod_row).astype(jnp.bfloat16)
    merged = None
    for n, y in enumerate((ya_ref[...], yb, yc_ref[...])):
        r0 = _R_MERGE - _BACK_START + n * D_MODEL
        gate = jax.nn.sigmoid(_dot_nt(hn, wg_ref[r0:r0 + D_MODEL, :]))
        term = gate * _dot(y, wbr_ref[n * BRANCH_DIM:(n + 1) * BRANCH_DIM, :])
        merged = term if merged is None else merged + term
    out = _rms(_dot(merged.astype(jnp.bfloat16), wo_ref[...]), g_post)
    out_ref[...] = x + mod_row[:, 2 * D_MODEL:3 * D_MODEL] * out


def _back_kernel(l, ctx, dec, n_cast, n_mod_steps, *refs):
    x_c, q_c, k_c, v_c, ya_c, yc_c, sbz_c = refs[0:7]
    x_d, q_d, k_d, v_d, kc_ref, vc_ref, ya_d, yc_d, sbz_d = refs[7:16]
    mod_ref = refs[16]
    n_w = 17 + _N_BACK_WEIGHTS
    weights = refs[17:n_w]
    n_mod_in = 3 if n_mod_steps else 0
    n_in = n_w + n_cast + n_mod_in
    out_c, out_d = refs[n_in:n_in + 2]
    i = pl.program_id(0)
    _cast_chunks(refs[n_w:n_w + n_cast], refs[n_in + 2:n_in + 2 + n_cast])
    if n_mod_steps:
        @pl.when(i < n_mod_steps)
        def _():
            _mod_chunk(l + 1, *refs[n_w + n_cast:n_in], refs[n_in + 2 + n_cast])

    @pl.when(i < dec.first)
    def _():
        _back_tile(l, x_c, q_c, k_c, v_c, None, ya_c, yc_c, sbz_c, _mod_row(mod_ref, ctx, dec, i), weights, out_c)

    @pl.when(i >= dec.first)
    def _():
        _back_tile(l, x_d, q_d, k_d, v_d, (kc_ref, vc_ref), ya_d, yc_d, sbz_d, _mod_row(mod_ref, ctx, dec, i),
                   weights, out_d)


def _back(l, x_prompt, x_sample, mod, w, big_bf, ctx_fr, dec_fr, kc, vc, big_f32, cast_windows, next_mod_args):
    ctx, dec = _passes(x_prompt, x_sample, BACK_TILE)
    past = kc.shape[3]
    n_steps = ctx.steps + dec.steps

    def pass_specs(p):
        tq = p.tm
        tile = pl.BlockSpec((None, tq, D_MODEL), lambda i: (p.batch(i), p.tile(i), 0))
        q = pl.BlockSpec((None, N_HEADS, tq, HEAD_PAD), lambda i: (p.batch(i), 0, p.tile(i), 0))
        k = pl.BlockSpec((None, N_HEADS, p.seq, LANES), lambda i: (p.batch(i), 0, 0, 0))
        v = pl.BlockSpec((None, N_PAIRS, p.seq, LANES), lambda i: (p.batch(i), 0, 0, 0))
        br = pl.BlockSpec((None, tq, BRANCH_DIM), lambda i: (p.batch(i), p.tile(i), 0))
        return tile, q, k, v, br

    tile_c, q_c, k_c, v_c, br_c = pass_specs(ctx)
    tile_d, q_d, k_d, v_d, br_d = pass_specs(dec)
    kc_spec = pl.BlockSpec((None, None, N_HEADS, past, LANES), lambda i: (l, dec.batch(i), 0, 0, 0))
    vc_spec = pl.BlockSpec((None, None, N_PAIRS, past, LANES), lambda i: (l, dec.batch(i), 0, 0, 0))
    in_specs = ([tile_c, q_c, k_c, v_c, br_c, br_c, br_c]
                + [tile_d, q_d, k_d, v_d, kc_spec, vc_spec, br_d, br_d, br_d]
                + [_whole_spec(mod),
                   _whole_spec(w["g_pre"]),
                   _row_block_spec(_BACK_ROWS, 0),
                   _row_block_spec(3 * BRANCH_DIM, 0),
                   _row_block_spec(D_MODEL, 0),
                   _whole_spec(w["g_post"])])
    q_cx, k_cx, v_cx, ya_cx, yc_cx, sbz_cx = ctx_fr
    q_dx, k_dx, v_dx, ya_dx, yc_dx, sbz_dx = dec_fr
    args = [x_prompt, q_cx, k_cx, v_cx, ya_cx, yc_cx, sbz_cx,
            x_sample, q_dx, k_dx, v_dx, kc, vc, ya_dx, yc_dx, sbz_dx,
            mod, w["g_pre"], *big_bf, w["g_post"]]
    out_specs = [tile_c, tile_d]
    out_shape = [jax.ShapeDtypeStruct(x_prompt.shape, jnp.float32), jax.ShapeDtypeStruct(x_sample.shape, jnp.float32)]
    cast_in, cast_out, cast_shapes = _cast_specs(l + 1, cast_windows, n_steps)
    in_specs += cast_in
    args += [big_f32[src] for src, _, _ in cast_windows]
    out_specs += cast_out
    out_shape += cast_shapes
    n_mod_steps = 0
    if next_mod_args is not None:
        mod_in, mod_out, mod_shape, n_mod_steps = _mod_specs(l + 1, LANES)
        assert n_mod_steps <= n_steps
        in_specs += mod_in
        args += list(next_mod_args)
        out_specs.append(mod_out)
        out_shape.append(mod_shape)
    outs = pl.pallas_call(
        functools.partial(_back_kernel, l, ctx, dec, len(cast_windows), n_mod_steps),
        grid=(n_steps,),
        in_specs=in_specs,
        out_specs=out_specs,
        out_shape=out_shape,
        compiler_params=_params(),
        name="back",
    )(*args)
    n_cast = len(cast_windows)
    return outs[0], outs[1], tuple(outs[2:2 + n_cast]), (outs[2 + n_cast] if n_mod_steps else None)


def _rope_swap_perm():
    quarter = QK_ROPE // 4
    idx = np.arange(QK_ROPE).reshape(2, 2, quarter)
    return idx[:, ::-1, :].reshape(-1)


def _rope_tables(n_tokens, rotate):
    if rotate:
        t = np.arange(n_tokens)
        pos = np.stack([t // GRID_W, t % GRID_W], axis=1).astype(np.float64)
        axis_dim = QK_ROPE // 2
        inv = 1.0 / (ROPE_THETA ** (np.arange(0, axis_dim, 2, dtype=np.float64) / axis_dim))
        ang = pos[:, :, None] * inv
        cos, sin = np.cos(ang), np.sin(ang)
        cfull = np.stack([cos, cos], axis=2).reshape(n_tokens, QK_ROPE)
        sfull = np.stack([-sin, sin], axis=2).reshape(n_tokens, QK_ROPE)
    else:
        cfull = np.ones((n_tokens, QK_ROPE))
        sfull = np.zeros((n_tokens, QK_ROPE))
    scale = (QK_NOPE + QK_ROPE) ** -0.5 * np.log2(np.e)
    qtab = np.concatenate([np.full((n_tokens, QK_NOPE), scale), cfull * scale, sfull * scale], axis=1)
    pad = np.zeros((n_tokens, LANES - QK_ROPE))
    tabs = (qtab, np.concatenate([cfull, pad], axis=1), np.concatenate([sfull, pad], axis=1))
    return tuple(jnp.asarray(tab, jnp.float32) for tab in tabs)


def _prepare_weights(g_pre, g_post, conv_w, conv_b, g_q, w_uq, g_kv, w_ukv, pool_w, pool_scale):
    bf = jnp.bfloat16
    perm = _rope_swap_perm()

    uq = w_uq.reshape(DEPTH, Q_LORA, N_HEADS, QK_NOPE + QK_ROPE)
    rope_q = uq[..., QK_NOPE:]
    wuq = jnp.concatenate([uq, rope_q[..., perm]], axis=-1).reshape(DEPTH, Q_LORA, HP).astype(bf)

    ukv = w_ukv.reshape(DEPTH, KV_LORA, N_HEADS, QK_NOPE + V_HEAD)
    wk = ukv[..., :QK_NOPE].reshape(DEPTH, KV_LORA, N_HEADS * QK_NOPE).astype(bf)
    wv = ukv[..., QK_NOPE:].reshape(DEPTH, KV_LORA, N_HEADS * V_HEAD).astype(bf)

    pw = pool_w.astype(bf).reshape(DEPTH, len(POOL_WINDOWS) // 2, 2, POOL_GROUP, POOL_GROUP)
    zero = jnp.zeros_like(pw[:, :, 0])
    pool_bd = jnp.concatenate([jnp.concatenate([pw[:, :, 0], zero], axis=-1),
                               jnp.concatenate([zero, pw[:, :, 1]], axis=-1)], axis=-2)

    return {
        "g_pre": g_pre, "g_post": g_post, "conv_w": jnp.swapaxes(conv_w, 0, 1), "conv_b": conv_b,
        "g_q": g_q, "g_kv": g_kv, "pool_scale": pool_scale,
        "wuq": wuq, "wk": wk, "wv": wv, "pool_w": pool_bd,
    }


def kernel(x_prompt, x_sample, cache_mla_latent, c, c_ctx, w_mod, b_mod, g_pre, g_post, w_in, conv_w, conv_b,
           g_q, w_uq, g_kv, w_ukv, pool_w, pool_scale, w_branch, w_o):
    n_dec = x_sample.shape[0]
    assert 1 + n_dec <= COND_ROWS
    assert w_in.shape == (DEPTH, D_MODEL, W_IN_COLS)
    w = _prepare_weights(g_pre, g_post, conv_w, conv_b, g_q, w_uq, g_kv, w_ukv, pool_w, pool_scale)
    big_f32 = (jnp.swapaxes(w_in, 1, 2), w_branch.reshape(DEPTH, 3 * BRANCH_DIM, D_MODEL), w_o)
    w_front = _cast_front_window(0, big_f32)
    cond = jnp.concatenate([c_ctx[None, :], c, jnp.zeros((COND_ROWS - 1 - n_dec, D_MODEL), jnp.float32)], axis=0)
    mod = _modulation(0, cond, w_mod, b_mod)
    cache_t = jnp.swapaxes(cache_mla_latent, 2, 3)
    ctx_tabs = _rope_tables(x_prompt.shape[1], rotate=False)
    dec_tabs = _rope_tables(x_sample.shape[1], rotate=True)

    back_windows = (_W_BACK, _W_BRANCH, _W_OUT)
    h, hs, state_t, big_bf, cache_kv = x_prompt, x_sample, None, None, None
    for l in range(DEPTH):
        first, last = l == 0, l + 1 == DEPTH
        ctx_fr, dec_fr, cast, new_kv = _front(l, h, hs, mod, w, w_front, ctx_tabs, dec_tabs, state_t, big_f32,
                                              back_windows if first else (), cache_t if first else None)
        state_t = ctx_fr[6]
        if first:
            big_bf, cache_kv = cast, new_kv
        next_windows = () if last else (_W_FRONT,) + back_windows
        h, hs, cast, mod = _back(l, h, hs, mod, w, big_bf, ctx_fr[:6], dec_fr, *cache_kv, big_f32, next_windows,
                                 None if last else (cond, w_mod, b_mod))
        if not last:
            w_front, big_bf = cast[0], cast[1:]
    return (h, hs, jnp.swapaxes(state_t, 2, 3))
```

```python
import functools

import numpy as np
import jax
import jax.numpy as jnp
from jax import lax
from jax.experimental import pallas as pl
from jax.experimental.pallas import tpu as pltpu

D_MODEL = 1024
DEPTH = 2
GRID_W = 64
EPS = 1e-6
BRANCH_DIM = 512
N_HEADS = 8
QK_NOPE = 64
QK_ROPE = 32
V_HEAD = 64
Q_LORA = 384
KV_LORA = 256
MLA_LATENT = KV_LORA + QK_ROPE
ROPE_THETA = 10000.0
POOL_WINDOWS = (2, 4, 8, 16)
POOL_GROUP = 128

LANES = 128
HEAD_PAD = LANES
HP = N_HEADS * HEAD_PAD
N_PAIRS = N_HEADS * V_HEAD // LANES
FRONT_TILE = 512
BACK_TILE = 256
HALO = 16
COND_ROWS = 8
VMEM_LIMIT = 56 * 1024 * 1024

_R_AB = 0
_R_ACX = BRANCH_DIM
_R_MAIN = 3 * BRANCH_DIM
_MAIN_ROWS = BRANCH_DIM + Q_LORA + KV_LORA + LANES
_R_BZ = 4 * BRANCH_DIM + Q_LORA + MLA_LATENT
_R_CU = _R_BZ + BRANCH_DIM
_R_CZ = _R_CU + BRANCH_DIM
_R_MERGE = _R_CZ + BRANCH_DIM
W_IN_COLS = _R_MERGE + 3 * D_MODEL
_FRONT_ROWS = -(-_R_MERGE // LANES) * LANES
_QD0 = BRANCH_DIM
_CKV0 = _QD0 + Q_LORA
_KR0 = _CKV0 + KV_LORA
_LAT_EXT = KV_LORA + LANES


def _dot(a, b):
    return jnp.dot(a, b, preferred_element_type=jnp.float32)


def _dot_nt(a, b):
    return lax.dot_general(a, b, (((1,), (1,)), ((), ())), preferred_element_type=jnp.float32)


def _rms(x, g):
    return x * lax.rsqrt(jnp.mean(x * x, axis=-1, keepdims=True) + EPS) * g


def _silu(x):
    return x * jax.nn.sigmoid(x)


def _modulated_norm(x, g_pre, mod_row):
    shift = mod_row[:, 0:D_MODEL]
    scale = mod_row[:, D_MODEL:2 * D_MODEL]
    return _rms(x, g_pre) * (1.0 + scale) + shift


def _params(n_grid_dims=1):
    return pltpu.CompilerParams(dimension_semantics=("arbitrary",) * n_grid_dims, vmem_limit_bytes=VMEM_LIMIT)


_SRC_W_IN_T, _SRC_W_BRANCH, _SRC_W_O = 0, 1, 2
_BACK_START = (_R_MERGE // 1024) * 1024
_W_FRONT = (_SRC_W_IN_T, 0, _FRONT_ROWS)
_W_BACK = (_SRC_W_IN_T, _BACK_START, W_IN_COLS - _BACK_START)
_W_BRANCH = (_SRC_W_BRANCH, 0, 3 * BRANCH_DIM)
_W_OUT = (_SRC_W_O, 0, D_MODEL)
_BACK_ROWS = _W_BACK[2]


def _chunk_rows(first_row, rows, n_steps):
    if first_row == 0:
        return pl.cdiv(pl.cdiv(rows, n_steps), 16) * 16
    per = LANES
    while first_row % per or pl.cdiv(rows, per) > n_steps:
        per *= 2
        assert per <= first_row
    return per


def _cast_specs(layer, windows, n_steps):
    in_specs, out_specs, out_shapes = [], [], []
    for _, first_row, rows in windows:
        per = _chunk_rows(first_row, rows, n_steps)
        n_blocks = pl.cdiv(rows, per)
        first_blk = first_row // per

        def in_map(i, first_blk=first_blk, last=n_blocks - 1):
            return (layer, first_blk + jnp.minimum(i, last), 0)

        def out_map(i, last=n_blocks - 1):
            return (jnp.minimum(i, last), 0)

        in_specs.append(pl.BlockSpec((None, per, D_MODEL), in_map))
        out_specs.append(pl.BlockSpec((per, D_MODEL), out_map))
        out_shapes.append(jax.ShapeDtypeStruct((rows, D_MODEL), jnp.bfloat16))
    return in_specs, out_specs, out_shapes


def _cast_chunks(in_refs, out_refs):
    for src, dst in zip(in_refs, out_refs):
        dst[...] = src[...].astype(jnp.bfloat16)


def _store_keys(k_ref, k_nope, kr):
    k_rope = pltpu.roll(kr, QK_NOPE, 1) + pltpu.roll(kr, QK_NOPE + QK_ROPE, 1)
    nope_lanes = lax.broadcasted_iota(jnp.int32, kr.shape, 1) < QK_NOPE
    for h in range(N_HEADS):
        pair = k_nope[:, (h // 2) * LANES:(h // 2 + 1) * LANES]
        if h % 2:
            pair = pltpu.roll(pair, LANES - QK_NOPE, 1)
        k_ref[h] = jnp.where(nope_lanes, pair, k_rope).astype(jnp.bfloat16)


def _cache_kv_kernel(lat_t_ref, wk_ref, wv_ref, k_ref, v_ref):
    lat_t = lat_t_ref[...]
    pad = jnp.zeros((_LAT_EXT - MLA_LATENT, lat_t.shape[1]), jnp.float32)
    lat = jnp.concatenate([lat_t, pad], axis=0).T
    ckv_bf = lat[:, 0:KV_LORA].astype(jnp.bfloat16)
    _store_keys(k_ref, _dot(ckv_bf, wk_ref[...]), lat[:, KV_LORA:_LAT_EXT])
    v = _dot(ckv_bf, wv_ref[...]).astype(jnp.bfloat16)
    for p in range(N_PAIRS):
        v_ref[p] = v[:, p * LANES:(p + 1) * LANES]


_MOD_COL_TILES = 2


def _prologue_kernel(n_cache_jobs, cond_ref, wmod_ref, bmod_ref, wsrc_ref, lat_t_ref, wk_ref, wv_ref,
                     mod_ref, wfront_ref, kc_ref, vc_ref):
    i = pl.program_id(0)
    h = _silu(cond_ref[...]).astype(jnp.bfloat16)
    mod_ref[...] = _dot(h, wmod_ref[...].astype(jnp.bfloat16)) + bmod_ref[pl.ds(i // _MOD_COL_TILES, 1), :]
    _cast_chunks([wsrc_ref], [wfront_ref])

    @pl.when(i < n_cache_jobs)
    def _():
        _cache_kv_kernel(lat_t_ref, wk_ref, wv_ref, kc_ref, vc_ref)


def _prologue(cond, w_mod, b_mod, big_f32, cache_t, wk, wv):
    n_steps = DEPTH * _MOD_COL_TILES
    col_tile = 3 * D_MODEL // _MOD_COL_TILES
    nb, _, _, past = cache_t.shape
    n_cache_jobs = DEPTH * nb
    assert n_cache_jobs <= n_steps

    def mod_block(i):
        return (i // _MOD_COL_TILES, 0, i % _MOD_COL_TILES)

    def cache_layer(i):
        return jnp.minimum(i, n_cache_jobs - 1) // nb

    def cache_request(i):
        return jnp.minimum(i, n_cache_jobs - 1) % nb

    def kv(n):
        return (jax.ShapeDtypeStruct((DEPTH, nb, n, past, LANES), jnp.bfloat16),
                pl.BlockSpec((None, None, n, past, LANES), lambda i: (cache_layer(i), cache_request(i), 0, 0, 0)))

    (k_shape, k_spec), (v_shape, v_spec) = kv(N_HEADS), kv(N_PAIRS)
    cast_in, cast_out, cast_shapes = _cast_specs(0, [_W_FRONT], n_steps)
    return pl.pallas_call(
        functools.partial(_prologue_kernel, n_cache_jobs),
        grid=(n_steps,),
        in_specs=[
            pl.BlockSpec((COND_ROWS, D_MODEL), lambda i: (0, 0)),
            pl.BlockSpec((None, D_MODEL, col_tile), mod_block),
            pl.BlockSpec((DEPTH, col_tile), lambda i: (0, i % _MOD_COL_TILES)),
            cast_in[0],
            pl.BlockSpec((None, None, MLA_LATENT, past), lambda i: (cache_request(i), cache_layer(i), 0, 0)),
            pl.BlockSpec((None, KV_LORA, N_HEADS * QK_NOPE), lambda i: (cache_layer(i), 0, 0)),
            pl.BlockSpec((None, KV_LORA, N_HEADS * V_HEAD), lambda i: (cache_layer(i), 0, 0)),
        ],
        out_specs=[pl.BlockSpec((None, COND_ROWS, col_tile), mod_block), cast_out[0], k_spec, v_spec],
        out_shape=[jax.ShapeDtypeStruct((DEPTH, COND_ROWS, 3 * D_MODEL), jnp.float32), cast_shapes[0],
                   k_shape, v_shape],
        compiler_params=_params(),
        name="prologue",
    )(cond, w_mod, b_mod, big_f32[_SRC_W_IN_T], cache_t, wk, wv)


class _Pass:
    def __init__(self, x, tile, first_step, first_cond_row, shared_cond):
        self.nb, self.seq, _ = x.shape
        self.tm = min(tile, self.seq)
        assert self.seq % self.tm == 0
        self.nt = self.seq // self.tm
        self.steps = self.nb * self.nt
        self.first = first_step
        self.first_cond_row = first_cond_row
        self.shared_cond = shared_cond

    def local(self, i):
        return jnp.clip(i - self.first, 0, self.steps - 1)

    def batch(self, i):
        return self.local(i) // self.nt

    def tile(self, i):
        return self.local(i) % self.nt

    def cond_row(self, i):
        return self.first_cond_row if self.shared_cond else self.first_cond_row + self.batch(i)


def _passes(x_prompt, x_sample, tile):
    ctx = _Pass(x_prompt, tile, 0, 0, True)
    dec = _Pass(x_sample, tile, ctx.steps, 1, False)
    return ctx, dec


def _const_spec(l, shape):
    return pl.BlockSpec((None,) + shape, lambda i: (l,) + (0,) * len(shape), pipeline_mode=pl.Buffered(1))


def _whole_spec(arr):
    return pl.BlockSpec(arr.shape, lambda i: (0,) * arr.ndim, pipeline_mode=pl.Buffered(1))


def _row_block_spec(rows, block_index):
    return pl.BlockSpec((rows, D_MODEL), lambda i: (block_index, 0), pipeline_mode=pl.Buffered(1))


_N_FRONT_WEIGHTS = 11


def _front_tile(l, seq_len, j, x_ref, xp_ref, xn_ref, qtab_ref, kcos_ref, ksin_ref, mod_row, weights, outs, scratch):
    gpre_ref, w_ref, convw_ref, convb_ref, gq_ref, wuq_ref, gkv_ref, wk_ref, wv_ref, poolw_ref, pscale_ref = weights
    g_pre, conv_b, g_q, g_kv, p_scale = (r[l:l + 1, :] for r in (gpre_ref, convb_ref, gq_ref, gkv_ref, pscale_ref))
    conv_w = [convw_ref[tap, l:l + 1, :] for tap in range(3)]
    q_out, k_out, v_out, ya_out, yc_out, sbz_out = outs[:6]
    u_scr, cu_scr = scratch
    tm = x_ref.shape[0]
    last_j = seq_len // tm - 1

    if last_j == 0:
        hn = _modulated_norm(x_ref[...], g_pre, mod_row).astype(jnp.bfloat16)
        acx = _dot_nt(hn, w_ref[_R_ACX:_R_ACX + 2 * BRANCH_DIM, :])
        zeros = jnp.zeros((HALO, BRANCH_DIM), jnp.float32)
        for scr in (u_scr, cu_scr):
            scr[0:HALO, :] = zeros
            scr[HALO + tm:2 * HALO + tm, :] = zeros
        u_scr[HALO:HALO + tm, :] = acx[:, 0:BRANCH_DIM] * acx[:, BRANCH_DIM:2 * BRANCH_DIM]
        cu_scr[HALO:HALO + tm, :] = _dot_nt(hn, w_ref[_R_CU:_R_CU + BRANCH_DIM, :])
    else:
        x_ext = jnp.concatenate([xp_ref[...], x_ref[...], xn_ref[...]], axis=0)
        hn_ext = _modulated_norm(x_ext, g_pre, mod_row).astype(jnp.bfloat16)
        hn = hn_ext[HALO:HALO + tm]
        acx = _dot_nt(hn_ext, w_ref[_R_ACX:_R_ACX + 2 * BRANCH_DIM, :])
        row = lax.broadcasted_iota(jnp.int32, (tm + 2 * HALO, 1), 0)
        first_valid = jnp.where(j > 0, 0, HALO)
        end_valid = jnp.where(j < last_j, tm + 2 * HALO, tm + HALO)
        valid = jnp.logical_and(row >= first_valid, row < end_valid)
        ext_rows = slice(0, tm + 2 * HALO)
        u_scr[ext_rows, :] = jnp.where(valid, acx[:, 0:BRANCH_DIM] * acx[:, BRANCH_DIM:2 * BRANCH_DIM], 0.0)
        cu_scr[ext_rows, :] = jnp.where(valid, _dot_nt(hn_ext, w_ref[_R_CU:_R_CU + BRANCH_DIM, :]), 0.0)

    pm = _dot_nt(hn, w_ref[_R_MAIN:_R_MAIN + _MAIN_ROWS, :])
    conv = (u_scr[HALO - 1:HALO - 1 + tm, :] * conv_w[0]
            + u_scr[HALO:HALO + tm, :] * conv_w[1]
            + u_scr[HALO + 1:HALO + 1 + tm, :] * conv_w[2]
            + conv_b)
    ya = _silu(pm[:, 0:BRANCH_DIM]) * (_dot_nt(hn, w_ref[_R_AB:_R_AB + BRANCH_DIM, :]) * conv)
    ya_out[...] = ya.astype(jnp.bfloat16)

    t = j * tm + lax.broadcasted_iota(jnp.int32, (tm, POOL_GROUP), 0)
    pooled = []
    for gi, win in enumerate(POOL_WINDOWS):
        half = win // 2
        cols = slice(gi * POOL_GROUP, (gi + 1) * POOL_GROUP)
        total = cu_scr[HALO - half:HALO - half + tm, cols]
        for k in range(-half + 1, half):
            total = total + cu_scr[HALO + k:HALO + k + tm, cols]
        count = jnp.minimum(t + half, seq_len) - jnp.maximum(t - half, 0)
        pooled.append((total / count.astype(jnp.float32) - cu_scr[HALO:HALO + tm, cols]).astype(jnp.bfloat16))
    mixed = [_dot(jnp.concatenate(pooled[2 * n:2 * n + 2], axis=1), poolw_ref[n]) for n in range(len(pooled) // 2)]
    mixed = jnp.concatenate(mixed, axis=1) * p_scale
    c_z = _dot_nt(hn, w_ref[_R_CZ:_R_CZ + BRANCH_DIM, :])
    yc_out[...] = (_silu(c_z) * mixed).astype(jnp.bfloat16)

    qn = _rms(pm[:, _QD0:_QD0 + Q_LORA], g_q).astype(jnp.bfloat16)
    q = _dot(qn, wuq_ref[...])
    qtab = qtab_ref[...]
    ckv = _rms(pm[:, _CKV0:_CKV0 + KV_LORA], g_kv)
    kr = pm[:, _KR0:_KR0 + LANES]
    if len(outs) > 6:
        lat_out = outs[6]
        if len(lat_out.shape) == 3:
            for later in range(1, lat_out.shape[0]):
                lat_out[later] = jnp.zeros(lat_out.shape[1:], jnp.float32)
            lat_out = lat_out.at[0]
        lat_out[0:KV_LORA, :] = ckv.T
        lat_out[KV_LORA:MLA_LATENT, :] = kr.T[0:QK_ROPE, :]
    lane = lax.broadcasted_iota(jnp.int32, (tm, LANES), 1)
    quarter = QK_ROPE // 4
    partner = jnp.where(lane % (2 * quarter) < quarter,
                        pltpu.roll(kr, LANES - quarter, 1), pltpu.roll(kr, quarter, 1))
    kr_rot = jnp.where(lane < QK_ROPE, kr * kcos_ref[...] + partner * ksin_ref[...], 0.0)
    ckv_bf = ckv.astype(jnp.bfloat16)
    _store_keys(k_out, _dot(ckv_bf, wk_ref[...]), kr_rot)
    v = _dot(ckv_bf, wv_ref[...]).astype(jnp.bfloat16)
    for h in range(N_HEADS):
        q_out[h] = (q[:, h * HEAD_PAD:(h + 1) * HEAD_PAD] * qtab).astype(jnp.bfloat16)
    for p in range(N_PAIRS):
        v_out[p] = v[:, p * LANES:(p + 1) * LANES]
    sbz_out[...] = _silu(_dot_nt(hn, w_ref[_R_BZ:_R_BZ + BRANCH_DIM, :])).astype(jnp.bfloat16)


def _mod_row(mod_ref, ctx, dec, i):
    row = jnp.where(i < dec.first, ctx.cond_row(i), dec.cond_row(i))
    return mod_ref[pl.ds(row, 1), :]


def _front_kernel(l, ctx, dec, n_cast, n_aliased, *refs):
    n_in = 6
    ctx_in, dec_in = refs[0:n_in], refs[n_in:2 * n_in]
    mod_ref = refs[2 * n_in]
    n_w = 2 * n_in + 1 + _N_FRONT_WEIGHTS
    weights = refs[2 * n_in + 1:n_w]
    outs = refs[n_w + n_cast + n_aliased:-2]
    ctx_out, dec_out = outs[0:7], outs[7:13]
    scratch = refs[-2:]
    i = pl.program_id(0)
    _cast_chunks(refs[n_w:n_w + n_cast], outs[13:])

    @pl.when(i < dec.first)
    def _():
        _front_tile(l, ctx.seq, ctx.tile(i), *ctx_in, _mod_row(mod_ref, ctx, dec, i), weights, ctx_out, scratch)

    @pl.when(i >= dec.first)
    def _():
        _front_tile(l, dec.seq, dec.tile(i), *dec_in, _mod_row(mod_ref, ctx, dec, i), weights, dec_out, scratch)


def _front(l, x_prompt, x_sample, mod, w, w_front, ctx_tabs, dec_tabs, state_t, big_f32, cast_windows):
    ctx, dec = _passes(x_prompt, x_sample, FRONT_TILE)
    n_steps = ctx.steps + dec.steps

    def pass_in_specs(p):
        tm = p.tm
        hb = tm // HALO
        n_hblk = p.seq // HALO
        tab = pl.BlockSpec((tm, LANES), lambda i: (p.tile(i), 0))
        return [
            pl.BlockSpec((None, tm, D_MODEL), lambda i: (p.batch(i), p.tile(i), 0)),
            pl.BlockSpec((None, HALO, D_MODEL), lambda i: (p.batch(i), jnp.maximum(p.tile(i) * hb - 1, 0), 0)),
            pl.BlockSpec((None, HALO, D_MODEL),
                         lambda i: (p.batch(i), jnp.minimum((p.tile(i) + 1) * hb, n_hblk - 1), 0)),
            tab, tab, tab,
        ]

    def pass_out(p, with_latent):
        tm = p.tm

        def blocks(n):
            return (jax.ShapeDtypeStruct((p.nb, n, p.seq, LANES), jnp.bfloat16),
                    pl.BlockSpec((None, n, tm, LANES), lambda i: (p.batch(i), 0, p.tile(i), 0)))

        (head_shape, head_spec), (pair_shape, pair_spec) = blocks(N_HEADS), blocks(N_PAIRS)
        br_shape = jax.ShapeDtypeStruct((p.nb, p.seq, BRANCH_DIM), jnp.bfloat16)
        br_spec = pl.BlockSpec((None, tm, BRANCH_DIM), lambda i: (p.batch(i), p.tile(i), 0))
        shapes = [head_shape, head_shape, pair_shape] + [br_shape] * 3
        specs = [head_spec, head_spec, pair_spec] + [br_spec] * 3
        if with_latent:
            shapes.append(jax.ShapeDtypeStruct((p.nb, DEPTH, MLA_LATENT, p.seq), jnp.float32))
            if state_t is None:
                specs.append(pl.BlockSpec((None, DEPTH, MLA_LATENT, tm), lambda i: (p.batch(i), 0, 0, p.tile(i))))
            else:
                specs.append(pl.BlockSpec((None, None, MLA_LATENT, tm), lambda i: (p.batch(i), l, 0, p.tile(i))))
        return shapes, specs

    weight_specs = [
        _whole_spec(w["g_pre"]),
        _row_block_spec(_FRONT_ROWS, 0),
        _whole_spec(w["conv_w"]),
        _whole_spec(w["conv_b"]),
        _whole_spec(w["g_q"]),
        _const_spec(l, (Q_LORA, HP)),
        _whole_spec(w["g_kv"]),
        _const_spec(l, (KV_LORA, N_HEADS * QK_NOPE)),
        _const_spec(l, (KV_LORA, N_HEADS * V_HEAD)),
        _const_spec(l, (len(POOL_WINDOWS) // 2, 2 * POOL_GROUP, 2 * POOL_GROUP)),
        _whole_spec(w["pool_scale"]),
    ]
    assert len(weight_specs) == _N_FRONT_WEIGHTS
    ctx_shapes, ctx_specs = pass_out(ctx, True)
    dec_shapes, dec_specs = pass_out(dec, False)
    in_specs = pass_in_specs(ctx) + pass_in_specs(dec) + [_const_spec(l, mod.shape[1:])] + weight_specs
    args = [x_prompt, x_prompt, x_prompt, *ctx_tabs, x_sample, x_sample, x_sample, *dec_tabs, mod,
            w["g_pre"], w_front, w["conv_w"], w["conv_b"], w["g_q"], w["wuq"], w["g_kv"],
            w["wk"], w["wv"], w["pool_w"], w["pool_scale"]]
    cast_in, cast_out, cast_shapes = _cast_specs(l, cast_windows, n_steps)
    in_specs += cast_in
    args += [big_f32[src] for src, _, _ in cast_windows]
    aliases = {}
    if state_t is not None:
        aliases = {len(args): len(ctx_shapes) - 1}
        in_specs.append(pl.BlockSpec(memory_space=pl.ANY))
        args.append(state_t)
    outs = pl.pallas_call(
        functools.partial(_front_kernel, l, ctx, dec, len(cast_windows), len(aliases)),
        grid=(n_steps,),
        in_specs=in_specs,
        out_specs=ctx_specs + dec_specs + cast_out,
        out_shape=ctx_shapes + dec_shapes + cast_shapes,
        scratch_shapes=[pltpu.VMEM((max(ctx.tm, dec.tm) + 2 * HALO, BRANCH_DIM), jnp.float32)] * 2,
        input_output_aliases=aliases,
        compiler_params=_params(),
        name="front",
    )(*args)
    return outs[:7], outs[7:13], tuple(outs[13:])


_N_BACK_WEIGHTS = 5


def _back_tile(l, x_ref, q_ref, k_ref, v_ref, cache, ya_ref, yc_ref, sbz_ref, mod_row, weights, out_ref):
    gpre_ref, wg_ref, wbr_ref, wo_ref, gpost_ref = weights
    g_pre, g_post = gpre_ref[l:l + 1, :], gpost_ref[l:l + 1, :]
    x = x_ref[...]

    def with_ones(v):
        return jnp.concatenate([v, jnp.ones_like(v)], axis=1)

    low_half = lax.broadcasted_iota(jnp.int32, (x.shape[0], LANES), 1) < V_HEAD
    pairs = []
    for p in range(N_PAIRS):
        v_ext = with_ones(v_ref[p])
        if cache is not None:
            kc_ref, vc_ref = cache
            vc_ext = with_ones(vc_ref[p])
        halves = []
        for h in (2 * p, 2 * p + 1):
            q = q_ref[h]
            s = _dot_nt(q, k_ref[h])
            m = jnp.max(s, axis=-1, keepdims=True)
            if cache is not None:
                sc = _dot_nt(q, kc_ref[h])
                m = jnp.maximum(m, jnp.max(sc, axis=-1, keepdims=True))
            o = _dot(jnp.exp2((s - m).astype(jnp.bfloat16)), v_ext)
            if cache is not None:
                o = o + _dot(jnp.exp2((sc - m).astype(jnp.bfloat16)), vc_ext)
            halves.append(o[:, 0:LANES] / o[:, LANES:2 * LANES])
        pairs.append(jnp.where(low_half, halves[0], halves[1]))
    attn = jnp.concatenate(pairs, axis=1)
    yb = (sbz_ref[...].astype(jnp.float32) * attn).astype(jnp.bfloat16)

    hn = _modulated_norm(x, g_pre, mod_row).astype(jnp.bfloat16)
    merged = None
    for n, y in enumerate((ya_ref[...], yb, yc_ref[...])):
        r0 = _R_MERGE - _BACK_START + n * D_MODEL
        gate = jax.nn.sigmoid(_dot_nt(hn, wg_ref[r0:r0 + D_MODEL, :]))
        term = gate * _dot(y, wbr_ref[n * BRANCH_DIM:(n + 1) * BRANCH_DIM, :])
        merged = term if merged is None else merged + term
    out = _rms(_dot(merged.astype(jnp.bfloat16), wo_ref[...]), g_post)
    out_ref[...] = x + mod_row[:, 2 * D_MODEL:3 * D_MODEL] * out


def _back_kernel(l, ctx, dec, n_cast, *refs):
    x_c, q_c, k_c, v_c, ya_c, yc_c, sbz_c = refs[0:7]
    x_d, q_d, k_d, v_d, kc_ref, vc_ref, ya_d, yc_d, sbz_d = refs[7:16]
    mod_ref = refs[16]
    n_w = 17 + _N_BACK_WEIGHTS
    weights = refs[17:n_w]
    out_c, out_d = refs[n_w + n_cast:n_w + n_cast + 2]
    i = pl.program_id(0)
    _cast_chunks(refs[n_w:n_w + n_cast], refs[n_w + n_cast + 2:])

    @pl.when(i < dec.first)
    def _():
        _back_tile(l, x_c, q_c, k_c, v_c, None, ya_c, yc_c, sbz_c, _mod_row(mod_ref, ctx, dec, i), weights, out_c)

    @pl.when(i >= dec.first)
    def _():
        _back_tile(l, x_d, q_d, k_d, v_d, (kc_ref, vc_ref), ya_d, yc_d, sbz_d, _mod_row(mod_ref, ctx, dec, i),
                   weights, out_d)


def _back(l, x_prompt, x_sample, mod, w, big_bf, ctx_fr, dec_fr, kc, vc, big_f32, cast_windows):
    ctx, dec = _passes(x_prompt, x_sample, BACK_TILE)
    past = kc.shape[3]
    n_steps = ctx.steps + dec.steps

    def pass_specs(p):
        tq = p.tm
        tile = pl.BlockSpec((None, tq, D_MODEL), lambda i: (p.batch(i), p.tile(i), 0))
        q = pl.BlockSpec((None, N_HEADS, tq, HEAD_PAD), lambda i: (p.batch(i), 0, p.tile(i), 0))
        k = pl.BlockSpec((None, N_HEADS, p.seq, LANES), lambda i: (p.batch(i), 0, 0, 0))
        v = pl.BlockSpec((None, N_PAIRS, p.seq, LANES), lambda i: (p.batch(i), 0, 0, 0))
        br = pl.BlockSpec((None, tq, BRANCH_DIM), lambda i: (p.batch(i), p.tile(i), 0))
        return tile, q, k, v, br

    tile_c, q_c, k_c, v_c, br_c = pass_specs(ctx)
    tile_d, q_d, k_d, v_d, br_d = pass_specs(dec)
    kc_spec = pl.BlockSpec((None, None, N_HEADS, past, LANES), lambda i: (l, dec.batch(i), 0, 0, 0))
    vc_spec = pl.BlockSpec((None, None, N_PAIRS, past, LANES), lambda i: (l, dec.batch(i), 0, 0, 0))
    in_specs = ([tile_c, q_c, k_c, v_c, br_c, br_c, br_c]
                + [tile_d, q_d, k_d, v_d, kc_spec, vc_spec, br_d, br_d, br_d]
                + [_const_spec(l, mod.shape[1:]),
                   _whole_spec(w["g_pre"]),
                   _row_block_spec(_BACK_ROWS, 0),
                   _row_block_spec(3 * BRANCH_DIM, 0),
                   _row_block_spec(D_MODEL, 0),
                   _whole_spec(w["g_post"])])
    q_cx, k_cx, v_cx, ya_cx, yc_cx, sbz_cx = ctx_fr
    q_dx, k_dx, v_dx, ya_dx, yc_dx, sbz_dx = dec_fr
    args = [x_prompt, q_cx, k_cx, v_cx, ya_cx, yc_cx, sbz_cx,
            x_sample, q_dx, k_dx, v_dx, kc, vc, ya_dx, yc_dx, sbz_dx,
            mod, w["g_pre"], *big_bf, w["g_post"]]
    out_specs = [tile_c, tile_d]
    out_shape = [jax.ShapeDtypeStruct(x_prompt.shape, jnp.float32), jax.ShapeDtypeStruct(x_sample.shape, jnp.float32)]
    cast_in, cast_out, cast_shapes = _cast_specs(l + 1, cast_windows, n_steps)
    in_specs += cast_in
    args += [big_f32[src] for src, _, _ in cast_windows]
    out_specs += cast_out
    out_shape += cast_shapes
    outs = pl.pallas_call(
        functools.partial(_back_kernel, l, ctx, dec, len(cast_windows)),
        grid=(n_steps,),
        in_specs=in_specs,
        out_specs=out_specs,
        out_shape=out_shape,
        compiler_params=_params(),
        name="back",
    )(*args)
    return outs[0], outs[1], tuple(outs[2:])


def _rope_swap_perm():
    quarter = QK_ROPE // 4
    idx = np.arange(QK_ROPE).reshape(2, 2, quarter)
    return idx[:, ::-1, :].reshape(-1)


def _rope_tables(n_tokens, rotate):
    if rotate:
        t = np.arange(n_tokens)
        pos = np.stack([t // GRID_W, t % GRID_W], axis=1).astype(np.float64)
        axis_dim = QK_ROPE // 2
        inv = 1.0 / (ROPE_THETA ** (np.arange(0, axis_dim, 2, dtype=np.float64) / axis_dim))
        ang = pos[:, :, None] * inv
        cos, sin = np.cos(ang), np.sin(ang)
        cfull = np.stack([cos, cos], axis=2).reshape(n_tokens, QK_ROPE)
        sfull = np.stack([-sin, sin], axis=2).reshape(n_tokens, QK_ROPE)
    else:
        cfull = np.ones((n_tokens, QK_ROPE))
        sfull = np.zeros((n_tokens, QK_ROPE))
    scale = (QK_NOPE + QK_ROPE) ** -0.5 * np.log2(np.e)
    qtab = np.concatenate([np.full((n_tokens, QK_NOPE), scale), cfull * scale, sfull * scale], axis=1)
    pad = np.zeros((n_tokens, LANES - QK_ROPE))
    tabs = (qtab, np.concatenate([cfull, pad], axis=1), np.concatenate([sfull, pad], axis=1))
    return tuple(jnp.asarray(tab, jnp.float32) for tab in tabs)


def _prepare_weights(g_pre, g_post, conv_w, conv_b, g_q, w_uq, g_kv, w_ukv, pool_w, pool_scale):
    bf = jnp.bfloat16
    perm = _rope_swap_perm()

    uq = w_uq.reshape(DEPTH, Q_LORA, N_HEADS, QK_NOPE + QK_ROPE)
    rope_q = uq[..., QK_NOPE:]
    wuq = jnp.concatenate([uq, rope_q[..., perm]], axis=-1).reshape(DEPTH, Q_LORA, HP).astype(bf)

    ukv = w_ukv.reshape(DEPTH, KV_LORA, N_HEADS, QK_NOPE + V_HEAD)
    wk = ukv[..., :QK_NOPE].reshape(DEPTH, KV_LORA, N_HEADS * QK_NOPE).astype(bf)
    wv = ukv[..., QK_NOPE:].reshape(DEPTH, KV_LORA, N_HEADS * V_HEAD).astype(bf)

    pw = pool_w.astype(bf).reshape(DEPTH, len(POOL_WINDOWS) // 2, 2, POOL_GROUP, POOL_GROUP)
    zero = jnp.zeros_like(pw[:, :, 0])
    pool_bd = jnp.concatenate([jnp.concatenate([pw[:, :, 0], zero], axis=-1),
                               jnp.concatenate([zero, pw[:, :, 1]], axis=-1)], axis=-2)

    return {
        "g_pre": g_pre, "g_post": g_post, "conv_w": jnp.swapaxes(conv_w, 0, 1), "conv_b": conv_b,
        "g_q": g_q, "g_kv": g_kv, "pool_scale": pool_scale,
        "wuq": wuq, "wk": wk, "wv": wv, "pool_w": pool_bd,
    }


def kernel(x_prompt, x_sample, cache_mla_latent, c, c_ctx, w_mod, b_mod, g_pre, g_post, w_in, conv_w, conv_b,
           g_q, w_uq, g_kv, w_ukv, pool_w, pool_scale, w_branch, w_o):
    n_dec = x_sample.shape[0]
    assert 1 + n_dec <= COND_ROWS
    assert w_in.shape == (DEPTH, D_MODEL, W_IN_COLS)
    w = _prepare_weights(g_pre, g_post, conv_w, conv_b, g_q, w_uq, g_kv, w_ukv, pool_w, pool_scale)
    big_f32 = (jnp.swapaxes(w_in, 1, 2), w_branch.reshape(DEPTH, 3 * BRANCH_DIM, D_MODEL), w_o)
    cond = jnp.concatenate([c_ctx[None, :], c, jnp.zeros((COND_ROWS - 1 - n_dec, D_MODEL), jnp.float32)], axis=0)
    cache_t = jnp.swapaxes(cache_mla_latent, 2, 3)
    mod, w_front, kc, vc = _prologue(cond, w_mod, b_mod, big_f32, cache_t, w["wk"], w["wv"])
    ctx_tabs = _rope_tables(x_prompt.shape[1], rotate=False)
    dec_tabs = _rope_tables(x_sample.shape[1], rotate=True)

    back_windows = (_W_BACK, _W_BRANCH, _W_OUT)
    h, hs, state_t, big_bf = x_prompt, x_sample, None, None
    for l in range(DEPTH):
        ctx_fr, dec_fr, cast = _front(l, h, hs, mod, w, w_front, ctx_tabs, dec_tabs, state_t, big_f32,
                                      back_windows if big_bf is None else ())
        state_t = ctx_fr[6]
        big_bf = cast if big_bf is None else big_bf
        next_windows = (_W_FRONT,) + back_windows if l + 1 < DEPTH else ()
        h, hs, cast = _back(l, h, hs, mod, w, big_bf, ctx_fr[:6], dec_fr, kc, vc, big_f32, next_windows)
        if next_windows:
            w_front, big_bf = cast[0], cast[1:]
    return (h, hs, jnp.swapaxes(state_t, 2, 3))
```

```python
import functools

import numpy as np
import jax
import jax.numpy as jnp
from jax import lax
from jax.experimental import pallas as pl
from jax.experimental.pallas import tpu as pltpu

D_MODEL = 1024
DEPTH = 2
GRID_W = 64
EPS = 1e-6
BRANCH_DIM = 512
N_HEADS = 8
QK_NOPE = 64
QK_ROPE = 32
V_HEAD = 64
Q_LORA = 384
KV_LORA = 256
MLA_LATENT = KV_LORA + QK_ROPE
ROPE_THETA = 10000.0
POOL_WINDOWS = (2, 4, 8, 16)
POOL_GROUP = 128

LANES = 128
HEAD_PAD = LANES
HP = N_HEADS * HEAD_PAD
N_PAIRS = N_HEADS * V_HEAD // LANES
FRONT_TILE = 512
BACK_TILE = 256
CTX_GROUP = 2
HALO = 16
COND_ROWS = 8
VMEM_LIMIT = 56 * 1024 * 1024

_R_AB = 0
_R_ACX = BRANCH_DIM
_R_MAIN = 3 * BRANCH_DIM
_MAIN_ROWS = BRANCH_DIM + Q_LORA + KV_LORA + LANES
_R_BZ = 4 * BRANCH_DIM + Q_LORA + MLA_LATENT
_R_CU = _R_BZ + BRANCH_DIM
_R_CZ = _R_CU + BRANCH_DIM
_R_MERGE = _R_CZ + BRANCH_DIM
W_IN_COLS = _R_MERGE + 3 * D_MODEL
_FRONT_ROWS = -(-_R_MERGE // LANES) * LANES
_QD0 = BRANCH_DIM
_CKV0 = _QD0 + Q_LORA
_KR0 = _CKV0 + KV_LORA
_LAT_EXT = KV_LORA + LANES


def _dot(a, b):
    return jnp.dot(a, b, preferred_element_type=jnp.float32)


def _dot_nt(a, b):
    return lax.dot_general(a, b, (((1,), (1,)), ((), ())), preferred_element_type=jnp.float32)


def _rms(x, g):
    return x * lax.rsqrt(jnp.mean(x * x, axis=-1, keepdims=True) + EPS) * g


def _silu(x):
    return x * jax.nn.sigmoid(x)


def _modulated_norm(x, g_pre, mod_row):
    shift = mod_row[:, 0:D_MODEL]
    scale = mod_row[:, D_MODEL:2 * D_MODEL]
    return _rms(x, g_pre) * (1.0 + scale) + shift


def _params(n_grid_dims=1):
    return pltpu.CompilerParams(dimension_semantics=("arbitrary",) * n_grid_dims, vmem_limit_bytes=VMEM_LIMIT)


_SRC_W_IN_T, _SRC_W_BRANCH, _SRC_W_O = 0, 1, 2
_BACK_START = (_R_MERGE // 1024) * 1024
_W_FRONT = (_SRC_W_IN_T, 0, _FRONT_ROWS)
_W_BACK = (_SRC_W_IN_T, _BACK_START, W_IN_COLS - _BACK_START)
_W_BRANCH = (_SRC_W_BRANCH, 0, 3 * BRANCH_DIM)
_W_OUT = (_SRC_W_O, 0, D_MODEL)
_BACK_ROWS = _W_BACK[2]


def _chunk_rows(first_row, rows, n_steps):
    if first_row == 0:
        return pl.cdiv(pl.cdiv(rows, n_steps), 16) * 16
    per = LANES
    while first_row % per or pl.cdiv(rows, per) > n_steps:
        per *= 2
        assert per <= first_row
    return per


def _cast_specs(layer, windows, n_steps):
    in_specs, out_specs, out_shapes = [], [], []
    for _, first_row, rows in windows:
        per = _chunk_rows(first_row, rows, n_steps)
        n_blocks = pl.cdiv(rows, per)
        first_blk = first_row // per

        def in_map(i, first_blk=first_blk, last=n_blocks - 1):
            return (layer, first_blk + jnp.minimum(i, last), 0)

        def out_map(i, last=n_blocks - 1):
            return (jnp.minimum(i, last), 0)

        in_specs.append(pl.BlockSpec((None, per, D_MODEL), in_map))
        out_specs.append(pl.BlockSpec((per, D_MODEL), out_map))
        out_shapes.append(jax.ShapeDtypeStruct((rows, D_MODEL), jnp.bfloat16))
    return in_specs, out_specs, out_shapes


def _cast_chunks(in_refs, out_refs):
    for src, dst in zip(in_refs, out_refs):
        dst[...] = src[...].astype(jnp.bfloat16)


def _store_keys(k_ref, k_nope, kr):
    k_rope = pltpu.roll(kr, QK_NOPE, 1) + pltpu.roll(kr, QK_NOPE + QK_ROPE, 1)
    nope_lanes = lax.broadcasted_iota(jnp.int32, kr.shape, 1) < QK_NOPE
    for h in range(N_HEADS):
        pair = k_nope[:, (h // 2) * LANES:(h // 2 + 1) * LANES]
        if h % 2:
            pair = pltpu.roll(pair, LANES - QK_NOPE, 1)
        k_ref[h] = jnp.where(nope_lanes, pair, k_rope).astype(jnp.bfloat16)


def _cache_kv_kernel(lat_t_ref, wk_ref, wv_ref, k_ref, v_ref):
    lat_t = lat_t_ref[...]
    pad = jnp.zeros((_LAT_EXT - MLA_LATENT, lat_t.shape[1]), jnp.float32)
    lat = jnp.concatenate([lat_t, pad], axis=0).T
    ckv_bf = lat[:, 0:KV_LORA].astype(jnp.bfloat16)
    _store_keys(k_ref, _dot(ckv_bf, wk_ref[...]), lat[:, KV_LORA:_LAT_EXT])
    v = _dot(ckv_bf, wv_ref[...]).astype(jnp.bfloat16)
    for p in range(N_PAIRS):
        v_ref[p] = v[:, p * LANES:(p + 1) * LANES]


_MOD_COL_TILES = 2


def _prologue_kernel(n_cache_jobs, cond_ref, wmod_ref, bmod_ref, wsrc_ref, lat_t_ref, wk_ref, wv_ref,
                     mod_ref, wfront_ref, kc_ref, vc_ref):
    i = pl.program_id(0)
    h = _silu(cond_ref[...]).astype(jnp.bfloat16)
    mod_ref[...] = _dot(h, wmod_ref[...].astype(jnp.bfloat16)) + bmod_ref[pl.ds(i // _MOD_COL_TILES, 1), :]
    _cast_chunks([wsrc_ref], [wfront_ref])

    @pl.when(i < n_cache_jobs)
    def _():
        _cache_kv_kernel(lat_t_ref, wk_ref, wv_ref, kc_ref, vc_ref)


def _prologue(cond, w_mod, b_mod, big_f32, cache_t, wk, wv):
    n_steps = DEPTH * _MOD_COL_TILES
    col_tile = 3 * D_MODEL // _MOD_COL_TILES
    nb, _, _, past = cache_t.shape
    n_cache_jobs = DEPTH * nb
    assert n_cache_jobs <= n_steps

    def mod_block(i):
        return (i // _MOD_COL_TILES, 0, i % _MOD_COL_TILES)

    def cache_layer(i):
        return jnp.minimum(i, n_cache_jobs - 1) // nb

    def cache_request(i):
        return jnp.minimum(i, n_cache_jobs - 1) % nb

    def kv(n):
        return (jax.ShapeDtypeStruct((DEPTH, nb, n, past, LANES), jnp.bfloat16),
                pl.BlockSpec((None, None, n, past, LANES), lambda i: (cache_layer(i), cache_request(i), 0, 0, 0)))

    (k_shape, k_spec), (v_shape, v_spec) = kv(N_HEADS), kv(N_PAIRS)
    cast_in, cast_out, cast_shapes = _cast_specs(0, [_W_FRONT], n_steps)
    return pl.pallas_call(
        functools.partial(_prologue_kernel, n_cache_jobs),
        grid=(n_steps,),
        in_specs=[
            pl.BlockSpec((COND_ROWS, D_MODEL), lambda i: (0, 0)),
            pl.BlockSpec((None, D_MODEL, col_tile), mod_block),
            pl.BlockSpec((DEPTH, col_tile), lambda i: (0, i % _MOD_COL_TILES)),
            cast_in[0],
            pl.BlockSpec((None, None, MLA_LATENT, past), lambda i: (cache_request(i), cache_layer(i), 0, 0)),
            pl.BlockSpec((None, KV_LORA, N_HEADS * QK_NOPE), lambda i: (cache_layer(i), 0, 0)),
            pl.BlockSpec((None, KV_LORA, N_HEADS * V_HEAD), lambda i: (cache_layer(i), 0, 0)),
        ],
        out_specs=[pl.BlockSpec((None, COND_ROWS, col_tile), mod_block), cast_out[0], k_spec, v_spec],
        out_shape=[jax.ShapeDtypeStruct((DEPTH, COND_ROWS, 3 * D_MODEL), jnp.float32), cast_shapes[0],
                   k_shape, v_shape],
        compiler_params=_params(),
        name="prologue",
    )(cond, w_mod, b_mod, big_f32[_SRC_W_IN_T], cache_t, wk, wv)


class _Pass:
    def __init__(self, x, tile, first_step, first_cond_row, shared_cond, group=1):
        self.nb, self.seq, _ = x.shape
        self.tm = min(tile, self.seq)
        assert self.seq % self.tm == 0
        self.nt = self.seq // self.tm
        self.group = group if self.nt == 1 and shared_cond else 1
        assert self.nb % self.group == 0
        self.steps = self.nb // self.group * self.nt
        self.first = first_step
        self.first_cond_row = first_cond_row
        self.shared_cond = shared_cond

    def local(self, i):
        return jnp.clip(i - self.first, 0, self.steps - 1)

    def batch(self, i):
        return self.local(i) // self.nt

    def tile(self, i):
        return self.local(i) % self.nt

    def cond_row(self, i):
        return self.first_cond_row if self.shared_cond else self.first_cond_row + self.batch(i)


def _passes(x_prompt, x_sample, tile, ctx_group=1):
    ctx = _Pass(x_prompt, tile, 0, 0, True, ctx_group)
    dec = _Pass(x_sample, tile, ctx.steps, 1, False)
    return ctx, dec


def _const_spec(l, shape):
    return pl.BlockSpec((None,) + shape, lambda i: (l,) + (0,) * len(shape), pipeline_mode=pl.Buffered(1))


def _whole_spec(arr):
    return pl.BlockSpec(arr.shape, lambda i: (0,) * arr.ndim, pipeline_mode=pl.Buffered(1))


def _row_block_spec(rows, block_index):
    return pl.BlockSpec((rows, D_MODEL), lambda i: (block_index, 0), pipeline_mode=pl.Buffered(1))


_N_FRONT_WEIGHTS = 11


def _front_tile(l, seq_len, j, x_ref, xp_ref, xn_ref, qtab_ref, kcos_ref, ksin_ref, mod_row, weights, outs, scratch):
    gpre_ref, w_ref, convw_ref, convb_ref, gq_ref, wuq_ref, gkv_ref, wk_ref, wv_ref, poolw_ref, pscale_ref = weights
    g_pre, conv_b, g_q, g_kv, p_scale = (r[l:l + 1, :] for r in (gpre_ref, convb_ref, gq_ref, gkv_ref, pscale_ref))
    conv_w = [convw_ref[tap, l:l + 1, :] for tap in range(3)]
    q_out, k_out, v_out, ya_out, yc_out, sbz_out = outs[:6]
    u_scr, cu_scr = scratch
    tm = x_ref.shape[0]
    last_j = seq_len // tm - 1

    if last_j == 0:
        hn = _modulated_norm(x_ref[...], g_pre, mod_row).astype(jnp.bfloat16)
        acx = _dot_nt(hn, w_ref[_R_ACX:_R_ACX + 2 * BRANCH_DIM, :])
        zeros = jnp.zeros((HALO, BRANCH_DIM), jnp.float32)
        for scr in (u_scr, cu_scr):
            scr[0:HALO, :] = zeros
            scr[HALO + tm:2 * HALO + tm, :] = zeros
        u_scr[HALO:HALO + tm, :] = acx[:, 0:BRANCH_DIM] * acx[:, BRANCH_DIM:2 * BRANCH_DIM]
        cu_scr[HALO:HALO + tm, :] = _dot_nt(hn, w_ref[_R_CU:_R_CU + BRANCH_DIM, :])
    else:
        x_ext = jnp.concatenate([xp_ref[...], x_ref[...], xn_ref[...]], axis=0)
        hn_ext = _modulated_norm(x_ext, g_pre, mod_row).astype(jnp.bfloat16)
        hn = hn_ext[HALO:HALO + tm]
        acx = _dot_nt(hn_ext, w_ref[_R_ACX:_R_ACX + 2 * BRANCH_DIM, :])
        row = lax.broadcasted_iota(jnp.int32, (tm + 2 * HALO, 1), 0)
        first_valid = jnp.where(j > 0, 0, HALO)
        end_valid = jnp.where(j < last_j, tm + 2 * HALO, tm + HALO)
        valid = jnp.logical_and(row >= first_valid, row < end_valid)
        ext_rows = slice(0, tm + 2 * HALO)
        u_scr[ext_rows, :] = jnp.where(valid, acx[:, 0:BRANCH_DIM] * acx[:, BRANCH_DIM:2 * BRANCH_DIM], 0.0)
        cu_scr[ext_rows, :] = jnp.where(valid, _dot_nt(hn_ext, w_ref[_R_CU:_R_CU + BRANCH_DIM, :]), 0.0)

    pm = _dot_nt(hn, w_ref[_R_MAIN:_R_MAIN + _MAIN_ROWS, :])
    conv = (u_scr[HALO - 1:HALO - 1 + tm, :] * conv_w[0]
            + u_scr[HALO:HALO + tm, :] * conv_w[1]
            + u_scr[HALO + 1:HALO + 1 + tm, :] * conv_w[2]
            + conv_b)
    ya = _silu(pm[:, 0:BRANCH_DIM]) * (_dot_nt(hn, w_ref[_R_AB:_R_AB + BRANCH_DIM, :]) * conv)
    ya_out[...] = ya.astype(jnp.bfloat16)

    t = j * tm + lax.broadcasted_iota(jnp.int32, (tm, POOL_GROUP), 0)
    pooled = []
    for gi, win in enumerate(POOL_WINDOWS):
        half = win // 2
        cols = slice(gi * POOL_GROUP, (gi + 1) * POOL_GROUP)
        total = cu_scr[HALO - half:HALO - half + tm, cols]
        for k in range(-half + 1, half):
            total = total + cu_scr[HALO + k:HALO + k + tm, cols]
        count = jnp.minimum(t + half, seq_len) - jnp.maximum(t - half, 0)
        pooled.append((total / count.astype(jnp.float32) - cu_scr[HALO:HALO + tm, cols]).astype(jnp.bfloat16))
    mixed = [_dot(jnp.concatenate(pooled[2 * n:2 * n + 2], axis=1), poolw_ref[n]) for n in range(len(pooled) // 2)]
    mixed = jnp.concatenate(mixed, axis=1) * p_scale
    c_z = _dot_nt(hn, w_ref[_R_CZ:_R_CZ + BRANCH_DIM, :])
    yc_out[...] = (_silu(c_z) * mixed).astype(jnp.bfloat16)

    qn = _rms(pm[:, _QD0:_QD0 + Q_LORA], g_q).astype(jnp.bfloat16)
    q = _dot(qn, wuq_ref[...])
    qtab = qtab_ref[...]
    ckv = _rms(pm[:, _CKV0:_CKV0 + KV_LORA], g_kv)
    kr = pm[:, _KR0:_KR0 + LANES]
    if len(outs) > 6:
        lat_out = outs[6]
        if len(lat_out.shape) == 3:
            for later in range(1, lat_out.shape[0]):
                lat_out[later] = jnp.zeros(lat_out.shape[1:], jnp.float32)
            lat_out = lat_out.at[0]
        lat_out[0:KV_LORA, :] = ckv.T
        lat_out[KV_LORA:MLA_LATENT, :] = kr.T[0:QK_ROPE, :]
    lane = lax.broadcasted_iota(jnp.int32, (tm, LANES), 1)
    quarter = QK_ROPE // 4
    partner = jnp.where(lane % (2 * quarter) < quarter,
                        pltpu.roll(kr, LANES - quarter, 1), pltpu.roll(kr, quarter, 1))
    kr_rot = jnp.where(lane < QK_ROPE, kr * kcos_ref[...] + partner * ksin_ref[...], 0.0)
    ckv_bf = ckv.astype(jnp.bfloat16)
    _store_keys(k_out, _dot(ckv_bf, wk_ref[...]), kr_rot)
    v = _dot(ckv_bf, wv_ref[...]).astype(jnp.bfloat16)
    for h in range(N_HEADS):
        q_out[h] = (q[:, h * HEAD_PAD:(h + 1) * HEAD_PAD] * qtab).astype(jnp.bfloat16)
    for p in range(N_PAIRS):
        v_out[p] = v[:, p * LANES:(p + 1) * LANES]
    sbz_out[...] = _silu(_dot_nt(hn, w_ref[_R_BZ:_R_BZ + BRANCH_DIM, :])).astype(jnp.bfloat16)


def _mod_row(mod_ref, ctx, dec, i):
    row = jnp.where(i < dec.first, ctx.cond_row(i), dec.cond_row(i))
    return mod_ref[pl.ds(row, 1), :]


def _front_kernel(l, ctx, dec, n_cast, n_aliased, *refs):
    n_in = 6
    ctx_in, dec_in = refs[0:n_in], refs[n_in:2 * n_in]
    mod_ref = refs[2 * n_in]
    n_w = 2 * n_in + 1 + _N_FRONT_WEIGHTS
    weights = refs[2 * n_in + 1:n_w]
    outs = refs[n_w + n_cast + n_aliased:-2]
    ctx_out, dec_out = outs[0:7], outs[7:13]
    scratch = refs[-2:]
    i = pl.program_id(0)
    _cast_chunks(refs[n_w:n_w + n_cast], outs[13:])

    @pl.when(i < dec.first)
    def _():
        _front_tile(l, ctx.seq, ctx.tile(i), *ctx_in, _mod_row(mod_ref, ctx, dec, i), weights, ctx_out, scratch)

    @pl.when(i >= dec.first)
    def _():
        _front_tile(l, dec.seq, dec.tile(i), *dec_in, _mod_row(mod_ref, ctx, dec, i), weights, dec_out, scratch)


def _front(l, x_prompt, x_sample, mod, w, w_front, ctx_tabs, dec_tabs, state_t, big_f32, cast_windows):
    ctx, dec = _passes(x_prompt, x_sample, FRONT_TILE)
    n_steps = ctx.steps + dec.steps

    def pass_in_specs(p):
        tm = p.tm
        hb = tm // HALO
        n_hblk = p.seq // HALO
        tab = pl.BlockSpec((tm, LANES), lambda i: (p.tile(i), 0))
        return [
            pl.BlockSpec((None, tm, D_MODEL), lambda i: (p.batch(i), p.tile(i), 0)),
            pl.BlockSpec((None, HALO, D_MODEL), lambda i: (p.batch(i), jnp.maximum(p.tile(i) * hb - 1, 0), 0)),
            pl.BlockSpec((None, HALO, D_MODEL),
                         lambda i: (p.batch(i), jnp.minimum((p.tile(i) + 1) * hb, n_hblk - 1), 0)),
            tab, tab, tab,
        ]

    def pass_out(p, with_latent):
        tm = p.tm

        def blocks(n):
            return (jax.ShapeDtypeStruct((p.nb, n, p.seq, LANES), jnp.bfloat16),
                    pl.BlockSpec((None, n, tm, LANES), lambda i: (p.batch(i), 0, p.tile(i), 0)))

        (head_shape, head_spec), (pair_shape, pair_spec) = blocks(N_HEADS), blocks(N_PAIRS)
        br_shape = jax.ShapeDtypeStruct((p.nb, p.seq, BRANCH_DIM), jnp.bfloat16)
        br_spec = pl.BlockSpec((None, tm, BRANCH_DIM), lambda i: (p.batch(i), p.tile(i), 0))
        shapes = [head_shape, head_shape, pair_shape] + [br_shape] * 3
        specs = [head_spec, head_spec, pair_spec] + [br_spec] * 3
        if with_latent:
            shapes.append(jax.ShapeDtypeStruct((p.nb, DEPTH, MLA_LATENT, p.seq), jnp.float32))
            if state_t is None:
                specs.append(pl.BlockSpec((None, DEPTH, MLA_LATENT, tm), lambda i: (p.batch(i), 0, 0, p.tile(i))))
            else:
                specs.append(pl.BlockSpec((None, None, MLA_LATENT, tm), lambda i: (p.batch(i), l, 0, p.tile(i))))
        return shapes, specs

    weight_specs = [
        _whole_spec(w["g_pre"]),
        _row_block_spec(_FRONT_ROWS, 0),
        _whole_spec(w["conv_w"]),
        _whole_spec(w["conv_b"]),
        _whole_spec(w["g_q"]),
        _const_spec(l, (Q_LORA, HP)),
        _whole_spec(w["g_kv"]),
        _const_spec(l, (KV_LORA, N_HEADS * QK_NOPE)),
        _const_spec(l, (KV_LORA, N_HEADS * V_HEAD)),
        _const_spec(l, (len(POOL_WINDOWS) // 2, 2 * POOL_GROUP, 2 * POOL_GROUP)),
        _whole_spec(w["pool_scale"]),
    ]
    assert len(weight_specs) == _N_FRONT_WEIGHTS
    ctx_shapes, ctx_specs = pass_out(ctx, True)
    dec_shapes, dec_specs = pass_out(dec, False)
    in_specs = pass_in_specs(ctx) + pass_in_specs(dec) + [_const_spec(l, mod.shape[1:])] + weight_specs
    args = [x_prompt, x_prompt, x_prompt, *ctx_tabs, x_sample, x_sample, x_sample, *dec_tabs, mod,
            w["g_pre"], w_front, w["conv_w"], w["conv_b"], w["g_q"], w["wuq"], w["g_kv"],
            w["wk"], w["wv"], w["pool_w"], w["pool_scale"]]
    cast_in, cast_out, cast_shapes = _cast_specs(l, cast_windows, n_steps)
    in_specs += cast_in
    args += [big_f32[src] for src, _, _ in cast_windows]
    aliases = {}
    if state_t is not None:
        aliases = {len(args): len(ctx_shapes) - 1}
        in_specs.append(pl.BlockSpec(memory_space=pl.ANY))
        args.append(state_t)
    outs = pl.pallas_call(
        functools.partial(_front_kernel, l, ctx, dec, len(cast_windows), len(aliases)),
        grid=(n_steps,),
        in_specs=in_specs,
        out_specs=ctx_specs + dec_specs + cast_out,
        out_shape=ctx_shapes + dec_shapes + cast_shapes,
        scratch_shapes=[pltpu.VMEM((max(ctx.tm, dec.tm) + 2 * HALO, BRANCH_DIM), jnp.float32)] * 2,
        input_output_aliases=aliases,
        compiler_params=_params(),
        name="front",
    )(*args)
    return outs[:7], outs[7:13], tuple(outs[13:])


_N_BACK_WEIGHTS = 5


def _back_tile(l, x_ref, q_ref, k_ref, v_ref, cache, ya_ref, yc_ref, sbz_ref, mod_row, weights, out_ref):
    gpre_ref, wg_ref, wbr_ref, wo_ref, gpost_ref = weights
    g_pre, g_post = gpre_ref[l:l + 1, :], gpost_ref[l:l + 1, :]
    x = x_ref[...]

    def with_ones(v):
        return jnp.concatenate([v, jnp.ones_like(v)], axis=1)

    low_half = lax.broadcasted_iota(jnp.int32, (x.shape[0], LANES), 1) < V_HEAD
    pairs = []
    for p in range(N_PAIRS):
        v_ext = with_ones(v_ref[p])
        if cache is not None:
            kc_ref, vc_ref = cache
            vc_ext = with_ones(vc_ref[p])
        halves = []
        for h in (2 * p, 2 * p + 1):
            q = q_ref[h]
            s = _dot_nt(q, k_ref[h])
            m = jnp.max(s, axis=-1, keepdims=True)
            if cache is not None:
                sc = _dot_nt(q, kc_ref[h])
                m = jnp.maximum(m, jnp.max(sc, axis=-1, keepdims=True))
            o = _dot(jnp.exp2((s - m).astype(jnp.bfloat16)), v_ext)
            if cache is not None:
                o = o + _dot(jnp.exp2((sc - m).astype(jnp.bfloat16)), vc_ext)
            halves.append(o[:, 0:LANES] / o[:, LANES:2 * LANES])
        pairs.append(jnp.where(low_half, halves[0], halves[1]))
    attn = jnp.concatenate(pairs, axis=1)
    yb = (sbz_ref[...].astype(jnp.float32) * attn).astype(jnp.bfloat16)

    hn = _modulated_norm(x, g_pre, mod_row).astype(jnp.bfloat16)
    merged = None
    for n, y in enumerate((ya_ref[...], yb, yc_ref[...])):
        r0 = _R_MERGE - _BACK_START + n * D_MODEL
        gate = jax.nn.sigmoid(_dot_nt(hn, wg_ref[r0:r0 + D_MODEL, :]))
        term = gate * _dot(y, wbr_ref[n * BRANCH_DIM:(n + 1) * BRANCH_DIM, :])
        merged = term if merged is None else merged + term
    out = _rms(_dot(merged.astype(jnp.bfloat16), wo_ref[...]), g_post)
    out_ref[...] = x + mod_row[:, 2 * D_MODEL:3 * D_MODEL] * out


def _back_kernel(l, ctx, dec, n_cast, *refs):
    x_c, q_c, k_c, v_c, ya_c, yc_c, sbz_c = refs[0:7]
    x_d, q_d, k_d, v_d, kc_ref, vc_ref, ya_d, yc_d, sbz_d = refs[7:16]
    mod_ref = refs[16]
    n_w = 17 + _N_BACK_WEIGHTS
    weights = refs[17:n_w]
    out_c, out_d = refs[n_w + n_cast:n_w + n_cast + 2]
    i = pl.program_id(0)
    _cast_chunks(refs[n_w:n_w + n_cast], refs[n_w + n_cast + 2:])

    @pl.when(i < dec.first)
    def _():
        mod_row = _mod_row(mod_ref, ctx, dec, i)
        for s in range(ctx.group):
            _back_tile(l, x_c.at[s], q_c.at[s], k_c.at[s], v_c.at[s], None, ya_c.at[s], yc_c.at[s], sbz_c.at[s],
                       mod_row, weights, out_c.at[s])

    @pl.when(i >= dec.first)
    def _():
        _back_tile(l, x_d.at[0], q_d.at[0], k_d.at[0], v_d.at[0], (kc_ref, vc_ref), ya_d.at[0], yc_d.at[0],
                   sbz_d.at[0], _mod_row(mod_ref, ctx, dec, i), weights, out_d.at[0])


def _back(l, x_prompt, x_sample, mod, w, big_bf, ctx_fr, dec_fr, kc, vc, big_f32, cast_windows):
    ctx, dec = _passes(x_prompt, x_sample, BACK_TILE, CTX_GROUP)
    past = kc.shape[3]
    n_steps = ctx.steps + dec.steps

    def pass_specs(p):
        tq, g = p.tm, p.group
        tile = pl.BlockSpec((g, tq, D_MODEL), lambda i: (p.batch(i), p.tile(i), 0))
        q = pl.BlockSpec((g, N_HEADS, tq, HEAD_PAD), lambda i: (p.batch(i), 0, p.tile(i), 0))
        seq_mode = pl.Buffered(1) if p.nt > 1 else None
        k = pl.BlockSpec((g, N_HEADS, p.seq, LANES), lambda i: (p.batch(i), 0, 0, 0), pipeline_mode=seq_mode)
        v = pl.BlockSpec((g, N_PAIRS, p.seq, LANES), lambda i: (p.batch(i), 0, 0, 0), pipeline_mode=seq_mode)
        br = pl.BlockSpec((g, tq, BRANCH_DIM), lambda i: (p.batch(i), p.tile(i), 0))
        return tile, q, k, v, br

    tile_c, q_c, k_c, v_c, br_c = pass_specs(ctx)
    tile_d, q_d, k_d, v_d, br_d = pass_specs(dec)
    kc_spec = pl.BlockSpec((None, None, N_HEADS, past, LANES), lambda i: (l, dec.batch(i), 0, 0, 0))
    vc_spec = pl.BlockSpec((None, None, N_PAIRS, past, LANES), lambda i: (l, dec.batch(i), 0, 0, 0))
    in_specs = ([tile_c, q_c, k_c, v_c, br_c, br_c, br_c]
                + [tile_d, q_d, k_d, v_d, kc_spec, vc_spec, br_d, br_d, br_d]
                + [_const_spec(l, mod.shape[1:]),
                   _whole_spec(w["g_pre"]),
                   _row_block_spec(_BACK_ROWS, 0),
                   _row_block_spec(3 * BRANCH_DIM, 0),
                   _row_block_spec(D_MODEL, 0),
                   _whole_spec(w["g_post"])])
    q_cx, k_cx, v_cx, ya_cx, yc_cx, sbz_cx = ctx_fr
    q_dx, k_dx, v_dx, ya_dx, yc_dx, sbz_dx = dec_fr
    args = [x_prompt, q_cx, k_cx, v_cx, ya_cx, yc_cx, sbz_cx,
            x_sample, q_dx, k_dx, v_dx, kc, vc, ya_dx, yc_dx, sbz_dx,
            mod, w["g_pre"], *big_bf, w["g_post"]]
    out_specs = [tile_c, tile_d]
    out_shape = [jax.ShapeDtypeStruct(x_prompt.shape, jnp.float32), jax.ShapeDtypeStruct(x_sample.shape, jnp.float32)]
    cast_in, cast_out, cast_shapes = _cast_specs(l + 1, cast_windows, n_steps)
    in_specs += cast_in
    args += [big_f32[src] for src, _, _ in cast_windows]
    out_specs += cast_out
    out_shape += cast_shapes
    outs = pl.pallas_call(
        functools.partial(_back_kernel, l, ctx, dec, len(cast_windows)),
        grid=(n_steps,),
        in_specs=in_specs,
        out_specs=out_specs,
        out_shape=out_shape,
        compiler_params=_params(),
        name="back",
    )(*args)
    return outs[0], outs[1], tuple(outs[2:])


def _rope_swap_perm():
    quarter = QK_ROPE // 4
    idx = np.arange(QK_ROPE).reshape(2, 2, quarter)
    return idx[:, ::-1, :].reshape(-1)


def _rope_tables(n_tokens, rotate):
    if rotate:
        t = np.arange(n_tokens)
        pos = np.stack([t // GRID_W, t % GRID_W], axis=1).astype(np.float64)
        axis_dim = QK_ROPE // 2
        inv = 1.0 / (ROPE_THETA ** (np.arange(0, axis_dim, 2, dtype=np.float64) / axis_dim))
        ang = pos[:, :, None] * inv
        cos, sin = np.cos(ang), np.sin(ang)
        cfull = np.stack([cos, cos], axis=2).reshape(n_tokens, QK_ROPE)
        sfull = np.stack([-sin, sin], axis=2).reshape(n_tokens, QK_ROPE)
    else:
        cfull = np.ones((n_tokens, QK_ROPE))
        sfull = np.zeros((n_tokens, QK_ROPE))
    scale = (QK_NOPE + QK_ROPE) ** -0.5 * np.log2(np.e)
    qtab = np.concatenate([np.full((n_tokens, QK_NOPE), scale), cfull * scale, sfull * scale], axis=1)
    pad = np.zeros((n_tokens, LANES - QK_ROPE))
    tabs = (qtab, np.concatenate([cfull, pad], axis=1), np.concatenate([sfull, pad], axis=1))
    return tuple(jnp.asarray(tab, jnp.float32) for tab in tabs)


def _prepare_weights(g_pre, g_post, conv_w, conv_b, g_q, w_uq, g_kv, w_ukv, pool_w, pool_scale):
    bf = jnp.bfloat16
    perm = _rope_swap_perm()

    uq = w_uq.reshape(DEPTH, Q_LORA, N_HEADS, QK_NOPE + QK_ROPE)
    rope_q = uq[..., QK_NOPE:]
    wuq = jnp.concatenate([uq, rope_q[..., perm]], axis=-1).reshape(DEPTH, Q_LORA, HP).astype(bf)

    ukv = w_ukv.reshape(DEPTH, KV_LORA, N_HEADS, QK_NOPE + V_HEAD)
    wk = ukv[..., :QK_NOPE].reshape(DEPTH, KV_LORA, N_HEADS * QK_NOPE).astype(bf)
    wv = ukv[..., QK_NOPE:].reshape(DEPTH, KV_LORA, N_HEADS * V_HEAD).astype(bf)

    pw = pool_w.astype(bf).reshape(DEPTH, len(POOL_WINDOWS) // 2, 2, POOL_GROUP, POOL_GROUP)
    zero = jnp.zeros_like(pw[:, :, 0])
    pool_bd = jnp.concatenate([jnp.concatenate([pw[:, :, 0], zero], axis=-1),
                               jnp.concatenate([zero, pw[:, :, 1]], axis=-1)], axis=-2)

    return {
        "g_pre": g_pre, "g_post": g_post, "conv_w": jnp.swapaxes(conv_w, 0, 1), "conv_b": conv_b,
        "g_q": g_q, "g_kv": g_kv, "pool_scale": pool_scale,
        "wuq": wuq, "wk": wk, "wv": wv, "pool_w": pool_bd,
    }


def kernel(x_prompt, x_sample, cache_mla_latent, c, c_ctx, w_mod, b_mod, g_pre, g_post, w_in, conv_w, conv_b,
           g_q, w_uq, g_kv, w_ukv, pool_w, pool_scale, w_branch, w_o):
    n_dec = x_sample.shape[0]
    assert 1 + n_dec <= COND_ROWS
    assert w_in.shape == (DEPTH, D_MODEL, W_IN_COLS)
    w = _prepare_weights(g_pre, g_post, conv_w, conv_b, g_q, w_uq, g_kv, w_ukv, pool_w, pool_scale)
    big_f32 = (jnp.swapaxes(w_in, 1, 2), w_branch.reshape(DEPTH, 3 * BRANCH_DIM, D_MODEL), w_o)
    cond = jnp.concatenate([c_ctx[None, :], c, jnp.zeros((COND_ROWS - 1 - n_dec, D_MODEL), jnp.float32)], axis=0)
    cache_t = jnp.swapaxes(cache_mla_latent, 2, 3)
    mod, w_front, kc, vc = _prologue(cond, w_mod, b_mod, big_f32, cache_t, w["wk"], w["wv"])
    ctx_tabs = _rope_tables(x_prompt.shape[1], rotate=False)
    dec_tabs = _rope_tables(x_sample.shape[1], rotate=True)

    back_windows = (_W_BACK, _W_BRANCH, _W_OUT)
    h, hs, state_t, big_bf = x_prompt, x_sample, None, None
    for l in range(DEPTH):
        ctx_fr, dec_fr, cast = _front(l, h, hs, mod, w, w_front, ctx_tabs, dec_tabs, state_t, big_f32,
                                      back_windows if big_bf is None else ())
        state_t = ctx_fr[6]
        big_bf = cast if big_bf is None else big_bf
        next_windows = (_W_FRONT,) + back_windows if l + 1 < DEPTH else ()
        h, hs, cast = _back(l, h, hs, mod, w, big_bf, ctx_fr[:6], dec_fr, kc, vc, big_f32, next_windows)
        if next_windows:
            w_front, big_bf = cast[0], cast[1:]
    return (h, hs, jnp.swapaxes(state_t, 2, 3))
```

```python
import functools

import numpy as np
import jax
import jax.numpy as jnp
from jax import lax
from jax.experimental import pallas as pl
from jax.experimental.pallas import tpu as pltpu

D_MODEL = 1024
DEPTH = 2
GRID_W = 64
EPS = 1e-6
BRANCH_DIM = 512
N_HEADS = 8
QK_NOPE = 64
QK_ROPE = 32
V_HEAD = 64
Q_LORA = 384
KV_LORA = 256
MLA_LATENT = KV_LORA + QK_ROPE
ROPE_THETA = 10000.0
POOL_WINDOWS = (2, 4, 8, 16)
POOL_GROUP = 128

LANES = 128
HEAD_PAD = LANES
HP = N_HEADS * HEAD_PAD
N_PAIRS = N_HEADS * V_HEAD // LANES
FRONT_TILE = 512
BACK_TILE = 256
HALO = 16
COND_ROWS = 8
VMEM_LIMIT = 56 * 1024 * 1024

_R_AB = 0
_R_ACX = BRANCH_DIM
_R_MAIN = 3 * BRANCH_DIM
_MAIN_ROWS = BRANCH_DIM + Q_LORA + KV_LORA + LANES
_R_BZ = 4 * BRANCH_DIM + Q_LORA + MLA_LATENT
_R_CU = _R_BZ + BRANCH_DIM
_R_CZ = _R_CU + BRANCH_DIM
_R_MERGE = _R_CZ + BRANCH_DIM
W_IN_COLS = _R_MERGE + 3 * D_MODEL
_FRONT_ROWS = -(-_R_MERGE // LANES) * LANES
_QD0 = BRANCH_DIM
_CKV0 = _QD0 + Q_LORA
_KR0 = _CKV0 + KV_LORA
_LAT_EXT = KV_LORA + LANES


def _dot(a, b):
    return jnp.dot(a, b, preferred_element_type=jnp.float32)


def _dot_nt(a, b):
    return lax.dot_general(a, b, (((1,), (1,)), ((), ())), preferred_element_type=jnp.float32)


def _rms(x, g):
    return x * lax.rsqrt(jnp.mean(x * x, axis=-1, keepdims=True) + EPS) * g


def _sigmoid(x):
    return 0.5 * jnp.tanh(0.5 * x) + 0.5


def _silu(x):
    return x * _sigmoid(x)


def _modulated_norm(x, g_pre, mod_row):
    shift = mod_row[:, 0:D_MODEL]
    scale = mod_row[:, D_MODEL:2 * D_MODEL]
    return _rms(x, g_pre) * (1.0 + scale) + shift


def _params(n_grid_dims=1):
    return pltpu.CompilerParams(dimension_semantics=("arbitrary",) * n_grid_dims, vmem_limit_bytes=VMEM_LIMIT)


_SRC_W_IN_T, _SRC_W_BRANCH, _SRC_W_O = 0, 1, 2
_BACK_START = (_R_MERGE // 1024) * 1024
_W_FRONT = (_SRC_W_IN_T, 0, _FRONT_ROWS)
_W_BACK = (_SRC_W_IN_T, _BACK_START, W_IN_COLS - _BACK_START)
_W_BRANCH = (_SRC_W_BRANCH, 0, 3 * BRANCH_DIM)
_W_OUT = (_SRC_W_O, 0, D_MODEL)
_BACK_ROWS = _W_BACK[2]


def _chunk_rows(first_row, rows, n_steps):
    if first_row == 0:
        return pl.cdiv(pl.cdiv(rows, n_steps), 16) * 16
    per = LANES
    while first_row % per or pl.cdiv(rows, per) > n_steps:
        per *= 2
        assert per <= first_row
    return per


def _cast_specs(layer, windows, n_steps):
    in_specs, out_specs, out_shapes = [], [], []
    for _, first_row, rows in windows:
        per = _chunk_rows(first_row, rows, n_steps)
        n_blocks = pl.cdiv(rows, per)
        first_blk = first_row // per

        def in_map(i, first_blk=first_blk, last=n_blocks - 1):
            return (layer, first_blk + jnp.minimum(i, last), 0)

        def out_map(i, last=n_blocks - 1):
            return (jnp.minimum(i, last), 0)

        in_specs.append(pl.BlockSpec((None, per, D_MODEL), in_map))
        out_specs.append(pl.BlockSpec((per, D_MODEL), out_map))
        out_shapes.append(jax.ShapeDtypeStruct((rows, D_MODEL), jnp.bfloat16))
    return in_specs, out_specs, out_shapes


def _cast_chunks(in_refs, out_refs):
    for src, dst in zip(in_refs, out_refs):
        dst[...] = src[...].astype(jnp.bfloat16)


def _store_keys(k_ref, k_nope, kr):
    k_rope = pltpu.roll(kr, QK_NOPE, 1) + pltpu.roll(kr, QK_NOPE + QK_ROPE, 1)
    nope_lanes = lax.broadcasted_iota(jnp.int32, kr.shape, 1) < QK_NOPE
    for h in range(N_HEADS):
        pair = k_nope[:, (h // 2) * LANES:(h // 2 + 1) * LANES]
        if h % 2:
            pair = pltpu.roll(pair, LANES - QK_NOPE, 1)
        k_ref[h] = jnp.where(nope_lanes, pair, k_rope).astype(jnp.bfloat16)


def _cache_kv_kernel(lat_t_ref, wk_ref, wv_ref, k_ref, v_ref):
    lat_t = lat_t_ref[...]
    pad = jnp.zeros((_LAT_EXT - MLA_LATENT, lat_t.shape[1]), jnp.float32)
    lat = jnp.concatenate([lat_t, pad], axis=0).T
    ckv_bf = lat[:, 0:KV_LORA].astype(jnp.bfloat16)
    _store_keys(k_ref, _dot(ckv_bf, wk_ref[...]), lat[:, KV_LORA:_LAT_EXT])
    v = _dot(ckv_bf, wv_ref[...]).astype(jnp.bfloat16)
    for p in range(N_PAIRS):
        v_ref[p] = v[:, p * LANES:(p + 1) * LANES]


_MOD_COL_TILES = 2


def _prologue_kernel(n_cache_jobs, cond_ref, wmod_ref, bmod_ref, wsrc_ref, lat_t_ref, wk_ref, wv_ref,
                     mod_ref, wfront_ref, kc_ref, vc_ref):
    i = pl.program_id(0)
    h = _silu(cond_ref[...]).astype(jnp.bfloat16)
    mod_ref[...] = _dot(h, wmod_ref[...].astype(jnp.bfloat16)) + bmod_ref[pl.ds(i // _MOD_COL_TILES, 1), :]
    _cast_chunks([wsrc_ref], [wfront_ref])

    @pl.when(i < n_cache_jobs)
    def _():
        _cache_kv_kernel(lat_t_ref, wk_ref, wv_ref, kc_ref, vc_ref)


def _prologue(cond, w_mod, b_mod, big_f32, cache_t, wk, wv):
    n_steps = DEPTH * _MOD_COL_TILES
    col_tile = 3 * D_MODEL // _MOD_COL_TILES
    nb, _, _, past = cache_t.shape
    n_cache_jobs = DEPTH * nb
    assert n_cache_jobs <= n_steps

    def mod_block(i):
        return (i // _MOD_COL_TILES, 0, i % _MOD_COL_TILES)

    def cache_layer(i):
        return jnp.minimum(i, n_cache_jobs - 1) // nb

    def cache_request(i):
        return jnp.minimum(i, n_cache_jobs - 1) % nb

    def kv(n):
        return (jax.ShapeDtypeStruct((DEPTH, nb, n, past, LANES), jnp.bfloat16),
                pl.BlockSpec((None, None, n, past, LANES), lambda i: (cache_layer(i), cache_request(i), 0, 0, 0)))

    (k_shape, k_spec), (v_shape, v_spec) = kv(N_HEADS), kv(N_PAIRS)
    cast_in, cast_out, cast_shapes = _cast_specs(0, [_W_FRONT], n_steps)
    return pl.pallas_call(
        functools.partial(_prologue_kernel, n_cache_jobs),
        grid=(n_steps,),
        in_specs=[
            pl.BlockSpec((COND_ROWS, D_MODEL), lambda i: (0, 0)),
            pl.BlockSpec((None, D_MODEL, col_tile), mod_block),
            pl.BlockSpec((DEPTH, col_tile), lambda i: (0, i % _MOD_COL_TILES)),
            cast_in[0],
            pl.BlockSpec((None, None, MLA_LATENT, past), lambda i: (cache_request(i), cache_layer(i), 0, 0)),
            pl.BlockSpec((None, KV_LORA, N_HEADS * QK_NOPE), lambda i: (cache_layer(i), 0, 0)),
            pl.BlockSpec((None, KV_LORA, N_HEADS * V_HEAD), lambda i: (cache_layer(i), 0, 0)),
        ],
        out_specs=[pl.BlockSpec((None, COND_ROWS, col_tile), mod_block), cast_out[0], k_spec, v_spec],
        out_shape=[jax.ShapeDtypeStruct((DEPTH, COND_ROWS, 3 * D_MODEL), jnp.float32), cast_shapes[0],
                   k_shape, v_shape],
        compiler_params=_params(),
        name="prologue",
    )(cond, w_mod, b_mod, big_f32[_SRC_W_IN_T], cache_t, wk, wv)


class _Pass:
    def __init__(self, x, tile, first_step, first_cond_row, shared_cond):
        self.nb, self.seq, _ = x.shape
        self.tm = min(tile, self.seq)
        assert self.seq % self.tm == 0
        self.nt = self.seq // self.tm
        self.steps = self.nb * self.nt
        self.first = first_step
        self.first_cond_row = first_cond_row
        self.shared_cond = shared_cond

    def local(self, i):
        return jnp.clip(i - self.first, 0, self.steps - 1)

    def batch(self, i):
        return self.local(i) // self.nt

    def tile(self, i):
        return self.local(i) % self.nt

    def cond_row(self, i):
        return self.first_cond_row if self.shared_cond else self.first_cond_row + self.batch(i)


def _passes(x_prompt, x_sample, tile):
    ctx = _Pass(x_prompt, tile, 0, 0, True)
    dec = _Pass(x_sample, tile, ctx.steps, 1, False)
    return ctx, dec


def _const_spec(l, shape):
    return pl.BlockSpec((None,) + shape, lambda i: (l,) + (0,) * len(shape), pipeline_mode=pl.Buffered(1))


def _whole_spec(arr):
    return pl.BlockSpec(arr.shape, lambda i: (0,) * arr.ndim, pipeline_mode=pl.Buffered(1))


def _row_block_spec(rows, block_index):
    return pl.BlockSpec((rows, D_MODEL), lambda i: (block_index, 0), pipeline_mode=pl.Buffered(1))


_N_FRONT_WEIGHTS = 11


def _front_tile(l, seq_len, j, x_ref, xp_ref, xn_ref, qtab_ref, kcos_ref, ksin_ref, mod_row, weights, outs, scratch):
    gpre_ref, w_ref, convw_ref, convb_ref, gq_ref, wuq_ref, gkv_ref, wk_ref, wv_ref, poolw_ref, pscale_ref = weights
    g_pre, conv_b, g_q, g_kv, p_scale = (r[l:l + 1, :] for r in (gpre_ref, convb_ref, gq_ref, gkv_ref, pscale_ref))
    conv_w = [convw_ref[tap, l:l + 1, :] for tap in range(3)]
    q_out, k_out, v_out, ya_out, yc_out, sbz_out = outs[:6]
    u_scr, cu_scr = scratch
    tm = x_ref.shape[0]
    last_j = seq_len // tm - 1

    if last_j == 0:
        hn = _modulated_norm(x_ref[...], g_pre, mod_row).astype(jnp.bfloat16)
        acx = _dot_nt(hn, w_ref[_R_ACX:_R_ACX + 2 * BRANCH_DIM, :])
        zeros = jnp.zeros((HALO, BRANCH_DIM), jnp.float32)
        for scr in (u_scr, cu_scr):
            scr[0:HALO, :] = zeros
            scr[HALO + tm:2 * HALO + tm, :] = zeros
        u_scr[HALO:HALO + tm, :] = acx[:, 0:BRANCH_DIM] * acx[:, BRANCH_DIM:2 * BRANCH_DIM]
        cu_scr[HALO:HALO + tm, :] = _dot_nt(hn, w_ref[_R_CU:_R_CU + BRANCH_DIM, :])
    else:
        x_ext = jnp.concatenate([xp_ref[...], x_ref[...], xn_ref[...]], axis=0)
        hn_ext = _modulated_norm(x_ext, g_pre, mod_row).astype(jnp.bfloat16)
        hn = hn_ext[HALO:HALO + tm]
        acx = _dot_nt(hn_ext, w_ref[_R_ACX:_R_ACX + 2 * BRANCH_DIM, :])
        row = lax.broadcasted_iota(jnp.int32, (tm + 2 * HALO, 1), 0)
        first_valid = jnp.where(j > 0, 0, HALO)
        end_valid = jnp.where(j < last_j, tm + 2 * HALO, tm + HALO)
        valid = jnp.logical_and(row >= first_valid, row < end_valid)
        ext_rows = slice(0, tm + 2 * HALO)
        u_scr[ext_rows, :] = jnp.where(valid, acx[:, 0:BRANCH_DIM] * acx[:, BRANCH_DIM:2 * BRANCH_DIM], 0.0)
        cu_scr[ext_rows, :] = jnp.where(valid, _dot_nt(hn_ext, w_ref[_R_CU:_R_CU + BRANCH_DIM, :]), 0.0)

    pm = _dot_nt(hn, w_ref[_R_MAIN:_R_MAIN + _MAIN_ROWS, :])
    conv = (u_scr[HALO - 1:HALO - 1 + tm, :] * conv_w[0]
            + u_scr[HALO:HALO + tm, :] * conv_w[1]
            + u_scr[HALO + 1:HALO + 1 + tm, :] * conv_w[2]
            + conv_b)
    ya = _silu(pm[:, 0:BRANCH_DIM]) * (_dot_nt(hn, w_ref[_R_AB:_R_AB + BRANCH_DIM, :]) * conv)
    ya_out[...] = ya.astype(jnp.bfloat16)

    t = j * tm + lax.broadcasted_iota(jnp.int32, (tm, POOL_GROUP), 0)
    pooled = []
    for gi, win in enumerate(POOL_WINDOWS):
        half = win // 2
        cols = slice(gi * POOL_GROUP, (gi + 1) * POOL_GROUP)
        shifts = [1 << b for b in range(win.bit_length() - 1)]
        first = HALO - half
        total = cu_scr[first:first + tm + sum(shifts), cols]
        for shift in shifts:
            total = total[:-shift] + total[shift:]
        count =jnp.minimum(t + half, seq_len) - jnp.maximum(t - half, 0)
        pooled.append((total / count.astype(jnp.float32) - cu_scr[HALO:HALO + tm, cols]).astype(jnp.bfloat16))
    mixed = [_dot(jnp.concatenate(pooled[2 * n:2 * n + 2], axis=1), poolw_ref[n]) for n in range(len(pooled) // 2)]
    mixed = jnp.concatenate(mixed, axis=1) * p_scale
    c_z = _dot_nt(hn, w_ref[_R_CZ:_R_CZ + BRANCH_DIM, :])
    yc_out[...] = (_silu(c_z) * mixed).astype(jnp.bfloat16)

    qn = _rms(pm[:, _QD0:_QD0 + Q_LORA], g_q).astype(jnp.bfloat16)
    q = _dot(qn, wuq_ref[...])
    qtab = qtab_ref[...]
    ckv = _rms(pm[:, _CKV0:_CKV0 + KV_LORA], g_kv)
    kr = pm[:, _KR0:_KR0 + LANES]
    if len(outs) > 6:
        lat_out = outs[6]
        if len(lat_out.shape) == 3:
            for later in range(1, lat_out.shape[0]):
                lat_out[later] = jnp.zeros(lat_out.shape[1:], jnp.float32)
            lat_out = lat_out.at[0]
        lat_out[0:KV_LORA, :] = ckv.T
        lat_out[KV_LORA:MLA_LATENT, :] = kr.T[0:QK_ROPE, :]
    lane = lax.broadcasted_iota(jnp.int32, (tm, LANES), 1)
    quarter = QK_ROPE // 4
    partner = jnp.where(lane % (2 * quarter) < quarter,
                        pltpu.roll(kr, LANES - quarter, 1), pltpu.roll(kr, quarter, 1))
    kr_rot = jnp.where(lane < QK_ROPE, kr * kcos_ref[...] + partner * ksin_ref[...], 0.0)
    ckv_bf = ckv.astype(jnp.bfloat16)
    _store_keys(k_out, _dot(ckv_bf, wk_ref[...]), kr_rot)
    v = _dot(ckv_bf, wv_ref[...]).astype(jnp.bfloat16)
    for h in range(N_HEADS):
        q_out[h] = (q[:, h * HEAD_PAD:(h + 1) * HEAD_PAD] * qtab).astype(jnp.bfloat16)
    for p in range(N_PAIRS):
        v_out[p] = v[:, p * LANES:(p + 1) * LANES]
    sbz_out[...] = _silu(_dot_nt(hn, w_ref[_R_BZ:_R_BZ + BRANCH_DIM, :])).astype(jnp.bfloat16)


def _mod_row(mod_ref, ctx, dec, i):
    row = jnp.where(i < dec.first, ctx.cond_row(i), dec.cond_row(i))
    return mod_ref[pl.ds(row, 1), :]


def _front_kernel(l, ctx, dec, n_cast, n_aliased, *refs):
    n_in = 6
    ctx_in, dec_in = refs[0:n_in], refs[n_in:2 * n_in]
    mod_ref = refs[2 * n_in]
    n_w = 2 * n_in + 1 + _N_FRONT_WEIGHTS
    weights = refs[2 * n_in + 1:n_w]
    outs = refs[n_w + n_cast + n_aliased:-2]
    ctx_out, dec_out = outs[0:7], outs[7:13]
    scratch = refs[-2:]
    i = pl.program_id(0)
    _cast_chunks(refs[n_w:n_w + n_cast], outs[13:])

    @pl.when(i < dec.first)
    def _():
        _front_tile(l, ctx.seq, ctx.tile(i), *ctx_in, _mod_row(mod_ref, ctx, dec, i), weights, ctx_out, scratch)

    @pl.when(i >= dec.first)
    def _():
        _front_tile(l, dec.seq, dec.tile(i), *dec_in, _mod_row(mod_ref, ctx, dec, i), weights, dec_out, scratch)


def _front(l, x_prompt, x_sample, mod, w, w_front, ctx_tabs, dec_tabs, state_t, big_f32, cast_windows):
    ctx, dec = _passes(x_prompt, x_sample, FRONT_TILE)
    n_steps = ctx.steps + dec.steps

    def pass_in_specs(p):
        tm = p.tm
        hb = tm // HALO
        n_hblk = p.seq // HALO
        tab = pl.BlockSpec((tm, LANES), lambda i: (p.tile(i), 0))
        return [
            pl.BlockSpec((None, tm, D_MODEL), lambda i: (p.batch(i), p.tile(i), 0)),
            pl.BlockSpec((None, HALO, D_MODEL), lambda i: (p.batch(i), jnp.maximum(p.tile(i) * hb - 1, 0), 0)),
            pl.BlockSpec((None, HALO, D_MODEL),
                         lambda i: (p.batch(i), jnp.minimum((p.tile(i) + 1) * hb, n_hblk - 1), 0)),
            tab, tab, tab,
        ]

    def pass_out(p, with_latent):
        tm = p.tm

        def blocks(n):
            return (jax.ShapeDtypeStruct((p.nb, n, p.seq, LANES), jnp.bfloat16),
                    pl.BlockSpec((None, n, tm, LANES), lambda i: (p.batch(i), 0, p.tile(i), 0)))

        (head_shape, head_spec), (pair_shape, pair_spec) = blocks(N_HEADS), blocks(N_PAIRS)
        br_shape = jax.ShapeDtypeStruct((p.nb, p.seq, BRANCH_DIM), jnp.bfloat16)
        br_spec = pl.BlockSpec((None, tm, BRANCH_DIM), lambda i: (p.batch(i), p.tile(i), 0))
        shapes = [head_shape, head_shape, pair_shape] + [br_shape] * 3
        specs = [head_spec, head_spec, pair_spec] + [br_spec] * 3
        if with_latent:
            shapes.append(jax.ShapeDtypeStruct((p.nb, DEPTH, MLA_LATENT, p.seq), jnp.float32))
            if state_t is None:
                specs.append(pl.BlockSpec((None, DEPTH, MLA_LATENT, tm), lambda i: (p.batch(i), 0, 0, p.tile(i))))
            else:
                specs.append(pl.BlockSpec((None, None, MLA_LATENT, tm), lambda i: (p.batch(i), l, 0, p.tile(i))))
        return shapes, specs

    weight_specs = [
        _whole_spec(w["g_pre"]),
        _row_block_spec(_FRONT_ROWS, 0),
        _whole_spec(w["conv_w"]),
        _whole_spec(w["conv_b"]),
        _whole_spec(w["g_q"]),
        _const_spec(l, (Q_LORA, HP)),
        _whole_spec(w["g_kv"]),
        _const_spec(l, (KV_LORA, N_HEADS * QK_NOPE)),
        _const_spec(l, (KV_LORA, N_HEADS * V_HEAD)),
        _const_spec(l, (len(POOL_WINDOWS) // 2, 2 * POOL_GROUP, 2 * POOL_GROUP)),
        _whole_spec(w["pool_scale"]),
    ]
    assert len(weight_specs) == _N_FRONT_WEIGHTS
    ctx_shapes, ctx_specs = pass_out(ctx, True)
    dec_shapes, dec_specs = pass_out(dec, False)
    in_specs = pass_in_specs(ctx) + pass_in_specs(dec) + [_const_spec(l, mod.shape[1:])] + weight_specs
    args = [x_prompt, x_prompt, x_prompt, *ctx_tabs, x_sample, x_sample, x_sample, *dec_tabs, mod,
            w["g_pre"], w_front, w["conv_w"], w["conv_b"], w["g_q"], w["wuq"], w["g_kv"],
            w["wk"], w["wv"], w["pool_w"], w["pool_scale"]]
    cast_in, cast_out, cast_shapes = _cast_specs(l, cast_windows, n_steps)
    in_specs += cast_in
    args += [big_f32[src] for src, _, _ in cast_windows]
    aliases = {}
    if state_t is not None:
        aliases = {len(args): len(ctx_shapes) - 1}
        in_specs.append(pl.BlockSpec(memory_space=pl.ANY))
        args.append(state_t)
    outs = pl.pallas_call(
        functools.partial(_front_kernel, l, ctx, dec, len(cast_windows), len(aliases)),
        grid=(n_steps,),
        in_specs=in_specs,
        out_specs=ctx_specs + dec_specs + cast_out,
        out_shape=ctx_shapes + dec_shapes + cast_shapes,
        scratch_shapes=[pltpu.VMEM((max(ctx.tm, dec.tm) + 2 * HALO, BRANCH_DIM), jnp.float32)] * 2,
        input_output_aliases=aliases,
        compiler_params=_params(),
        name="front",
    )(*args)
    return outs[:7], outs[7:13], tuple(outs[13:])


_N_BACK_WEIGHTS = 5


def _back_tile(l, x_ref, q_ref, k_ref, v_ref, cache, ya_ref, yc_ref, sbz_ref, mod_row, weights, out_ref):
    gpre_ref, wg_ref, wbr_ref, wo_ref, gpost_ref = weights
    g_pre, g_post = gpre_ref[l:l + 1, :], gpost_ref[l:l + 1, :]
    x = x_ref[...]

    def with_ones(v):
        return jnp.concatenate([v, jnp.ones_like(v)], axis=1)

    low_half = lax.broadcasted_iota(jnp.int32, (x.shape[0], LANES), 1) < V_HEAD
    pairs = []
    for p in range(N_PAIRS):
        v_ext = with_ones(v_ref[p])
        if cache is not None:
            kc_ref, vc_ref = cache
            vc_ext = with_ones(vc_ref[p])
        halves = []
        for h in (2 * p, 2 * p + 1):
            q = q_ref[h]
            s = _dot_nt(q, k_ref[h])
            m = jnp.max(s, axis=-1, keepdims=True)
            if cache is not None:
                sc = _dot_nt(q, kc_ref[h])
                m = jnp.maximum(m, jnp.max(sc, axis=-1, keepdims=True))
            o = _dot(jnp.exp2((s - m).astype(jnp.bfloat16)), v_ext)
            if cache is not None:
                o = o + _dot(jnp.exp2((sc - m).astype(jnp.bfloat16)), vc_ext)
            halves.append(o[:, 0:LANES] / o[:, LANES:2 * LANES])
        pairs.append(jnp.where(low_half, halves[0], halves[1]))
    attn = jnp.concatenate(pairs, axis=1)
    yb = (sbz_ref[...].astype(jnp.float32) * attn).astype(jnp.bfloat16)

    hn = _modulated_norm(x, g_pre, mod_row).astype(jnp.bfloat16)
    merged = None
    for n, y in enumerate((ya_ref[...], yb, yc_ref[...])):
        r0 = _R_MERGE - _BACK_START + n * D_MODEL
        gate = _sigmoid(_dot_nt(hn, wg_ref[r0:r0 + D_MODEL, :]))
        term = gate * _dot(y, wbr_ref[n * BRANCH_DIM:(n + 1) * BRANCH_DIM, :])
        merged = term if merged is None else merged + term
    out = _rms(_dot(merged.astype(jnp.bfloat16), wo_ref[...]), g_post)
    out_ref[...] = x + mod_row[:, 2 * D_MODEL:3 * D_MODEL] * out


def _back_kernel(l, ctx, dec, n_cast, *refs):
    x_c, q_c, k_c, v_c, ya_c, yc_c, sbz_c = refs[0:7]
    x_d, q_d, k_d, v_d, kc_ref, vc_ref, ya_d, yc_d, sbz_d = refs[7:16]
    mod_ref = refs[16]
    n_w = 17 + _N_BACK_WEIGHTS
    weights = refs[17:n_w]
    out_c, out_d = refs[n_w + n_cast:n_w + n_cast + 2]
    i = pl.program_id(0)
    _cast_chunks(refs[n_w:n_w + n_cast], refs[n_w + n_cast + 2:])

    @pl.when(i < dec.first)
    def _():
        _back_tile(l, x_c, q_c, k_c, v_c, None, ya_c, yc_c, sbz_c, _mod_row(mod_ref, ctx, dec, i), weights, out_c)

    @pl.when(i >= dec.first)
    def _():
        _back_tile(l, x_d, q_d, k_d, v_d, (kc_ref, vc_ref), ya_d, yc_d, sbz_d, _mod_row(mod_ref, ctx, dec, i),
                   weights, out_d)


def _back(l, x_prompt, x_sample, mod, w, big_bf, ctx_fr, dec_fr, kc, vc, big_f32, cast_windows):
    ctx, dec = _passes(x_prompt, x_sample, BACK_TILE)
    past = kc.shape[3]
    n_steps = ctx.steps + dec.steps

    def pass_specs(p):
        tq = p.tm
        tile = pl.BlockSpec((None, tq, D_MODEL), lambda i: (p.batch(i), p.tile(i), 0))
        q = pl.BlockSpec((None, N_HEADS, tq, HEAD_PAD), lambda i: (p.batch(i), 0, p.tile(i), 0))
        k = pl.BlockSpec((None, N_HEADS, p.seq, LANES), lambda i: (p.batch(i), 0, 0, 0))
        v = pl.BlockSpec((None, N_PAIRS, p.seq, LANES), lambda i: (p.batch(i), 0, 0, 0))
        br = pl.BlockSpec((None, tq, BRANCH_DIM), lambda i: (p.batch(i), p.tile(i), 0))
        return tile, q, k, v, br

    tile_c, q_c, k_c, v_c, br_c = pass_specs(ctx)
    tile_d, q_d, k_d, v_d, br_d = pass_specs(dec)
    kc_spec = pl.BlockSpec((None, None, N_HEADS, past, LANES), lambda i: (l, dec.batch(i), 0, 0, 0))
    vc_spec = pl.BlockSpec((None, None, N_PAIRS, past, LANES), lambda i: (l, dec.batch(i), 0, 0, 0))
    in_specs = ([tile_c, q_c, k_c, v_c, br_c, br_c, br_c]
                + [tile_d, q_d, k_d, v_d, kc_spec, vc_spec, br_d, br_d, br_d]
                + [_const_spec(l, mod.shape[1:]),
                   _whole_spec(w["g_pre"]),
                   _row_block_spec(_BACK_ROWS, 0),
                   _row_block_spec(3 * BRANCH_DIM, 0),
                   _row_block_spec(D_MODEL, 0),
                   _whole_spec(w["g_post"])])
    q_cx, k_cx, v_cx, ya_cx, yc_cx, sbz_cx = ctx_fr
    q_dx, k_dx, v_dx, ya_dx, yc_dx, sbz_dx = dec_fr
    args = [x_prompt, q_cx, k_cx, v_cx, ya_cx, yc_cx, sbz_cx,
            x_sample, q_dx, k_dx, v_dx, kc, vc, ya_dx, yc_dx, sbz_dx,
            mod, w["g_pre"], *big_bf, w["g_post"]]
    out_specs = [tile_c, tile_d]
    out_shape = [jax.ShapeDtypeStruct(x_prompt.shape, jnp.float32), jax.ShapeDtypeStruct(x_sample.shape, jnp.float32)]
    cast_in, cast_out, cast_shapes = _cast_specs(l + 1, cast_windows, n_steps)
    in_specs += cast_in
    args += [big_f32[src] for src, _, _ in cast_windows]
    out_specs += cast_out
    out_shape += cast_shapes
    outs = pl.pallas_call(
        functools.partial(_back_kernel, l, ctx, dec, len(cast_windows)),
        grid=(n_steps,),
        in_specs=in_specs,
        out_specs=out_specs,
        out_shape=out_shape,
        compiler_params=_params(),
        name="back",
    )(*args)
    return outs[0], outs[1], tuple(outs[2:])


def _rope_swap_perm():
    quarter = QK_ROPE // 4
    idx = np.arange(QK_ROPE).reshape(2, 2, quarter)
    return idx[:, ::-1, :].reshape(-1)


def _rope_tables(n_tokens, rotate):
    if rotate:
        t = np.arange(n_tokens)
        pos = np.stack([t // GRID_W, t % GRID_W], axis=1).astype(np.float64)
        axis_dim = QK_ROPE // 2
        inv = 1.0 / (ROPE_THETA ** (np.arange(0, axis_dim, 2, dtype=np.float64) / axis_dim))
        ang = pos[:, :, None] * inv
        cos, sin = np.cos(ang), np.sin(ang)
        cfull = np.stack([cos, cos], axis=2).reshape(n_tokens, QK_ROPE)
        sfull = np.stack([-sin, sin], axis=2).reshape(n_tokens, QK_ROPE)
    else:
        cfull = np.ones((n_tokens, QK_ROPE))
        sfull = np.zeros((n_tokens, QK_ROPE))
    scale = (QK_NOPE + QK_ROPE) ** -0.5 * np.log2(np.e)
    qtab = np.concatenate([np.full((n_tokens, QK_NOPE), scale), cfull * scale, sfull * scale], axis=1)
    pad = np.zeros((n_tokens, LANES - QK_ROPE))
    tabs = (qtab, np.concatenate([cfull, pad], axis=1), np.concatenate([sfull, pad], axis=1))
    return tuple(jnp.asarray(tab, jnp.float32) for tab in tabs)


def _prepare_weights(g_pre, g_post, conv_w, conv_b, g_q, w_uq, g_kv, w_ukv, pool_w, pool_scale):
    bf = jnp.bfloat16
    perm = _rope_swap_perm()

    uq = w_uq.reshape(DEPTH, Q_LORA, N_HEADS, QK_NOPE + QK_ROPE)
    rope_q = uq[..., QK_NOPE:]
    wuq = jnp.concatenate([uq, rope_q[..., perm]], axis=-1).reshape(DEPTH, Q_LORA, HP).astype(bf)

    ukv = w_ukv.reshape(DEPTH, KV_LORA, N_HEADS, QK_NOPE + V_HEAD)
    wk = ukv[..., :QK_NOPE].reshape(DEPTH, KV_LORA, N_HEADS * QK_NOPE).astype(bf)
    wv = ukv[..., QK_NOPE:].reshape(DEPTH, KV_LORA, N_HEADS * V_HEAD).astype(bf)

    pw = pool_w.astype(bf).reshape(DEPTH, len(POOL_WINDOWS) // 2, 2, POOL_GROUP, POOL_GROUP)
    zero = jnp.zeros_like(pw[:, :, 0])
    pool_bd = jnp.concatenate([jnp.concatenate([pw[:, :, 0], zero], axis=-1),
                               jnp.concatenate([zero, pw[:, :, 1]], axis=-1)], axis=-2)

    return {
        "g_pre": g_pre, "g_post": g_post, "conv_w": jnp.swapaxes(conv_w, 0, 1), "conv_b": conv_b,
        "g_q": g_q, "g_kv": g_kv, "pool_scale": pool_scale,
        "wuq": wuq, "wk": wk, "wv": wv, "pool_w": pool_bd,
    }


def kernel(x_prompt, x_sample, cache_mla_latent, c, c_ctx, w_mod, b_mod, g_pre, g_post, w_in, conv_w, conv_b,
           g_q, w_uq, g_kv, w_ukv, pool_w, pool_scale, w_branch, w_o):
    n_dec = x_sample.shape[0]
    assert 1 + n_dec <= COND_ROWS
    assert w_in.shape == (DEPTH, D_MODEL, W_IN_COLS)
    w = _prepare_weights(g_pre, g_post, conv_w, conv_b, g_q, w_uq, g_kv, w_ukv, pool_w, pool_scale)
    big_f32 = (jnp.swapaxes(w_in, 1, 2), w_branch.reshape(DEPTH, 3 * BRANCH_DIM, D_MODEL), w_o)
    cond = jnp.concatenate([c_ctx[None, :], c, jnp.zeros((COND_ROWS - 1 - n_dec, D_MODEL), jnp.float32)], axis=0)
    cache_t = jnp.swapaxes(cache_mla_latent, 2, 3)
    mod, w_front, kc, vc = _prologue(cond, w_mod, b_mod, big_f32, cache_t, w["wk"], w["wv"])
    ctx_tabs = _rope_tables(x_prompt.shape[1], rotate=False)
    dec_tabs = _rope_tables(x_sample.shape[1], rotate=True)

    back_windows = (_W_BACK, _W_BRANCH, _W_OUT)
    h, hs, state_t, big_bf = x_prompt, x_sample, None, None
    for l in range(DEPTH):
        ctx_fr, dec_fr, cast = _front(l, h, hs, mod, w, w_front, ctx_tabs, dec_tabs, state_t, big_f32,
                                      back_windows if big_bf is None else ())
        state_t = ctx_fr[6]
        big_bf = cast if big_bf is None else big_bf
        next_windows = (_W_FRONT,) + back_windows if l + 1 < DEPTH else ()
        h, hs, cast = _back(l, h, hs, mod, w, big_bf, ctx_fr[:6], dec_fr, kc, vc, big_f32, next_windows)
        if next_windows:
            w_front, big_bf = cast[0], cast[1:]
    return (h, hs, jnp.swapaxes(state_t, 2, 3))
```

```python
import functools

import numpy as np
import jax
import jax.numpy as jnp
from jax import lax
from jax.experimental import pallas as pl
from jax.experimental.pallas import tpu as pltpu

D_MODEL = 1024
DEPTH = 2
GRID_W = 64
EPS = 1e-6
BRANCH_DIM = 512
N_HEADS = 8
QK_NOPE = 64
QK_ROPE = 32
V_HEAD = 64
Q_LORA = 384
KV_LORA = 256
MLA_LATENT = KV_LORA + QK_ROPE
ROPE_THETA = 10000.0
POOL_WINDOWS = (2, 4, 8, 16)
POOL_GROUP = 128

LANES = 128
HEAD_PAD = LANES
HP = N_HEADS * HEAD_PAD
N_PAIRS = N_HEADS * V_HEAD // LANES
FRONT_TILE = 512
BACK_TILE = 256
HALO = 16
COND_ROWS = 8
VMEM_LIMIT = 56 * 1024 * 1024

_R_AB = 0
_R_ACX = BRANCH_DIM
_R_MAIN = 3 * BRANCH_DIM
_MAIN_ROWS = BRANCH_DIM + Q_LORA + KV_LORA + LANES
_R_BZ = 4 * BRANCH_DIM + Q_LORA + MLA_LATENT
_R_CU = _R_BZ + BRANCH_DIM
_R_CZ = _R_CU + BRANCH_DIM
_R_MERGE = _R_CZ + BRANCH_DIM
W_IN_COLS = _R_MERGE + 3 * D_MODEL
_FRONT_ROWS = -(-_R_MERGE // LANES) * LANES
_QD0 = BRANCH_DIM
_CKV0 = _QD0 + Q_LORA
_KR0 = _CKV0 + KV_LORA
_LAT_EXT = KV_LORA + LANES


def _dot(a, b):
    return jnp.dot(a, b, preferred_element_type=jnp.float32)


def _dot_nt(a, b):
    return lax.dot_general(a, b, (((1,), (1,)), ((), ())), preferred_element_type=jnp.float32)


def _rms(x, g):
    return x * lax.rsqrt(jnp.mean(x * x, axis=-1, keepdims=True) + EPS) * g


def _sigmoid(x):
    return 0.5 * jnp.tanh(0.5 * x) + 0.5


def _silu(x):
    return x * _sigmoid(x)


def _modulated_norm(x, g_pre, mod_row):
    shift = mod_row[:, 0:D_MODEL]
    scale = mod_row[:, D_MODEL:2 * D_MODEL]
    return _rms(x, g_pre) * (1.0 + scale) + shift


def _params(n_grid_dims=1):
    return pltpu.CompilerParams(dimension_semantics=("arbitrary",) * n_grid_dims, vmem_limit_bytes=VMEM_LIMIT)


_SRC_W_IN_T, _SRC_W_BRANCH, _SRC_W_O = 0, 1, 2
_BACK_START = (_R_MERGE // 1024) * 1024
_W_FRONT = (_SRC_W_IN_T, 0, _FRONT_ROWS)
_W_BACK = (_SRC_W_IN_T, _BACK_START, W_IN_COLS - _BACK_START)
_W_BRANCH = (_SRC_W_BRANCH, 0, 3 * BRANCH_DIM)
_W_OUT = (_SRC_W_O, 0, D_MODEL)
_BACK_ROWS = _W_BACK[2]


def _chunk_rows(first_row, rows, n_steps):
    if first_row == 0:
        return pl.cdiv(pl.cdiv(rows, n_steps), 16) * 16
    per = LANES
    while first_row % per or pl.cdiv(rows, per) > n_steps:
        per *= 2
        assert per <= first_row
    return per


def _cast_specs(layer, windows, n_steps):
    in_specs, out_specs, out_shapes = [], [], []
    for _, first_row, rows in windows:
        per = _chunk_rows(first_row, rows, n_steps)
        n_blocks = pl.cdiv(rows, per)
        first_blk = first_row // per

        def in_map(i, first_blk=first_blk, last=n_blocks - 1):
            return (layer, first_blk + jnp.minimum(i, last), 0)

        def out_map(i, last=n_blocks - 1):
            return (jnp.minimum(i, last), 0)

        in_specs.append(pl.BlockSpec((None, per, D_MODEL), in_map))
        out_specs.append(pl.BlockSpec((per, D_MODEL), out_map))
        out_shapes.append(jax.ShapeDtypeStruct((rows, D_MODEL), jnp.bfloat16))
    return in_specs, out_specs, out_shapes


def _cast_chunks(in_refs, out_refs):
    for src, dst in zip(in_refs, out_refs):
        dst[...] = src[...].astype(jnp.bfloat16)


def _store_keys(k_ref, k_nope, kr):
    k_rope = pltpu.roll(kr, QK_NOPE, 1) + pltpu.roll(kr, QK_NOPE + QK_ROPE, 1)
    nope_lanes = lax.broadcasted_iota(jnp.int32, kr.shape, 1) < QK_NOPE
    for h in range(N_HEADS):
        pair = k_nope[:, (h // 2) * LANES:(h // 2 + 1) * LANES]
        if h % 2:
            pair = pltpu.roll(pair, LANES - QK_NOPE, 1)
        k_ref[h] = jnp.where(nope_lanes, pair, k_rope).astype(jnp.bfloat16)


def _cache_kv_kernel(lat_t_ref, wk_ref, wv_ref, k_ref, v_ref):
    lat_t = lat_t_ref[...]
    pad = jnp.zeros((_LAT_EXT - MLA_LATENT, lat_t.shape[1]), jnp.float32)
    lat = jnp.concatenate([lat_t, pad], axis=0).T
    ckv_bf = lat[:, 0:KV_LORA].astype(jnp.bfloat16)
    _store_keys(k_ref, _dot(ckv_bf, wk_ref[...]), lat[:, KV_LORA:_LAT_EXT])
    v = _dot(ckv_bf, wv_ref[...]).astype(jnp.bfloat16)
    for p in range(N_PAIRS):
        v_ref[p] = v[:, p * LANES:(p + 1) * LANES]


_MOD_COL_TILES = 2


def _prologue_kernel(n_cache_jobs, cond_ref, wmod_ref, bmod_ref, wsrc_ref, lat_t_ref, wk_ref, wv_ref,
                     mod_ref, wfront_ref, kc_ref, vc_ref):
    i = pl.program_id(0)
    h = _silu(cond_ref[...]).astype(jnp.bfloat16)
    mod_ref[...] = _dot(h, wmod_ref[...].astype(jnp.bfloat16)) + bmod_ref[pl.ds(i // _MOD_COL_TILES, 1), :]
    _cast_chunks([wsrc_ref], [wfront_ref])

    @pl.when(i < n_cache_jobs)
    def _():
        _cache_kv_kernel(lat_t_ref, wk_ref, wv_ref, kc_ref, vc_ref)


def _prologue(cond, w_mod, b_mod, big_f32, cache_t, wk, wv):
    n_steps = DEPTH * _MOD_COL_TILES
    col_tile = 3 * D_MODEL // _MOD_COL_TILES
    nb, _, _, past = cache_t.shape
    n_cache_jobs = DEPTH * nb
    assert n_cache_jobs <= n_steps

    def mod_block(i):
        return (i // _MOD_COL_TILES, 0, i % _MOD_COL_TILES)

    def cache_layer(i):
        return jnp.minimum(i, n_cache_jobs - 1) // nb

    def cache_request(i):
        return jnp.minimum(i, n_cache_jobs - 1) % nb

    def kv(n):
        return (jax.ShapeDtypeStruct((DEPTH, nb, n, past, LANES), jnp.bfloat16),
                pl.BlockSpec((None, None, n, past, LANES), lambda i: (cache_layer(i), cache_request(i), 0, 0, 0)))

    (k_shape, k_spec), (v_shape, v_spec) = kv(N_HEADS), kv(N_PAIRS)
    cast_in, cast_out, cast_shapes = _cast_specs(0, [_W_FRONT], n_steps)
    return pl.pallas_call(
        functools.partial(_prologue_kernel, n_cache_jobs),
        grid=(n_steps,),
        in_specs=[
            pl.BlockSpec((COND_ROWS, D_MODEL), lambda i: (0, 0)),
            pl.BlockSpec((None, D_MODEL, col_tile), mod_block),
            pl.BlockSpec((DEPTH, col_tile), lambda i: (0, i % _MOD_COL_TILES)),
            cast_in[0],
            pl.BlockSpec((None, None, MLA_LATENT, past), lambda i: (cache_request(i), cache_layer(i), 0, 0)),
            pl.BlockSpec((None, KV_LORA, N_HEADS * QK_NOPE), lambda i: (cache_layer(i), 0, 0)),
            pl.BlockSpec((None, KV_LORA, N_HEADS * V_HEAD), lambda i: (cache_layer(i), 0, 0)),
        ],
        out_specs=[pl.BlockSpec((None, COND_ROWS, col_tile), mod_block), cast_out[0], k_spec, v_spec],
        out_shape=[jax.ShapeDtypeStruct((DEPTH, COND_ROWS, 3 * D_MODEL), jnp.float32), cast_shapes[0],
                   k_shape, v_shape],
        compiler_params=_params(),
        name="prologue",
    )(cond, w_mod, b_mod, big_f32[_SRC_W_IN_T], cache_t, wk, wv)


class _Pass:
    def __init__(self, x, tile, first_step, first_cond_row, shared_cond):
        self.nb, self.seq, _ = x.shape
        self.tm = min(tile, self.seq)
        assert self.seq % self.tm == 0
        self.nt = self.seq // self.tm
        self.steps = self.nb * self.nt
        self.first = first_step
        self.first_cond_row = first_cond_row
        self.shared_cond = shared_cond

    def local(self, i):
        return jnp.clip(i - self.first, 0, self.steps - 1)

    def batch(self, i):
        return self.local(i) // self.nt

    def tile(self, i):
        return self.local(i) % self.nt

    def cond_row(self, i):
        return self.first_cond_row if self.shared_cond else self.first_cond_row + self.batch(i)


def _passes(x_prompt, x_sample, tile):
    ctx = _Pass(x_prompt, tile, 0, 0, True)
    dec = _Pass(x_sample, tile, ctx.steps, 1, False)
    return ctx, dec


def _const_spec(l, shape):
    return pl.BlockSpec((None,) + shape, lambda i: (l,) + (0,) * len(shape), pipeline_mode=pl.Buffered(1))


def _whole_spec(arr):
    return pl.BlockSpec(arr.shape, lambda i: (0,) * arr.ndim, pipeline_mode=pl.Buffered(1))


def _row_block_spec(rows, block_index):
    return pl.BlockSpec((rows, D_MODEL), lambda i: (block_index, 0), pipeline_mode=pl.Buffered(1))


_N_FRONT_WEIGHTS = 11


def _front_tile(l, seq_len, j, x_ref, xp_ref, xn_ref, qtab_ref, kcos_ref, ksin_ref, mod_row, weights, outs, scratch):
    gpre_ref, w_ref, convw_ref, convb_ref, gq_ref, wuq_ref, gkv_ref, wk_ref, wv_ref, poolw_ref, pscale_ref = weights
    g_pre, conv_b, g_q, g_kv, p_scale = (r[l:l + 1, :] for r in (gpre_ref, convb_ref, gq_ref, gkv_ref, pscale_ref))
    conv_w = [convw_ref[tap, l:l + 1, :] for tap in range(3)]
    q_out, k_out, v_out, ya_out, yc_out, sbz_out = outs[:6]
    u_scr, cu_scr = scratch
    tm = x_ref.shape[0]
    last_j = seq_len // tm - 1

    if last_j == 0:
        hn = _modulated_norm(x_ref[...], g_pre, mod_row).astype(jnp.bfloat16)
        acx = _dot_nt(hn, w_ref[_R_ACX:_R_ACX + 2 * BRANCH_DIM, :])
        zeros = jnp.zeros((HALO, BRANCH_DIM), jnp.float32)
        for scr in (u_scr, cu_scr):
            scr[0:HALO, :] = zeros
            scr[HALO + tm:2 * HALO + tm, :] = zeros
        u_scr[HALO:HALO + tm, :] = acx[:, 0:BRANCH_DIM] * acx[:, BRANCH_DIM:2 * BRANCH_DIM]
        cu_scr[HALO:HALO + tm, :] = _dot_nt(hn, w_ref[_R_CU:_R_CU + BRANCH_DIM, :])
    else:
        x_ext = jnp.concatenate([xp_ref[...], x_ref[...], xn_ref[...]], axis=0)
        hn_ext = _modulated_norm(x_ext, g_pre, mod_row).astype(jnp.bfloat16)
        hn = hn_ext[HALO:HALO + tm]
        acx = _dot_nt(hn_ext, w_ref[_R_ACX:_R_ACX + 2 * BRANCH_DIM, :])
        u_ext = acx[:, 0:BRANCH_DIM] * acx[:, BRANCH_DIM:2 * BRANCH_DIM]
        cu_ext = _dot_nt(hn_ext, w_ref[_R_CU:_R_CU + BRANCH_DIM, :])
        head, body, tail = slice(0, HALO), slice(HALO, HALO + tm), slice(HALO + tm, 2 * HALO + tm)
        for scr, ext in ((u_scr, u_ext), (cu_scr, cu_ext)):
            scr[head, :] = jnp.where(j > 0, ext[head], 0.0)
            scr[body, :] = ext[body]
            scr[tail, :] = jnp.where(j < last_j, ext[tail], 0.0)

    pm = _dot_nt(hn, w_ref[_R_MAIN:_R_MAIN + _MAIN_ROWS, :])
    conv = (u_scr[HALO - 1:HALO - 1 + tm, :] * conv_w[0]
            + u_scr[HALO:HALO + tm, :] * conv_w[1]
            + u_scr[HALO + 1:HALO + 1 + tm, :] * conv_w[2]
            + conv_b)
    ya = _silu(pm[:, 0:BRANCH_DIM]) * (_dot_nt(hn, w_ref[_R_AB:_R_AB + BRANCH_DIM, :]) * conv)
    ya_out[...] = ya.astype(jnp.bfloat16)

    t = j * tm + lax.broadcasted_iota(jnp.int32, (tm, POOL_GROUP), 0)
    pooled = []
    for gi, win in enumerate(POOL_WINDOWS):
        half = win // 2
        cols = slice(gi * POOL_GROUP, (gi + 1) * POOL_GROUP)
        shifts = [1 << b for b in range(win.bit_length() - 1)]
        first = HALO - half
        total = cu_scr[first:first + tm + sum(shifts), cols]
        for shift in shifts:
            total = total[:-shift] + total[shift:]
        edge = 8
        assert half <= edge

        def clipped_mean(rows):
            count = jnp.minimum(t[rows] + half, seq_len) - jnp.maximum(t[rows] - half, 0)
            return total[rows] / count.astype(jnp.float32)

        mean = jnp.concatenate([clipped_mean(slice(0, edge)), total[edge:tm - edge] * (1.0 / win),
                                clipped_mean(slice(tm - edge, tm))], axis=0)
        pooled.append((mean - cu_scr[HALO:HALO + tm, cols]).astype(jnp.bfloat16))
    mixed = [_dot(jnp.concatenate(pooled[2 * n:2 * n + 2], axis=1), poolw_ref[n]) for n in range(len(pooled) // 2)]
    mixed = jnp.concatenate(mixed, axis=1) * p_scale
    c_z = _dot_nt(hn, w_ref[_R_CZ:_R_CZ + BRANCH_DIM, :])
    yc_out[...] = (_silu(c_z) * mixed).astype(jnp.bfloat16)

    qn = _rms(pm[:, _QD0:_QD0 + Q_LORA], g_q).astype(jnp.bfloat16)
    q = _dot(qn, wuq_ref[...])
    qtab = qtab_ref[...]
    ckv = _rms(pm[:, _CKV0:_CKV0 + KV_LORA], g_kv)
    kr = pm[:, _KR0:_KR0 + LANES]
    if len(outs) > 6:
        lat_out = outs[6]
        if len(lat_out.shape) == 3:
            for later in range(1, lat_out.shape[0]):
                lat_out[later] = jnp.zeros(lat_out.shape[1:], jnp.float32)
            lat_out = lat_out.at[0]
        lat_out[0:KV_LORA, :] = ckv.T
        lat_out[KV_LORA:MLA_LATENT, :] = kr.T[0:QK_ROPE, :]
    lane = lax.broadcasted_iota(jnp.int32, (tm, LANES), 1)
    quarter = QK_ROPE // 4
    partner = jnp.where(lane % (2 * quarter) < quarter,
                        pltpu.roll(kr, LANES - quarter, 1), pltpu.roll(kr, quarter, 1))
    kr_rot = jnp.where(lane < QK_ROPE, kr * kcos_ref[...] + partner * ksin_ref[...], 0.0)
    ckv_bf = ckv.astype(jnp.bfloat16)
    _store_keys(k_out, _dot(ckv_bf, wk_ref[...]), kr_rot)
    v = _dot(ckv_bf, wv_ref[...]).astype(jnp.bfloat16)
    for h in range(N_HEADS):
        q_out[h] = (q[:, h * HEAD_PAD:(h + 1) * HEAD_PAD] * qtab).astype(jnp.bfloat16)
    for p in range(N_PAIRS):
        v_out[p] = v[:, p * LANES:(p + 1) * LANES]
    sbz_out[...] = _silu(_dot_nt(hn, w_ref[_R_BZ:_R_BZ + BRANCH_DIM, :])).astype(jnp.bfloat16)


def _mod_row(mod_ref, ctx, dec, i):
    row = jnp.where(i < dec.first, ctx.cond_row(i), dec.cond_row(i))
    return mod_ref[pl.ds(row, 1), :]


def _front_kernel(l, ctx, dec, n_cast, n_aliased, *refs):
    n_in = 6
    ctx_in, dec_in = refs[0:n_in], refs[n_in:2 * n_in]
    mod_ref = refs[2 * n_in]
    n_w = 2 * n_in + 1 + _N_FRONT_WEIGHTS
    weights = refs[2 * n_in + 1:n_w]
    outs = refs[n_w + n_cast + n_aliased:-2]
    ctx_out, dec_out = outs[0:7], outs[7:13]
    scratch = refs[-2:]
    i = pl.program_id(0)
    _cast_chunks(refs[n_w:n_w + n_cast], outs[13:])

    @pl.when(i < dec.first)
    def _():
        _front_tile(l, ctx.seq, ctx.tile(i), *ctx_in, _mod_row(mod_ref, ctx, dec, i), weights, ctx_out, scratch)

    @pl.when(i >= dec.first)
    def _():
        _front_tile(l, dec.seq, dec.tile(i), *dec_in, _mod_row(mod_ref, ctx, dec, i), weights, dec_out, scratch)


def _front(l, x_prompt, x_sample, mod, w, w_front, ctx_tabs, dec_tabs, state_t, big_f32, cast_windows):
    ctx, dec = _passes(x_prompt, x_sample, FRONT_TILE)
    n_steps = ctx.steps + dec.steps

    def pass_in_specs(p):
        tm = p.tm
        hb = tm // HALO
        n_hblk = p.seq // HALO
        tab = pl.BlockSpec((tm, LANES), lambda i: (p.tile(i), 0))
        return [
            pl.BlockSpec((None, tm, D_MODEL), lambda i: (p.batch(i), p.tile(i), 0)),
            pl.BlockSpec((None, HALO, D_MODEL), lambda i: (p.batch(i), jnp.maximum(p.tile(i) * hb - 1, 0), 0)),
            pl.BlockSpec((None, HALO, D_MODEL),
                         lambda i: (p.batch(i), jnp.minimum((p.tile(i) + 1) * hb, n_hblk - 1), 0)),
            tab, tab, tab,
        ]

    def pass_out(p, with_latent):
        tm = p.tm

        def blocks(n):
            return (jax.ShapeDtypeStruct((p.nb, n, p.seq, LANES), jnp.bfloat16),
                    pl.BlockSpec((None, n, tm, LANES), lambda i: (p.batch(i), 0, p.tile(i), 0)))

        (head_shape, head_spec), (pair_shape, pair_spec) = blocks(N_HEADS), blocks(N_PAIRS)
        br_shape = jax.ShapeDtypeStruct((p.nb, p.seq, BRANCH_DIM), jnp.bfloat16)
        br_spec = pl.BlockSpec((None, tm, BRANCH_DIM), lambda i: (p.batch(i), p.tile(i), 0))
        shapes = [head_shape, head_shape, pair_shape] + [br_shape] * 3
        specs = [head_spec, head_spec, pair_spec] + [br_spec] * 3
        if with_latent:
            shapes.append(jax.ShapeDtypeStruct((p.nb, DEPTH, MLA_LATENT, p.seq), jnp.float32))
            if state_t is None:
                specs.append(pl.BlockSpec((None, DEPTH, MLA_LATENT, tm), lambda i: (p.batch(i), 0, 0, p.tile(i))))
            else:
                specs.append(pl.BlockSpec((None, None, MLA_LATENT, tm), lambda i: (p.batch(i), l, 0, p.tile(i))))
        return shapes, specs

    weight_specs = [
        _whole_spec(w["g_pre"]),
        _row_block_spec(_FRONT_ROWS, 0),
        _whole_spec(w["conv_w"]),
        _whole_spec(w["conv_b"]),
        _whole_spec(w["g_q"]),
        _const_spec(l, (Q_LORA, HP)),
        _whole_spec(w["g_kv"]),
        _const_spec(l, (KV_LORA, N_HEADS * QK_NOPE)),
        _const_spec(l, (KV_LORA, N_HEADS * V_HEAD)),
        _const_spec(l, (len(POOL_WINDOWS) // 2, 2 * POOL_GROUP, 2 * POOL_GROUP)),
        _whole_spec(w["pool_scale"]),
    ]
    assert len(weight_specs) == _N_FRONT_WEIGHTS
    ctx_shapes, ctx_specs = pass_out(ctx, True)
    dec_shapes, dec_specs = pass_out(dec, False)
    in_specs = pass_in_specs(ctx) + pass_in_specs(dec) + [_const_spec(l, mod.shape[1:])] + weight_specs
    args = [x_prompt, x_prompt, x_prompt, *ctx_tabs, x_sample, x_sample, x_sample, *dec_tabs, mod,
            w["g_pre"], w_front, w["conv_w"], w["conv_b"], w["g_q"], w["wuq"], w["g_kv"],
            w["wk"], w["wv"], w["pool_w"], w["pool_scale"]]
    cast_in, cast_out, cast_shapes = _cast_specs(l, cast_windows, n_steps)
    in_specs += cast_in
    args += [big_f32[src] for src, _, _ in cast_windows]
    aliases = {}
    if state_t is not None:
        aliases = {len(args): len(ctx_shapes) - 1}
        in_specs.append(pl.BlockSpec(memory_space=pl.ANY))
        args.append(state_t)
    outs = pl.pallas_call(
        functools.partial(_front_kernel, l, ctx, dec, len(cast_windows), len(aliases)),
        grid=(n_steps,),
        in_specs=in_specs,
        out_specs=ctx_specs + dec_specs + cast_out,
        out_shape=ctx_shapes + dec_shapes + cast_shapes,
        scratch_shapes=[pltpu.VMEM((max(ctx.tm, dec.tm) + 2 * HALO, BRANCH_DIM), jnp.float32)] * 2,
        input_output_aliases=aliases,
        compiler_params=_params(),
        name="front",
    )(*args)
    return outs[:7], outs[7:13], tuple(outs[13:])


_N_BACK_WEIGHTS = 5


def _back_tile(l, x_ref, q_ref, k_ref, v_ref, cache, ya_ref, yc_ref, sbz_ref, mod_row, weights, out_ref):
    gpre_ref, wg_ref, wbr_ref, wo_ref, gpost_ref = weights
    g_pre, g_post = gpre_ref[l:l + 1, :], gpost_ref[l:l + 1, :]
    x = x_ref[...]

    def with_ones(v):
        return jnp.concatenate([v, jnp.ones_like(v)], axis=1)

    low_half = lax.broadcasted_iota(jnp.int32, (x.shape[0], LANES), 1) < V_HEAD
    pairs = []
    for p in range(N_PAIRS):
        v_ext = with_ones(v_ref[p])
        if cache is not None:
            kc_ref, vc_ref = cache
            vc_ext = with_ones(vc_ref[p])
        halves = []
        for h in (2 * p, 2 * p + 1):
            q = q_ref[h]
            s = _dot_nt(q, k_ref[h])
            m = jnp.max(s, axis=-1, keepdims=True)
            if cache is not None:
                sc = _dot_nt(q, kc_ref[h])
                m = jnp.maximum(m, jnp.max(sc, axis=-1, keepdims=True))
            o = _dot(jnp.exp2((s - m).astype(jnp.bfloat16)), v_ext)
            if cache is not None:
                o = o + _dot(jnp.exp2((sc - m).astype(jnp.bfloat16)), vc_ext)
            halves.append(o[:, 0:LANES] / o[:, LANES:2 * LANES])
        pairs.append(jnp.where(low_half, halves[0], halves[1]))
    attn = jnp.concatenate(pairs, axis=1)
    yb = (sbz_ref[...].astype(jnp.float32) * attn).astype(jnp.bfloat16)

    hn = _modulated_norm(x, g_pre, mod_row).astype(jnp.bfloat16)
    merged = None
    for n, y in enumerate((ya_ref[...], yb, yc_ref[...])):
        r0 = _R_MERGE - _BACK_START + n * D_MODEL
        gate = _sigmoid(_dot_nt(hn, wg_ref[r0:r0 + D_MODEL, :]))
        term = gate * _dot(y, wbr_ref[n * BRANCH_DIM:(n + 1) * BRANCH_DIM, :])
        merged = term if merged is None else merged + term
    out = _rms(_dot(merged.astype(jnp.bfloat16), wo_ref[...]), g_post)
    out_ref[...] = x + mod_row[:, 2 * D_MODEL:3 * D_MODEL] * out


def _back_kernel(l, ctx, dec, n_cast, *refs):
    x_c, q_c, k_c, v_c, ya_c, yc_c, sbz_c = refs[0:7]
    x_d, q_d, k_d, v_d, kc_ref, vc_ref, ya_d, yc_d, sbz_d = refs[7:16]
    mod_ref = refs[16]
    n_w = 17 + _N_BACK_WEIGHTS
    weights = refs[17:n_w]
    out_c, out_d = refs[n_w + n_cast:n_w + n_cast + 2]
    i = pl.program_id(0)
    _cast_chunks(refs[n_w:n_w + n_cast], refs[n_w + n_cast + 2:])

    @pl.when(i < dec.first)
    def _():
        _back_tile(l, x_c, q_c, k_c, v_c, None, ya_c, yc_c, sbz_c, _mod_row(mod_ref, ctx, dec, i), weights, out_c)

    @pl.when(i >= dec.first)
    def _():
        _back_tile(l, x_d, q_d, k_d, v_d, (kc_ref, vc_ref), ya_d, yc_d, sbz_d, _mod_row(mod_ref, ctx, dec, i),
                   weights, out_d)


def _back(l, x_prompt, x_sample, mod, w, big_bf, ctx_fr, dec_fr, kc, vc, big_f32, cast_windows):
    ctx, dec = _passes(x_prompt, x_sample, BACK_TILE)
    past = kc.shape[3]
    n_steps = ctx.steps + dec.steps

    def pass_specs(p):
        tq = p.tm
        tile = pl.BlockSpec((None, tq, D_MODEL), lambda i: (p.batch(i), p.tile(i), 0))
        q = pl.BlockSpec((None, N_HEADS, tq, HEAD_PAD), lambda i: (p.batch(i), 0, p.tile(i), 0))
        k = pl.BlockSpec((None, N_HEADS, p.seq, LANES), lambda i: (p.batch(i), 0, 0, 0))
        v = pl.BlockSpec((None, N_PAIRS, p.seq, LANES), lambda i: (p.batch(i), 0, 0, 0))
        br = pl.BlockSpec((None, tq, BRANCH_DIM), lambda i: (p.batch(i), p.tile(i), 0))
        return tile, q, k, v, br

    tile_c, q_c, k_c, v_c, br_c = pass_specs(ctx)
    tile_d, q_d, k_d, v_d, br_d = pass_specs(dec)
    kc_spec = pl.BlockSpec((None, None, N_HEADS, past, LANES), lambda i: (l, dec.batch(i), 0, 0, 0))
    vc_spec = pl.BlockSpec((None, None, N_PAIRS, past, LANES), lambda i: (l, dec.batch(i), 0, 0, 0))
    in_specs = ([tile_c, q_c, k_c, v_c, br_c, br_c, br_c]
                + [tile_d, q_d, k_d, v_d, kc_spec, vc_spec, br_d, br_d, br_d]
                + [_const_spec(l, mod.shape[1:]),
                   _whole_spec(w["g_pre"]),
                   _row_block_spec(_BACK_ROWS, 0),
                   _row_block_spec(3 * BRANCH_DIM, 0),
                   _row_block_spec(D_MODEL, 0),
                   _whole_spec(w["g_post"])])
    q_cx, k_cx, v_cx, ya_cx, yc_cx, sbz_cx = ctx_fr
    q_dx, k_dx, v_dx, ya_dx, yc_dx, sbz_dx = dec_fr
    args = [x_prompt, q_cx, k_cx, v_cx, ya_cx, yc_cx, sbz_cx,
            x_sample, q_dx, k_dx, v_dx, kc, vc, ya_dx, yc_dx, sbz_dx,
            mod, w["g_pre"], *big_bf, w["g_post"]]
    out_specs = [tile_c, tile_d]
    out_shape = [jax.ShapeDtypeStruct(x_prompt.shape, jnp.float32), jax.ShapeDtypeStruct(x_sample.shape, jnp.float32)]
    cast_in, cast_out, cast_shapes = _cast_specs(l + 1, cast_windows, n_steps)
    in_specs += cast_in
    args += [big_f32[src] for src, _, _ in cast_windows]
    out_specs += cast_out
    out_shape += cast_shapes
    outs = pl.pallas_call(
        functools.partial(_back_kernel, l, ctx, dec, len(cast_windows)),
        grid=(n_steps,),
        in_specs=in_specs,
        out_specs=out_specs,
        out_shape=out_shape,
        compiler_params=_params(),
        name="back",
    )(*args)
    return outs[0], outs[1], tuple(outs[2:])


def _rope_swap_perm():
    quarter = QK_ROPE // 4
    idx = np.arange(QK_ROPE).reshape(2, 2, quarter)
    return idx[:, ::-1, :].reshape(-1)


def _rope_tables(n_tokens, rotate):
    if rotate:
        t = np.arange(n_tokens)
        pos = np.stack([t // GRID_W, t % GRID_W], axis=1).astype(np.float64)
        axis_dim = QK_ROPE // 2
        inv = 1.0 / (ROPE_THETA ** (np.arange(0, axis_dim, 2, dtype=np.float64) / axis_dim))
        ang = pos[:, :, None] * inv
        cos, sin = np.cos(ang), np.sin(ang)
        cfull = np.stack([cos, cos], axis=2).reshape(n_tokens, QK_ROPE)
        sfull = np.stack([-sin, sin], axis=2).reshape(n_tokens, QK_ROPE)
    else:
        cfull = np.ones((n_tokens, QK_ROPE))
        sfull = np.zeros((n_tokens, QK_ROPE))
    scale = (QK_NOPE + QK_ROPE) ** -0.5 * np.log2(np.e)
    qtab = np.concatenate([np.full((n_tokens, QK_NOPE), scale), cfull * scale, sfull * scale], axis=1)
    pad = np.zeros((n_tokens, LANES - QK_ROPE))
    tabs = (qtab, np.concatenate([cfull, pad], axis=1), np.concatenate([sfull, pad], axis=1))
    return tuple(jnp.asarray(tab, jnp.float32) for tab in tabs)


def _prepare_weights(g_pre, g_post, conv_w, conv_b, g_q, w_uq, g_kv, w_ukv, pool_w, pool_scale):
    bf = jnp.bfloat16
    perm = _rope_swap_perm()

    uq = w_uq.reshape(DEPTH, Q_LORA, N_HEADS, QK_NOPE + QK_ROPE)
    rope_q = uq[..., QK_NOPE:]
    wuq = jnp.concatenate([uq, rope_q[..., perm]], axis=-1).reshape(DEPTH, Q_LORA, HP).astype(bf)

    ukv = w_ukv.reshape(DEPTH, KV_LORA, N_HEADS, QK_NOPE + V_HEAD)
    wk = ukv[..., :QK_NOPE].reshape(DEPTH, KV_LORA, N_HEADS * QK_NOPE).astype(bf)
    wv = ukv[..., QK_NOPE:].reshape(DEPTH, KV_LORA, N_HEADS * V_HEAD).astype(bf)

    pw = pool_w.astype(bf).reshape(DEPTH, len(POOL_WINDOWS) // 2, 2, POOL_GROUP, POOL_GROUP)
    zero = jnp.zeros_like(pw[:, :, 0])
    pool_bd = jnp.concatenate([jnp.concatenate([pw[:, :, 0], zero], axis=-1),
                               jnp.concatenate([zero, pw[:, :, 1]], axis=-1)], axis=-2)

    return {
        "g_pre": g_pre, "g_post": g_post, "conv_w": jnp.swapaxes(conv_w, 0, 1), "conv_b": conv_b,
        "g_q": g_q, "g_kv": g_kv, "pool_scale": pool_scale,
        "wuq": wuq, "wk": wk, "wv": wv, "pool_w": pool_bd,
    }


def kernel(x_prompt, x_sample, cache_mla_latent, c, c_ctx, w_mod, b_mod, g_pre, g_post, w_in, conv_w, conv_b,
           g_q, w_uq, g_kv, w_ukv, pool_w, pool_scale, w_branch, w_o):
    n_dec = x_sample.shape[0]
    assert 1 + n_dec <= COND_ROWS
    assert w_in.shape == (DEPTH, D_MODEL, W_IN_COLS)
    w = _prepare_weights(g_pre, g_post, conv_w, conv_b, g_q, w_uq, g_kv, w_ukv, pool_w, pool_scale)
    big_f32 = (jnp.swapaxes(w_in, 1, 2), w_branch.reshape(DEPTH, 3 * BRANCH_DIM, D_MODEL), w_o)
    cond = jnp.concatenate([c_ctx[None, :], c, jnp.zeros((COND_ROWS - 1 - n_dec, D_MODEL), jnp.float32)], axis=0)
    cache_t = jnp.swapaxes(cache_mla_latent, 2, 3)
    mod, w_front, kc, vc = _prologue(cond, w_mod, b_mod, big_f32, cache_t, w["wk"], w["wv"])
    ctx_tabs = _rope_tables(x_prompt.shape[1], rotate=False)
    dec_tabs = _rope_tables(x_sample.shape[1], rotate=True)

    back_windows = (_W_BACK, _W_BRANCH, _W_OUT)
    h, hs, state_t, big_bf = x_prompt, x_sample, None, None
    for l in range(DEPTH):
        ctx_fr, dec_fr, cast = _front(l, h, hs, mod, w, w_front, ctx_tabs, dec_tabs, state_t, big_f32,
                                      back_windows if big_bf is None else ())
        state_t = ctx_fr[6]
        big_bf = cast if big_bf is None else big_bf
        next_windows = (_W_FRONT,) + back_windows if l + 1 < DEPTH else ()
        h, hs, cast = _back(l, h, hs, mod, w, big_bf, ctx_fr[:6], dec_fr, kc, vc, big_f32, next_windows)
        if next_windows:
            w_front, big_bf = cast[0], cast[1:]
    return (h, hs, jnp.swapaxes(state_t, 2, 3))
```

```python
import functools

import numpy as np
import jax
import jax.numpy as jnp
from jax import lax
from jax.experimental import pallas as pl
from jax.experimental.pallas import tpu as pltpu

D_MODEL = 1024
DEPTH = 2
GRID_W = 64
EPS = 1e-6
BRANCH_DIM = 512
N_HEADS = 8
QK_NOPE = 64
QK_ROPE = 32
V_HEAD = 64
Q_LORA = 384
KV_LORA = 256
MLA_LATENT = KV_LORA + QK_ROPE
ROPE_THETA = 10000.0
POOL_WINDOWS = (2, 4, 8, 16)
POOL_GROUP = 128

LANES = 128
HEAD_PAD = LANES
HP = N_HEADS * HEAD_PAD
N_PAIRS = N_HEADS * V_HEAD // LANES
FRONT_TILE = 512
BACK_TILE = 256
HALO = 16
COND_ROWS = 8
VMEM_LIMIT = 56 * 1024 * 1024

_R_AB = 0
_R_ACX = BRANCH_DIM
_R_MAIN = 3 * BRANCH_DIM
_MAIN_ROWS = BRANCH_DIM + Q_LORA + KV_LORA + LANES
_R_BZ = 4 * BRANCH_DIM + Q_LORA + MLA_LATENT
_R_CU = _R_BZ + BRANCH_DIM
_R_CZ = _R_CU + BRANCH_DIM
_R_MERGE = _R_CZ + BRANCH_DIM
W_IN_COLS = _R_MERGE + 3 * D_MODEL
_FRONT_ROWS = -(-_R_MERGE // LANES) * LANES
_QD0 = BRANCH_DIM
_CKV0 = _QD0 + Q_LORA
_KR0 = _CKV0 + KV_LORA
_LAT_EXT = KV_LORA + LANES


def _dot(a, b):
    return jnp.dot(a, b, preferred_element_type=jnp.float32)


def _dot_nt(a, b):
    return lax.dot_general(a, b, (((1,), (1,)), ((), ())), preferred_element_type=jnp.float32)


def _rms(x, g):
    return x * lax.rsqrt(jnp.mean(x * x, axis=-1, keepdims=True) + EPS) * g


def _sigmoid(x):
    return 0.5 * jnp.tanh(0.5 * x) + 0.5


def _silu(x):
    half = 0.5 * x
    return half * (jnp.tanh(half) + 1.0)


def _modulated_norm(x, g_pre, mod_row):
    shift = mod_row[:, 0:D_MODEL]
    scale = mod_row[:, D_MODEL:2 * D_MODEL]
    return _rms(x, g_pre * (1.0 + scale)) + shift


def _params(n_grid_dims=1):
    return pltpu.CompilerParams(dimension_semantics=("arbitrary",) * n_grid_dims, vmem_limit_bytes=VMEM_LIMIT)


_SRC_W_IN_T, _SRC_W_BRANCH, _SRC_W_O = 0, 1, 2
_BACK_START = (_R_MERGE // 1024) * 1024
_W_FRONT = (_SRC_W_IN_T, 0, _FRONT_ROWS)
_W_BACK = (_SRC_W_IN_T, _BACK_START, W_IN_COLS - _BACK_START)
_W_BRANCH = (_SRC_W_BRANCH, 0, 3 * BRANCH_DIM)
_W_OUT = (_SRC_W_O, 0, D_MODEL)
_BACK_ROWS = _W_BACK[2]


def _chunk_rows(first_row, rows, n_steps):
    if first_row == 0:
        return pl.cdiv(pl.cdiv(rows, n_steps), 16) * 16
    per = LANES
    while first_row % per or pl.cdiv(rows, per) > n_steps:
        per *= 2
        assert per <= first_row
    return per


def _cast_specs(layer, windows, n_steps):
    in_specs, out_specs, out_shapes = [], [], []
    for _, first_row, rows in windows:
        per = _chunk_rows(first_row, rows, n_steps)
        n_blocks = pl.cdiv(rows, per)
        first_blk = first_row // per

        def in_map(i, first_blk=first_blk, last=n_blocks - 1):
            return (layer, first_blk + jnp.minimum(i, last), 0)

        def out_map(i, last=n_blocks - 1):
            return (jnp.minimum(i, last), 0)

        in_specs.append(pl.BlockSpec((None, per, D_MODEL), in_map))
        out_specs.append(pl.BlockSpec((per, D_MODEL), out_map))
        out_shapes.append(jax.ShapeDtypeStruct((rows, D_MODEL), jnp.bfloat16))
    return in_specs, out_specs, out_shapes


def _cast_chunks(in_refs, out_refs):
    for src, dst in zip(in_refs, out_refs):
        dst[...] = src[...].astype(jnp.bfloat16)


def _store_keys(k_ref, k_nope, kr):
    k_rope = pltpu.roll(kr, QK_NOPE, 1) + pltpu.roll(kr, QK_NOPE + QK_ROPE, 1)
    nope_lanes = lax.broadcasted_iota(jnp.int32, kr.shape, 1) < QK_NOPE
    for h in range(N_HEADS):
        pair = k_nope[:, (h // 2) * LANES:(h // 2 + 1) * LANES]
        if h % 2:
            pair = pltpu.roll(pair, LANES - QK_NOPE, 1)
        k_ref[h] = jnp.where(nope_lanes, pair, k_rope).astype(jnp.bfloat16)


def _cache_kv_kernel(lat_t_ref, wk_ref, wv_ref, k_ref, v_ref):
    lat_t = lat_t_ref[...]
    pad = jnp.zeros((_LAT_EXT - MLA_LATENT, lat_t.shape[1]), jnp.float32)
    lat = jnp.concatenate([lat_t, pad], axis=0).T
    ckv_bf = lat[:, 0:KV_LORA].astype(jnp.bfloat16)
    _store_keys(k_ref, _dot(ckv_bf, wk_ref[...]), lat[:, KV_LORA:_LAT_EXT])
    v = _dot(ckv_bf, wv_ref[...]).astype(jnp.bfloat16)
    for p in range(N_PAIRS):
        v_ref[p] = v[:, p * LANES:(p + 1) * LANES]


_MOD_COL_TILES = 2


def _prologue_kernel(n_cache_jobs, cond_ref, wmod_ref, bmod_ref, wsrc_ref, lat_t_ref, wk_ref, wv_ref,
                     mod_ref, wfront_ref, kc_ref, vc_ref):
    i = pl.program_id(0)
    h = _silu(cond_ref[...]).astype(jnp.bfloat16)
    mod_ref[...] = _dot(h, wmod_ref[...].astype(jnp.bfloat16)) + bmod_ref[pl.ds(i // _MOD_COL_TILES, 1), :]
    _cast_chunks([wsrc_ref], [wfront_ref])

    @pl.when(i < n_cache_jobs)
    def _():
        _cache_kv_kernel(lat_t_ref, wk_ref, wv_ref, kc_ref, vc_ref)


def _prologue(cond, w_mod, b_mod, big_f32, cache_t, wk, wv):
    n_steps = DEPTH * _MOD_COL_TILES
    col_tile = 3 * D_MODEL // _MOD_COL_TILES
    nb, _, _, past = cache_t.shape
    n_cache_jobs = DEPTH * nb
    assert n_cache_jobs <= n_steps

    def mod_block(i):
        return (i // _MOD_COL_TILES, 0, i % _MOD_COL_TILES)

    def cache_layer(i):
        return jnp.minimum(i, n_cache_jobs - 1) // nb

    def cache_request(i):
        return jnp.minimum(i, n_cache_jobs - 1) % nb

    def kv(n):
        return (jax.ShapeDtypeStruct((DEPTH, nb, n, past, LANES), jnp.bfloat16),
                pl.BlockSpec((None, None, n, past, LANES), lambda i: (cache_layer(i), cache_request(i), 0, 0, 0)))

    (k_shape, k_spec), (v_shape, v_spec) = kv(N_HEADS), kv(N_PAIRS)
    cast_in, cast_out, cast_shapes = _cast_specs(0, [_W_FRONT], n_steps)
    return pl.pallas_call(
        functools.partial(_prologue_kernel, n_cache_jobs),
        grid=(n_steps,),
        in_specs=[
            pl.BlockSpec((COND_ROWS, D_MODEL), lambda i: (0, 0)),
            pl.BlockSpec((None, D_MODEL, col_tile), mod_block),
            pl.BlockSpec((DEPTH, col_tile), lambda i: (0, i % _MOD_COL_TILES)),
            cast_in[0],
            pl.BlockSpec((None, None, MLA_LATENT, past), lambda i: (cache_request(i), cache_layer(i), 0, 0)),
            pl.BlockSpec((None, KV_LORA, N_HEADS * QK_NOPE), lambda i: (cache_layer(i), 0, 0)),
            pl.BlockSpec((None, KV_LORA, N_HEADS * V_HEAD), lambda i: (cache_layer(i), 0, 0)),
        ],
        out_specs=[pl.BlockSpec((None, COND_ROWS, col_tile), mod_block), cast_out[0], k_spec, v_spec],
        out_shape=[jax.ShapeDtypeStruct((DEPTH, COND_ROWS, 3 * D_MODEL), jnp.float32), cast_shapes[0],
                   k_shape, v_shape],
        compiler_params=_params(),
        name="prologue",
    )(cond, w_mod, b_mod, big_f32[_SRC_W_IN_T], cache_t, wk, wv)


class _Pass:
    def __init__(self, x, tile, first_step, first_cond_row, shared_cond):
        self.nb, self.seq, _ = x.shape
        self.tm = min(tile, self.seq)
        assert self.seq % self.tm == 0
        self.nt = self.seq // self.tm
        self.steps = self.nb * self.nt
        self.first = first_step
        self.first_cond_row = first_cond_row
        self.shared_cond = shared_cond

    def local(self, i):
        return jnp.clip(i - self.first, 0, self.steps - 1)

    def batch(self, i):
        return self.local(i) // self.nt

    def tile(self, i):
        return self.local(i) % self.nt

    def cond_row(self, i):
        return self.first_cond_row if self.shared_cond else self.first_cond_row + self.batch(i)


def _passes(x_prompt, x_sample, tile):
    ctx = _Pass(x_prompt, tile, 0, 0, True)
    dec = _Pass(x_sample, tile, ctx.steps, 1, False)
    return ctx, dec


def _const_spec(l, shape):
    return pl.BlockSpec((None,) + shape, lambda i: (l,) + (0,) * len(shape), pipeline_mode=pl.Buffered(1))


def _whole_spec(arr):
    return pl.BlockSpec(arr.shape, lambda i: (0,) * arr.ndim, pipeline_mode=pl.Buffered(1))


def _row_block_spec(rows, block_index):
    return pl.BlockSpec((rows, D_MODEL), lambda i: (block_index, 0), pipeline_mode=pl.Buffered(1))


_N_FRONT_WEIGHTS = 11


def _front_tile(l, seq_len, j, x_ref, xp_ref, xn_ref, qtab_ref, kcos_ref, ksin_ref, mod_row, weights, outs, scratch):
    gpre_ref, w_ref, convw_ref, convb_ref, gq_ref, wuq_ref, gkv_ref, wk_ref, wv_ref, poolw_ref, pscale_ref = weights
    g_pre, conv_b, g_q, g_kv, p_scale = (r[l:l + 1, :] for r in (gpre_ref, convb_ref, gq_ref, gkv_ref, pscale_ref))
    conv_w = [convw_ref[tap, l:l + 1, :] for tap in range(3)]
    q_out, k_out, v_out, ya_out, yc_out, sbz_out = outs[:6]
    u_scr, cu_scr = scratch
    tm = x_ref.shape[0]
    last_j = seq_len // tm - 1

    if last_j == 0:
        hn = _modulated_norm(x_ref[...], g_pre, mod_row).astype(jnp.bfloat16)
        acx = _dot_nt(hn, w_ref[_R_ACX:_R_ACX + 2 * BRANCH_DIM, :])
        zeros = jnp.zeros((HALO, BRANCH_DIM), jnp.float32)
        for scr in (u_scr, cu_scr):
            scr[0:HALO, :] = zeros
            scr[HALO + tm:2 * HALO + tm, :] = zeros
        u_scr[HALO:HALO + tm, :] = acx[:, 0:BRANCH_DIM] * acx[:, BRANCH_DIM:2 * BRANCH_DIM]
        cu_scr[HALO:HALO + tm, :] = _dot_nt(hn, w_ref[_R_CU:_R_CU + BRANCH_DIM, :])
    else:
        x_ext = jnp.concatenate([xp_ref[...], x_ref[...], xn_ref[...]], axis=0)
        hn_ext = _modulated_norm(x_ext, g_pre, mod_row).astype(jnp.bfloat16)
        hn = hn_ext[HALO:HALO + tm]
        acx = _dot_nt(hn_ext, w_ref[_R_ACX:_R_ACX + 2 * BRANCH_DIM, :])
        u_ext = acx[:, 0:BRANCH_DIM] * acx[:, BRANCH_DIM:2 * BRANCH_DIM]
        cu_ext = _dot_nt(hn_ext, w_ref[_R_CU:_R_CU + BRANCH_DIM, :])
        head, body, tail = slice(0, HALO), slice(HALO, HALO + tm), slice(HALO + tm, 2 * HALO + tm)
        for scr, ext in ((u_scr, u_ext), (cu_scr, cu_ext)):
            scr[head, :] = jnp.where(j > 0, ext[head], 0.0)
            scr[body, :] = ext[body]
            scr[tail, :] = jnp.where(j < last_j, ext[tail], 0.0)

    pm = _dot_nt(hn, w_ref[_R_MAIN:_R_MAIN + _MAIN_ROWS, :])
    conv = (u_scr[HALO - 1:HALO - 1 + tm, :] * conv_w[0]
            + u_scr[HALO:HALO + tm, :] * conv_w[1]
            + u_scr[HALO + 1:HALO + 1 + tm, :] * conv_w[2]
            + conv_b)
    ya = _silu(pm[:, 0:BRANCH_DIM]) * (_dot_nt(hn, w_ref[_R_AB:_R_AB + BRANCH_DIM, :]) * conv)
    ya_out[...] = ya.astype(jnp.bfloat16)

    t = j * tm + lax.broadcasted_iota(jnp.int32, (tm, POOL_GROUP), 0)
    pooled = []
    for gi, win in enumerate(POOL_WINDOWS):
        half = win // 2
        cols = slice(gi * POOL_GROUP, (gi + 1) * POOL_GROUP)
        shifts = [1 << b for b in range(win.bit_length() - 1)]
        first = HALO - half
        total = cu_scr[first:first + tm + sum(shifts), cols]
        for shift in shifts:
            total = total[:-shift] + total[shift:]
        edge = 8
        assert half <= edge

        def clipped_mean(rows):
            count = jnp.minimum(t[rows] + half, seq_len) - jnp.maximum(t[rows] - half, 0)
            return total[rows] / count.astype(jnp.float32)

        mean = jnp.concatenate([clipped_mean(slice(0, edge)), total[edge:tm - edge] * (1.0 / win),
                                clipped_mean(slice(tm - edge, tm))], axis=0)
        pooled.append((mean - cu_scr[HALO:HALO + tm, cols]).astype(jnp.bfloat16))
    mixed = [_dot(jnp.concatenate(pooled[2 * n:2 * n + 2], axis=1), poolw_ref[n]) for n in range(len(pooled) // 2)]
    mixed = jnp.concatenate(mixed, axis=1) * p_scale
    c_z = _dot_nt(hn, w_ref[_R_CZ:_R_CZ + BRANCH_DIM, :])
    yc_out[...] = (_silu(c_z) * mixed).astype(jnp.bfloat16)

    qn = _rms(pm[:, _QD0:_QD0 + Q_LORA], g_q).astype(jnp.bfloat16)
    q = _dot(qn, wuq_ref[...])
    qtab = qtab_ref[...]
    ckv = _rms(pm[:, _CKV0:_CKV0 + KV_LORA], g_kv)
    kr = pm[:, _KR0:_KR0 + LANES]
    if len(outs) > 6:
        lat_out = outs[6]
        if len(lat_out.shape) == 3:
            for later in range(1, lat_out.shape[0]):
                lat_out[later] = jnp.zeros(lat_out.shape[1:], jnp.float32)
            lat_out = lat_out.at[0]
        lat_out[0:KV_LORA, :] = ckv.T
        lat_out[KV_LORA:MLA_LATENT, :] = kr.T[0:QK_ROPE, :]
    lane = lax.broadcasted_iota(jnp.int32, (tm, LANES), 1)
    quarter = QK_ROPE // 4
    partner = jnp.where(lane % (2 * quarter) < quarter,
                        pltpu.roll(kr, LANES - quarter, 1), pltpu.roll(kr, quarter, 1))
    kr_rot = jnp.where(lane < QK_ROPE, kr * kcos_ref[...] + partner * ksin_ref[...], 0.0)
    ckv_bf = ckv.astype(jnp.bfloat16)
    _store_keys(k_out, _dot(ckv_bf, wk_ref[...]), kr_rot)
    v = _dot(ckv_bf, wv_ref[...]).astype(jnp.bfloat16)
    for h in range(N_HEADS):
        q_out[h] = (q[:, h * HEAD_PAD:(h + 1) * HEAD_PAD] * qtab).astype(jnp.bfloat16)
    for p in range(N_PAIRS):
        v_out[p] = v[:, p * LANES:(p + 1) * LANES]
    sbz_out[...] = _silu(_dot_nt(hn, w_ref[_R_BZ:_R_BZ + BRANCH_DIM, :])).astype(jnp.bfloat16)


def _mod_row(mod_ref, ctx, dec, i):
    row = jnp.where(i < dec.first, ctx.cond_row(i), dec.cond_row(i))
    return mod_ref[pl.ds(row, 1), :]


def _front_kernel(l, ctx, dec, n_cast, n_aliased, *refs):
    n_in = 6
    ctx_in, dec_in = refs[0:n_in], refs[n_in:2 * n_in]
    mod_ref = refs[2 * n_in]
    n_w = 2 * n_in + 1 + _N_FRONT_WEIGHTS
    weights = refs[2 * n_in + 1:n_w]
    outs = refs[n_w + n_cast + n_aliased:-2]
    ctx_out, dec_out = outs[0:7], outs[7:13]
    scratch = refs[-2:]
    i = pl.program_id(0)
    _cast_chunks(refs[n_w:n_w + n_cast], outs[13:])

    @pl.when(i < dec.first)
    def _():
        _front_tile(l, ctx.seq, ctx.tile(i), *ctx_in, _mod_row(mod_ref, ctx, dec, i), weights, ctx_out, scratch)

    @pl.when(i >= dec.first)
    def _():
        _front_tile(l, dec.seq, dec.tile(i), *dec_in, _mod_row(mod_ref, ctx, dec, i), weights, dec_out, scratch)


def _front(l, x_prompt, x_sample, mod, w, w_front, ctx_tabs, dec_tabs, state_t, big_f32, cast_windows):
    ctx, dec = _passes(x_prompt, x_sample, FRONT_TILE)
    n_steps = ctx.steps + dec.steps

    def pass_in_specs(p):
        tm = p.tm
        hb = tm // HALO
        n_hblk = p.seq // HALO
        tab = pl.BlockSpec((tm, LANES), lambda i: (p.tile(i), 0))
        return [
            pl.BlockSpec((None, tm, D_MODEL), lambda i: (p.batch(i), p.tile(i), 0)),
            pl.BlockSpec((None, HALO, D_MODEL), lambda i: (p.batch(i), jnp.maximum(p.tile(i) * hb - 1, 0), 0)),
            pl.BlockSpec((None, HALO, D_MODEL),
                         lambda i: (p.batch(i), jnp.minimum((p.tile(i) + 1) * hb, n_hblk - 1), 0)),
            tab, tab, tab,
        ]

    def pass_out(p, with_latent):
        tm = p.tm

        def blocks(n):
            return (jax.ShapeDtypeStruct((p.nb, n, p.seq, LANES), jnp.bfloat16),
                    pl.BlockSpec((None, n, tm, LANES), lambda i: (p.batch(i), 0, p.tile(i), 0)))

        (head_shape, head_spec), (pair_shape, pair_spec) = blocks(N_HEADS), blocks(N_PAIRS)
        br_shape = jax.ShapeDtypeStruct((p.nb, p.seq, BRANCH_DIM), jnp.bfloat16)
        br_spec = pl.BlockSpec((None, tm, BRANCH_DIM), lambda i: (p.batch(i), p.tile(i), 0))
        shapes = [head_shape, head_shape, pair_shape] + [br_shape] * 3
        specs = [head_spec, head_spec, pair_spec] + [br_spec] * 3
        if with_latent:
            shapes.append(jax.ShapeDtypeStruct((p.nb, DEPTH, MLA_LATENT, p.seq), jnp.float32))
            if state_t is None:
                specs.append(pl.BlockSpec((None, DEPTH, MLA_LATENT, tm), lambda i: (p.batch(i), 0, 0, p.tile(i))))
            else:
                specs.append(pl.BlockSpec((None, None, MLA_LATENT, tm), lambda i: (p.batch(i), l, 0, p.tile(i))))
        return shapes, specs

    weight_specs = [
        _whole_spec(w["g_pre"]),
        _row_block_spec(_FRONT_ROWS, 0),
        _whole_spec(w["conv_w"]),
        _whole_spec(w["conv_b"]),
        _whole_spec(w["g_q"]),
        _const_spec(l, (Q_LORA, HP)),
        _whole_spec(w["g_kv"]),
        _const_spec(l, (KV_LORA, N_HEADS * QK_NOPE)),
        _const_spec(l, (KV_LORA, N_HEADS * V_HEAD)),
        _const_spec(l, (len(POOL_WINDOWS) // 2, 2 * POOL_GROUP, 2 * POOL_GROUP)),
        _whole_spec(w["pool_scale"]),
    ]
    assert len(weight_specs) == _N_FRONT_WEIGHTS
    ctx_shapes, ctx_specs = pass_out(ctx, True)
    dec_shapes, dec_specs = pass_out(dec, False)
    in_specs = pass_in_specs(ctx) + pass_in_specs(dec) + [_const_spec(l, mod.shape[1:])] + weight_specs
    args = [x_prompt, x_prompt, x_prompt, *ctx_tabs, x_sample, x_sample, x_sample, *dec_tabs, mod,
            w["g_pre"], w_front, w["conv_w"], w["conv_b"], w["g_q"], w["wuq"], w["g_kv"],
            w["wk"], w["wv"], w["pool_w"], w["pool_scale"]]
    cast_in, cast_out, cast_shapes = _cast_specs(l, cast_windows, n_steps)
    in_specs += cast_in
    args += [big_f32[src] for src, _, _ in cast_windows]
    aliases = {}
    if state_t is not None:
        aliases = {len(args): len(ctx_shapes) - 1}
        in_specs.append(pl.BlockSpec(memory_space=pl.ANY))
        args.append(state_t)
    outs = pl.pallas_call(
        functools.partial(_front_kernel, l, ctx, dec, len(cast_windows), len(aliases)),
        grid=(n_steps,),
        in_specs=in_specs,
        out_specs=ctx_specs + dec_specs + cast_out,
        out_shape=ctx_shapes + dec_shapes + cast_shapes,
        scratch_shapes=[pltpu.VMEM((max(ctx.tm, dec.tm) + 2 * HALO, BRANCH_DIM), jnp.float32)] * 2,
        input_output_aliases=aliases,
        compiler_params=_params(),
        name="front",
    )(*args)
    return outs[:7], outs[7:13], tuple(outs[13:])


_N_BACK_WEIGHTS = 5


def _back_tile(l, x_ref, q_ref, k_ref, v_ref, cache, ya_ref, yc_ref, sbz_ref, mod_row, weights, out_ref):
    gpre_ref, wg_ref, wbr_ref, wo_ref, gpost_ref = weights
    g_pre, g_post = gpre_ref[l:l + 1, :], gpost_ref[l:l + 1, :]
    x = x_ref[...]

    def with_ones(v):
        return jnp.concatenate([v, jnp.ones_like(v)], axis=1)

    low_half = lax.broadcasted_iota(jnp.int32, (x.shape[0], LANES), 1) < V_HEAD
    pairs = []
    for p in range(N_PAIRS):
        v_ext = with_ones(v_ref[p])
        if cache is not None:
            kc_ref, vc_ref = cache
            vc_ext = with_ones(vc_ref[p])
        halves = []
        for h in (2 * p, 2 * p + 1):
            q = q_ref[h]
            s = _dot_nt(q, k_ref[h])
            m = jnp.max(s, axis=-1, keepdims=True)
            if cache is not None:
                sc = _dot_nt(q, kc_ref[h])
                m = jnp.maximum(m, jnp.max(sc, axis=-1, keepdims=True))
            o = _dot(jnp.exp2((s - m).astype(jnp.bfloat16)), v_ext)
            if cache is not None:
                o = o + _dot(jnp.exp2((sc - m).astype(jnp.bfloat16)), vc_ext)
            halves.append(o[:, 0:LANES] / o[:, LANES:2 * LANES])
        pairs.append(jnp.where(low_half, halves[0], halves[1]))
    attn = jnp.concatenate(pairs, axis=1)
    yb = (sbz_ref[...].astype(jnp.float32) * attn).astype(jnp.bfloat16)

    hn = _modulated_norm(x, g_pre, mod_row).astype(jnp.bfloat16)
    merged = None
    for n, y in enumerate((ya_ref[...], yb, yc_ref[...])):
        r0 = _R_MERGE - _BACK_START + n * D_MODEL
        gate = _sigmoid(_dot_nt(hn, wg_ref[r0:r0 + D_MODEL, :]))
        term = gate * _dot(y, wbr_ref[n * BRANCH_DIM:(n + 1) * BRANCH_DIM, :])
        merged = term if merged is None else merged + term
    out = _rms(_dot(merged.astype(jnp.bfloat16), wo_ref[...]), g_post)
    out_ref[...] = x + mod_row[:, 2 * D_MODEL:3 * D_MODEL] * out


def _back_kernel(l, ctx, dec, n_cast, *refs):
    x_c, q_c, k_c, v_c, ya_c, yc_c, sbz_c = refs[0:7]
    x_d, q_d, k_d, v_d, kc_ref, vc_ref, ya_d, yc_d, sbz_d = refs[7:16]
    mod_ref = refs[16]
    n_w = 17 + _N_BACK_WEIGHTS
    weights = refs[17:n_w]
    out_c, out_d = refs[n_w + n_cast:n_w + n_cast + 2]
    i = pl.program_id(0)
    _cast_chunks(refs[n_w:n_w + n_cast], refs[n_w + n_cast + 2:])

    @pl.when(i < dec.first)
    def _():
        _back_tile(l, x_c, q_c, k_c, v_c, None, ya_c, yc_c, sbz_c, _mod_row(mod_ref, ctx, dec, i), weights, out_c)

    @pl.when(i >= dec.first)
    def _():
        _back_tile(l, x_d, q_d, k_d, v_d, (kc_ref, vc_ref), ya_d, yc_d, sbz_d, _mod_row(mod_ref, ctx, dec, i),
                   weights, out_d)


def _back(l, x_prompt, x_sample, mod, w, big_bf, ctx_fr, dec_fr, kc, vc, big_f32, cast_windows):
    ctx, dec = _passes(x_prompt, x_sample, BACK_TILE)
    past = kc.shape[3]
    n_steps = ctx.steps + dec.steps

    def pass_specs(p):
        tq = p.tm
        tile = pl.BlockSpec((None, tq, D_MODEL), lambda i: (p.batch(i), p.tile(i), 0))
        q = pl.BlockSpec((None, N_HEADS, tq, HEAD_PAD), lambda i: (p.batch(i), 0, p.tile(i), 0))
        k = pl.BlockSpec((None, N_HEADS, p.seq, LANES), lambda i: (p.batch(i), 0, 0, 0))
        v = pl.BlockSpec((None, N_PAIRS, p.seq, LANES), lambda i: (p.batch(i), 0, 0, 0))
        br = pl.BlockSpec((None, tq, BRANCH_DIM), lambda i: (p.batch(i), p.tile(i), 0))
        return tile, q, k, v, br

    tile_c, q_c, k_c, v_c, br_c = pass_specs(ctx)
    tile_d, q_d, k_d, v_d, br_d = pass_specs(dec)
    kc_spec = pl.BlockSpec((None, None, N_HEADS, past, LANES), lambda i: (l, dec.batch(i), 0, 0, 0))
    vc_spec = pl.BlockSpec((None, None, N_PAIRS, past, LANES), lambda i: (l, dec.batch(i), 0, 0, 0))
    in_specs = ([tile_c, q_c, k_c, v_c, br_c, br_c, br_c]
                + [tile_d, q_d, k_d, v_d, kc_spec, vc_spec, br_d, br_d, br_d]
                + [_const_spec(l, mod.shape[1:]),
                   _whole_spec(w["g_pre"]),
                   _row_block_spec(_BACK_ROWS, 0),
                   _row_block_spec(3 * BRANCH_DIM, 0),
                   _row_block_spec(D_MODEL, 0),
                   _whole_spec(w["g_post"])])
    q_cx, k_cx, v_cx, ya_cx, yc_cx, sbz_cx = ctx_fr
    q_dx, k_dx, v_dx, ya_dx, yc_dx, sbz_dx = dec_fr
    args = [x_prompt, q_cx, k_cx, v_cx, ya_cx, yc_cx, sbz_cx,
            x_sample, q_dx, k_dx, v_dx, kc, vc, ya_dx, yc_dx, sbz_dx,
            mod, w["g_pre"], *big_bf, w["g_post"]]
    out_specs = [tile_c, tile_d]
    out_shape = [jax.ShapeDtypeStruct(x_prompt.shape, jnp.float32), jax.ShapeDtypeStruct(x_sample.shape, jnp.float32)]
    cast_in, cast_out, cast_shapes = _cast_specs(l + 1, cast_windows, n_steps)
    in_specs += cast_in
    args += [big_f32[src] for src, _, _ in cast_windows]
    out_specs += cast_out
    out_shape += cast_shapes
    outs = pl.pallas_call(
        functools.partial(_back_kernel, l, ctx, dec, len(cast_windows)),
        grid=(n_steps,),
        in_specs=in_specs,
        out_specs=out_specs,
        out_shape=out_shape,
        compiler_params=_params(),
        name="back",
    )(*args)
    return outs[0], outs[1], tuple(outs[2:])


def _rope_swap_perm():
    quarter = QK_ROPE // 4
    idx = np.arange(QK_ROPE).reshape(2, 2, quarter)
    return idx[:, ::-1, :].reshape(-1)


def _rope_tables(n_tokens, rotate):
    if rotate:
        t = np.arange(n_tokens)
        pos = np.stack([t // GRID_W, t % GRID_W], axis=1).astype(np.float64)
        axis_dim = QK_ROPE // 2
        inv = 1.0 / (ROPE_THETA ** (np.arange(0, axis_dim, 2, dtype=np.float64) / axis_dim))
        ang = pos[:, :, None] * inv
        cos, sin = np.cos(ang), np.sin(ang)
        cfull = np.stack([cos, cos], axis=2).reshape(n_tokens, QK_ROPE)
        sfull = np.stack([-sin, sin], axis=2).reshape(n_tokens, QK_ROPE)
    else:
        cfull = np.ones((n_tokens, QK_ROPE))
        sfull = np.zeros((n_tokens, QK_ROPE))
    scale = (QK_NOPE + QK_ROPE) ** -0.5 * np.log2(np.e)
    qtab = np.concatenate([np.full((n_tokens, QK_NOPE), scale), cfull * scale, sfull * scale], axis=1)
    pad = np.zeros((n_tokens, LANES - QK_ROPE))
    tabs = (qtab, np.concatenate([cfull, pad], axis=1), np.concatenate([sfull, pad], axis=1))
    return tuple(jnp.asarray(tab, jnp.float32) for tab in tabs)


def _prepare_weights(g_pre, g_post, conv_w, conv_b, g_q, w_uq, g_kv, w_ukv, pool_w, pool_scale):
    bf = jnp.bfloat16
    perm = _rope_swap_perm()

    uq = w_uq.reshape(DEPTH, Q_LORA, N_HEADS, QK_NOPE + QK_ROPE)
    rope_q = uq[..., QK_NOPE:]
    wuq = jnp.concatenate([uq, rope_q[..., perm]], axis=-1).reshape(DEPTH, Q_LORA, HP).astype(bf)

    ukv = w_ukv.reshape(DEPTH, KV_LORA, N_HEADS, QK_NOPE + V_HEAD)
    wk = ukv[..., :QK_NOPE].reshape(DEPTH, KV_LORA, N_HEADS * QK_NOPE).astype(bf)
    wv = ukv[..., QK_NOPE:].reshape(DEPTH, KV_LORA, N_HEADS * V_HEAD).astype(bf)

    pw = pool_w.astype(bf).reshape(DEPTH, len(POOL_WINDOWS) // 2, 2, POOL_GROUP, POOL_GROUP)
    zero = jnp.zeros_like(pw[:, :, 0])
    pool_bd = jnp.concatenate([jnp.concatenate([pw[:, :, 0], zero], axis=-1),
                               jnp.concatenate([zero, pw[:, :, 1]], axis=-1)], axis=-2)

    return {
        "g_pre": g_pre, "g_post": g_post, "conv_w": jnp.swapaxes(conv_w, 0, 1), "conv_b": conv_b,
        "g_q": g_q, "g_kv": g_kv, "pool_scale": pool_scale,
        "wuq": wuq, "wk": wk, "wv": wv, "pool_w": pool_bd,
    }


def kernel(x_prompt, x_sample, cache_mla_latent, c, c_ctx, w_mod, b_mod, g_pre, g_post, w_in, conv_w, conv_b,
           g_q, w_uq, g_kv, w_ukv, pool_w, pool_scale, w_branch, w_o):
    n_dec = x_sample.shape[0]
    assert 1 + n_dec <= COND_ROWS
    assert w_in.shape == (DEPTH, D_MODEL, W_IN_COLS)
    w = _prepare_weights(g_pre, g_post, conv_w, conv_b, g_q, w_uq, g_kv, w_ukv, pool_w, pool_scale)
    big_f32 = (jnp.swapaxes(w_in, 1, 2), w_branch.reshape(DEPTH, 3 * BRANCH_DIM, D_MODEL), w_o)
    cond = jnp.concatenate([c_ctx[None, :], c, jnp.zeros((COND_ROWS - 1 - n_dec, D_MODEL), jnp.float32)], axis=0)
    cache_t = jnp.swapaxes(cache_mla_latent, 2, 3)
    mod, w_front, kc, vc = _prologue(cond, w_mod, b_mod, big_f32, cache_t, w["wk"], w["wv"])
    ctx_tabs = _rope_tables(x_prompt.shape[1], rotate=False)
    dec_tabs = _rope_tables(x_sample.shape[1], rotate=True)

    back_windows = (_W_BACK, _W_BRANCH, _W_OUT)
    h, hs, state_t, big_bf = x_prompt, x_sample, None, None
    for l in range(DEPTH):
        ctx_fr, dec_fr, cast = _front(l, h, hs, mod, w, w_front, ctx_tabs, dec_tabs, state_t, big_f32,
                                      back_windows if big_bf is None else ())
        state_t = ctx_fr[6]
        big_bf = cast if big_bf is None else big_bf
        next_windows = (_W_FRONT,) + back_windows if l + 1 < DEPTH else ()
        h, hs, cast = _back(l, h, hs, mod, w, big_bf, ctx_fr[:6], dec_fr, kc, vc, big_f32, next_windows)
        if next_windows:
            w_front, big_bf = cast[0], cast[1:]
    return (h, hs, jnp.swapaxes(state_t, 2, 3))
```

```python
import functools

import numpy as np
import jax
import jax.numpy as jnp
from jax import lax
from jax.experimental import pallas as pl
from jax.experimental.pallas import tpu as pltpu

D_MODEL = 1024
DEPTH = 2
GRID_W = 64
EPS = 1e-6
BRANCH_DIM = 512
N_HEADS = 8
QK_NOPE = 64
QK_ROPE = 32
V_HEAD = 64
Q_LORA = 384
KV_LORA = 256
MLA_LATENT = KV_LORA + QK_ROPE
ROPE_THETA = 10000.0
POOL_WINDOWS = (2, 4, 8, 16)
POOL_GROUP = 128

LANES = 128
HEAD_PAD = LANES
HP = N_HEADS * HEAD_PAD
N_PAIRS = N_HEADS * V_HEAD // LANES
FRONT_TILE = 512
BACK_TILE = 256
HALO = 16
COND_ROWS = 8
VMEM_LIMIT = 56 * 1024 * 1024

_R_AB = 0
_R_ACX = BRANCH_DIM
_R_MAIN = 3 * BRANCH_DIM
_MAIN_ROWS = BRANCH_DIM + Q_LORA + KV_LORA + LANES
_R_BZ = 4 * BRANCH_DIM + Q_LORA + MLA_LATENT
_R_CU = _R_BZ + BRANCH_DIM
_R_CZ = _R_CU + BRANCH_DIM
_R_MERGE = _R_CZ + BRANCH_DIM
W_IN_COLS = _R_MERGE + 3 * D_MODEL
_FRONT_ROWS = -(-_R_MERGE // LANES) * LANES
_QD0 = BRANCH_DIM
_CKV0 = _QD0 + Q_LORA
_KR0 = _CKV0 + KV_LORA
_LAT_EXT = KV_LORA + LANES


def _dot(a, b):
    return jnp.dot(a, b, preferred_element_type=jnp.float32)


def _dot_nt(a, b):
    return lax.dot_general(a, b, (((1,), (1,)), ((), ())), preferred_element_type=jnp.float32)


def _rms(x, g):
    return x * lax.rsqrt(jnp.mean(x * x, axis=-1, keepdims=True) + EPS) * g


def _sigmoid(x):
    return 0.5 * jnp.tanh(0.5 * x) + 0.5


def _silu(x):
    return x * _sigmoid(x)


def _modulated_norm(x, g_pre, mod_row):
    shift = mod_row[:, 0:D_MODEL]
    scale = mod_row[:, D_MODEL:2 * D_MODEL]
    return _rms(x, g_pre) * (1.0 + scale) + shift


def _params(n_grid_dims=1):
    return pltpu.CompilerParams(dimension_semantics=("arbitrary",) * n_grid_dims, vmem_limit_bytes=VMEM_LIMIT)


_SRC_W_IN_T, _SRC_W_BRANCH, _SRC_W_O = 0, 1, 2
_BACK_START = (_R_MERGE // 1024) * 1024
_W_FRONT = (_SRC_W_IN_T, 0, _FRONT_ROWS)
_W_BACK = (_SRC_W_IN_T, _BACK_START, W_IN_COLS - _BACK_START)
_W_BRANCH = (_SRC_W_BRANCH, 0, 3 * BRANCH_DIM)
_W_OUT = (_SRC_W_O, 0, D_MODEL)
_BACK_ROWS = _W_BACK[2]


def _chunk_rows(first_row, rows, n_steps):
    if first_row == 0:
        return pl.cdiv(pl.cdiv(rows, n_steps), 16) * 16
    per = LANES
    while first_row % per or pl.cdiv(rows, per) > n_steps:
        per *= 2
        assert per <= first_row
    return per


def _cast_specs(layer, windows, n_steps):
    in_specs, out_specs, out_shapes = [], [], []
    for _, first_row, rows in windows:
        per = _chunk_rows(first_row, rows, n_steps)
        n_blocks = pl.cdiv(rows, per)
        first_blk = first_row // per

        def in_map(i, first_blk=first_blk, last=n_blocks - 1):
            return (layer, first_blk + jnp.minimum(i, last), 0)

        def out_map(i, last=n_blocks - 1):
            return (jnp.minimum(i, last), 0)

        in_specs.append(pl.BlockSpec((None, per, D_MODEL), in_map))
        out_specs.append(pl.BlockSpec((per, D_MODEL), out_map))
        out_shapes.append(jax.ShapeDtypeStruct((rows, D_MODEL), jnp.bfloat16))
    return in_specs, out_specs, out_shapes


def _cast_chunks(in_refs, out_refs):
    for src, dst in zip(in_refs, out_refs):
        dst[...] = src[...].astype(jnp.bfloat16)


def _store_keys_values(k_ref, v_ref, ckv_bf, wukv_ref, kr):
    kv = _dot(ckv_bf, wukv_ref[...].astype(jnp.bfloat16))
    k_rope = pltpu.roll(kr, QK_NOPE, 1) + pltpu.roll(kr, QK_NOPE + QK_ROPE, 1)
    low_half = lax.broadcasted_iota(jnp.int32, kr.shape, 1) < QK_NOPE
    blocks = [kv[:, h * LANES:(h + 1) * LANES] for h in range(N_HEADS)]
    for h in range(N_HEADS):
        k_ref[h] = jnp.where(low_half, blocks[h], k_rope).astype(jnp.bfloat16)
    for p in range(N_PAIRS):
        even_v = pltpu.roll(blocks[2 * p], LANES - V_HEAD, 1)
        v_ref[p] = jnp.where(low_half, even_v, blocks[2 * p + 1]).astype(jnp.bfloat16)


def _cache_kv_kernel(lat_t_ref, wukv_ref, k_ref, v_ref):
    lat_t = lat_t_ref[...]
    pad = jnp.zeros((_LAT_EXT - MLA_LATENT, lat_t.shape[1]), jnp.float32)
    lat = jnp.concatenate([lat_t, pad], axis=0).T
    _store_keys_values(k_ref, v_ref, lat[:, 0:KV_LORA].astype(jnp.bfloat16), wukv_ref, lat[:, KV_LORA:_LAT_EXT])


_MOD_COL_TILES = 2


def _prologue_kernel(n_cache_jobs, cond_ref, wmod_ref, bmod_ref, wsrc_ref, lat_t_ref, wukv_ref,
                     mod_ref, wfront_ref, kc_ref, vc_ref):
    i = pl.program_id(0)
    h = _silu(cond_ref[...]).astype(jnp.bfloat16)
    mod_ref[...] = _dot(h, wmod_ref[...].astype(jnp.bfloat16)) + bmod_ref[pl.ds(i // _MOD_COL_TILES, 1), :]
    _cast_chunks([wsrc_ref], [wfront_ref])

    @pl.when(i < n_cache_jobs)
    def _():
        _cache_kv_kernel(lat_t_ref, wukv_ref, kc_ref, vc_ref)


def _prologue(cond, w_mod, b_mod, big_f32, cache_t, w_ukv):
    n_steps = DEPTH * _MOD_COL_TILES
    col_tile = 3 * D_MODEL // _MOD_COL_TILES
    nb, _, _, past = cache_t.shape
    n_cache_jobs = DEPTH * nb
    assert n_cache_jobs <= n_steps

    def mod_block(i):
        return (i // _MOD_COL_TILES, 0, i % _MOD_COL_TILES)

    def cache_layer(i):
        return jnp.minimum(i, n_cache_jobs - 1) // nb

    def cache_request(i):
        return jnp.minimum(i, n_cache_jobs - 1) % nb

    def kv(n):
        return (jax.ShapeDtypeStruct((DEPTH, nb, n, past, LANES), jnp.bfloat16),
                pl.BlockSpec((None, None, n, past, LANES), lambda i: (cache_layer(i), cache_request(i), 0, 0, 0)))

    (k_shape, k_spec), (v_shape, v_spec) = kv(N_HEADS), kv(N_PAIRS)
    cast_in, cast_out, cast_shapes = _cast_specs(0, [_W_FRONT], n_steps)
    return pl.pallas_call(
        functools.partial(_prologue_kernel, n_cache_jobs),
        grid=(n_steps,),
        in_specs=[
            pl.BlockSpec((COND_ROWS, D_MODEL), lambda i: (0, 0)),
            pl.BlockSpec((None, D_MODEL, col_tile), mod_block),
            pl.BlockSpec((DEPTH, col_tile), lambda i: (0, i % _MOD_COL_TILES)),
            cast_in[0],
            pl.BlockSpec((None, None, MLA_LATENT, past), lambda i: (cache_request(i), cache_layer(i), 0, 0)),
            pl.BlockSpec((None, KV_LORA, HP), lambda i: (cache_layer(i), 0, 0)),
        ],
        out_specs=[pl.BlockSpec((None, COND_ROWS, col_tile), mod_block), cast_out[0], k_spec, v_spec],
        out_shape=[jax.ShapeDtypeStruct((DEPTH, COND_ROWS, 3 * D_MODEL), jnp.float32), cast_shapes[0],
                   k_shape, v_shape],
        compiler_params=_params(),
        name="prologue",
    )(cond, w_mod, b_mod, big_f32[_SRC_W_IN_T], cache_t, w_ukv)


class _Pass:
    def __init__(self, x, tile, first_step, first_cond_row, shared_cond):
        self.nb, self.seq, _ = x.shape
        self.tm = min(tile, self.seq)
        assert self.seq % self.tm == 0
        self.nt = self.seq // self.tm
        self.steps = self.nb * self.nt
        self.first = first_step
        self.first_cond_row = first_cond_row
        self.shared_cond = shared_cond

    def local(self, i):
        return jnp.clip(i - self.first, 0, self.steps - 1)

    def batch(self, i):
        return self.local(i) // self.nt

    def tile(self, i):
        return self.local(i) % self.nt

    def cond_row(self, i):
        return self.first_cond_row if self.shared_cond else self.first_cond_row + self.batch(i)


def _passes(x_prompt, x_sample, tile):
    ctx = _Pass(x_prompt, tile, 0, 0, True)
    dec = _Pass(x_sample, tile, ctx.steps, 1, False)
    return ctx, dec


def _const_spec(l, shape):
    return pl.BlockSpec((None,) + shape, lambda i: (l,) + (0,) * len(shape), pipeline_mode=pl.Buffered(1))


def _whole_spec(arr):
    return pl.BlockSpec(arr.shape, lambda i: (0,) * arr.ndim, pipeline_mode=pl.Buffered(1))


def _row_block_spec(rows, block_index):
    return pl.BlockSpec((rows, D_MODEL), lambda i: (block_index, 0), pipeline_mode=pl.Buffered(1))


_N_FRONT_WEIGHTS = 10


def _front_tile(l, seq_len, j, x_ref, xp_ref, xn_ref, qtab_ref, kcos_ref, ksin_ref, mod_row, weights, outs, scratch):
    gpre_ref, w_ref, convw_ref, convb_ref, gq_ref, wuq_ref, gkv_ref, wukv_ref, poolw_ref, pscale_ref = weights
    g_pre, conv_b, g_q, g_kv, p_scale = (r[l:l + 1, :] for r in (gpre_ref, convb_ref, gq_ref, gkv_ref, pscale_ref))
    conv_w = [convw_ref[tap, l:l + 1, :] for tap in range(3)]
    q_out, k_out, v_out, ya_out, yc_out, sbz_out = outs[:6]
    u_scr, cu_scr = scratch
    tm = x_ref.shape[0]
    last_j = seq_len // tm - 1

    if last_j == 0:
        hn = _modulated_norm(x_ref[...], g_pre, mod_row).astype(jnp.bfloat16)
        acx = _dot_nt(hn, w_ref[_R_ACX:_R_ACX + 2 * BRANCH_DIM, :])
        zeros = jnp.zeros((HALO, BRANCH_DIM), jnp.float32)
        for scr in (u_scr, cu_scr):
            scr[0:HALO, :] = zeros
            scr[HALO + tm:2 * HALO + tm, :] = zeros
        u_scr[HALO:HALO + tm, :] = acx[:, 0:BRANCH_DIM] * acx[:, BRANCH_DIM:2 * BRANCH_DIM]
        cu_scr[HALO:HALO + tm, :] = _dot_nt(hn, w_ref[_R_CU:_R_CU + BRANCH_DIM, :])
    else:
        x_ext = jnp.concatenate([xp_ref[...], x_ref[...], xn_ref[...]], axis=0)
        hn_ext = _modulated_norm(x_ext, g_pre, mod_row).astype(jnp.bfloat16)
        hn = hn_ext[HALO:HALO + tm]
        acx = _dot_nt(hn_ext, w_ref[_R_ACX:_R_ACX + 2 * BRANCH_DIM, :])
        u_ext = acx[:, 0:BRANCH_DIM] * acx[:, BRANCH_DIM:2 * BRANCH_DIM]
        cu_ext = _dot_nt(hn_ext, w_ref[_R_CU:_R_CU + BRANCH_DIM, :])
        head, body, tail = slice(0, HALO), slice(HALO, HALO + tm), slice(HALO + tm, 2 * HALO + tm)
        for scr, ext in ((u_scr, u_ext), (cu_scr, cu_ext)):
            scr[head, :] = jnp.where(j > 0, ext[head], 0.0)
            scr[body, :] = ext[body]
            scr[tail, :] = jnp.where(j < last_j, ext[tail], 0.0)

    pm = _dot_nt(hn, w_ref[_R_MAIN:_R_MAIN + _MAIN_ROWS, :])
    conv = (u_scr[HALO - 1:HALO - 1 + tm, :] * conv_w[0]
            + u_scr[HALO:HALO + tm, :] * conv_w[1]
            + u_scr[HALO + 1:HALO + 1 + tm, :] * conv_w[2]
            + conv_b)
    ya = _silu(pm[:, 0:BRANCH_DIM]) * (_dot_nt(hn, w_ref[_R_AB:_R_AB + BRANCH_DIM, :]) * conv)
    ya_out[...] = ya.astype(jnp.bfloat16)

    t = j * tm + lax.broadcasted_iota(jnp.int32, (tm, POOL_GROUP), 0)
    pooled = []
    for gi, win in enumerate(POOL_WINDOWS):
        half = win // 2
        cols = slice(gi * POOL_GROUP, (gi + 1) * POOL_GROUP)
        shifts = [1 << b for b in range(win.bit_length() - 1)]
        first = HALO - half
        total = cu_scr[first:first + tm + sum(shifts), cols]
        for shift in shifts:
            total = total[:-shift] + total[shift:]
        edge = 8
        assert half <= edge

        def clipped_mean(rows):
            count = jnp.minimum(t[rows] + half, seq_len) - jnp.maximum(t[rows] - half, 0)
            return total[rows] / count.astype(jnp.float32)

        mean = jnp.concatenate([clipped_mean(slice(0, edge)), total[edge:tm - edge] * (1.0 / win),
                                clipped_mean(slice(tm - edge, tm))], axis=0)
        pooled.append((mean - cu_scr[HALO:HALO + tm, cols]).astype(jnp.bfloat16))
    zero = jnp.zeros((POOL_GROUP, POOL_GROUP), jnp.float32)
    mixed = []
    for n in range(len(pooled) // 2):
        both = jnp.concatenate([jnp.concatenate([poolw_ref[2 * n], zero], axis=1),
                                jnp.concatenate([zero, poolw_ref[2 * n + 1]], axis=1)], axis=0).astype(jnp.bfloat16)
        mixed.append(_dot(jnp.concatenate(pooled[2 * n:2 * n + 2], axis=1), both))
    mixed = jnp.concatenate(mixed, axis=1) * p_scale
    c_z = _dot_nt(hn, w_ref[_R_CZ:_R_CZ + BRANCH_DIM, :])
    yc_out[...] = (_silu(c_z) * mixed).astype(jnp.bfloat16)

    qn = _rms(pm[:, _QD0:_QD0 + Q_LORA], g_q).astype(jnp.bfloat16)
    q = _dot(qn, wuq_ref[...])
    qtab = qtab_ref[...]
    ckv = _rms(pm[:, _CKV0:_CKV0 + KV_LORA], g_kv)
    kr = pm[:, _KR0:_KR0 + LANES]
    if len(outs) > 6:
        lat_out = outs[6]
        if len(lat_out.shape) == 3:
            for later in range(1, lat_out.shape[0]):
                lat_out[later] = jnp.zeros(lat_out.shape[1:], jnp.float32)
            lat_out = lat_out.at[0]
        lat_out[0:KV_LORA, :] = ckv.T
        lat_out[KV_LORA:MLA_LATENT, :] = kr.T[0:QK_ROPE, :]
    lane = lax.broadcasted_iota(jnp.int32, (tm, LANES), 1)
    quarter = QK_ROPE // 4
    partner = jnp.where(lane % (2 * quarter) < quarter,
                        pltpu.roll(kr, LANES - quarter, 1), pltpu.roll(kr, quarter, 1))
    kr_rot = jnp.where(lane < QK_ROPE, kr * kcos_ref[...] + partner * ksin_ref[...], 0.0)
    _store_keys_values(k_out, v_out, ckv.astype(jnp.bfloat16), wukv_ref, kr_rot)
    for h in range(N_HEADS):
        q_out[h] = (q[:, h * HEAD_PAD:(h + 1) * HEAD_PAD] * qtab).astype(jnp.bfloat16)
    sbz_out[...] = _silu(_dot_nt(hn, w_ref[_R_BZ:_R_BZ + BRANCH_DIM, :])).astype(jnp.bfloat16)


def _mod_row(mod_ref, ctx, dec, i):
    row = jnp.where(i < dec.first, ctx.cond_row(i), dec.cond_row(i))
    return mod_ref[pl.ds(row, 1), :]


def _front_kernel(l, ctx, dec, n_cast, n_aliased, *refs):
    n_in = 6
    ctx_in, dec_in = refs[0:n_in], refs[n_in:2 * n_in]
    mod_ref = refs[2 * n_in]
    n_w = 2 * n_in + 1 + _N_FRONT_WEIGHTS
    weights = refs[2 * n_in + 1:n_w]
    outs = refs[n_w + n_cast + n_aliased:-2]
    ctx_out, dec_out = outs[0:7], outs[7:13]
    scratch = refs[-2:]
    i = pl.program_id(0)
    _cast_chunks(refs[n_w:n_w + n_cast], outs[13:])

    @pl.when(i < dec.first)
    def _():
        _front_tile(l, ctx.seq, ctx.tile(i), *ctx_in, _mod_row(mod_ref, ctx, dec, i), weights, ctx_out, scratch)

    @pl.when(i >= dec.first)
    def _():
        _front_tile(l, dec.seq, dec.tile(i), *dec_in, _mod_row(mod_ref, ctx, dec, i), weights, dec_out, scratch)


def _front(l, x_prompt, x_sample, mod, w, w_front, ctx_tabs, dec_tabs, state_t, big_f32, cast_windows):
    ctx, dec = _passes(x_prompt, x_sample, FRONT_TILE)
    n_steps = ctx.steps + dec.steps

    def pass_in_specs(p):
        tm = p.tm
        hb = tm // HALO
        n_hblk = p.seq // HALO
        tab = pl.BlockSpec((tm, LANES), lambda i: (p.tile(i), 0))
        return [
            pl.BlockSpec((None, tm, D_MODEL), lambda i: (p.batch(i), p.tile(i), 0)),
            pl.BlockSpec((None, HALO, D_MODEL), lambda i: (p.batch(i), jnp.maximum(p.tile(i) * hb - 1, 0), 0)),
            pl.BlockSpec((None, HALO, D_MODEL),
                         lambda i: (p.batch(i), jnp.minimum((p.tile(i) + 1) * hb, n_hblk - 1), 0)),
            tab, tab, tab,
        ]

    def pass_out(p, with_latent):
        tm = p.tm

        def blocks(n):
            return (jax.ShapeDtypeStruct((p.nb, n, p.seq, LANES), jnp.bfloat16),
                    pl.BlockSpec((None, n, tm, LANES), lambda i: (p.batch(i), 0, p.tile(i), 0)))

        (head_shape, head_spec), (pair_shape, pair_spec) = blocks(N_HEADS), blocks(N_PAIRS)
        br_shape = jax.ShapeDtypeStruct((p.nb, p.seq, BRANCH_DIM), jnp.bfloat16)
        br_spec = pl.BlockSpec((None, tm, BRANCH_DIM), lambda i: (p.batch(i), p.tile(i), 0))
        shapes = [head_shape, head_shape, pair_shape] + [br_shape] * 3
        specs = [head_spec, head_spec, pair_spec] + [br_spec] * 3
        if with_latent:
            shapes.append(jax.ShapeDtypeStruct((p.nb, DEPTH, MLA_LATENT, p.seq), jnp.float32))
            if state_t is None:
                specs.append(pl.BlockSpec((None, DEPTH, MLA_LATENT, tm), lambda i: (p.batch(i), 0, 0, p.tile(i))))
            else:
                specs.append(pl.BlockSpec((None, None, MLA_LATENT, tm), lambda i: (p.batch(i), l, 0, p.tile(i))))
        return shapes, specs

    weight_specs = [
        _whole_spec(w["g_pre"]),
        _row_block_spec(_FRONT_ROWS, 0),
        _whole_spec(w["conv_w"]),
        _whole_spec(w["conv_b"]),
        _whole_spec(w["g_q"]),
        _const_spec(l, (Q_LORA, HP)),
        _whole_spec(w["g_kv"]),
        _const_spec(l, (KV_LORA, HP)),
        _const_spec(l, (len(POOL_WINDOWS), POOL_GROUP, POOL_GROUP)),
        _whole_spec(w["pool_scale"]),
    ]
    assert len(weight_specs) == _N_FRONT_WEIGHTS
    ctx_shapes, ctx_specs = pass_out(ctx, True)
    dec_shapes, dec_specs = pass_out(dec, False)
    in_specs = pass_in_specs(ctx) + pass_in_specs(dec) + [_const_spec(l, mod.shape[1:])] + weight_specs
    args = [x_prompt, x_prompt, x_prompt, *ctx_tabs, x_sample, x_sample, x_sample, *dec_tabs, mod,
            w["g_pre"], w_front, w["conv_w"], w["conv_b"], w["g_q"], w["wuq"], w["g_kv"],
            w["w_ukv"], w["pool_w"], w["pool_scale"]]
    cast_in, cast_out, cast_shapes = _cast_specs(l, cast_windows, n_steps)
    in_specs += cast_in
    args += [big_f32[src] for src, _, _ in cast_windows]
    aliases = {}
    if state_t is not None:
        aliases = {len(args): len(ctx_shapes) - 1}
        in_specs.append(pl.BlockSpec(memory_space=pl.ANY))
        args.append(state_t)
    outs = pl.pallas_call(
        functools.partial(_front_kernel, l, ctx, dec, len(cast_windows), len(aliases)),
        grid=(n_steps,),
        in_specs=in_specs,
        out_specs=ctx_specs + dec_specs + cast_out,
        out_shape=ctx_shapes + dec_shapes + cast_shapes,
        scratch_shapes=[pltpu.VMEM((max(ctx.tm, dec.tm) + 2 * HALO, BRANCH_DIM), jnp.float32)] * 2,
        input_output_aliases=aliases,
        compiler_params=_params(),
        name="front",
    )(*args)
    return outs[:7], outs[7:13], tuple(outs[13:])


_N_BACK_WEIGHTS = 5


def _back_tile(l, x_ref, q_ref, k_ref, v_ref, cache, ya_ref, yc_ref, sbz_ref, mod_row, weights, out_ref):
    gpre_ref, wg_ref, wbr_ref, wo_ref, gpost_ref = weights
    g_pre, g_post = gpre_ref[l:l + 1, :], gpost_ref[l:l + 1, :]
    x = x_ref[...]

    def with_ones(v):
        return jnp.concatenate([v, jnp.ones_like(v)], axis=1)

    low_half = lax.broadcasted_iota(jnp.int32, (x.shape[0], LANES), 1) < V_HEAD
    pairs = []
    for p in range(N_PAIRS):
        v_ext = with_ones(v_ref[p])
        if cache is not None:
            kc_ref, vc_ref = cache
            vc_ext = with_ones(vc_ref[p])
        halves = []
        for h in (2 * p, 2 * p + 1):
            q = q_ref[h]
            s = _dot_nt(q, k_ref[h])
            m = jnp.max(s, axis=-1, keepdims=True)
            if cache is not None:
                sc = _dot_nt(q, kc_ref[h])
                m = jnp.maximum(m, jnp.max(sc, axis=-1, keepdims=True))
            o = _dot(jnp.exp2((s - m).astype(jnp.bfloat16)), v_ext)
            if cache is not None:
                o = o + _dot(jnp.exp2((sc - m).astype(jnp.bfloat16)), vc_ext)
            halves.append(o[:, 0:LANES] / o[:, LANES:2 * LANES])
        pairs.append(jnp.where(low_half, halves[0], halves[1]))
    attn = jnp.concatenate(pairs, axis=1)
    yb = (sbz_ref[...].astype(jnp.float32) * attn).astype(jnp.bfloat16)

    hn = _modulated_norm(x, g_pre, mod_row).astype(jnp.bfloat16)
    merged = None
    for n, y in enumerate((ya_ref[...], yb, yc_ref[...])):
        r0 = _R_MERGE - _BACK_START + n * D_MODEL
        gate = _sigmoid(_dot_nt(hn, wg_ref[r0:r0 + D_MODEL, :]))
        term = gate * _dot(y, wbr_ref[n * BRANCH_DIM:(n + 1) * BRANCH_DIM, :])
        merged = term if merged is None else merged + term
    out = _rms(_dot(merged.astype(jnp.bfloat16), wo_ref[...]), g_post)
    out_ref[...] = x + mod_row[:, 2 * D_MODEL:3 * D_MODEL] * out


def _back_kernel(l, ctx, dec, n_cast, *refs):
    x_c, q_c, k_c, v_c, ya_c, yc_c, sbz_c = refs[0:7]
    x_d, q_d, k_d, v_d, kc_ref, vc_ref, ya_d, yc_d, sbz_d = refs[7:16]
    mod_ref = refs[16]
    n_w = 17 + _N_BACK_WEIGHTS
    weights = refs[17:n_w]
    out_c, out_d = refs[n_w + n_cast:n_w + n_cast + 2]
    i = pl.program_id(0)
    _cast_chunks(refs[n_w:n_w + n_cast], refs[n_w + n_cast + 2:])

    @pl.when(i < dec.first)
    def _():
        _back_tile(l, x_c, q_c, k_c, v_c, None, ya_c, yc_c, sbz_c, _mod_row(mod_ref, ctx, dec, i), weights, out_c)

    @pl.when(i >= dec.first)
    def _():
        _back_tile(l, x_d, q_d, k_d, v_d, (kc_ref, vc_ref), ya_d, yc_d, sbz_d, _mod_row(mod_ref, ctx, dec, i),
                   weights, out_d)


def _back(l, x_prompt, x_sample, mod, w, big_bf, ctx_fr, dec_fr, kc, vc, big_f32, cast_windows):
    ctx, dec = _passes(x_prompt, x_sample, BACK_TILE)
    past = kc.shape[3]
    n_steps = ctx.steps + dec.steps

    def pass_specs(p):
        tq = p.tm
        tile = pl.BlockSpec((None, tq, D_MODEL), lambda i: (p.batch(i), p.tile(i), 0))
        q = pl.BlockSpec((None, N_HEADS, tq, HEAD_PAD), lambda i: (p.batch(i), 0, p.tile(i), 0))
        k = pl.BlockSpec((None, N_HEADS, p.seq, LANES), lambda i: (p.batch(i), 0, 0, 0))
        v = pl.BlockSpec((None, N_PAIRS, p.seq, LANES), lambda i: (p.batch(i), 0, 0, 0))
        br = pl.BlockSpec((None, tq, BRANCH_DIM), lambda i: (p.batch(i), p.tile(i), 0))
        return tile, q, k, v, br

    tile_c, q_c, k_c, v_c, br_c = pass_specs(ctx)
    tile_d, q_d, k_d, v_d, br_d = pass_specs(dec)
    kc_spec = pl.BlockSpec((None, None, N_HEADS, past, LANES), lambda i: (l, dec.batch(i), 0, 0, 0))
    vc_spec = pl.BlockSpec((None, None, N_PAIRS, past, LANES), lambda i: (l, dec.batch(i), 0, 0, 0))
    in_specs = ([tile_c, q_c, k_c, v_c, br_c, br_c, br_c]
                + [tile_d, q_d, k_d, v_d, kc_spec, vc_spec, br_d, br_d, br_d]
                + [_const_spec(l, mod.shape[1:]),
                   _whole_spec(w["g_pre"]),
                   _row_block_spec(_BACK_ROWS, 0),
                   _row_block_spec(3 * BRANCH_DIM, 0),
                   _row_block_spec(D_MODEL, 0),
                   _whole_spec(w["g_post"])])
    q_cx, k_cx, v_cx, ya_cx, yc_cx, sbz_cx = ctx_fr
    q_dx, k_dx, v_dx, ya_dx, yc_dx, sbz_dx = dec_fr
    args = [x_prompt, q_cx, k_cx, v_cx, ya_cx, yc_cx, sbz_cx,
            x_sample, q_dx, k_dx, v_dx, kc, vc, ya_dx, yc_dx, sbz_dx,
            mod, w["g_pre"], *big_bf, w["g_post"]]
    out_specs = [tile_c, tile_d]
    out_shape = [jax.ShapeDtypeStruct(x_prompt.shape, jnp.float32), jax.ShapeDtypeStruct(x_sample.shape, jnp.float32)]
    cast_in, cast_out, cast_shapes = _cast_specs(l + 1, cast_windows, n_steps)
    in_specs += cast_in
    args += [big_f32[src] for src, _, _ in cast_windows]
    out_specs += cast_out
    out_shape += cast_shapes
    outs = pl.pallas_call(
        functools.partial(_back_kernel, l, ctx, dec, len(cast_windows)),
        grid=(n_steps,),
        in_specs=in_specs,
        out_specs=out_specs,
        out_shape=out_shape,
        compiler_params=_params(),
        name="back",
    )(*args)
    return outs[0], outs[1], tuple(outs[2:])


def _rope_swap_perm():
    quarter = QK_ROPE // 4
    idx = np.arange(QK_ROPE).reshape(2, 2, quarter)
    return idx[:, ::-1, :].reshape(-1)


def _rope_tables(n_tokens, rotate):
    if rotate:
        t = np.arange(n_tokens)
        pos = np.stack([t // GRID_W, t % GRID_W], axis=1).astype(np.float64)
        axis_dim = QK_ROPE // 2
        inv = 1.0 / (ROPE_THETA ** (np.arange(0, axis_dim, 2, dtype=np.float64) / axis_dim))
        ang = pos[:, :, None] * inv
        cos, sin = np.cos(ang), np.sin(ang)
        cfull = np.stack([cos, cos], axis=2).reshape(n_tokens, QK_ROPE)
        sfull = np.stack([-sin, sin], axis=2).reshape(n_tokens, QK_ROPE)
    else:
        cfull = np.ones((n_tokens, QK_ROPE))
        sfull = np.zeros((n_tokens, QK_ROPE))
    scale = (QK_NOPE + QK_ROPE) ** -0.5 * np.log2(np.e)
    qtab = np.concatenate([np.full((n_tokens, QK_NOPE), scale), cfull * scale, sfull * scale], axis=1)
    pad = np.zeros((n_tokens, LANES - QK_ROPE))
    tabs = (qtab, np.concatenate([cfull, pad], axis=1), np.concatenate([sfull, pad], axis=1))
    return tuple(jnp.asarray(tab, jnp.float32) for tab in tabs)


def _prepare_weights(g_pre, g_post, conv_w, conv_b, g_q, w_uq, g_kv, w_ukv, pool_w, pool_scale):
    bf = jnp.bfloat16
    perm = _rope_swap_perm()

    uq = w_uq.reshape(DEPTH, Q_LORA, N_HEADS, QK_NOPE + QK_ROPE)
    rope_q = uq[..., QK_NOPE:]
    wuq = jnp.concatenate([uq, rope_q[..., perm]], axis=-1).reshape(DEPTH, Q_LORA, HP).astype(bf)
    assert w_ukv.shape == (DEPTH, KV_LORA, HP) and QK_NOPE == V_HEAD and QK_NOPE + V_HEAD == LANES

    return {
        "g_pre": g_pre, "g_post": g_post, "conv_w": jnp.swapaxes(conv_w, 0, 1), "conv_b": conv_b,
        "g_q": g_q, "g_kv": g_kv, "pool_scale": pool_scale,
        "wuq": wuq, "w_ukv": w_ukv, "pool_w": pool_w,
    }


def kernel(x_prompt, x_sample, cache_mla_latent, c, c_ctx, w_mod, b_mod, g_pre, g_post, w_in, conv_w, conv_b,
           g_q, w_uq, g_kv, w_ukv, pool_w, pool_scale, w_branch, w_o):
    n_dec = x_sample.shape[0]
    assert 1 + n_dec <= COND_ROWS
    assert w_in.shape == (DEPTH, D_MODEL, W_IN_COLS)
    w = _prepare_weights(g_pre, g_post, conv_w, conv_b, g_q, w_uq, g_kv, w_ukv, pool_w, pool_scale)
    big_f32 = (jnp.swapaxes(w_in, 1, 2), w_branch.reshape(DEPTH, 3 * BRANCH_DIM, D_MODEL), w_o)
    cond = jnp.concatenate([c_ctx[None, :], c, jnp.zeros((COND_ROWS - 1 - n_dec, D_MODEL), jnp.float32)], axis=0)
    cache_t = jnp.swapaxes(cache_mla_latent, 2, 3)
    mod, w_front, kc, vc = _prologue(cond, w_mod, b_mod, big_f32, cache_t, w_ukv)
    ctx_tabs = _rope_tables(x_prompt.shape[1], rotate=False)
    dec_tabs = _rope_tables(x_sample.shape[1], rotate=True)

    back_windows = (_W_BACK, _W_BRANCH, _W_OUT)
    h, hs, state_t, big_bf = x_prompt, x_sample, None, None
    for l in range(DEPTH):
        ctx_fr, dec_fr, cast = _front(l, h, hs, mod, w, w_front, ctx_tabs, dec_tabs, state_t, big_f32,
                                      back_windows if big_bf is None else ())
        state_t = ctx_fr[6]
        big_bf = cast if big_bf is None else big_bf
        next_windows = (_W_FRONT,) + back_windows if l + 1 < DEPTH else ()
        h, hs, cast = _back(l, h, hs, mod, w, big_bf, ctx_fr[:6], dec_fr, kc, vc, big_f32, next_windows)
        if next_windows:
            w_front, big_bf = cast[0], cast[1:]
    return (h, hs, jnp.swapaxes(state_t, 2, 3))
```

```python
import functools

import numpy as np
import jax
import jax.numpy as jnp
from jax import lax
from jax.experimental import pallas as pl
from jax.experimental.pallas import tpu as pltpu

D_MODEL = 1024
DEPTH = 2
GRID_W = 64
EPS = 1e-6
BRANCH_DIM = 512
N_HEADS = 8
QK_NOPE = 64
QK_ROPE = 32
V_HEAD = 64
Q_LORA = 384
KV_LORA = 256
MLA_LATENT = KV_LORA + QK_ROPE
ROPE_THETA = 10000.0
POOL_WINDOWS = (2, 4, 8, 16)
POOL_GROUP = 128

LANES = 128
HEAD_PAD = LANES
HP = N_HEADS * HEAD_PAD
N_PAIRS = N_HEADS * V_HEAD // LANES
FRONT_TILE = 512
BACK_TILE = 256
HALO = 16
COND_ROWS = 8
VMEM_LIMIT = 56 * 1024 * 1024

_R_AB = 0
_R_ACX = BRANCH_DIM
_R_MAIN = 3 * BRANCH_DIM
_MAIN_ROWS = BRANCH_DIM + Q_LORA + KV_LORA + LANES
_R_BZ = 4 * BRANCH_DIM + Q_LORA + MLA_LATENT
_R_CU = _R_BZ + BRANCH_DIM
_R_CZ = _R_CU + BRANCH_DIM
_R_MERGE = _R_CZ + BRANCH_DIM
W_IN_COLS = _R_MERGE + 3 * D_MODEL
_FRONT_ROWS = -(-_R_MERGE // LANES) * LANES
_QD0 = BRANCH_DIM
_CKV0 = _QD0 + Q_LORA
_KR0 = _CKV0 + KV_LORA
_LAT_EXT = KV_LORA + LANES


def _dot(a, b):
    return jnp.dot(a, b, preferred_element_type=jnp.float32)


def _dot_nt(a, b):
    return lax.dot_general(a, b, (((1,), (1,)), ((), ())), preferred_element_type=jnp.float32)


def _rms(x, g):
    return x * lax.rsqrt(jnp.mean(x * x, axis=-1, keepdims=True) + EPS) * g


def _sigmoid(x):
    return 0.5 * jnp.tanh(0.5 * x) + 0.5


def _silu(x):
    return x * _sigmoid(x)


def _modulated_norm(x, g_pre, mod_row):
    shift = mod_row[:, 0:D_MODEL]
    scale = mod_row[:, D_MODEL:2 * D_MODEL]
    return _rms(x, g_pre) * (1.0 + scale) + shift


def _params(n_grid_dims=1):
    return pltpu.CompilerParams(dimension_semantics=("arbitrary",) * n_grid_dims, vmem_limit_bytes=VMEM_LIMIT)


_SRC_W_IN_T, _SRC_W_BRANCH, _SRC_W_O = 0, 1, 2
_BACK_START = (_R_MERGE // 1024) * 1024
_W_FRONT = (_SRC_W_IN_T, 0, _FRONT_ROWS)
_W_BACK = (_SRC_W_IN_T, _BACK_START, W_IN_COLS - _BACK_START)
_W_BRANCH = (_SRC_W_BRANCH, 0, 3 * BRANCH_DIM)
_W_OUT = (_SRC_W_O, 0, D_MODEL)
_BACK_ROWS = _W_BACK[2]


def _chunk_rows(first_row, rows, n_steps):
    if first_row == 0:
        return pl.cdiv(pl.cdiv(rows, n_steps), 16) * 16
    per = LANES
    while first_row % per or pl.cdiv(rows, per) > n_steps:
        per *= 2
        assert per <= first_row
    return per


def _cast_specs(layer, windows, n_steps):
    in_specs, out_specs, out_shapes = [], [], []
    for _, first_row, rows in windows:
        per = _chunk_rows(first_row, rows, n_steps)
        n_blocks = pl.cdiv(rows, per)
        first_blk = first_row // per

        def in_map(i, first_blk=first_blk, last=n_blocks - 1):
            return (layer, first_blk + jnp.minimum(i, last), 0)

        def out_map(i, last=n_blocks - 1):
            return (jnp.minimum(i, last), 0)

        in_specs.append(pl.BlockSpec((None, per, D_MODEL), in_map))
        out_specs.append(pl.BlockSpec((per, D_MODEL), out_map))
        out_shapes.append(jax.ShapeDtypeStruct((rows, D_MODEL), jnp.bfloat16))
    return in_specs, out_specs, out_shapes


def _cast_chunks(in_refs, out_refs):
    for src, dst in zip(in_refs, out_refs):
        dst[...] = src[...].astype(jnp.bfloat16)


def _store_keys_values(k_refs, v_refs, ckv_bf, wukv_ref, kr):
    kv = _dot(ckv_bf, wukv_ref[...].astype(jnp.bfloat16))
    k_rope = pltpu.roll(kr, QK_NOPE, 1) + pltpu.roll(kr, QK_NOPE + QK_ROPE, 1)
    low_half = lax.broadcasted_iota(jnp.int32, kr.shape, 1) < QK_NOPE
    blocks = [kv[:, h * LANES:(h + 1) * LANES] for h in range(N_HEADS)]
    per_seq = kr.shape[0] // len(k_refs)
    for h in range(N_HEADS):
        key = jnp.where(low_half, blocks[h], k_rope).astype(jnp.bfloat16)
        for s, k_ref in enumerate(k_refs):
            k_ref[h] = key[s * per_seq:(s + 1) * per_seq]
    for p in range(N_PAIRS):
        even_v = pltpu.roll(blocks[2 * p], LANES - V_HEAD, 1)
        value = jnp.where(low_half, even_v, blocks[2 * p + 1]).astype(jnp.bfloat16)
        for s, v_ref in enumerate(v_refs):
            v_ref[p] = value[s * per_seq:(s + 1) * per_seq]


def _cache_kv_kernel(lat_t_ref, wukv_ref, k_ref, v_ref):
    lat_t = lat_t_ref[...]
    pad = jnp.zeros((_LAT_EXT - MLA_LATENT, lat_t.shape[1]), jnp.float32)
    lat = jnp.concatenate([lat_t, pad], axis=0).T
    _store_keys_values([k_ref], [v_ref], lat[:, 0:KV_LORA].astype(jnp.bfloat16), wukv_ref, lat[:, KV_LORA:_LAT_EXT])


_MOD_COL_TILES = 2


def _prologue_kernel(n_cache_jobs, cond_ref, wmod_ref, bmod_ref, wsrc_ref, lat_t_ref, wukv_ref,
                     mod_ref, wfront_ref, kc_ref, vc_ref):
    i = pl.program_id(0)
    h = _silu(cond_ref[...]).astype(jnp.bfloat16)
    mod_ref[...] = _dot(h, wmod_ref[...].astype(jnp.bfloat16)) + bmod_ref[pl.ds(i // _MOD_COL_TILES, 1), :]
    _cast_chunks([wsrc_ref], [wfront_ref])

    @pl.when(i < n_cache_jobs)
    def _():
        _cache_kv_kernel(lat_t_ref, wukv_ref, kc_ref, vc_ref)


def _prologue(cond, w_mod, b_mod, big_f32, cache_t, w_ukv):
    n_steps = DEPTH * _MOD_COL_TILES
    col_tile = 3 * D_MODEL // _MOD_COL_TILES
    nb, _, _, past = cache_t.shape
    n_cache_jobs = DEPTH * nb
    assert n_cache_jobs <= n_steps

    def mod_block(i):
        return (i // _MOD_COL_TILES, 0, i % _MOD_COL_TILES)

    def cache_layer(i):
        return jnp.minimum(i, n_cache_jobs - 1) // nb

    def cache_request(i):
        return jnp.minimum(i, n_cache_jobs - 1) % nb

    def kv(n):
        return (jax.ShapeDtypeStruct((DEPTH, nb, n, past, LANES), jnp.bfloat16),
                pl.BlockSpec((None, None, n, past, LANES), lambda i: (cache_layer(i), cache_request(i), 0, 0, 0)))

    (k_shape, k_spec), (v_shape, v_spec) = kv(N_HEADS), kv(N_PAIRS)
    cast_in, cast_out, cast_shapes = _cast_specs(0, [_W_FRONT], n_steps)
    return pl.pallas_call(
        functools.partial(_prologue_kernel, n_cache_jobs),
        grid=(n_steps,),
        in_specs=[
            pl.BlockSpec((COND_ROWS, D_MODEL), lambda i: (0, 0)),
            pl.BlockSpec((None, D_MODEL, col_tile), mod_block),
            pl.BlockSpec((DEPTH, col_tile), lambda i: (0, i % _MOD_COL_TILES)),
            cast_in[0],
            pl.BlockSpec((None, None, MLA_LATENT, past), lambda i: (cache_request(i), cache_layer(i), 0, 0)),
            pl.BlockSpec((None, KV_LORA, HP), lambda i: (cache_layer(i), 0, 0)),
        ],
        out_specs=[pl.BlockSpec((None, COND_ROWS, col_tile), mod_block), cast_out[0], k_spec, v_spec],
        out_shape=[jax.ShapeDtypeStruct((DEPTH, COND_ROWS, 3 * D_MODEL), jnp.float32), cast_shapes[0],
                   k_shape, v_shape],
        compiler_params=_params(),
        name="prologue",
    )(cond, w_mod, b_mod, big_f32[_SRC_W_IN_T], cache_t, w_ukv)


class _Pass:
    def __init__(self, x, tile, first_step, first_cond_row, shared_cond, group=1):
        self.nb, self.seq, _ = x.shape
        self.tm = min(tile, self.seq)
        assert self.seq % self.tm == 0
        self.nt = self.seq // self.tm
        self.group = group if self.nt == 1 and shared_cond else 1
        assert self.nb % self.group == 0
        self.steps = self.nb // self.group * self.nt
        self.first = first_step
        self.first_cond_row = first_cond_row
        self.shared_cond = shared_cond

    def local(self, i):
        return jnp.clip(i - self.first, 0, self.steps - 1)

    def batch(self, i):
        return self.local(i) // self.nt

    def tile(self, i):
        return self.local(i) % self.nt

    def cond_row(self, i):
        return self.first_cond_row if self.shared_cond else self.first_cond_row + self.batch(i)


def _passes(x_prompt, x_sample, tile, ctx_group=1):
    ctx = _Pass(x_prompt, tile, 0, 0, True, ctx_group)
    dec = _Pass(x_sample, tile, ctx.steps, 1, False)
    return ctx, dec


def _const_spec(l, shape):
    return pl.BlockSpec((None,) + shape, lambda i: (l,) + (0,) * len(shape), pipeline_mode=pl.Buffered(1))


def _whole_spec(arr):
    return pl.BlockSpec(arr.shape, lambda i: (0,) * arr.ndim, pipeline_mode=pl.Buffered(1))


def _row_block_spec(rows, block_index):
    return pl.BlockSpec((rows, D_MODEL), lambda i: (block_index, 0), pipeline_mode=pl.Buffered(1))


_N_FRONT_WEIGHTS = 10


def _front_tile(l, seq_len, j, x_ref, xp_ref, xn_ref, qtab_ref, kcos_ref, ksin_ref, mod_row, weights, outs, scratch):
    gpre_ref, w_ref, convw_ref, convb_ref, gq_ref, wuq_ref, gkv_ref, wukv_ref, poolw_ref, pscale_ref = weights
    g_pre, conv_b, g_q, g_kv, p_scale = (r[l:l + 1, :] for r in (gpre_ref, convb_ref, gq_ref, gkv_ref, pscale_ref))
    conv_w = [convw_ref[tap, l:l + 1, :] for tap in range(3)]
    q_out, k_out, v_out, ya_out, yc_out, sbz_out = outs[:6]
    u_scr, cu_scr = scratch
    n_seq, tm = x_ref.shape[0], x_ref.shape[1]
    last_j = seq_len // tm - 1
    stride = tm + HALO

    def seq_rows(s):
        return slice(s * tm, (s + 1) * tm)

    if last_j == 0:
        hn = _modulated_norm(x_ref[...].reshape(n_seq * tm, D_MODEL), g_pre, mod_row).astype(jnp.bfloat16)
        acx = _dot_nt(hn, w_ref[_R_ACX:_R_ACX + 2 * BRANCH_DIM, :])
        u_all = acx[:, 0:BRANCH_DIM] * acx[:, BRANCH_DIM:2 * BRANCH_DIM]
        cu_all = _dot_nt(hn, w_ref[_R_CU:_R_CU + BRANCH_DIM, :])
        zeros = jnp.zeros((HALO, BRANCH_DIM), jnp.float32)
        for scr, val in ((u_scr, u_all), (cu_scr, cu_all)):
            for s in range(n_seq + 1):
                scr[s * stride:s * stride + HALO, :] = zeros
            for s in range(n_seq):
                scr[s * stride + HALO:s * stride + HALO + tm, :] = val[seq_rows(s)]
    else:
        assert n_seq == 1
        x_ext = jnp.concatenate([xp_ref[0], x_ref[0], xn_ref[0]], axis=0)
        hn_ext = _modulated_norm(x_ext, g_pre, mod_row).astype(jnp.bfloat16)
        hn = hn_ext[HALO:HALO + tm]
        acx = _dot_nt(hn_ext, w_ref[_R_ACX:_R_ACX + 2 * BRANCH_DIM, :])
        u_ext = acx[:, 0:BRANCH_DIM] * acx[:, BRANCH_DIM:2 * BRANCH_DIM]
        cu_ext = _dot_nt(hn_ext, w_ref[_R_CU:_R_CU + BRANCH_DIM, :])
        head, body, tail = slice(0, HALO), slice(HALO, HALO + tm), slice(HALO + tm, 2 * HALO + tm)
        for scr, ext in ((u_scr, u_ext), (cu_scr, cu_ext)):
            scr[head, :] = jnp.where(j > 0, ext[head], 0.0)
            scr[body, :] = ext[body]
            scr[tail, :] = jnp.where(j < last_j, ext[tail], 0.0)

    pm = _dot_nt(hn, w_ref[_R_MAIN:_R_MAIN + _MAIN_ROWS, :])
    def conv_of(base):
        return (u_scr[base + HALO - 1:base + HALO - 1 + tm, :] * conv_w[0]
                + u_scr[base + HALO:base + HALO + tm, :] * conv_w[1]
                + u_scr[base + HALO + 1:base + HALO + 1 + tm, :] * conv_w[2]
                + conv_b)

    conv = jnp.concatenate([conv_of(s * stride) for s in range(n_seq)], axis=0)
    ya = _silu(pm[:, 0:BRANCH_DIM]) * (_dot_nt(hn, w_ref[_R_AB:_R_AB + BRANCH_DIM, :]) * conv)
    for s in range(n_seq):
        ya_out[s] = ya[seq_rows(s)].astype(jnp.bfloat16)

    t = j * tm + lax.broadcasted_iota(jnp.int32, (tm, POOL_GROUP), 0)

    def pooled_of(base, gi, win):
        half = win // 2
        cols = slice(gi * POOL_GROUP, (gi + 1) * POOL_GROUP)
        shifts = [1 << b for b in range(win.bit_length() - 1)]
        first = base + HALO - half
        total = cu_scr[first:first + tm + sum(shifts), cols]
        for shift in shifts:
            total = total[:-shift] + total[shift:]
        edge = 8
        assert half <= edge

        def clipped_mean(rows):
            count = jnp.minimum(t[rows] + half, seq_len) - jnp.maximum(t[rows] - half, 0)
            return total[rows] / count.astype(jnp.float32)

        mean = jnp.concatenate([clipped_mean(slice(0, edge)), total[edge:tm - edge] * (1.0 / win),
                                clipped_mean(slice(tm - edge, tm))], axis=0)
        return (mean - cu_scr[base + HALO:base + HALO + tm, cols]).astype(jnp.bfloat16)

    pooled = [jnp.concatenate([pooled_of(s * stride, gi, win) for s in range(n_seq)], axis=0)
              for gi, win in enumerate(POOL_WINDOWS)]
    zero = jnp.zeros((POOL_GROUP, POOL_GROUP), jnp.float32)
    mixed = []
    for n in range(len(pooled) // 2):
        both = jnp.concatenate([jnp.concatenate([poolw_ref[2 * n], zero], axis=1),
                                jnp.concatenate([zero, poolw_ref[2 * n + 1]], axis=1)], axis=0).astype(jnp.bfloat16)
        mixed.append(_dot(jnp.concatenate(pooled[2 * n:2 * n + 2], axis=1), both))
    mixed = jnp.concatenate(mixed, axis=1) * p_scale
    c_z = _dot_nt(hn, w_ref[_R_CZ:_R_CZ + BRANCH_DIM, :])
    yc = (_silu(c_z) * mixed).astype(jnp.bfloat16)
    for s in range(n_seq):
        yc_out[s] = yc[seq_rows(s)]

    qn = _rms(pm[:, _QD0:_QD0 + Q_LORA], g_q).astype(jnp.bfloat16)
    q = _dot(qn, wuq_ref[...])
    qtab = qtab_ref[...]
    ckv = _rms(pm[:, _CKV0:_CKV0 + KV_LORA], g_kv)
    kr = pm[:, _KR0:_KR0 + LANES]
    if len(outs) > 6:
        for s in range(n_seq):
            lat_out = outs[6].at[s]
            if len(lat_out.shape) == 3:
                for later in range(1, lat_out.shape[0]):
                    lat_out[later] = jnp.zeros(lat_out.shape[1:], jnp.float32)
                lat_out = lat_out.at[0]
            lat_out[0:KV_LORA, :] = ckv[seq_rows(s)].T
            lat_out[KV_LORA:MLA_LATENT, :] = kr[seq_rows(s)].T[0:QK_ROPE, :]
    lane = lax.broadcasted_iota(jnp.int32, kr.shape, 1)
    quarter = QK_ROPE // 4
    partner = jnp.where(lane % (2 * quarter) < quarter,
                        pltpu.roll(kr, LANES - quarter, 1), pltpu.roll(kr, quarter, 1))
    kcos, ksin = (jnp.concatenate([tab[...]] * n_seq, axis=0) for tab in (kcos_ref, ksin_ref))
    kr_rot = jnp.where(lane < QK_ROPE, kr * kcos + partner * ksin, 0.0)
    _store_keys_values([k_out.at[s] for s in range(n_seq)], [v_out.at[s] for s in range(n_seq)],
                       ckv.astype(jnp.bfloat16), wukv_ref, kr_rot)
    sbz = _silu(_dot_nt(hn, w_ref[_R_BZ:_R_BZ + BRANCH_DIM, :])).astype(jnp.bfloat16)
    for s in range(n_seq):
        for h in range(N_HEADS):
            q_out[s, h] = (q[seq_rows(s), h * HEAD_PAD:(h + 1) * HEAD_PAD] * qtab).astype(jnp.bfloat16)
        sbz_out[s] = sbz[seq_rows(s)]


def _mod_row(mod_ref, ctx, dec, i):
    row = jnp.where(i < dec.first, ctx.cond_row(i), dec.cond_row(i))
    return mod_ref[pl.ds(row, 1), :]


def _front_kernel(l, ctx, dec, n_cast, n_aliased, *refs):
    n_in = 6
    ctx_in, dec_in = refs[0:n_in], refs[n_in:2 * n_in]
    mod_ref = refs[2 * n_in]
    n_w = 2 * n_in + 1 + _N_FRONT_WEIGHTS
    weights = refs[2 * n_in + 1:n_w]
    outs = refs[n_w + n_cast + n_aliased:-2]
    ctx_out, dec_out = outs[0:7], outs[7:13]
    scratch = refs[-2:]
    i = pl.program_id(0)
    _cast_chunks(refs[n_w:n_w + n_cast], outs[13:])

    @pl.when(i < dec.first)
    def _():
        _front_tile(l, ctx.seq, ctx.tile(i), *ctx_in, _mod_row(mod_ref, ctx, dec, i), weights, ctx_out, scratch)

    @pl.when(i >= dec.first)
    def _():
        _front_tile(l, dec.seq, dec.tile(i), *dec_in, _mod_row(mod_ref, ctx, dec, i), weights, dec_out, scratch)


def _front(l, x_prompt, x_sample, mod, w, w_front, ctx_tabs, dec_tabs, state_t, big_f32, cast_windows):
    ctx, dec = _passes(x_prompt, x_sample, FRONT_TILE, max(1, FRONT_TILE // x_prompt.shape[1]))
    n_steps = ctx.steps + dec.steps

    def pass_in_specs(p):
        tm = p.tm
        hb = tm // HALO
        n_hblk = p.seq // HALO
        tab = pl.BlockSpec((tm, LANES), lambda i: (p.tile(i), 0))
        return [
            pl.BlockSpec((p.group, tm, D_MODEL), lambda i: (p.batch(i), p.tile(i), 0)),
            pl.BlockSpec((p.group, HALO, D_MODEL), lambda i: (p.batch(i), jnp.maximum(p.tile(i) * hb - 1, 0), 0)),
            pl.BlockSpec((p.group, HALO, D_MODEL),
                         lambda i: (p.batch(i), jnp.minimum((p.tile(i) + 1) * hb, n_hblk - 1), 0)),
            tab, tab, tab,
        ]

    def pass_out(p, with_latent):
        tm = p.tm

        def blocks(n):
            return (jax.ShapeDtypeStruct((p.nb, n, p.seq, LANES), jnp.bfloat16),
                    pl.BlockSpec((p.group, n, tm, LANES), lambda i: (p.batch(i), 0, p.tile(i), 0)))

        (head_shape, head_spec), (pair_shape, pair_spec) = blocks(N_HEADS), blocks(N_PAIRS)
        br_shape = jax.ShapeDtypeStruct((p.nb, p.seq, BRANCH_DIM), jnp.bfloat16)
        br_spec = pl.BlockSpec((p.group, tm, BRANCH_DIM), lambda i: (p.batch(i), p.tile(i), 0))
        shapes = [head_shape, head_shape, pair_shape] + [br_shape] * 3
        specs = [head_spec, head_spec, pair_spec] + [br_spec] * 3
        if with_latent:
            shapes.append(jax.ShapeDtypeStruct((p.nb, DEPTH, MLA_LATENT, p.seq), jnp.float32))
            if state_t is None:
                specs.append(pl.BlockSpec((p.group, DEPTH, MLA_LATENT, tm), lambda i: (p.batch(i), 0, 0, p.tile(i))))
            else:
                specs.append(pl.BlockSpec((p.group, None, MLA_LATENT, tm), lambda i: (p.batch(i), l, 0, p.tile(i))))
        return shapes, specs

    weight_specs = [
        _whole_spec(w["g_pre"]),
        _row_block_spec(_FRONT_ROWS, 0),
        _whole_spec(w["conv_w"]),
        _whole_spec(w["conv_b"]),
        _whole_spec(w["g_q"]),
        _const_spec(l, (Q_LORA, HP)),
        _whole_spec(w["g_kv"]),
        _const_spec(l, (KV_LORA, HP)),
        _const_spec(l, (len(POOL_WINDOWS), POOL_GROUP, POOL_GROUP)),
        _whole_spec(w["pool_scale"]),
    ]
    assert len(weight_specs) == _N_FRONT_WEIGHTS
    ctx_shapes, ctx_specs = pass_out(ctx, True)
    dec_shapes, dec_specs = pass_out(dec, False)
    in_specs = pass_in_specs(ctx) + pass_in_specs(dec) + [_const_spec(l, mod.shape[1:])] + weight_specs
    args = [x_prompt, x_prompt, x_prompt, *ctx_tabs, x_sample, x_sample, x_sample, *dec_tabs, mod,
            w["g_pre"], w_front, w["conv_w"], w["conv_b"], w["g_q"], w["wuq"], w["g_kv"],
            w["w_ukv"], w["pool_w"], w["pool_scale"]]
    cast_in, cast_out, cast_shapes = _cast_specs(l, cast_windows, n_steps)
    in_specs += cast_in
    args += [big_f32[src] for src, _, _ in cast_windows]
    aliases = {}
    if state_t is not None:
        aliases = {len(args): len(ctx_shapes) - 1}
        in_specs.append(pl.BlockSpec(memory_space=pl.ANY))
        args.append(state_t)
    outs = pl.pallas_call(
        functools.partial(_front_kernel, l, ctx, dec, len(cast_windows), len(aliases)),
        grid=(n_steps,),
        in_specs=in_specs,
        out_specs=ctx_specs + dec_specs + cast_out,
        out_shape=ctx_shapes + dec_shapes + cast_shapes,
        scratch_shapes=[pltpu.VMEM((max(p.group * (p.tm + HALO) + HALO for p in (ctx, dec)), BRANCH_DIM),
                                   jnp.float32)] * 2,
        input_output_aliases=aliases,
        compiler_params=_params(),
        name="front",
    )(*args)
    return outs[:7], outs[7:13], tuple(outs[13:])


_N_BACK_WEIGHTS = 5


def _back_tile(l, x_ref, q_ref, k_ref, v_ref, cache, ya_ref, yc_ref, sbz_ref, mod_row, weights, out_ref):
    gpre_ref, wg_ref, wbr_ref, wo_ref, gpost_ref = weights
    g_pre, g_post = gpre_ref[l:l + 1, :], gpost_ref[l:l + 1, :]
    x = x_ref[...]

    def with_ones(v):
        return jnp.concatenate([v, jnp.ones_like(v)], axis=1)

    low_half = lax.broadcasted_iota(jnp.int32, (x.shape[0], LANES), 1) < V_HEAD
    pairs = []
    for p in range(N_PAIRS):
        v_ext = with_ones(v_ref[p])
        if cache is not None:
            kc_ref, vc_ref = cache
            vc_ext = with_ones(vc_ref[p])
        halves = []
        for h in (2 * p, 2 * p + 1):
            q = q_ref[h]
            s = _dot_nt(q, k_ref[h])
            m = jnp.max(s, axis=-1, keepdims=True)
            if cache is not None:
                sc = _dot_nt(q, kc_ref[h])
                m = jnp.maximum(m, jnp.max(sc, axis=-1, keepdims=True))
            o = _dot(jnp.exp2((s - m).astype(jnp.bfloat16)), v_ext)
            if cache is not None:
                o = o + _dot(jnp.exp2((sc - m).astype(jnp.bfloat16)), vc_ext)
            halves.append(o[:, 0:LANES] / o[:, LANES:2 * LANES])
        pairs.append(jnp.where(low_half, halves[0], halves[1]))
    attn = jnp.concatenate(pairs, axis=1)
    yb = (sbz_ref[...].astype(jnp.float32) * attn).astype(jnp.bfloat16)

    hn = _modulated_norm(x, g_pre, mod_row).astype(jnp.bfloat16)
    merged = None
    for n, y in enumerate((ya_ref[...], yb, yc_ref[...])):
        r0 = _R_MERGE - _BACK_START + n * D_MODEL
        gate = _sigmoid(_dot_nt(hn, wg_ref[r0:r0 + D_MODEL, :]))
        term = gate * _dot(y, wbr_ref[n * BRANCH_DIM:(n + 1) * BRANCH_DIM, :])
        merged = term if merged is None else merged + term
    out = _rms(_dot(merged.astype(jnp.bfloat16), wo_ref[...]), g_post)
    out_ref[...] = x + mod_row[:, 2 * D_MODEL:3 * D_MODEL] * out


def _back_kernel(l, ctx, dec, n_cast, *refs):
    x_c, q_c, k_c, v_c, ya_c, yc_c, sbz_c = refs[0:7]
    x_d, q_d, k_d, v_d, kc_ref, vc_ref, ya_d, yc_d, sbz_d = refs[7:16]
    mod_ref = refs[16]
    n_w = 17 + _N_BACK_WEIGHTS
    weights = refs[17:n_w]
    out_c, out_d = refs[n_w + n_cast:n_w + n_cast + 2]
    i = pl.program_id(0)
    _cast_chunks(refs[n_w:n_w + n_cast], refs[n_w + n_cast + 2:])

    @pl.when(i < dec.first)
    def _():
        _back_tile(l, x_c, q_c, k_c, v_c, None, ya_c, yc_c, sbz_c, _mod_row(mod_ref, ctx, dec, i), weights, out_c)

    @pl.when(i >= dec.first)
    def _():
        _back_tile(l, x_d, q_d, k_d, v_d, (kc_ref, vc_ref), ya_d, yc_d, sbz_d, _mod_row(mod_ref, ctx, dec, i),
                   weights, out_d)


def _back(l, x_prompt, x_sample, mod, w, big_bf, ctx_fr, dec_fr, kc, vc, big_f32, cast_windows):
    ctx, dec = _passes(x_prompt, x_sample, BACK_TILE)
    past = kc.shape[3]
    n_steps = ctx.steps + dec.steps

    def pass_specs(p):
        tq = p.tm
        tile = pl.BlockSpec((None, tq, D_MODEL), lambda i: (p.batch(i), p.tile(i), 0))
        q = pl.BlockSpec((None, N_HEADS, tq, HEAD_PAD), lambda i: (p.batch(i), 0, p.tile(i), 0))
        k = pl.BlockSpec((None, N_HEADS, p.seq, LANES), lambda i: (p.batch(i), 0, 0, 0))
        v = pl.BlockSpec((None, N_PAIRS, p.seq, LANES), lambda i: (p.batch(i), 0, 0, 0))
        br = pl.BlockSpec((None, tq, BRANCH_DIM), lambda i: (p.batch(i), p.tile(i), 0))
        return tile, q, k, v, br

    tile_c, q_c, k_c, v_c, br_c = pass_specs(ctx)
    tile_d, q_d, k_d, v_d, br_d = pass_specs(dec)
    kc_spec = pl.BlockSpec((None, None, N_HEADS, past, LANES), lambda i: (l, dec.batch(i), 0, 0, 0))
    vc_spec = pl.BlockSpec((None, None, N_PAIRS, past, LANES), lambda i: (l, dec.batch(i), 0, 0, 0))
    in_specs = ([tile_c, q_c, k_c, v_c, br_c, br_c, br_c]
                + [tile_d, q_d, k_d, v_d, kc_spec, vc_spec, br_d, br_d, br_d]
                + [_const_spec(l, mod.shape[1:]),
                   _whole_spec(w["g_pre"]),
                   _row_block_spec(_BACK_ROWS, 0),
                   _row_block_spec(3 * BRANCH_DIM, 0),
                   _row_block_spec(D_MODEL, 0),
                   _whole_spec(w["g_post"])])
    q_cx, k_cx, v_cx, ya_cx, yc_cx, sbz_cx = ctx_fr
    q_dx, k_dx, v_dx, ya_dx, yc_dx, sbz_dx = dec_fr
    args = [x_prompt, q_cx, k_cx, v_cx, ya_cx, yc_cx, sbz_cx,
            x_sample, q_dx, k_dx, v_dx, kc, vc, ya_dx, yc_dx, sbz_dx,
            mod, w["g_pre"], *big_bf, w["g_post"]]
    out_specs = [tile_c, tile_d]
    out_shape = [jax.ShapeDtypeStruct(x_prompt.shape, jnp.float32), jax.ShapeDtypeStruct(x_sample.shape, jnp.float32)]
    cast_in, cast_out, cast_shapes = _cast_specs(l + 1, cast_windows, n_steps)
    in_specs += cast_in
    args += [big_f32[src] for src, _, _ in cast_windows]
    out_specs += cast_out
    out_shape += cast_shapes
    outs = pl.pallas_call(
        functools.partial(_back_kernel, l, ctx, dec, len(cast_windows)),
        grid=(n_steps,),
        in_specs=in_specs,
        out_specs=out_specs,
        out_shape=out_shape,
        compiler_params=_params(),
        name="back",
    )(*args)
    return outs[0], outs[1], tuple(outs[2:])


def _rope_swap_perm():
    quarter = QK_ROPE // 4
    idx = np.arange(QK_ROPE).reshape(2, 2, quarter)
    return idx[:, ::-1, :].reshape(-1)


def _rope_tables(n_tokens, rotate):
    if rotate:
        t = np.arange(n_tokens)
        pos = np.stack([t // GRID_W, t % GRID_W], axis=1).astype(np.float64)
        axis_dim = QK_ROPE // 2
        inv = 1.0 / (ROPE_THETA ** (np.arange(0, axis_dim, 2, dtype=np.float64) / axis_dim))
        ang = pos[:, :, None] * inv
        cos, sin = np.cos(ang), np.sin(ang)
        cfull = np.stack([cos, cos], axis=2).reshape(n_tokens, QK_ROPE)
        sfull = np.stack([-sin, sin], axis=2).reshape(n_tokens, QK_ROPE)
    else:
        cfull = np.ones((n_tokens, QK_ROPE))
        sfull = np.zeros((n_tokens, QK_ROPE))
    scale = (QK_NOPE + QK_ROPE) ** -0.5 * np.log2(np.e)
    qtab = np.concatenate([np.full((n_tokens, QK_NOPE), scale), cfull * scale, sfull * scale], axis=1)
    pad = np.zeros((n_tokens, LANES - QK_ROPE))
    tabs = (qtab, np.concatenate([cfull, pad], axis=1), np.concatenate([sfull, pad], axis=1))
    return tuple(jnp.asarray(tab, jnp.float32) for tab in tabs)


def _prepare_weights(g_pre, g_post, conv_w, conv_b, g_q, w_uq, g_kv, w_ukv, pool_w, pool_scale):
    bf = jnp.bfloat16
    perm = _rope_swap_perm()

    uq = w_uq.reshape(DEPTH, Q_LORA, N_HEADS, QK_NOPE + QK_ROPE)
    rope_q = uq[..., QK_NOPE:]
    wuq = jnp.concatenate([uq, rope_q[..., perm]], axis=-1).reshape(DEPTH, Q_LORA, HP).astype(bf)
    assert w_ukv.shape == (DEPTH, KV_LORA, HP) and QK_NOPE == V_HEAD and QK_NOPE + V_HEAD == LANES

    return {
        "g_pre": g_pre, "g_post": g_post, "conv_w": jnp.swapaxes(conv_w, 0, 1), "conv_b": conv_b,
        "g_q": g_q, "g_kv": g_kv, "pool_scale": pool_scale,
        "wuq": wuq, "w_ukv": w_ukv, "pool_w": pool_w,
    }


def kernel(x_prompt, x_sample, cache_mla_latent, c, c_ctx, w_mod, b_mod, g_pre, g_post, w_in, conv_w, conv_b,
           g_q, w_uq, g_kv, w_ukv, pool_w, pool_scale, w_branch, w_o):
    n_dec = x_sample.shape[0]
    assert 1 + n_dec <= COND_ROWS
    assert w_in.shape == (DEPTH, D_MODEL, W_IN_COLS)
    w = _prepare_weights(g_pre, g_post, conv_w, conv_b, g_q, w_uq, g_kv, w_ukv, pool_w, pool_scale)
    big_f32 = (jnp.swapaxes(w_in, 1, 2), w_branch.reshape(DEPTH, 3 * BRANCH_DIM, D_MODEL), w_o)
    cond = jnp.concatenate([c_ctx[None, :], c, jnp.zeros((COND_ROWS - 1 - n_dec, D_MODEL), jnp.float32)], axis=0)
    cache_t = jnp.swapaxes(cache_mla_latent, 2, 3)
    mod, w_front, kc, vc = _prologue(cond, w_mod, b_mod, big_f32, cache_t, w_ukv)
    ctx_tabs = _rope_tables(x_prompt.shape[1], rotate=False)
    dec_tabs = _rope_tables(x_sample.shape[1], rotate=True)

    back_windows = (_W_BACK, _W_BRANCH, _W_OUT)
    h, hs, state_t, big_bf = x_prompt, x_sample, None, None
    for l in range(DEPTH):
        ctx_fr, dec_fr, cast = _front(l, h, hs, mod, w, w_front, ctx_tabs, dec_tabs, state_t, big_f32,
                                      back_windows if big_bf is None else ())
        state_t = ctx_fr[6]
        big_bf = cast if big_bf is None else big_bf
        next_windows = (_W_FRONT,) + back_windows if l + 1 < DEPTH else ()
        h, hs, cast = _back(l, h, hs, mod, w, big_bf, ctx_fr[:6], dec_fr, kc, vc, big_f32, next_windows)
        if next_windows:
            w_front, big_bf = cast[0], cast[1:]
    return (h, hs, jnp.swapaxes(state_t, 2, 3))
```

```python
import functools

import numpy as np
import jax
import jax.numpy as jnp
from jax import lax
from jax.experimental import pallas as pl
from jax.experimental.pallas import tpu as pltpu

D_MODEL = 1024
DEPTH = 2
GRID_W = 64
EPS = 1e-6
BRANCH_DIM = 512
N_HEADS = 8
QK_NOPE = 64
QK_ROPE = 32
V_HEAD = 64
Q_LORA = 384
KV_LORA = 256
MLA_LATENT = KV_LORA + QK_ROPE
ROPE_THETA = 10000.0
POOL_WINDOWS = (2, 4, 8, 16)
POOL_GROUP = 128

LANES = 128
HEAD_PAD = LANES
HP = N_HEADS * HEAD_PAD
N_PAIRS = N_HEADS * V_HEAD // LANES
FRONT_TILE = 512
BACK_TILE = 256
BACK_DENSE_ROWS = 512
HALO = 16
COND_ROWS = 8
VMEM_LIMIT = 56 * 1024 * 1024

_R_AB = 0
_R_ACX = BRANCH_DIM
_R_MAIN = 3 * BRANCH_DIM
_MAIN_ROWS = BRANCH_DIM + Q_LORA + KV_LORA + LANES
_R_BZ = 4 * BRANCH_DIM + Q_LORA + MLA_LATENT
_R_CU = _R_BZ + BRANCH_DIM
_R_CZ = _R_CU + BRANCH_DIM
_R_MERGE = _R_CZ + BRANCH_DIM
W_IN_COLS = _R_MERGE + 3 * D_MODEL
_FRONT_ROWS = -(-_R_MERGE // LANES) * LANES
_QD0 = BRANCH_DIM
_CKV0 = _QD0 + Q_LORA
_KR0 = _CKV0 + KV_LORA
_LAT_EXT = KV_LORA + LANES


def _dot(a, b):
    return jnp.dot(a, b, preferred_element_type=jnp.float32)


def _dot_nt(a, b):
    return lax.dot_general(a, b, (((1,), (1,)), ((), ())), preferred_element_type=jnp.float32)


def _rms(x, g):
    return x * lax.rsqrt(jnp.mean(x * x, axis=-1, keepdims=True) + EPS) * g


def _sigmoid(x):
    return 0.5 * jnp.tanh(0.5 * x) + 0.5


def _silu(x):
    return x * _sigmoid(x)


def _modulated_norm(x, g_pre, mod_row):
    shift = mod_row[:, 0:D_MODEL]
    scale = mod_row[:, D_MODEL:2 * D_MODEL]
    return _rms(x, g_pre) * (1.0 + scale) + shift


def _params(n_grid_dims=1):
    return pltpu.CompilerParams(dimension_semantics=("arbitrary",) * n_grid_dims, vmem_limit_bytes=VMEM_LIMIT)


_SRC_W_IN_T, _SRC_W_BRANCH, _SRC_W_O = 0, 1, 2
_BACK_START = (_R_MERGE // 1024) * 1024
_W_FRONT = (_SRC_W_IN_T, 0, _FRONT_ROWS)
_W_BACK = (_SRC_W_IN_T, _BACK_START, W_IN_COLS - _BACK_START)
_W_BRANCH = (_SRC_W_BRANCH, 0, 3 * BRANCH_DIM)
_W_OUT = (_SRC_W_O, 0, D_MODEL)
_BACK_ROWS = _W_BACK[2]


def _chunk_rows(first_row, rows, n_steps):
    if first_row == 0:
        return pl.cdiv(pl.cdiv(rows, n_steps), 16) * 16
    per = LANES
    while first_row % per or pl.cdiv(rows, per) > n_steps:
        per *= 2
        assert per <= first_row
    return per


def _cast_specs(layer, windows, n_steps):
    in_specs, out_specs, out_shapes = [], [], []
    for _, first_row, rows in windows:
        per = _chunk_rows(first_row, rows, n_steps)
        n_blocks = pl.cdiv(rows, per)
        first_blk = first_row // per

        def in_map(i, first_blk=first_blk, last=n_blocks - 1):
            return (layer, first_blk + jnp.minimum(i, last), 0)

        def out_map(i, last=n_blocks - 1):
            return (jnp.minimum(i, last), 0)

        in_specs.append(pl.BlockSpec((None, per, D_MODEL), in_map))
        out_specs.append(pl.BlockSpec((per, D_MODEL), out_map))
        out_shapes.append(jax.ShapeDtypeStruct((rows, D_MODEL), jnp.bfloat16))
    return in_specs, out_specs, out_shapes


def _cast_chunks(in_refs, out_refs):
    for src, dst in zip(in_refs, out_refs):
        dst[...] = src[...].astype(jnp.bfloat16)


def _store_keys_values(k_refs, v_refs, ckv_bf, wukv_ref, kr):
    kv = _dot(ckv_bf, wukv_ref[...].astype(jnp.bfloat16))
    k_rope = pltpu.roll(kr, QK_NOPE, 1) + pltpu.roll(kr, QK_NOPE + QK_ROPE, 1)
    low_half = lax.broadcasted_iota(jnp.int32, kr.shape, 1) < QK_NOPE
    blocks = [kv[:, h * LANES:(h + 1) * LANES] for h in range(N_HEADS)]
    per_seq = kr.shape[0] // len(k_refs)
    for h in range(N_HEADS):
        key = jnp.where(low_half, blocks[h], k_rope).astype(jnp.bfloat16)
        for s, k_ref in enumerate(k_refs):
            k_ref[h] = key[s * per_seq:(s + 1) * per_seq]
    for p in range(N_PAIRS):
        even_v = pltpu.roll(blocks[2 * p], LANES - V_HEAD, 1)
        value = jnp.where(low_half, even_v, blocks[2 * p + 1]).astype(jnp.bfloat16)
        for s, v_ref in enumerate(v_refs):
            v_ref[p] = value[s * per_seq:(s + 1) * per_seq]


def _cache_kv_kernel(lat_t_ref, wukv_ref, k_ref, v_ref):
    lat_t = lat_t_ref[...]
    pad = jnp.zeros((_LAT_EXT - MLA_LATENT, lat_t.shape[1]), jnp.float32)
    lat = jnp.concatenate([lat_t, pad], axis=0).T
    _store_keys_values([k_ref], [v_ref], lat[:, 0:KV_LORA].astype(jnp.bfloat16), wukv_ref, lat[:, KV_LORA:_LAT_EXT])


_MOD_COL_TILES = 2


def _prologue_kernel(n_cache_jobs, cond_ref, wmod_ref, bmod_ref, wsrc_ref, lat_t_ref, wukv_ref,
                     mod_ref, wfront_ref, kc_ref, vc_ref):
    i = pl.program_id(0)
    h = _silu(cond_ref[...]).astype(jnp.bfloat16)
    mod_ref[...] = _dot(h, wmod_ref[...].astype(jnp.bfloat16)) + bmod_ref[pl.ds(i // _MOD_COL_TILES, 1), :]
    _cast_chunks([wsrc_ref], [wfront_ref])

    @pl.when(i < n_cache_jobs)
    def _():
        _cache_kv_kernel(lat_t_ref, wukv_ref, kc_ref, vc_ref)


def _prologue(cond, w_mod, b_mod, big_f32, cache_t, w_ukv):
    n_steps = DEPTH * _MOD_COL_TILES
    col_tile = 3 * D_MODEL // _MOD_COL_TILES
    nb, _, _, past = cache_t.shape
    n_cache_jobs = DEPTH * nb
    assert n_cache_jobs <= n_steps

    def mod_block(i):
        return (i // _MOD_COL_TILES, 0, i % _MOD_COL_TILES)

    def cache_layer(i):
        return jnp.minimum(i, n_cache_jobs - 1) // nb

    def cache_request(i):
        return jnp.minimum(i, n_cache_jobs - 1) % nb

    def kv(n):
        return (jax.ShapeDtypeStruct((DEPTH, nb, n, past, LANES), jnp.bfloat16),
                pl.BlockSpec((None, None, n, past, LANES), lambda i: (cache_layer(i), cache_request(i), 0, 0, 0)))

    (k_shape, k_spec), (v_shape, v_spec) = kv(N_HEADS), kv(N_PAIRS)
    cast_in, cast_out, cast_shapes = _cast_specs(0, [_W_FRONT], n_steps)
    return pl.pallas_call(
        functools.partial(_prologue_kernel, n_cache_jobs),
        grid=(n_steps,),
        in_specs=[
            pl.BlockSpec((COND_ROWS, D_MODEL), lambda i: (0, 0)),
            pl.BlockSpec((None, D_MODEL, col_tile), mod_block),
            pl.BlockSpec((DEPTH, col_tile), lambda i: (0, i % _MOD_COL_TILES)),
            cast_in[0],
            pl.BlockSpec((None, None, MLA_LATENT, past), lambda i: (cache_request(i), cache_layer(i), 0, 0)),
            pl.BlockSpec((None, KV_LORA, HP), lambda i: (cache_layer(i), 0, 0)),
        ],
        out_specs=[pl.BlockSpec((None, COND_ROWS, col_tile), mod_block), cast_out[0], k_spec, v_spec],
        out_shape=[jax.ShapeDtypeStruct((DEPTH, COND_ROWS, 3 * D_MODEL), jnp.float32), cast_shapes[0],
                   k_shape, v_shape],
        compiler_params=_params(),
        name="prologue",
    )(cond, w_mod, b_mod, big_f32[_SRC_W_IN_T], cache_t, w_ukv)


class _Pass:
    def __init__(self, x, tile, first_step, first_cond_row, shared_cond, group=1):
        self.nb, self.seq, _ = x.shape
        self.tm = min(tile, self.seq)
        assert self.seq % self.tm == 0
        self.nt = self.seq // self.tm
        self.group = group if self.nt == 1 and shared_cond else 1
        assert self.nb % self.group == 0
        self.steps = self.nb // self.group * self.nt
        self.first = first_step
        self.first_cond_row = first_cond_row
        self.shared_cond = shared_cond

    def local(self, i):
        return jnp.clip(i - self.first, 0, self.steps - 1)

    def batch(self, i):
        return self.local(i) // self.nt

    def tile(self, i):
        return self.local(i) % self.nt

    def cond_row(self, i):
        return self.first_cond_row if self.shared_cond else self.first_cond_row + self.batch(i)


def _passes(x_prompt, x_sample, tile, ctx_group=1):
    ctx = _Pass(x_prompt, tile, 0, 0, True, ctx_group)
    dec = _Pass(x_sample, tile, ctx.steps, 1, False)
    return ctx, dec


def _const_spec(l, shape):
    return pl.BlockSpec((None,) + shape, lambda i: (l,) + (0,) * len(shape), pipeline_mode=pl.Buffered(1))


def _whole_spec(arr):
    return pl.BlockSpec(arr.shape, lambda i: (0,) * arr.ndim, pipeline_mode=pl.Buffered(1))


def _row_block_spec(rows, block_index):
    return pl.BlockSpec((rows, D_MODEL), lambda i: (block_index, 0), pipeline_mode=pl.Buffered(1))


_N_FRONT_WEIGHTS = 10


def _front_tile(l, seq_len, j, x_ref, xp_ref, xn_ref, qtab_ref, kcos_ref, ksin_ref, mod_row, weights, outs, scratch):
    gpre_ref, w_ref, convw_ref, convb_ref, gq_ref, wuq_ref, gkv_ref, wukv_ref, poolw_ref, pscale_ref = weights
    g_pre, conv_b, g_q, g_kv, p_scale = (r[l:l + 1, :] for r in (gpre_ref, convb_ref, gq_ref, gkv_ref, pscale_ref))
    conv_w = [convw_ref[tap, l:l + 1, :] for tap in range(3)]
    q_out, k_out, v_out, ya_out, yc_out, sbz_out = outs[:6]
    u_scr, cu_scr = scratch
    n_seq, tm = x_ref.shape[0], x_ref.shape[1]
    last_j = seq_len // tm - 1
    stride = tm + HALO

    def seq_rows(s):
        return slice(s * tm, (s + 1) * tm)

    if last_j == 0:
        hn = _modulated_norm(x_ref[...].reshape(n_seq * tm, D_MODEL), g_pre, mod_row).astype(jnp.bfloat16)
        acx = _dot_nt(hn, w_ref[_R_ACX:_R_ACX + 2 * BRANCH_DIM, :])
        u_all = acx[:, 0:BRANCH_DIM] * acx[:, BRANCH_DIM:2 * BRANCH_DIM]
        cu_all = _dot_nt(hn, w_ref[_R_CU:_R_CU + BRANCH_DIM, :])
        zeros = jnp.zeros((HALO, BRANCH_DIM), jnp.float32)
        for scr, val in ((u_scr, u_all), (cu_scr, cu_all)):
            for s in range(n_seq + 1):
                scr[s * stride:s * stride + HALO, :] = zeros
            for s in range(n_seq):
                scr[s * stride + HALO:s * stride + HALO + tm, :] = val[seq_rows(s)]
    else:
        assert n_seq == 1
        x_ext = jnp.concatenate([xp_ref[0], x_ref[0], xn_ref[0]], axis=0)
        hn_ext = _modulated_norm(x_ext, g_pre, mod_row).astype(jnp.bfloat16)
        hn = hn_ext[HALO:HALO + tm]
        acx = _dot_nt(hn_ext, w_ref[_R_ACX:_R_ACX + 2 * BRANCH_DIM, :])
        u_ext = acx[:, 0:BRANCH_DIM] * acx[:, BRANCH_DIM:2 * BRANCH_DIM]
        cu_ext = _dot_nt(hn_ext, w_ref[_R_CU:_R_CU + BRANCH_DIM, :])
        head, body, tail = slice(0, HALO), slice(HALO, HALO + tm), slice(HALO + tm, 2 * HALO + tm)
        for scr, ext in ((u_scr, u_ext), (cu_scr, cu_ext)):
            scr[head, :] = jnp.where(j > 0, ext[head], 0.0)
            scr[body, :] = ext[body]
            scr[tail, :] = jnp.where(j < last_j, ext[tail], 0.0)

    pm = _dot_nt(hn, w_ref[_R_MAIN:_R_MAIN + _MAIN_ROWS, :])
    def conv_of(base):
        return (u_scr[base + HALO - 1:base + HALO - 1 + tm, :] * conv_w[0]
                + u_scr[base + HALO:base + HALO + tm, :] * conv_w[1]
                + u_scr[base + HALO + 1:base + HALO + 1 + tm, :] * conv_w[2]
                + conv_b)

    conv = jnp.concatenate([conv_of(s * stride) for s in range(n_seq)], axis=0)
    ya = _silu(pm[:, 0:BRANCH_DIM]) * (_dot_nt(hn, w_ref[_R_AB:_R_AB + BRANCH_DIM, :]) * conv)
    for s in range(n_seq):
        ya_out[s] = ya[seq_rows(s)].astype(jnp.bfloat16)

    t = j * tm + lax.broadcasted_iota(jnp.int32, (tm, POOL_GROUP), 0)

    def pooled_of(base, gi, win):
        half = win // 2
        cols = slice(gi * POOL_GROUP, (gi + 1) * POOL_GROUP)
        shifts = [1 << b for b in range(win.bit_length() - 1)]
        first = base + HALO - half
        total = cu_scr[first:first + tm + sum(shifts), cols]
        for shift in shifts:
            total = total[:-shift] + total[shift:]
        edge = 8
        assert half <= edge

        def clipped_mean(rows):
            count = jnp.minimum(t[rows] + half, seq_len) - jnp.maximum(t[rows] - half, 0)
            return total[rows] / count.astype(jnp.float32)

        mean = jnp.concatenate([clipped_mean(slice(0, edge)), total[edge:tm - edge] * (1.0 / win),
                                clipped_mean(slice(tm - edge, tm))], axis=0)
        return (mean - cu_scr[base + HALO:base + HALO + tm, cols]).astype(jnp.bfloat16)

    pooled = [jnp.concatenate([pooled_of(s * stride, gi, win) for s in range(n_seq)], axis=0)
              for gi, win in enumerate(POOL_WINDOWS)]
    zero = jnp.zeros((POOL_GROUP, POOL_GROUP), jnp.float32)
    mixed = []
    for n in range(len(pooled) // 2):
        both = jnp.concatenate([jnp.concatenate([poolw_ref[2 * n], zero], axis=1),
                                jnp.concatenate([zero, poolw_ref[2 * n + 1]], axis=1)], axis=0).astype(jnp.bfloat16)
        mixed.append(_dot(jnp.concatenate(pooled[2 * n:2 * n + 2], axis=1), both))
    mixed = jnp.concatenate(mixed, axis=1) * p_scale
    c_z = _dot_nt(hn, w_ref[_R_CZ:_R_CZ + BRANCH_DIM, :])
    yc = (_silu(c_z) * mixed).astype(jnp.bfloat16)
    for s in range(n_seq):
        yc_out[s] = yc[seq_rows(s)]

    qn = _rms(pm[:, _QD0:_QD0 + Q_LORA], g_q).astype(jnp.bfloat16)
    q = _dot(qn, wuq_ref[...])
    qtab = qtab_ref[...]
    ckv = _rms(pm[:, _CKV0:_CKV0 + KV_LORA], g_kv)
    kr = pm[:, _KR0:_KR0 + LANES]
    if len(outs) > 6:
        for s in range(n_seq):
            lat_out = outs[6].at[s]
            if len(lat_out.shape) == 3:
                for later in range(1, lat_out.shape[0]):
                    lat_out[later] = jnp.zeros(lat_out.shape[1:], jnp.float32)
                lat_out = lat_out.at[0]
            lat_out[0:KV_LORA, :] = ckv[seq_rows(s)].T
            lat_out[KV_LORA:MLA_LATENT, :] = kr[seq_rows(s)].T[0:QK_ROPE, :]
    lane = lax.broadcasted_iota(jnp.int32, kr.shape, 1)
    quarter = QK_ROPE // 4
    partner = jnp.where(lane % (2 * quarter) < quarter,
                        pltpu.roll(kr, LANES - quarter, 1), pltpu.roll(kr, quarter, 1))
    kcos, ksin = (jnp.concatenate([tab[...]] * n_seq, axis=0) for tab in (kcos_ref, ksin_ref))
    kr_rot = jnp.where(lane < QK_ROPE, kr * kcos + partner * ksin, 0.0)
    _store_keys_values([k_out.at[s] for s in range(n_seq)], [v_out.at[s] for s in range(n_seq)],
                       ckv.astype(jnp.bfloat16), wukv_ref, kr_rot)
    sbz = _silu(_dot_nt(hn, w_ref[_R_BZ:_R_BZ + BRANCH_DIM, :])).astype(jnp.bfloat16)
    for s in range(n_seq):
        for h in range(N_HEADS):
            q_out[s, h] = (q[seq_rows(s), h * HEAD_PAD:(h + 1) * HEAD_PAD] * qtab).astype(jnp.bfloat16)
        sbz_out[s] = sbz[seq_rows(s)]


def _mod_row(mod_ref, ctx, dec, i):
    row = jnp.where(i < dec.first, ctx.cond_row(i), dec.cond_row(i))
    return mod_ref[pl.ds(row, 1), :]


def _front_kernel(l, ctx, dec, n_cast, n_aliased, *refs):
    n_in = 6
    ctx_in, dec_in = refs[0:n_in], refs[n_in:2 * n_in]
    mod_ref = refs[2 * n_in]
    n_w = 2 * n_in + 1 + _N_FRONT_WEIGHTS
    weights = refs[2 * n_in + 1:n_w]
    outs = refs[n_w + n_cast + n_aliased:-2]
    ctx_out, dec_out = outs[0:7], outs[7:13]
    scratch = refs[-2:]
    i = pl.program_id(0)
    _cast_chunks(refs[n_w:n_w + n_cast], outs[13:])

    @pl.when(i < dec.first)
    def _():
        _front_tile(l, ctx.seq, ctx.tile(i), *ctx_in, _mod_row(mod_ref, ctx, dec, i), weights, ctx_out, scratch)

    @pl.when(i >= dec.first)
    def _():
        _front_tile(l, dec.seq, dec.tile(i), *dec_in, _mod_row(mod_ref, ctx, dec, i), weights, dec_out, scratch)


def _front(l, x_prompt, x_sample, mod, w, w_front, ctx_tabs, dec_tabs, state_t, big_f32, cast_windows):
    ctx, dec = _passes(x_prompt, x_sample, FRONT_TILE, max(1, FRONT_TILE // x_prompt.shape[1]))
    n_steps = ctx.steps + dec.steps

    def pass_in_specs(p):
        tm = p.tm
        hb = tm // HALO
        n_hblk = p.seq // HALO
        tab = pl.BlockSpec((tm, LANES), lambda i: (p.tile(i), 0))
        return [
            pl.BlockSpec((p.group, tm, D_MODEL), lambda i: (p.batch(i), p.tile(i), 0)),
            pl.BlockSpec((p.group, HALO, D_MODEL), lambda i: (p.batch(i), jnp.maximum(p.tile(i) * hb - 1, 0), 0)),
            pl.BlockSpec((p.group, HALO, D_MODEL),
                         lambda i: (p.batch(i), jnp.minimum((p.tile(i) + 1) * hb, n_hblk - 1), 0)),
            tab, tab, tab,
        ]

    def pass_out(p, with_latent):
        tm = p.tm

        def blocks(n):
            return (jax.ShapeDtypeStruct((p.nb, n, p.seq, LANES), jnp.bfloat16),
                    pl.BlockSpec((p.group, n, tm, LANES), lambda i: (p.batch(i), 0, p.tile(i), 0)))

        (head_shape, head_spec), (pair_shape, pair_spec) = blocks(N_HEADS), blocks(N_PAIRS)
        br_shape = jax.ShapeDtypeStruct((p.nb, p.seq, BRANCH_DIM), jnp.bfloat16)
        br_spec = pl.BlockSpec((p.group, tm, BRANCH_DIM), lambda i: (p.batch(i), p.tile(i), 0))
        shapes = [head_shape, head_shape, pair_shape] + [br_shape] * 3
        specs = [head_spec, head_spec, pair_spec] + [br_spec] * 3
        if with_latent:
            shapes.append(jax.ShapeDtypeStruct((p.nb, DEPTH, MLA_LATENT, p.seq), jnp.float32))
            if state_t is None:
                specs.append(pl.BlockSpec((p.group, DEPTH, MLA_LATENT, tm), lambda i: (p.batch(i), 0, 0, p.tile(i))))
            else:
                specs.append(pl.BlockSpec((p.group, None, MLA_LATENT, tm), lambda i: (p.batch(i), l, 0, p.tile(i))))
        return shapes, specs

    weight_specs = [
        _whole_spec(w["g_pre"]),
        _row_block_spec(_FRONT_ROWS, 0),
        _whole_spec(w["conv_w"]),
        _whole_spec(w["conv_b"]),
        _whole_spec(w["g_q"]),
        _const_spec(l, (Q_LORA, HP)),
        _whole_spec(w["g_kv"]),
        _const_spec(l, (KV_LORA, HP)),
        _const_spec(l, (len(POOL_WINDOWS), POOL_GROUP, POOL_GROUP)),
        _whole_spec(w["pool_scale"]),
    ]
    assert len(weight_specs) == _N_FRONT_WEIGHTS
    ctx_shapes, ctx_specs = pass_out(ctx, True)
    dec_shapes, dec_specs = pass_out(dec, False)
    in_specs = pass_in_specs(ctx) + pass_in_specs(dec) + [_const_spec(l, mod.shape[1:])] + weight_specs
    args = [x_prompt, x_prompt, x_prompt, *ctx_tabs, x_sample, x_sample, x_sample, *dec_tabs, mod,
            w["g_pre"], w_front, w["conv_w"], w["conv_b"], w["g_q"], w["wuq"], w["g_kv"],
            w["w_ukv"], w["pool_w"], w["pool_scale"]]
    cast_in, cast_out, cast_shapes = _cast_specs(l, cast_windows, n_steps)
    in_specs += cast_in
    args += [big_f32[src] for src, _, _ in cast_windows]
    aliases = {}
    if state_t is not None:
        aliases = {len(args): len(ctx_shapes) - 1}
        in_specs.append(pl.BlockSpec(memory_space=pl.ANY))
        args.append(state_t)
    outs = pl.pallas_call(
        functools.partial(_front_kernel, l, ctx, dec, len(cast_windows), len(aliases)),
        grid=(n_steps,),
        in_specs=in_specs,
        out_specs=ctx_specs + dec_specs + cast_out,
        out_shape=ctx_shapes + dec_shapes + cast_shapes,
        scratch_shapes=[pltpu.VMEM((max(p.group * (p.tm + HALO) + HALO for p in (ctx, dec)), BRANCH_DIM),
                                   jnp.float32)] * 2,
        input_output_aliases=aliases,
        compiler_params=_params(),
        name="front",
    )(*args)
    return outs[:7], outs[7:13], tuple(outs[13:])


_N_BACK_WEIGHTS = 5


def _back_tile(l, x_ref, q_ref, k_ref, v_ref, cache, ya_ref, yc_ref, sbz_ref, mod_row, weights, out_ref):
    gpre_ref, wg_ref, wbr_ref, wo_ref, gpost_ref = weights
    g_pre, g_post = gpre_ref[l:l + 1, :], gpost_ref[l:l + 1, :]
    n_seq, tq = x_ref.shape[0], x_ref.shape[1]
    rows = n_seq * tq

    def with_ones(v):
        return jnp.concatenate([v, jnp.ones_like(v)], axis=1)

    low_half = lax.broadcasted_iota(jnp.int32, (tq, LANES), 1) < V_HEAD

    def attention(sq):
        pairs = []
        for p in range(N_PAIRS):
            v_ext = with_ones(v_ref[sq, p])
            if cache is not None:
                kc_ref, vc_ref = cache
                vc_ext = with_ones(vc_ref[p])
            halves = []
            for h in (2 * p, 2 * p + 1):
                q = q_ref[sq, h]
                s = _dot_nt(q, k_ref[sq, h])
                m = jnp.max(s, axis=-1, keepdims=True)
                if cache is not None:
                    sc = _dot_nt(q, kc_ref[h])
                    m = jnp.maximum(m, jnp.max(sc, axis=-1, keepdims=True))
                o = _dot(jnp.exp2((s - m).astype(jnp.bfloat16)), v_ext)
                if cache is not None:
                    o = o + _dot(jnp.exp2((sc - m).astype(jnp.bfloat16)), vc_ext)
                halves.append(o[:, 0:LANES] / o[:, LANES:2 * LANES])
            pairs.append(jnp.where(low_half, halves[0], halves[1]))
        return jnp.concatenate(pairs, axis=1)

    attn = jnp.concatenate([attention(sq) for sq in range(n_seq)], axis=0)
    branch_rows = (rows, BRANCH_DIM)
    yb = (sbz_ref[...].reshape(branch_rows).astype(jnp.float32) * attn).astype(jnp.bfloat16)

    x = x_ref[...].reshape(rows, D_MODEL)
    hn = _modulated_norm(x, g_pre, mod_row).astype(jnp.bfloat16)
    merged = None
    for n, y in enumerate((ya_ref[...].reshape(branch_rows), yb, yc_ref[...].reshape(branch_rows))):
        r0 = _R_MERGE - _BACK_START + n * D_MODEL
        gate = _sigmoid(_dot_nt(hn, wg_ref[r0:r0 + D_MODEL, :]))
        term = gate * _dot(y, wbr_ref[n * BRANCH_DIM:(n + 1) * BRANCH_DIM, :])
        merged = term if merged is None else merged + term
    out = _rms(_dot(merged.astype(jnp.bfloat16), wo_ref[...]), g_post)
    out_ref[...] = (x + mod_row[:, 2 * D_MODEL:3 * D_MODEL] * out).reshape(n_seq, tq, D_MODEL)


def _back_kernel(l, ctx, dec, n_cast, *refs):
    x_c, q_c, k_c, v_c, ya_c, yc_c, sbz_c = refs[0:7]
    x_d, q_d, k_d, v_d, kc_ref, vc_ref, ya_d, yc_d, sbz_d = refs[7:16]
    mod_ref = refs[16]
    n_w = 17 + _N_BACK_WEIGHTS
    weights = refs[17:n_w]
    out_c, out_d = refs[n_w + n_cast:n_w + n_cast + 2]
    i = pl.program_id(0)
    _cast_chunks(refs[n_w:n_w + n_cast], refs[n_w + n_cast + 2:])

    @pl.when(i < dec.first)
    def _():
        _back_tile(l, x_c, q_c, k_c, v_c, None, ya_c, yc_c, sbz_c, _mod_row(mod_ref, ctx, dec, i), weights, out_c)

    @pl.when(i >= dec.first)
    def _():
        _back_tile(l, x_d, q_d, k_d, v_d, (kc_ref, vc_ref), ya_d, yc_d, sbz_d, _mod_row(mod_ref, ctx, dec, i),
                   weights, out_d)


def _back(l, x_prompt, x_sample, mod, w, big_bf, ctx_fr, dec_fr, kc, vc, big_f32, cast_windows):
    ctx, dec = _passes(x_prompt, x_sample, BACK_TILE, max(1, BACK_DENSE_ROWS // x_prompt.shape[1]))
    past = kc.shape[3]
    n_steps = ctx.steps + dec.steps

    def pass_specs(p):
        tq, g = p.tm, p.group
        tile = pl.BlockSpec((g, tq, D_MODEL), lambda i: (p.batch(i), p.tile(i), 0))
        q = pl.BlockSpec((g, N_HEADS, tq, HEAD_PAD), lambda i: (p.batch(i), 0, p.tile(i), 0))
        seq_mode = pl.Buffered(1) if p.nt > 1 else None
        k = pl.BlockSpec((g, N_HEADS, p.seq, LANES), lambda i: (p.batch(i), 0, 0, 0), pipeline_mode=seq_mode)
        v = pl.BlockSpec((g, N_PAIRS, p.seq, LANES), lambda i: (p.batch(i), 0, 0, 0), pipeline_mode=seq_mode)
        br = pl.BlockSpec((g, tq, BRANCH_DIM), lambda i: (p.batch(i), p.tile(i), 0))
        return tile, q, k, v, br

    tile_c, q_c, k_c, v_c, br_c = pass_specs(ctx)
    tile_d, q_d, k_d, v_d, br_d = pass_specs(dec)
    kc_spec = pl.BlockSpec((None, None, N_HEADS, past, LANES), lambda i: (l, dec.batch(i), 0, 0, 0))
    vc_spec = pl.BlockSpec((None, None, N_PAIRS, past, LANES), lambda i: (l, dec.batch(i), 0, 0, 0))
    in_specs = ([tile_c, q_c, k_c, v_c, br_c, br_c, br_c]
                + [tile_d, q_d, k_d, v_d, kc_spec, vc_spec, br_d, br_d, br_d]
                + [_const_spec(l, mod.shape[1:]),
                   _whole_spec(w["g_pre"]),
                   _row_block_spec(_BACK_ROWS, 0),
                   _row_block_spec(3 * BRANCH_DIM, 0),
                   _row_block_spec(D_MODEL, 0),
                   _whole_spec(w["g_post"])])
    q_cx, k_cx, v_cx, ya_cx, yc_cx, sbz_cx = ctx_fr
    q_dx, k_dx, v_dx, ya_dx, yc_dx, sbz_dx = dec_fr
    args = [x_prompt, q_cx, k_cx, v_cx, ya_cx, yc_cx, sbz_cx,
            x_sample, q_dx, k_dx, v_dx, kc, vc, ya_dx, yc_dx, sbz_dx,
            mod, w["g_pre"], *big_bf, w["g_post"]]
    out_specs = [tile_c, tile_d]
    out_shape = [jax.ShapeDtypeStruct(x_prompt.shape, jnp.float32), jax.ShapeDtypeStruct(x_sample.shape, jnp.float32)]
    cast_in, cast_out, cast_shapes = _cast_specs(l + 1, cast_windows, n_steps)
    in_specs += cast_in
    args += [big_f32[src] for src, _, _ in cast_windows]
    out_specs += cast_out
    out_shape += cast_shapes
    outs = pl.pallas_call(
        functools.partial(_back_kernel, l, ctx, dec, len(cast_windows)),
        grid=(n_steps,),
        in_specs=in_specs,
        out_specs=out_specs,
        out_shape=out_shape,
        compiler_params=_params(),
        name="back",
    )(*args)
    return outs[0], outs[1], tuple(outs[2:])


def _rope_swap_perm():
    quarter = QK_ROPE // 4
    idx = np.arange(QK_ROPE).reshape(2, 2, quarter)
    return idx[:, ::-1, :].reshape(-1)


def _rope_tables(n_tokens, rotate):
    if rotate:
        t = np.arange(n_tokens)
        pos = np.stack([t // GRID_W, t % GRID_W], axis=1).astype(np.float64)
        axis_dim = QK_ROPE // 2
        inv = 1.0 / (ROPE_THETA ** (np.arange(0, axis_dim, 2, dtype=np.float64) / axis_dim))
        ang = pos[:, :, None] * inv
        cos, sin = np.cos(ang), np.sin(ang)
        cfull = np.stack([cos, cos], axis=2).reshape(n_tokens, QK_ROPE)
        sfull = np.stack([-sin, sin], axis=2).reshape(n_tokens, QK_ROPE)
    else:
        cfull = np.ones((n_tokens, QK_ROPE))
        sfull = np.zeros((n_tokens, QK_ROPE))
    scale = (QK_NOPE + QK_ROPE) ** -0.5 * np.log2(np.e)
    qtab = np.concatenate([np.full((n_tokens, QK_NOPE), scale), cfull * scale, sfull * scale], axis=1)
    pad = np.zeros((n_tokens, LANES - QK_ROPE))
    tabs = (qtab, np.concatenate([cfull, pad], axis=1), np.concatenate([sfull, pad], axis=1))
    return tuple(jnp.asarray(tab, jnp.float32) for tab in tabs)


def _prepare_weights(g_pre, g_post, conv_w, conv_b, g_q, w_uq, g_kv, w_ukv, pool_w, pool_scale):
    bf = jnp.bfloat16
    perm = _rope_swap_perm()

    uq = w_uq.reshape(DEPTH, Q_LORA, N_HEADS, QK_NOPE + QK_ROPE)
    rope_q = uq[..., QK_NOPE:]
    wuq = jnp.concatenate([uq, rope_q[..., perm]], axis=-1).reshape(DEPTH, Q_LORA, HP).astype(bf)
    assert w_ukv.shape == (DEPTH, KV_LORA, HP) and QK_NOPE == V_HEAD and QK_NOPE + V_HEAD == LANES

    return {
        "g_pre": g_pre, "g_post": g_post, "conv_w": jnp.swapaxes(conv_w, 0, 1), "conv_b": conv_b,
        "g_q": g_q, "g_kv": g_kv, "pool_scale": pool_scale,
        "wuq": wuq, "w_ukv": w_ukv, "pool_w": pool_w,
    }


def kernel(x_prompt, x_sample, cache_mla_latent, c, c_ctx, w_mod, b_mod, g_pre, g_post, w_in, conv_w, conv_b,
           g_q, w_uq, g_kv, w_ukv, pool_w, pool_scale, w_branch, w_o):
    n_dec = x_sample.shape[0]
    assert 1 + n_dec <= COND_ROWS
    assert w_in.shape == (DEPTH, D_MODEL, W_IN_COLS)
    w = _prepare_weights(g_pre, g_post, conv_w, conv_b, g_q, w_uq, g_kv, w_ukv, pool_w, pool_scale)
    big_f32 = (jnp.swapaxes(w_in, 1, 2), w_branch.reshape(DEPTH, 3 * BRANCH_DIM, D_MODEL), w_o)
    cond = jnp.concatenate([c_ctx[None, :], c, jnp.zeros((COND_ROWS - 1 - n_dec, D_MODEL), jnp.float32)], axis=0)
    cache_t = jnp.swapaxes(cache_mla_latent, 2, 3)
    mod, w_front, kc, vc = _prologue(cond, w_mod, b_mod, big_f32, cache_t, w_ukv)
    ctx_tabs = _rope_tables(x_prompt.shape[1], rotate=False)
    dec_tabs = _rope_tables(x_sample.shape[1], rotate=True)

    back_windows = (_W_BACK, _W_BRANCH, _W_OUT)
    h, hs, state_t, big_bf = x_prompt, x_sample, None, None
    for l in range(DEPTH):
        ctx_fr, dec_fr, cast = _front(l, h, hs, mod, w, w_front, ctx_tabs, dec_tabs, state_t, big_f32,
                                      back_windows if big_bf is None else ())
        state_t = ctx_fr[6]
        big_bf = cast if big_bf is None else big_bf
        next_windows = (_W_FRONT,) + back_windows if l + 1 < DEPTH else ()
        h, hs, cast = _back(l, h, hs, mod, w, big_bf, ctx_fr[:6], dec_fr, kc, vc, big_f32, next_windows)
        if next_windows:
            w_front, big_bf = cast[0], cast[1:]
    return (h, hs, jnp.swapaxes(state_t, 2, 3))
```

```python
import functools

import numpy as np
import jax
import jax.numpy as jnp
from jax import lax
from jax.experimental import pallas as pl
from jax.experimental.pallas import tpu as pltpu

D_MODEL = 1024
DEPTH = 2
GRID_W = 64
EPS = 1e-6
BRANCH_DIM = 512
N_HEADS = 8
QK_NOPE = 64
QK_ROPE = 32
V_HEAD = 64
Q_LORA = 384
KV_LORA = 256
MLA_LATENT = KV_LORA + QK_ROPE
ROPE_THETA = 10000.0
POOL_WINDOWS = (2, 4, 8, 16)
POOL_GROUP = 128

LANES = 128
HEAD_PAD = LANES
HP = N_HEADS * HEAD_PAD
N_PAIRS = N_HEADS * V_HEAD // LANES
FRONT_TILE = 512
BACK_TILE = 256
HALO = 16
COND_ROWS = 8
VMEM_LIMIT = 56 * 1024 * 1024

_R_AB = 0
_R_ACX = BRANCH_DIM
_R_MAIN = 3 * BRANCH_DIM
_MAIN_ROWS = BRANCH_DIM + Q_LORA + KV_LORA + LANES
_R_BZ = 4 * BRANCH_DIM + Q_LORA + MLA_LATENT
_R_CU = _R_BZ + BRANCH_DIM
_R_CZ = _R_CU + BRANCH_DIM
_R_MERGE = _R_CZ + BRANCH_DIM
W_IN_COLS = _R_MERGE + 3 * D_MODEL
_FRONT_ROWS = -(-_R_MERGE // LANES) * LANES
_QD0 = BRANCH_DIM
_CKV0 = _QD0 + Q_LORA
_KR0 = _CKV0 + KV_LORA
_LAT_EXT = KV_LORA + LANES


def _dot(a, b):
    return jnp.dot(a, b, preferred_element_type=jnp.float32)


def _dot_nt(a, b):
    return lax.dot_general(a, b, (((1,), (1,)), ((), ())), preferred_element_type=jnp.float32)


def _rms(x, g):
    return x * lax.rsqrt(jnp.mean(x * x, axis=-1, keepdims=True) + EPS) * g


def _sigmoid(x):
    return 0.5 * jnp.tanh(0.5 * x) + 0.5


def _silu(x):
    return x * _sigmoid(x)


def _modulated_norm(x, g_pre, mod_row):
    shift = mod_row[:, 0:D_MODEL]
    scale = mod_row[:, D_MODEL:2 * D_MODEL]
    return _rms(x, g_pre) * (1.0 + scale) + shift


def _params(n_grid_dims=1):
    return pltpu.CompilerParams(dimension_semantics=("arbitrary",) * n_grid_dims, vmem_limit_bytes=VMEM_LIMIT)


_SRC_W_IN_T, _SRC_W_BRANCH, _SRC_W_O = 0, 1, 2
_BACK_START = (_R_MERGE // 1024) * 1024
_W_FRONT = (_SRC_W_IN_T, 0, _FRONT_ROWS)
_W_BACK = (_SRC_W_IN_T, _BACK_START, W_IN_COLS - _BACK_START)
_W_BRANCH = (_SRC_W_BRANCH, 0, 3 * BRANCH_DIM)
_W_OUT = (_SRC_W_O, 0, D_MODEL)
_BACK_ROWS = _W_BACK[2]


def _chunk_rows(first_row, rows, n_steps):
    if first_row == 0:
        return pl.cdiv(pl.cdiv(rows, n_steps), 16) * 16
    per = LANES
    while first_row % per or pl.cdiv(rows, per) > n_steps:
        per *= 2
        assert per <= first_row
    return per


def _cast_specs(layer, windows, n_steps):
    in_specs, out_specs, out_shapes = [], [], []
    for _, first_row, rows in windows:
        per = _chunk_rows(first_row, rows, n_steps)
        n_blocks = pl.cdiv(rows, per)
        first_blk = first_row // per

        def in_map(i, first_blk=first_blk, last=n_blocks - 1):
            return (layer, first_blk + jnp.minimum(i, last), 0)

        def out_map(i, last=n_blocks - 1):
            return (jnp.minimum(i, last), 0)

        in_specs.append(pl.BlockSpec((None, per, D_MODEL), in_map))
        out_specs.append(pl.BlockSpec((per, D_MODEL), out_map))
        out_shapes.append(jax.ShapeDtypeStruct((rows, D_MODEL), jnp.bfloat16))
    return in_specs, out_specs, out_shapes


def _cast_chunks(in_refs, out_refs):
    for src, dst in zip(in_refs, out_refs):
        dst[...] = src[...].astype(jnp.bfloat16)


def _store_keys_values(k_refs, v_refs, ckv_bf, wukv_ref, kr):
    kv = _dot(ckv_bf, wukv_ref[...].astype(jnp.bfloat16))
    k_rope = pltpu.roll(kr, QK_NOPE, 1) + pltpu.roll(kr, QK_NOPE + QK_ROPE, 1)
    low_half = lax.broadcasted_iota(jnp.int32, kr.shape, 1) < QK_NOPE
    blocks = [kv[:, h * LANES:(h + 1) * LANES] for h in range(N_HEADS)]
    per_seq = kr.shape[0] // len(k_refs)
    for h in range(N_HEADS):
        key = jnp.where(low_half, blocks[h], k_rope).astype(jnp.bfloat16)
        for s, k_ref in enumerate(k_refs):
            k_ref[h] = key[s * per_seq:(s + 1) * per_seq]
    for p in range(N_PAIRS):
        even_v = pltpu.roll(blocks[2 * p], LANES - V_HEAD, 1)
        value = jnp.where(low_half, even_v, blocks[2 * p + 1]).astype(jnp.bfloat16)
        for s, v_ref in enumerate(v_refs):
            v_ref[p] = value[s * per_seq:(s + 1) * per_seq]


def _cache_kv_kernel(lat_t_ref, wukv_ref, k_ref, v_ref):
    lat_t = lat_t_ref[...]
    pad = jnp.zeros((_LAT_EXT - MLA_LATENT, lat_t.shape[1]), jnp.float32)
    lat = jnp.concatenate([lat_t, pad], axis=0).T
    _store_keys_values([k_ref], [v_ref], lat[:, 0:KV_LORA].astype(jnp.bfloat16), wukv_ref, lat[:, KV_LORA:_LAT_EXT])


_MOD_COL_TILES = 2


def _prologue_kernel(n_cache_jobs, cond_ref, wmod_ref, bmod_ref, wsrc_ref, lat_t_ref, wukv_ref,
                     mod_ref, wfront_ref, kc_ref, vc_ref):
    i = pl.program_id(0)
    h = _silu(cond_ref[...]).astype(jnp.bfloat16)
    mod_ref[...] = _dot(h, wmod_ref[...].astype(jnp.bfloat16)) + bmod_ref[pl.ds(i // _MOD_COL_TILES, 1), :]
    _cast_chunks([wsrc_ref], [wfront_ref])

    @pl.when(i < n_cache_jobs)
    def _():
        _cache_kv_kernel(lat_t_ref, wukv_ref, kc_ref, vc_ref)


def _prologue(cond, w_mod, b_mod, big_f32, cache_t, w_ukv):
    n_steps = DEPTH * _MOD_COL_TILES
    col_tile = 3 * D_MODEL // _MOD_COL_TILES
    nb, _, _, past = cache_t.shape
    n_cache_jobs = DEPTH * nb
    assert n_cache_jobs <= n_steps

    def mod_block(i):
        return (i // _MOD_COL_TILES, 0, i % _MOD_COL_TILES)

    def cache_layer(i):
        return jnp.minimum(i, n_cache_jobs - 1) // nb

    def cache_request(i):
        return jnp.minimum(i, n_cache_jobs - 1) % nb

    def kv(n):
        return (jax.ShapeDtypeStruct((DEPTH, nb, n, past, LANES), jnp.bfloat16),
                pl.BlockSpec((None, None, n, past, LANES), lambda i: (cache_layer(i), cache_request(i), 0, 0, 0)))

    (k_shape, k_spec), (v_shape, v_spec) = kv(N_HEADS), kv(N_PAIRS)
    cast_in, cast_out, cast_shapes = _cast_specs(0, [_W_FRONT], n_steps)
    return pl.pallas_call(
        functools.partial(_prologue_kernel, n_cache_jobs),
        grid=(n_steps,),
        in_specs=[
            pl.BlockSpec((COND_ROWS, D_MODEL), lambda i: (0, 0)),
            pl.BlockSpec((None, D_MODEL, col_tile), mod_block),
            pl.BlockSpec((DEPTH, col_tile), lambda i: (0, i % _MOD_COL_TILES)),
            cast_in[0],
            pl.BlockSpec((None, None, MLA_LATENT, past), lambda i: (cache_request(i), cache_layer(i), 0, 0)),
            pl.BlockSpec((None, KV_LORA, HP), lambda i: (cache_layer(i), 0, 0)),
        ],
        out_specs=[pl.BlockSpec((None, COND_ROWS, col_tile), mod_block), cast_out[0], k_spec, v_spec],
        out_shape=[jax.ShapeDtypeStruct((DEPTH, COND_ROWS, 3 * D_MODEL), jnp.float32), cast_shapes[0],
                   k_shape, v_shape],
        compiler_params=_params(),
        name="prologue",
    )(cond, w_mod, b_mod, big_f32[_SRC_W_IN_T], cache_t, w_ukv)


class _Pass:
    def __init__(self, x, tile, first_step, first_cond_row, shared_cond, group=1):
        self.nb, self.seq, _ = x.shape
        self.tm = min(tile, self.seq)
        assert self.seq % self.tm == 0
        self.nt = self.seq // self.tm
        self.group = group if self.nt == 1 and shared_cond else 1
        assert self.nb % self.group == 0
        self.steps = self.nb // self.group * self.nt
        self.first = first_step
        self.first_cond_row = first_cond_row
        self.shared_cond = shared_cond

    def local(self, i):
        return jnp.clip(i - self.first, 0, self.steps - 1)

    def batch(self, i):
        return self.local(i) // self.nt

    def tile(self, i):
        return self.local(i) % self.nt

    def cond_row(self, i):
        return self.first_cond_row if self.shared_cond else self.first_cond_row + self.batch(i)


def _passes(x_prompt, x_sample, tile, ctx_group=1):
    ctx = _Pass(x_prompt, tile, 0, 0, True, ctx_group)
    dec = _Pass(x_sample, tile, ctx.steps, 1, False)
    return ctx, dec


def _const_spec(l, shape):
    return pl.BlockSpec((None,) + shape, lambda i: (l,) + (0,) * len(shape), pipeline_mode=pl.Buffered(1))


def _whole_spec(arr):
    return pl.BlockSpec(arr.shape, lambda i: (0,) * arr.ndim, pipeline_mode=pl.Buffered(1))


def _row_block_spec(rows, block_index):
    return pl.BlockSpec((rows, D_MODEL), lambda i: (block_index, 0), pipeline_mode=pl.Buffered(1))


_N_FRONT_WEIGHTS = 11
_WUQ_SLOT = 5


def _front_tile(l, seq_len, j, x_ref, xp_ref, xn_ref, qtab_ref, kcos_ref, ksin_ref, mod_row, weights, outs, scratch):
    gpre_ref, w_ref, convw_ref, convb_ref, gq_ref, wuq_ref, gkv_ref, wukv_ref, poolw_ref, pscale_ref = weights
    g_pre, conv_b, g_q, g_kv, p_scale = (r[l:l + 1, :] for r in (gpre_ref, convb_ref, gq_ref, gkv_ref, pscale_ref))
    conv_w = [convw_ref[tap, l:l + 1, :] for tap in range(3)]
    q_out, k_out, v_out, ya_out, yc_out, sbz_out = outs[:6]
    u_scr, cu_scr = scratch
    n_seq, tm = x_ref.shape[0], x_ref.shape[1]
    last_j = seq_len // tm - 1
    stride = tm + HALO

    def seq_rows(s):
        return slice(s * tm, (s + 1) * tm)

    if last_j == 0:
        hn = _modulated_norm(x_ref[...].reshape(n_seq * tm, D_MODEL), g_pre, mod_row).astype(jnp.bfloat16)
        acx = _dot_nt(hn, w_ref[_R_ACX:_R_ACX + 2 * BRANCH_DIM, :])
        u_all = acx[:, 0:BRANCH_DIM] * acx[:, BRANCH_DIM:2 * BRANCH_DIM]
        cu_all = _dot_nt(hn, w_ref[_R_CU:_R_CU + BRANCH_DIM, :])
        zeros = jnp.zeros((HALO, BRANCH_DIM), jnp.float32)
        for scr, val in ((u_scr, u_all), (cu_scr, cu_all)):
            for s in range(n_seq + 1):
                scr[s * stride:s * stride + HALO, :] = zeros
            for s in range(n_seq):
                scr[s * stride + HALO:s * stride + HALO + tm, :] = val[seq_rows(s)]
    else:
        assert n_seq == 1
        x_ext = jnp.concatenate([xp_ref[0], x_ref[0], xn_ref[0]], axis=0)
        hn_ext = _modulated_norm(x_ext, g_pre, mod_row).astype(jnp.bfloat16)
        hn = hn_ext[HALO:HALO + tm]
        acx = _dot_nt(hn_ext, w_ref[_R_ACX:_R_ACX + 2 * BRANCH_DIM, :])
        u_ext = acx[:, 0:BRANCH_DIM] * acx[:, BRANCH_DIM:2 * BRANCH_DIM]
        cu_ext = _dot_nt(hn_ext, w_ref[_R_CU:_R_CU + BRANCH_DIM, :])
        head, body, tail = slice(0, HALO), slice(HALO, HALO + tm), slice(HALO + tm, 2 * HALO + tm)
        for scr, ext in ((u_scr, u_ext), (cu_scr, cu_ext)):
            scr[head, :] = jnp.where(j > 0, ext[head], 0.0)
            scr[body, :] = ext[body]
            scr[tail, :] = jnp.where(j < last_j, ext[tail], 0.0)

    pm = _dot_nt(hn, w_ref[_R_MAIN:_R_MAIN + _MAIN_ROWS, :])
    def conv_of(base):
        return (u_scr[base + HALO - 1:base + HALO - 1 + tm, :] * conv_w[0]
                + u_scr[base + HALO:base + HALO + tm, :] * conv_w[1]
                + u_scr[base + HALO + 1:base + HALO + 1 + tm, :] * conv_w[2]
                + conv_b)

    conv = jnp.concatenate([conv_of(s * stride) for s in range(n_seq)], axis=0)
    ya = _silu(pm[:, 0:BRANCH_DIM]) * (_dot_nt(hn, w_ref[_R_AB:_R_AB + BRANCH_DIM, :]) * conv)
    for s in range(n_seq):
        ya_out[s] = ya[seq_rows(s)].astype(jnp.bfloat16)

    t = j * tm + lax.broadcasted_iota(jnp.int32, (tm, POOL_GROUP), 0)

    def pooled_of(base, gi, win):
        half = win // 2
        cols = slice(gi * POOL_GROUP, (gi + 1) * POOL_GROUP)
        shifts = [1 << b for b in range(win.bit_length() - 1)]
        first = base + HALO - half
        total = cu_scr[first:first + tm + sum(shifts), cols]
        for shift in shifts:
            total = total[:-shift] + total[shift:]
        edge = 8
        assert half <= edge

        def clipped_mean(rows):
            count = jnp.minimum(t[rows] + half, seq_len) - jnp.maximum(t[rows] - half, 0)
            return total[rows] / count.astype(jnp.float32)

        mean = jnp.concatenate([clipped_mean(slice(0, edge)), total[edge:tm - edge] * (1.0 / win),
                                clipped_mean(slice(tm - edge, tm))], axis=0)
        return (mean - cu_scr[base + HALO:base + HALO + tm, cols]).astype(jnp.bfloat16)

    pooled = [jnp.concatenate([pooled_of(s * stride, gi, win) for s in range(n_seq)], axis=0)
              for gi, win in enumerate(POOL_WINDOWS)]
    zero = jnp.zeros((POOL_GROUP, POOL_GROUP), jnp.float32)
    mixed = []
    for n in range(len(pooled) // 2):
        both = jnp.concatenate([jnp.concatenate([poolw_ref[2 * n], zero], axis=1),
                                jnp.concatenate([zero, poolw_ref[2 * n + 1]], axis=1)], axis=0).astype(jnp.bfloat16)
        mixed.append(_dot(jnp.concatenate(pooled[2 * n:2 * n + 2], axis=1), both))
    mixed = jnp.concatenate(mixed, axis=1) * p_scale
    c_z = _dot_nt(hn, w_ref[_R_CZ:_R_CZ + BRANCH_DIM, :])
    yc = (_silu(c_z) * mixed).astype(jnp.bfloat16)
    for s in range(n_seq):
        yc_out[s] = yc[seq_rows(s)]

    qn = _rms(pm[:, _QD0:_QD0 + Q_LORA], g_q).astype(jnp.bfloat16)
    q = _dot(qn, wuq_ref[...])
    qtab = qtab_ref[...]
    ckv = _rms(pm[:, _CKV0:_CKV0 + KV_LORA], g_kv)
    kr = pm[:, _KR0:_KR0 + LANES]
    if len(outs) > 6:
        for s in range(n_seq):
            lat_out = outs[6].at[s]
            if len(lat_out.shape) == 3:
                for later in range(1, lat_out.shape[0]):
                    lat_out[later] = jnp.zeros(lat_out.shape[1:], jnp.float32)
                lat_out = lat_out.at[0]
            lat_out[0:KV_LORA, :] = ckv[seq_rows(s)].T
            lat_out[KV_LORA:MLA_LATENT, :] = kr[seq_rows(s)].T[0:QK_ROPE, :]
    lane = lax.broadcasted_iota(jnp.int32, kr.shape, 1)
    quarter = QK_ROPE // 4
    partner = jnp.where(lane % (2 * quarter) < quarter,
                        pltpu.roll(kr, LANES - quarter, 1), pltpu.roll(kr, quarter, 1))
    kcos, ksin = (jnp.concatenate([tab[...]] * n_seq, axis=0) for tab in (kcos_ref, ksin_ref))
    kr_rot = jnp.where(lane < QK_ROPE, kr * kcos + partner * ksin, 0.0)
    _store_keys_values([k_out.at[s] for s in range(n_seq)], [v_out.at[s] for s in range(n_seq)],
                       ckv.astype(jnp.bfloat16), wukv_ref, kr_rot)
    sbz = _silu(_dot_nt(hn, w_ref[_R_BZ:_R_BZ + BRANCH_DIM, :])).astype(jnp.bfloat16)
    for s in range(n_seq):
        for h in range(N_HEADS):
            q_out[s, h] = (q[seq_rows(s), h * HEAD_PAD:(h + 1) * HEAD_PAD] * qtab).astype(jnp.bfloat16)
        sbz_out[s] = sbz[seq_rows(s)]


def _mod_row(mod_ref, ctx, dec, i):
    row = jnp.where(i < dec.first, ctx.cond_row(i), dec.cond_row(i))
    return mod_ref[pl.ds(row, 1), :]


def _front_kernel(l, ctx, dec, n_cast, n_aliased, *refs):
    n_in = 6
    ctx_in, dec_in = refs[0:n_in], refs[n_in:2 * n_in]
    mod_ref = refs[2 * n_in]
    n_w = 2 * n_in + 1 + _N_FRONT_WEIGHTS
    weights = list(refs[2 * n_in + 1:n_w])
    outs = refs[n_w + n_cast + n_aliased:-3]
    ctx_out, dec_out = outs[0:7], outs[7:13]
    wuq_scr, scratch = refs[-3], refs[-2:]
    i = pl.program_id(0)
    _cast_chunks(refs[n_w:n_w + n_cast], outs[13:])

    place_ref = weights.pop()
    wuq_src = weights[_WUQ_SLOT]

    @pl.when(i == 0)
    def _():
        wuq_scr[...] = _dot(wuq_src[...].astype(jnp.bfloat16), place_ref[...]).astype(jnp.bfloat16)

    weights[_WUQ_SLOT] = wuq_scr

    @pl.when(i < dec.first)
    def _():
        _front_tile(l, ctx.seq, ctx.tile(i), *ctx_in, _mod_row(mod_ref, ctx, dec, i), weights, ctx_out, scratch)

    @pl.when(i >= dec.first)
    def _():
        _front_tile(l, dec.seq, dec.tile(i), *dec_in, _mod_row(mod_ref, ctx, dec, i), weights, dec_out, scratch)


def _front(l, x_prompt, x_sample, mod, w, w_front, ctx_tabs, dec_tabs, state_t, big_f32, cast_windows):
    ctx, dec = _passes(x_prompt, x_sample, FRONT_TILE, max(1, FRONT_TILE // x_prompt.shape[1]))
    n_steps = ctx.steps + dec.steps

    def pass_in_specs(p):
        tm = p.tm
        hb = tm // HALO
        n_hblk = p.seq // HALO
        tab = pl.BlockSpec((tm, LANES), lambda i: (p.tile(i), 0))
        return [
            pl.BlockSpec((p.group, tm, D_MODEL), lambda i: (p.batch(i), p.tile(i), 0)),
            pl.BlockSpec((p.group, HALO, D_MODEL), lambda i: (p.batch(i), jnp.maximum(p.tile(i) * hb - 1, 0), 0)),
            pl.BlockSpec((p.group, HALO, D_MODEL),
                         lambda i: (p.batch(i), jnp.minimum((p.tile(i) + 1) * hb, n_hblk - 1), 0)),
            tab, tab, tab,
        ]

    def pass_out(p, with_latent):
        tm = p.tm

        def blocks(n):
            return (jax.ShapeDtypeStruct((p.nb, n, p.seq, LANES), jnp.bfloat16),
                    pl.BlockSpec((p.group, n, tm, LANES), lambda i: (p.batch(i), 0, p.tile(i), 0)))

        (head_shape, head_spec), (pair_shape, pair_spec) = blocks(N_HEADS), blocks(N_PAIRS)
        br_shape = jax.ShapeDtypeStruct((p.nb, p.seq, BRANCH_DIM), jnp.bfloat16)
        br_spec = pl.BlockSpec((p.group, tm, BRANCH_DIM), lambda i: (p.batch(i), p.tile(i), 0))
        shapes = [head_shape, head_shape, pair_shape] + [br_shape] * 3
        specs = [head_spec, head_spec, pair_spec] + [br_spec] * 3
        if with_latent:
            shapes.append(jax.ShapeDtypeStruct((p.nb, DEPTH, MLA_LATENT, p.seq), jnp.float32))
            if state_t is None:
                specs.append(pl.BlockSpec((p.group, DEPTH, MLA_LATENT, tm), lambda i: (p.batch(i), 0, 0, p.tile(i))))
            else:
                specs.append(pl.BlockSpec((p.group, None, MLA_LATENT, tm), lambda i: (p.batch(i), l, 0, p.tile(i))))
        return shapes, specs

    weight_specs = [
        _whole_spec(w["g_pre"]),
        _row_block_spec(_FRONT_ROWS, 0),
        _whole_spec(w["conv_w"]),
        _whole_spec(w["conv_b"]),
        _whole_spec(w["g_q"]),
        _const_spec(l, (Q_LORA, w["w_uq"].shape[2])),
        _whole_spec(w["g_kv"]),
        _const_spec(l, (KV_LORA, HP)),
        _const_spec(l, (len(POOL_WINDOWS), POOL_GROUP, POOL_GROUP)),
        _whole_spec(w["pool_scale"]),
        _whole_spec(w["q_place"]),
    ]
    assert len(weight_specs) == _N_FRONT_WEIGHTS
    ctx_shapes, ctx_specs = pass_out(ctx, True)
    dec_shapes, dec_specs = pass_out(dec, False)
    in_specs = pass_in_specs(ctx) + pass_in_specs(dec) + [_const_spec(l, mod.shape[1:])] + weight_specs
    args = [x_prompt, x_prompt, x_prompt, *ctx_tabs, x_sample, x_sample, x_sample, *dec_tabs, mod,
            w["g_pre"], w_front, w["conv_w"], w["conv_b"], w["g_q"], w["w_uq"], w["g_kv"],
            w["w_ukv"], w["pool_w"], w["pool_scale"], w["q_place"]]
    cast_in, cast_out, cast_shapes = _cast_specs(l, cast_windows, n_steps)
    in_specs += cast_in
    args += [big_f32[src] for src, _, _ in cast_windows]
    aliases = {}
    if state_t is not None:
        aliases = {len(args): len(ctx_shapes) - 1}
        in_specs.append(pl.BlockSpec(memory_space=pl.ANY))
        args.append(state_t)
    outs = pl.pallas_call(
        functools.partial(_front_kernel, l, ctx, dec, len(cast_windows), len(aliases)),
        grid=(n_steps,),
        in_specs=in_specs,
        out_specs=ctx_specs + dec_specs + cast_out,
        out_shape=ctx_shapes + dec_shapes + cast_shapes,
        scratch_shapes=[pltpu.VMEM((Q_LORA, HP), jnp.bfloat16)]
        + [pltpu.VMEM((max(p.group * (p.tm + HALO) + HALO for p in (ctx, dec)), BRANCH_DIM), jnp.float32)] * 2,
        input_output_aliases=aliases,
        compiler_params=_params(),
        name="front",
    )(*args)
    return outs[:7], outs[7:13], tuple(outs[13:])


_N_BACK_WEIGHTS = 5


def _back_tile(l, x_ref, q_ref, k_ref, v_ref, cache, ya_ref, yc_ref, sbz_ref, mod_row, weights, out_ref):
    gpre_ref, wg_ref, wbr_ref, wo_ref, gpost_ref = weights
    g_pre, g_post = gpre_ref[l:l + 1, :], gpost_ref[l:l + 1, :]
    x = x_ref[...]

    def with_ones(v):
        return jnp.concatenate([v, jnp.ones_like(v)], axis=1)

    low_half = lax.broadcasted_iota(jnp.int32, (x.shape[0], LANES), 1) < V_HEAD
    pairs = []
    for p in range(N_PAIRS):
        v_ext = with_ones(v_ref[p])
        if cache is not None:
            kc_ref, vc_ref = cache
            vc_ext = with_ones(vc_ref[p])
        halves = []
        for h in (2 * p, 2 * p + 1):
            q = q_ref[h]
            s = _dot_nt(q, k_ref[h])
            m = jnp.max(s, axis=-1, keepdims=True)
            if cache is not None:
                sc = _dot_nt(q, kc_ref[h])
                m = jnp.maximum(m, jnp.max(sc, axis=-1, keepdims=True))
            o = _dot(jnp.exp2((s - m).astype(jnp.bfloat16)), v_ext)
            if cache is not None:
                o = o + _dot(jnp.exp2((sc - m).astype(jnp.bfloat16)), vc_ext)
            halves.append(o[:, 0:LANES] / o[:, LANES:2 * LANES])
        pairs.append(jnp.where(low_half, halves[0], halves[1]))
    attn = jnp.concatenate(pairs, axis=1)
    yb = (sbz_ref[...].astype(jnp.float32) * attn).astype(jnp.bfloat16)

    hn = _modulated_norm(x, g_pre, mod_row).astype(jnp.bfloat16)
    merged = None
    for n, y in enumerate((ya_ref[...], yb, yc_ref[...])):
        r0 = _R_MERGE - _BACK_START + n * D_MODEL
        gate = _sigmoid(_dot_nt(hn, wg_ref[r0:r0 + D_MODEL, :]))
        term = gate * _dot(y, wbr_ref[n * BRANCH_DIM:(n + 1) * BRANCH_DIM, :])
        merged = term if merged is None else merged + term
    out = _rms(_dot(merged.astype(jnp.bfloat16), wo_ref[...]), g_post)
    out_ref[...] = x + mod_row[:, 2 * D_MODEL:3 * D_MODEL] * out


def _back_kernel(l, ctx, dec, n_cast, *refs):
    x_c, q_c, k_c, v_c, ya_c, yc_c, sbz_c = refs[0:7]
    x_d, q_d, k_d, v_d, kc_ref, vc_ref, ya_d, yc_d, sbz_d = refs[7:16]
    mod_ref = refs[16]
    n_w = 17 + _N_BACK_WEIGHTS
    weights = refs[17:n_w]
    out_c, out_d = refs[n_w + n_cast:n_w + n_cast + 2]
    i = pl.program_id(0)
    _cast_chunks(refs[n_w:n_w + n_cast], refs[n_w + n_cast + 2:])

    @pl.when(i < dec.first)
    def _():
        _back_tile(l, x_c, q_c, k_c, v_c, None, ya_c, yc_c, sbz_c, _mod_row(mod_ref, ctx, dec, i), weights, out_c)

    @pl.when(i >= dec.first)
    def _():
        _back_tile(l, x_d, q_d, k_d, v_d, (kc_ref, vc_ref), ya_d, yc_d, sbz_d, _mod_row(mod_ref, ctx, dec, i),
                   weights, out_d)


def _back(l, x_prompt, x_sample, mod, w, big_bf, ctx_fr, dec_fr, kc, vc, big_f32, cast_windows):
    ctx, dec = _passes(x_prompt, x_sample, BACK_TILE)
    past = kc.shape[3]
    n_steps = ctx.steps + dec.steps

    def pass_specs(p):
        tq = p.tm
        tile = pl.BlockSpec((None, tq, D_MODEL), lambda i: (p.batch(i), p.tile(i), 0))
        q = pl.BlockSpec((None, N_HEADS, tq, HEAD_PAD), lambda i: (p.batch(i), 0, p.tile(i), 0))
        k = pl.BlockSpec((None, N_HEADS, p.seq, LANES), lambda i: (p.batch(i), 0, 0, 0))
        v = pl.BlockSpec((None, N_PAIRS, p.seq, LANES), lambda i: (p.batch(i), 0, 0, 0))
        br = pl.BlockSpec((None, tq, BRANCH_DIM), lambda i: (p.batch(i), p.tile(i), 0))
        return tile, q, k, v, br

    tile_c, q_c, k_c, v_c, br_c = pass_specs(ctx)
    tile_d, q_d, k_d, v_d, br_d = pass_specs(dec)
    kc_spec = pl.BlockSpec((None, None, N_HEADS, past, LANES), lambda i: (l, dec.batch(i), 0, 0, 0))
    vc_spec = pl.BlockSpec((None, None, N_PAIRS, past, LANES), lambda i: (l, dec.batch(i), 0, 0, 0))
    in_specs = ([tile_c, q_c, k_c, v_c, br_c, br_c, br_c]
                + [tile_d, q_d, k_d, v_d, kc_spec, vc_spec, br_d, br_d, br_d]
                + [_const_spec(l, mod.shape[1:]),
                   _whole_spec(w["g_pre"]),
                   _row_block_spec(_BACK_ROWS, 0),
                   _row_block_spec(3 * BRANCH_DIM, 0),
                   _row_block_spec(D_MODEL, 0),
                   _whole_spec(w["g_post"])])
    q_cx, k_cx, v_cx, ya_cx, yc_cx, sbz_cx = ctx_fr
    q_dx, k_dx, v_dx, ya_dx, yc_dx, sbz_dx = dec_fr
    args = [x_prompt, q_cx, k_cx, v_cx, ya_cx, yc_cx, sbz_cx,
            x_sample, q_dx, k_dx, v_dx, kc, vc, ya_dx, yc_dx, sbz_dx,
            mod, w["g_pre"], *big_bf, w["g_post"]]
    out_specs = [tile_c, tile_d]
    out_shape = [jax.ShapeDtypeStruct(x_prompt.shape, jnp.float32), jax.ShapeDtypeStruct(x_sample.shape, jnp.float32)]
    cast_in, cast_out, cast_shapes = _cast_specs(l + 1, cast_windows, n_steps)
    in_specs += cast_in
    args += [big_f32[src] for src, _, _ in cast_windows]
    out_specs += cast_out
    out_shape += cast_shapes
    outs = pl.pallas_call(
        functools.partial(_back_kernel, l, ctx, dec, len(cast_windows)),
        grid=(n_steps,),
        in_specs=in_specs,
        out_specs=out_specs,
        out_shape=out_shape,
        compiler_params=_params(),
        name="back",
    )(*args)
    return outs[0], outs[1], tuple(outs[2:])


def _rope_swap_perm():
    quarter = QK_ROPE // 4
    idx = np.arange(QK_ROPE).reshape(2, 2, quarter)
    return idx[:, ::-1, :].reshape(-1)


def _rope_tables(n_tokens, rotate):
    if rotate:
        t = np.arange(n_tokens)
        pos = np.stack([t // GRID_W, t % GRID_W], axis=1).astype(np.float64)
        axis_dim = QK_ROPE // 2
        inv = 1.0 / (ROPE_THETA ** (np.arange(0, axis_dim, 2, dtype=np.float64) / axis_dim))
        ang = pos[:, :, None] * inv
        cos, sin = np.cos(ang), np.sin(ang)
        cfull = np.stack([cos, cos], axis=2).reshape(n_tokens, QK_ROPE)
        sfull = np.stack([-sin, sin], axis=2).reshape(n_tokens, QK_ROPE)
    else:
        cfull = np.ones((n_tokens, QK_ROPE))
        sfull = np.zeros((n_tokens, QK_ROPE))
    scale = (QK_NOPE + QK_ROPE) ** -0.5 * np.log2(np.e)
    qtab = np.concatenate([np.full((n_tokens, QK_NOPE), scale), cfull * scale, sfull * scale], axis=1)
    pad = np.zeros((n_tokens, LANES - QK_ROPE))
    tabs = (qtab, np.concatenate([cfull, pad], axis=1), np.concatenate([sfull, pad], axis=1))
    return tuple(jnp.asarray(tab, jnp.float32) for tab in tabs)


def _prepare_weights(g_pre, g_post, conv_w, conv_b, g_q, w_uq, g_kv, w_ukv, pool_w, pool_scale):
    perm = _rope_swap_perm()
    head_in = QK_NOPE + QK_ROPE
    place = np.zeros((N_HEADS * head_in, HP), np.float32)
    for h in range(N_HEADS):
        for c in range(head_in):
            place[h * head_in + c, h * HEAD_PAD + c] = 1.0
        for c in range(QK_ROPE):
            place[h * head_in + QK_NOPE + perm[c], h * HEAD_PAD + head_in + c] = 1.0
    assert w_uq.shape == (DEPTH, Q_LORA, N_HEADS * head_in)
    assert w_ukv.shape == (DEPTH, KV_LORA, HP) and QK_NOPE == V_HEAD and QK_NOPE + V_HEAD == LANES

    return {
        "g_pre": g_pre, "g_post": g_post, "conv_w": jnp.swapaxes(conv_w, 0, 1), "conv_b": conv_b,
        "g_q": g_q, "g_kv": g_kv, "pool_scale": pool_scale,
        "w_uq": w_uq, "q_place": jnp.asarray(place, jnp.bfloat16), "w_ukv": w_ukv, "pool_w": pool_w,
    }


def kernel(x_prompt, x_sample, cache_mla_latent, c, c_ctx, w_mod, b_mod, g_pre, g_post, w_in, conv_w, conv_b,
           g_q, w_uq, g_kv, w_ukv, pool_w, pool_scale, w_branch, w_o):
    n_dec = x_sample.shape[0]
    assert 1 + n_dec <= COND_ROWS
    assert w_in.shape == (DEPTH, D_MODEL, W_IN_COLS)
    w = _prepare_weights(g_pre, g_post, conv_w, conv_b, g_q, w_uq, g_kv, w_ukv, pool_w, pool_scale)
    big_f32 = (jnp.swapaxes(w_in, 1, 2), w_branch.reshape(DEPTH, 3 * BRANCH_DIM, D_MODEL), w_o)
    cond = jnp.concatenate([c_ctx[None, :], c, jnp.zeros((COND_ROWS - 1 - n_dec, D_MODEL), jnp.float32)], axis=0)
    cache_t = jnp.swapaxes(cache_mla_latent, 2, 3)
    mod, w_front, kc, vc = _prologue(cond, w_mod, b_mod, big_f32, cache_t, w_ukv)
    ctx_tabs = _rope_tables(x_prompt.shape[1], rotate=False)
    dec_tabs = _rope_tables(x_sample.shape[1], rotate=True)

    back_windows = (_W_BACK, _W_BRANCH, _W_OUT)
    h, hs, state_t, big_bf = x_prompt, x_sample, None, None
    for l in range(DEPTH):
        ctx_fr, dec_fr, cast = _front(l, h, hs, mod, w, w_front, ctx_tabs, dec_tabs, state_t, big_f32,
                                      back_windows if big_bf is None else ())
        state_t = ctx_fr[6]
        big_bf = cast if big_bf is None else big_bf
        next_windows = (_W_FRONT,) + back_windows if l + 1 < DEPTH else ()
        h, hs, cast = _back(l, h, hs, mod, w, big_bf, ctx_fr[:6], dec_fr, kc, vc, big_f32, next_windows)
        if next_windows:
            w_front, big_bf = cast[0], cast[1:]
    return (h, hs, jnp.swapaxes(state_t, 2, 3))
```

```python
import functools

import numpy as np
import jax
import jax.numpy as jnp
from jax import lax
from jax.experimental import pallas as pl
from jax.experimental.pallas import tpu as pltpu

D_MODEL = 1024
DEPTH = 2
GRID_W = 64
EPS = 1e-6
BRANCH_DIM = 512
N_HEADS = 8
QK_NOPE = 64
QK_ROPE = 32
V_HEAD = 64
Q_LORA = 384
KV_LORA = 256
MLA_LATENT = KV_LORA + QK_ROPE
ROPE_THETA = 10000.0
POOL_WINDOWS = (2, 4, 8, 16)
POOL_GROUP = 128

LANES = 128
HEAD_PAD = LANES
HP = N_HEADS * HEAD_PAD
N_PAIRS = N_HEADS * V_HEAD // LANES
FRONT_TILE = 512
BACK_TILE = 256
HALO = 16
COND_ROWS = 8
VMEM_LIMIT = 56 * 1024 * 1024

_R_AB = 0
_R_ACX = BRANCH_DIM
_R_MAIN = 3 * BRANCH_DIM
_MAIN_ROWS = BRANCH_DIM + Q_LORA + KV_LORA + LANES
_R_BZ = 4 * BRANCH_DIM + Q_LORA + MLA_LATENT
_R_CU = _R_BZ + BRANCH_DIM
_R_CZ = _R_CU + BRANCH_DIM
_R_MERGE = _R_CZ + BRANCH_DIM
W_IN_COLS = _R_MERGE + 3 * D_MODEL
_FRONT_ROWS = -(-_R_MERGE // LANES) * LANES
_QD0 = BRANCH_DIM
_CKV0 = _QD0 + Q_LORA
_KR0 = _CKV0 + KV_LORA
_LAT_EXT = KV_LORA + LANES


def _dot(a, b):
    return jnp.dot(a, b, preferred_element_type=jnp.float32)


def _dot_nt(a, b):
    return lax.dot_general(a, b, (((1,), (1,)), ((), ())), preferred_element_type=jnp.float32)


def _rms(x, g):
    return x * lax.rsqrt(jnp.mean(x * x, axis=-1, keepdims=True) + EPS) * g


def _sigmoid(x):
    return 0.5 * jnp.tanh(0.5 * x) + 0.5


def _silu(x):
    return x * _sigmoid(x)


def _modulated_norm(x, g_pre, mod_row):
    shift = mod_row[:, 0:D_MODEL]
    scale = mod_row[:, D_MODEL:2 * D_MODEL]
    return _rms(x, g_pre) * (1.0 + scale) + shift


def _params(n_grid_dims=1):
    return pltpu.CompilerParams(dimension_semantics=("arbitrary",) * n_grid_dims, vmem_limit_bytes=VMEM_LIMIT)


_SRC_W_IN_T, _SRC_W_BRANCH, _SRC_W_O = 0, 1, 2
_BACK_START = (_R_MERGE // 1024) * 1024
_W_FRONT = (_SRC_W_IN_T, 0, _FRONT_ROWS)
_W_BACK = (_SRC_W_IN_T, _BACK_START, W_IN_COLS - _BACK_START)
_W_BRANCH = (_SRC_W_BRANCH, 0, 3 * BRANCH_DIM)
_W_OUT = (_SRC_W_O, 0, D_MODEL)
_BACK_ROWS = _W_BACK[2]


def _chunk_rows(first_row, rows, n_steps):
    if first_row == 0:
        return pl.cdiv(pl.cdiv(rows, n_steps), 16) * 16
    per = LANES
    while first_row % per or pl.cdiv(rows, per) > n_steps:
        per *= 2
        assert per <= first_row
    return per


def _cast_specs(layer, windows, n_steps):
    in_specs, out_specs, out_shapes = [], [], []
    for _, first_row, rows in windows:
        per = _chunk_rows(first_row, rows, n_steps)
        n_blocks = pl.cdiv(rows, per)
        first_blk = first_row // per

        def in_map(i, first_blk=first_blk, last=n_blocks - 1):
            return (layer, first_blk + jnp.minimum(i, last), 0)

        def out_map(i, last=n_blocks - 1):
            return (jnp.minimum(i, last), 0)

        in_specs.append(pl.BlockSpec((None, per, D_MODEL), in_map))
        out_specs.append(pl.BlockSpec((per, D_MODEL), out_map))
        out_shapes.append(jax.ShapeDtypeStruct((rows, D_MODEL), jnp.bfloat16))
    return in_specs, out_specs, out_shapes


def _cast_chunks(in_refs, out_refs):
    for src, dst in zip(in_refs, out_refs):
        dst[...] = src[...].astype(jnp.bfloat16)


def _store_keys_values(k_refs, v_refs, ckv_bf, wukv_ref, kr):
    kv = _dot(ckv_bf, wukv_ref[...].astype(jnp.bfloat16))
    k_rope = pltpu.roll(kr, QK_NOPE, 1) + pltpu.roll(kr, QK_NOPE + QK_ROPE, 1)
    low_half = lax.broadcasted_iota(jnp.int32, kr.shape, 1) < QK_NOPE
    blocks = [kv[:, h * LANES:(h + 1) * LANES] for h in range(N_HEADS)]
    per_seq = kr.shape[0] // len(k_refs)
    for h in range(N_HEADS):
        key = jnp.where(low_half, blocks[h], k_rope).astype(jnp.bfloat16)
        for s, k_ref in enumerate(k_refs):
            k_ref[h] = key[s * per_seq:(s + 1) * per_seq]
    for p in range(N_PAIRS):
        even_v = pltpu.roll(blocks[2 * p], LANES - V_HEAD, 1)
        value = jnp.where(low_half, even_v, blocks[2 * p + 1]).astype(jnp.bfloat16)
        for s, v_ref in enumerate(v_refs):
            v_ref[p] = value[s * per_seq:(s + 1) * per_seq]


def _cache_kv_kernel(lat_t_ref, wukv_ref, k_ref, v_ref):
    lat_t = lat_t_ref[...]
    pad = jnp.zeros((_LAT_EXT - MLA_LATENT, lat_t.shape[1]), jnp.float32)
    lat = jnp.concatenate([lat_t, pad], axis=0).T
    _store_keys_values([k_ref], [v_ref], lat[:, 0:KV_LORA].astype(jnp.bfloat16), wukv_ref, lat[:, KV_LORA:_LAT_EXT])


_MOD_COL_TILES = 2


def _prologue_kernel(n_cache_jobs, cctx_ref, c_ref, wmod_ref, bmod_ref, wsrc_ref, lat_t_ref, wukv_ref,
                     mod_ref, wfront_ref, kc_ref, vc_ref):
    i = pl.program_id(0)
    n_pad = COND_ROWS - 1 - c_ref.shape[0]
    rows = [cctx_ref[...], c_ref[...]] + ([jnp.zeros((n_pad, D_MODEL), jnp.float32)] if n_pad else [])
    h = _silu(jnp.concatenate(rows, axis=0)).astype(jnp.bfloat16)
    mod_ref[...] = _dot(h, wmod_ref[...].astype(jnp.bfloat16)) + bmod_ref[pl.ds(i // _MOD_COL_TILES, 1), :]
    _cast_chunks([wsrc_ref], [wfront_ref])

    @pl.when(i < n_cache_jobs)
    def _():
        _cache_kv_kernel(lat_t_ref, wukv_ref, kc_ref, vc_ref)


def _prologue(c_ctx, c, w_mod, b_mod, big_f32, cache_t, w_ukv):
    n_steps = DEPTH * _MOD_COL_TILES
    col_tile = 3 * D_MODEL // _MOD_COL_TILES
    nb, _, _, past = cache_t.shape
    n_cache_jobs = DEPTH * nb
    assert n_cache_jobs <= n_steps

    def mod_block(i):
        return (i // _MOD_COL_TILES, 0, i % _MOD_COL_TILES)

    def cache_layer(i):
        return jnp.minimum(i, n_cache_jobs - 1) // nb

    def cache_request(i):
        return jnp.minimum(i, n_cache_jobs - 1) % nb

    def kv(n):
        return (jax.ShapeDtypeStruct((DEPTH, nb, n, past, LANES), jnp.bfloat16),
                pl.BlockSpec((None, None, n, past, LANES), lambda i: (cache_layer(i), cache_request(i), 0, 0, 0)))

    (k_shape, k_spec), (v_shape, v_spec) = kv(N_HEADS), kv(N_PAIRS)
    cast_in, cast_out, cast_shapes = _cast_specs(0, [_W_FRONT], n_steps)
    return pl.pallas_call(
        functools.partial(_prologue_kernel, n_cache_jobs),
        grid=(n_steps,),
        in_specs=[
            pl.BlockSpec((1, D_MODEL), lambda i: (0, 0)),
            pl.BlockSpec(c.shape, lambda i: (0, 0)),
            pl.BlockSpec((None, D_MODEL, col_tile), mod_block),
            pl.BlockSpec((DEPTH, col_tile), lambda i: (0, i % _MOD_COL_TILES)),
            cast_in[0],
            pl.BlockSpec((None, None, MLA_LATENT, past), lambda i: (cache_request(i), cache_layer(i), 0, 0)),
            pl.BlockSpec((None, KV_LORA, HP), lambda i: (cache_layer(i), 0, 0)),
        ],
        out_specs=[pl.BlockSpec((None, COND_ROWS, col_tile), mod_block), cast_out[0], k_spec, v_spec],
        out_shape=[jax.ShapeDtypeStruct((DEPTH, COND_ROWS, 3 * D_MODEL), jnp.float32), cast_shapes[0],
                   k_shape, v_shape],
        compiler_params=_params(),
        name="prologue",
    )(c_ctx[None, :], c, w_mod, b_mod, big_f32[_SRC_W_IN_T], cache_t, w_ukv)


class _Pass:
    def __init__(self, x, tile, first_step, first_cond_row, shared_cond, group=1):
        self.nb, self.seq, _ = x.shape
        self.tm = min(tile, self.seq)
        assert self.seq % self.tm == 0
        self.nt = self.seq // self.tm
        self.group = group if self.nt == 1 and shared_cond else 1
        assert self.nb % self.group == 0
        self.steps = self.nb // self.group * self.nt
        self.first = first_step
        self.first_cond_row = first_cond_row
        self.shared_cond = shared_cond

    def local(self, i):
        return jnp.clip(i - self.first, 0, self.steps - 1)

    def batch(self, i):
        return self.local(i) // self.nt

    def tile(self, i):
        return self.local(i) % self.nt

    def cond_row(self, i):
        return self.first_cond_row if self.shared_cond else self.first_cond_row + self.batch(i)


def _passes(x_prompt, x_sample, tile, ctx_group=1):
    ctx = _Pass(x_prompt, tile, 0, 0, True, ctx_group)
    dec = _Pass(x_sample, tile, ctx.steps, 1, False)
    return ctx, dec


def _const_spec(l, shape):
    return pl.BlockSpec((None,) + shape, lambda i: (l,) + (0,) * len(shape), pipeline_mode=pl.Buffered(1))


def _whole_spec(arr):
    return pl.BlockSpec(arr.shape, lambda i: (0,) * arr.ndim, pipeline_mode=pl.Buffered(1))


def _row_block_spec(rows, block_index):
    return pl.BlockSpec((rows, D_MODEL), lambda i: (block_index, 0), pipeline_mode=pl.Buffered(1))


_N_FRONT_WEIGHTS = 11
_WUQ_SLOT = 5


def _front_tile(l, seq_len, j, x_ref, xp_ref, xn_ref, qtab_ref, kcos_ref, ksin_ref, mod_row, weights, outs, scratch):
    gpre_ref, w_ref, convw_ref, convb_ref, gq_ref, wuq_ref, gkv_ref, wukv_ref, poolw_ref, pscale_ref = weights
    g_pre, conv_b, g_q, g_kv, p_scale = (r[l:l + 1, :] for r in (gpre_ref, convb_ref, gq_ref, gkv_ref, pscale_ref))
    conv_w = [convw_ref[tap, l:l + 1, :] for tap in range(3)]
    q_out, k_out, v_out, ya_out, yc_out, sbz_out = outs[:6]
    u_scr, cu_scr = scratch
    n_seq, tm = x_ref.shape[0], x_ref.shape[1]
    last_j = seq_len // tm - 1
    stride = tm + HALO

    def seq_rows(s):
        return slice(s * tm, (s + 1) * tm)

    if last_j == 0:
        hn = _modulated_norm(x_ref[...].reshape(n_seq * tm, D_MODEL), g_pre, mod_row).astype(jnp.bfloat16)
        acx = _dot_nt(hn, w_ref[_R_ACX:_R_ACX + 2 * BRANCH_DIM, :])
        u_all = acx[:, 0:BRANCH_DIM] * acx[:, BRANCH_DIM:2 * BRANCH_DIM]
        cu_all = _dot_nt(hn, w_ref[_R_CU:_R_CU + BRANCH_DIM, :])
        zeros = jnp.zeros((HALO, BRANCH_DIM), jnp.float32)
        for scr, val in ((u_scr, u_all), (cu_scr, cu_all)):
            for s in range(n_seq + 1):
                scr[s * stride:s * stride + HALO, :] = zeros
            for s in range(n_seq):
                scr[s * stride + HALO:s * stride + HALO + tm, :] = val[seq_rows(s)]
    else:
        assert n_seq == 1
        x_ext = jnp.concatenate([xp_ref[0], x_ref[0], xn_ref[0]], axis=0)
        hn_ext = _modulated_norm(x_ext, g_pre, mod_row).astype(jnp.bfloat16)
        hn = hn_ext[HALO:HALO + tm]
        acx = _dot_nt(hn_ext, w_ref[_R_ACX:_R_ACX + 2 * BRANCH_DIM, :])
        u_ext = acx[:, 0:BRANCH_DIM] * acx[:, BRANCH_DIM:2 * BRANCH_DIM]
        cu_ext = _dot_nt(hn_ext, w_ref[_R_CU:_R_CU + BRANCH_DIM, :])
        head, body, tail = slice(0, HALO), slice(HALO, HALO + tm), slice(HALO + tm, 2 * HALO + tm)
        for scr, ext in ((u_scr, u_ext), (cu_scr, cu_ext)):
            scr[head, :] = jnp.where(j > 0, ext[head], 0.0)
            scr[body, :] = ext[body]
            scr[tail, :] = jnp.where(j < last_j, ext[tail], 0.0)

    pm = _dot_nt(hn, w_ref[_R_MAIN:_R_MAIN + _MAIN_ROWS, :])
    def conv_of(base):
        return (u_scr[base + HALO - 1:base + HALO - 1 + tm, :] * conv_w[0]
                + u_scr[base + HALO:base + HALO + tm, :] * conv_w[1]
                + u_scr[base + HALO + 1:base + HALO + 1 + tm, :] * conv_w[2]
                + conv_b)

    conv = jnp.concatenate([conv_of(s * stride) for s in range(n_seq)], axis=0)
    ya = _silu(pm[:, 0:BRANCH_DIM]) * (_dot_nt(hn, w_ref[_R_AB:_R_AB + BRANCH_DIM, :]) * conv)
    for s in range(n_seq):
        ya_out[s] = ya[seq_rows(s)].astype(jnp.bfloat16)

    t = j * tm + lax.broadcasted_iota(jnp.int32, (tm, POOL_GROUP), 0)

    def pooled_of(base, gi, win):
        half = win // 2
        cols = slice(gi * POOL_GROUP, (gi + 1) * POOL_GROUP)
        shifts = [1 << b for b in range(win.bit_length() - 1)]
        first = base + HALO - half
        total = cu_scr[first:first + tm + sum(shifts), cols]
        for shift in shifts:
            total = total[:-shift] + total[shift:]
        edge = 8
        assert half <= edge

        def clipped_mean(rows):
            count = jnp.minimum(t[rows] + half, seq_len) - jnp.maximum(t[rows] - half, 0)
            return total[rows] / count.astype(jnp.float32)

        mean = jnp.concatenate([clipped_mean(slice(0, edge)), total[edge:tm - edge] * (1.0 / win),
                                clipped_mean(slice(tm - edge, tm))], axis=0)
        return (mean - cu_scr[base + HALO:base + HALO + tm, cols]).astype(jnp.bfloat16)

    pooled = [jnp.concatenate([pooled_of(s * stride, gi, win) for s in range(n_seq)], axis=0)
              for gi, win in enumerate(POOL_WINDOWS)]
    zero = jnp.zeros((POOL_GROUP, POOL_GROUP), jnp.float32)
    mixed = []
    for n in range(len(pooled) // 2):
        both = jnp.concatenate([jnp.concatenate([poolw_ref[2 * n], zero], axis=1),
                                jnp.concatenate([zero, poolw_ref[2 * n + 1]], axis=1)], axis=0).astype(jnp.bfloat16)
        mixed.append(_dot(jnp.concatenate(pooled[2 * n:2 * n + 2], axis=1), both))
    mixed = jnp.concatenate(mixed, axis=1) * p_scale
    c_z = _dot_nt(hn, w_ref[_R_CZ:_R_CZ + BRANCH_DIM, :])
    yc = (_silu(c_z) * mixed).astype(jnp.bfloat16)
    for s in range(n_seq):
        yc_out[s] = yc[seq_rows(s)]

    qn = _rms(pm[:, _QD0:_QD0 + Q_LORA], g_q).astype(jnp.bfloat16)
    q = _dot(qn, wuq_ref[...])
    qtab = qtab_ref[...]
    ckv = _rms(pm[:, _CKV0:_CKV0 + KV_LORA], g_kv)
    kr = pm[:, _KR0:_KR0 + LANES]
    if len(outs) > 6:
        for s in range(n_seq):
            lat_out = outs[6].at[s]
            if len(lat_out.shape) == 3:
                for later in range(1, lat_out.shape[0]):
                    lat_out[later] = jnp.zeros(lat_out.shape[1:], jnp.float32)
                lat_out = lat_out.at[0]
            lat_out[0:KV_LORA, :] = ckv[seq_rows(s)].T
            lat_out[KV_LORA:MLA_LATENT, :] = kr[seq_rows(s)].T[0:QK_ROPE, :]
    lane = lax.broadcasted_iota(jnp.int32, kr.shape, 1)
    quarter = QK_ROPE // 4
    partner = jnp.where(lane % (2 * quarter) < quarter,
                        pltpu.roll(kr, LANES - quarter, 1), pltpu.roll(kr, quarter, 1))
    kcos, ksin = (jnp.concatenate([tab[...]] * n_seq, axis=0) for tab in (kcos_ref, ksin_ref))
    kr_rot = jnp.where(lane < QK_ROPE, kr * kcos + partner * ksin, 0.0)
    _store_keys_values([k_out.at[s] for s in range(n_seq)], [v_out.at[s] for s in range(n_seq)],
                       ckv.astype(jnp.bfloat16), wukv_ref, kr_rot)
    sbz = _silu(_dot_nt(hn, w_ref[_R_BZ:_R_BZ + BRANCH_DIM, :])).astype(jnp.bfloat16)
    for s in range(n_seq):
        for h in range(N_HEADS):
            q_out[s, h] = (q[seq_rows(s), h * HEAD_PAD:(h + 1) * HEAD_PAD] * qtab).astype(jnp.bfloat16)
        sbz_out[s] = sbz[seq_rows(s)]


def _mod_row(mod_ref, ctx, dec, i):
    row = jnp.where(i < dec.first, ctx.cond_row(i), dec.cond_row(i))
    return mod_ref[pl.ds(row, 1), :]


def _front_kernel(l, ctx, dec, n_cast, n_aliased, *refs):
    n_in = 6
    ctx_in, dec_in = refs[0:n_in], refs[n_in:2 * n_in]
    mod_ref = refs[2 * n_in]
    n_w = 2 * n_in + 1 + _N_FRONT_WEIGHTS
    weights = list(refs[2 * n_in + 1:n_w])
    outs = refs[n_w + n_cast + n_aliased:-3]
    ctx_out, dec_out = outs[0:7], outs[7:13]
    wuq_scr, scratch = refs[-3], refs[-2:]
    i = pl.program_id(0)
    _cast_chunks(refs[n_w:n_w + n_cast], outs[13:])

    place_ref = weights.pop()
    wuq_src = weights[_WUQ_SLOT]

    @pl.when(i == 0)
    def _():
        wuq_scr[...] = _dot(wuq_src[...].astype(jnp.bfloat16), place_ref[...]).astype(jnp.bfloat16)

    weights[_WUQ_SLOT] = wuq_scr

    @pl.when(i < dec.first)
    def _():
        _front_tile(l, ctx.seq, ctx.tile(i), *ctx_in, _mod_row(mod_ref, ctx, dec, i), weights, ctx_out, scratch)

    @pl.when(i >= dec.first)
    def _():
        _front_tile(l, dec.seq, dec.tile(i), *dec_in, _mod_row(mod_ref, ctx, dec, i), weights, dec_out, scratch)


def _front(l, x_prompt, x_sample, mod, w, w_front, ctx_tabs, dec_tabs, state_t, big_f32, cast_windows):
    ctx, dec = _passes(x_prompt, x_sample, FRONT_TILE, max(1, FRONT_TILE // x_prompt.shape[1]))
    n_steps = ctx.steps + dec.steps

    def pass_in_specs(p):
        tm = p.tm
        hb = tm // HALO
        n_hblk = p.seq // HALO
        tab = pl.BlockSpec((tm, LANES), lambda i: (p.tile(i), 0))
        return [
            pl.BlockSpec((p.group, tm, D_MODEL), lambda i: (p.batch(i), p.tile(i), 0)),
            pl.BlockSpec((p.group, HALO, D_MODEL), lambda i: (p.batch(i), jnp.maximum(p.tile(i) * hb - 1, 0), 0)),
            pl.BlockSpec((p.group, HALO, D_MODEL),
                         lambda i: (p.batch(i), jnp.minimum((p.tile(i) + 1) * hb, n_hblk - 1), 0)),
            tab, tab, tab,
        ]

    def pass_out(p, with_latent):
        tm = p.tm

        def blocks(n):
            return (jax.ShapeDtypeStruct((p.nb, n, p.seq, LANES), jnp.bfloat16),
                    pl.BlockSpec((p.group, n, tm, LANES), lambda i: (p.batch(i), 0, p.tile(i), 0)))

        (head_shape, head_spec), (pair_shape, pair_spec) = blocks(N_HEADS), blocks(N_PAIRS)
        br_shape = jax.ShapeDtypeStruct((p.nb, p.seq, BRANCH_DIM), jnp.bfloat16)
        br_spec = pl.BlockSpec((p.group, tm, BRANCH_DIM), lambda i: (p.batch(i), p.tile(i), 0))
        shapes = [head_shape, head_shape, pair_shape] + [br_shape] * 3
        specs = [head_spec, head_spec, pair_spec] + [br_spec] * 3
        if with_latent:
            shapes.append(jax.ShapeDtypeStruct((p.nb, DEPTH, MLA_LATENT, p.seq), jnp.float32))
            if state_t is None:
                specs.append(pl.BlockSpec((p.group, DEPTH, MLA_LATENT, tm), lambda i: (p.batch(i), 0, 0, p.tile(i))))
            else:
                specs.append(pl.BlockSpec((p.group, None, MLA_LATENT, tm), lambda i: (p.batch(i), l, 0, p.tile(i))))
        return shapes, specs

    weight_specs = [
        _whole_spec(w["g_pre"]),
        _row_block_spec(_FRONT_ROWS, 0),
        _whole_spec(w["conv_w"]),
        _whole_spec(w["conv_b"]),
        _whole_spec(w["g_q"]),
        _const_spec(l, (Q_LORA, w["w_uq"].shape[2])),
        _whole_spec(w["g_kv"]),
        _const_spec(l, (KV_LORA, HP)),
        _const_spec(l, (len(POOL_WINDOWS), POOL_GROUP, POOL_GROUP)),
        _whole_spec(w["pool_scale"]),
        _whole_spec(w["q_place"]),
    ]
    assert len(weight_specs) == _N_FRONT_WEIGHTS
    ctx_shapes, ctx_specs = pass_out(ctx, True)
    dec_shapes, dec_specs = pass_out(dec, False)
    in_specs = pass_in_specs(ctx) + pass_in_specs(dec) + [_const_spec(l, mod.shape[1:])] + weight_specs
    args = [x_prompt, x_prompt, x_prompt, *ctx_tabs, x_sample, x_sample, x_sample, *dec_tabs, mod,
            w["g_pre"], w_front, w["conv_w"], w["conv_b"], w["g_q"], w["w_uq"], w["g_kv"],
            w["w_ukv"], w["pool_w"], w["pool_scale"], w["q_place"]]
    cast_in, cast_out, cast_shapes = _cast_specs(l, cast_windows, n_steps)
    in_specs += cast_in
    args += [big_f32[src] for src, _, _ in cast_windows]
    aliases = {}
    if state_t is not None:
        aliases = {len(args): len(ctx_shapes) - 1}
        in_specs.append(pl.BlockSpec(memory_space=pl.ANY))
        args.append(state_t)
    outs = pl.pallas_call(
        functools.partial(_front_kernel, l, ctx, dec, len(cast_windows), len(aliases)),
        grid=(n_steps,),
        in_specs=in_specs,
        out_specs=ctx_specs + dec_specs + cast_out,
        out_shape=ctx_shapes + dec_shapes + cast_shapes,
        scratch_shapes=[pltpu.VMEM((Q_LORA, HP), jnp.bfloat16)]
        + [pltpu.VMEM((max(p.group * (p.tm + HALO) + HALO for p in (ctx, dec)), BRANCH_DIM), jnp.float32)] * 2,
        input_output_aliases=aliases,
        compiler_params=_params(),
        name="front",
    )(*args)
    return outs[:7], outs[7:13], tuple(outs[13:])


_N_BACK_WEIGHTS = 5


def _back_tile(l, x_ref, q_ref, k_ref, v_ref, cache, ya_ref, yc_ref, sbz_ref, mod_row, weights, out_ref):
    gpre_ref, wg_ref, wbr_ref, wo_ref, gpost_ref = weights
    g_pre, g_post = gpre_ref[l:l + 1, :], gpost_ref[l:l + 1, :]
    x = x_ref[...]

    def with_ones(v):
        return jnp.concatenate([v, jnp.ones_like(v)], axis=1)

    low_half = lax.broadcasted_iota(jnp.int32, (x.shape[0], LANES), 1) < V_HEAD
    pairs = []
    for p in range(N_PAIRS):
        v_ext = with_ones(v_ref[p])
        if cache is not None:
            kc_ref, vc_ref = cache
            vc_ext = with_ones(vc_ref[p])
        halves = []
        for h in (2 * p, 2 * p + 1):
            q = q_ref[h]
            s = _dot_nt(q, k_ref[h])
            m = jnp.max(s, axis=-1, keepdims=True)
            if cache is not None:
                sc = _dot_nt(q, kc_ref[h])
                m = jnp.maximum(m, jnp.max(sc, axis=-1, keepdims=True))
            o = _dot(jnp.exp2((s - m).astype(jnp.bfloat16)), v_ext)
            if cache is not None:
                o = o + _dot(jnp.exp2((sc - m).astype(jnp.bfloat16)), vc_ext)
            halves.append(o[:, 0:LANES] / o[:, LANES:2 * LANES])
        pairs.append(jnp.where(low_half, halves[0], halves[1]))
    attn = jnp.concatenate(pairs, axis=1)
    yb = (sbz_ref[...].astype(jnp.float32) * attn).astype(jnp.bfloat16)

    hn = _modulated_norm(x, g_pre, mod_row).astype(jnp.bfloat16)
    merged = None
    for n, y in enumerate((ya_ref[...], yb, yc_ref[...])):
        r0 = _R_MERGE - _BACK_START + n * D_MODEL
        gate = _sigmoid(_dot_nt(hn, wg_ref[r0:r0 + D_MODEL, :]))
        term = gate * _dot(y, wbr_ref[n * BRANCH_DIM:(n + 1) * BRANCH_DIM, :])
        merged = term if merged is None else merged + term
    out = _rms(_dot(merged.astype(jnp.bfloat16), wo_ref[...]), g_post)
    out_ref[...] = x + mod_row[:, 2 * D_MODEL:3 * D_MODEL] * out


def _back_kernel(l, ctx, dec, n_cast, *refs):
    x_c, q_c, k_c, v_c, ya_c, yc_c, sbz_c = refs[0:7]
    x_d, q_d, k_d, v_d, kc_ref, vc_ref, ya_d, yc_d, sbz_d = refs[7:16]
    mod_ref = refs[16]
    n_w = 17 + _N_BACK_WEIGHTS
    weights = refs[17:n_w]
    out_c, out_d = refs[n_w + n_cast:n_w + n_cast + 2]
    i = pl.program_id(0)
    _cast_chunks(refs[n_w:n_w + n_cast], refs[n_w + n_cast + 2:])

    @pl.when(i < dec.first)
    def _():
        _back_tile(l, x_c, q_c, k_c, v_c, None, ya_c, yc_c, sbz_c, _mod_row(mod_ref, ctx, dec, i), weights, out_c)

    @pl.when(i >= dec.first)
    def _():
        _back_tile(l, x_d, q_d, k_d, v_d, (kc_ref, vc_ref), ya_d, yc_d, sbz_d, _mod_row(mod_ref, ctx, dec, i),
                   weights, out_d)


def _back(l, x_prompt, x_sample, mod, w, big_bf, ctx_fr, dec_fr, kc, vc, big_f32, cast_windows):
    ctx, dec = _passes(x_prompt, x_sample, BACK_TILE)
    past = kc.shape[3]
    n_steps = ctx.steps + dec.steps

    def pass_specs(p):
        tq = p.tm
        tile = pl.BlockSpec((None, tq, D_MODEL), lambda i: (p.batch(i), p.tile(i), 0))
        q = pl.BlockSpec((None, N_HEADS, tq, HEAD_PAD), lambda i: (p.batch(i), 0, p.tile(i), 0))
        k = pl.BlockSpec((None, N_HEADS, p.seq, LANES), lambda i: (p.batch(i), 0, 0, 0))
        v = pl.BlockSpec((None, N_PAIRS, p.seq, LANES), lambda i: (p.batch(i), 0, 0, 0))
        br = pl.BlockSpec((None, tq, BRANCH_DIM), lambda i: (p.batch(i), p.tile(i), 0))
        return tile, q, k, v, br

    tile_c, q_c, k_c, v_c, br_c = pass_specs(ctx)
    tile_d, q_d, k_d, v_d, br_d = pass_specs(dec)
    kc_spec = pl.BlockSpec((None, None, N_HEADS, past, LANES), lambda i: (l, dec.batch(i), 0, 0, 0))
    vc_spec = pl.BlockSpec((None, None, N_PAIRS, past, LANES), lambda i: (l, dec.batch(i), 0, 0, 0))
    in_specs = ([tile_c, q_c, k_c, v_c, br_c, br_c, br_c]
                + [tile_d, q_d, k_d, v_d, kc_spec, vc_spec, br_d, br_d, br_d]
                + [_const_spec(l, mod.shape[1:]),
                   _whole_spec(w["g_pre"]),
                   _row_block_spec(_BACK_ROWS, 0),
                   _row_block_spec(3 * BRANCH_DIM, 0),
                   _row_block_spec(D_MODEL, 0),
                   _whole_spec(w["g_post"])])
    q_cx, k_cx, v_cx, ya_cx, yc_cx, sbz_cx = ctx_fr
    q_dx, k_dx, v_dx, ya_dx, yc_dx, sbz_dx = dec_fr
    args = [x_prompt, q_cx, k_cx, v_cx, ya_cx, yc_cx, sbz_cx,
            x_sample, q_dx, k_dx, v_dx, kc, vc, ya_dx, yc_dx, sbz_dx,
            mod, w["g_pre"], *big_bf, w["g_post"]]
    out_specs = [tile_c, tile_d]
    out_shape = [jax.ShapeDtypeStruct(x_prompt.shape, jnp.float32), jax.ShapeDtypeStruct(x_sample.shape, jnp.float32)]
    cast_in, cast_out, cast_shapes = _cast_specs(l + 1, cast_windows, n_steps)
    in_specs += cast_in
    args += [big_f32[src] for src, _, _ in cast_windows]
    out_specs += cast_out
    out_shape += cast_shapes
    outs = pl.pallas_call(
        functools.partial(_back_kernel, l, ctx, dec, len(cast_windows)),
        grid=(n_steps,),
        in_specs=in_specs,
        out_specs=out_specs,
        out_shape=out_shape,
        compiler_params=_params(),
        name="back",
    )(*args)
    return outs[0], outs[1], tuple(outs[2:])


def _rope_swap_perm():
    quarter = QK_ROPE // 4
    idx = np.arange(QK_ROPE).reshape(2, 2, quarter)
    return idx[:, ::-1, :].reshape(-1)


def _rope_tables(n_tokens, rotate):
    if rotate:
        t = np.arange(n_tokens)
        pos = np.stack([t // GRID_W, t % GRID_W], axis=1).astype(np.float64)
        axis_dim = QK_ROPE // 2
        inv = 1.0 / (ROPE_THETA ** (np.arange(0, axis_dim, 2, dtype=np.float64) / axis_dim))
        ang = pos[:, :, None] * inv
        cos, sin = np.cos(ang), np.sin(ang)
        cfull = np.stack([cos, cos], axis=2).reshape(n_tokens, QK_ROPE)
        sfull = np.stack([-sin, sin], axis=2).reshape(n_tokens, QK_ROPE)
    else:
        cfull = np.ones((n_tokens, QK_ROPE))
        sfull = np.zeros((n_tokens, QK_ROPE))
    scale = (QK_NOPE + QK_ROPE) ** -0.5 * np.log2(np.e)
    qtab = np.concatenate([np.full((n_tokens, QK_NOPE), scale), cfull * scale, sfull * scale], axis=1)
    pad = np.zeros((n_tokens, LANES - QK_ROPE))
    tabs = (qtab, np.concatenate([cfull, pad], axis=1), np.concatenate([sfull, pad], axis=1))
    return tuple(jnp.asarray(tab, jnp.float32) for tab in tabs)


def _prepare_weights(g_pre, g_post, conv_w, conv_b, g_q, w_uq, g_kv, w_ukv, pool_w, pool_scale):
    perm = _rope_swap_perm()
    head_in = QK_NOPE + QK_ROPE
    place = np.zeros((N_HEADS * head_in, HP), np.float32)
    for h in range(N_HEADS):
        for c in range(head_in):
            place[h * head_in + c, h * HEAD_PAD + c] = 1.0
        for c in range(QK_ROPE):
            place[h * head_in + QK_NOPE + perm[c], h * HEAD_PAD + head_in + c] = 1.0
    assert w_uq.shape == (DEPTH, Q_LORA, N_HEADS * head_in)
    assert w_ukv.shape == (DEPTH, KV_LORA, HP) and QK_NOPE == V_HEAD and QK_NOPE + V_HEAD == LANES

    return {
        "g_pre": g_pre, "g_post": g_post, "conv_w": jnp.swapaxes(conv_w, 0, 1), "conv_b": conv_b,
        "g_q": g_q, "g_kv": g_kv, "pool_scale": pool_scale,
        "w_uq": w_uq, "q_place": jnp.asarray(place, jnp.bfloat16), "w_ukv": w_ukv, "pool_w": pool_w,
    }


def kernel(x_prompt, x_sample, cache_mla_latent, c, c_ctx, w_mod, b_mod, g_pre, g_post, w_in, conv_w, conv_b,
           g_q, w_uq, g_kv, w_ukv, pool_w, pool_scale, w_branch, w_o):
    n_dec = x_sample.shape[0]
    assert 1 + n_dec <= COND_ROWS
    assert w_in.shape == (DEPTH, D_MODEL, W_IN_COLS)
    w = _prepare_weights(g_pre, g_post, conv_w, conv_b, g_q, w_uq, g_kv, w_ukv, pool_w, pool_scale)
    big_f32 = (jnp.swapaxes(w_in, 1, 2), w_branch.reshape(DEPTH, 3 * BRANCH_DIM, D_MODEL), w_o)
    cache_t = jnp.swapaxes(cache_mla_latent, 2, 3)
    mod, w_front, kc, vc = _prologue(c_ctx, c, w_mod, b_mod, big_f32, cache_t, w_ukv)
    ctx_tabs = _rope_tables(x_prompt.shape[1], rotate=False)
    dec_tabs = _rope_tables(x_sample.shape[1], rotate=True)

    back_windows = (_W_BACK, _W_BRANCH, _W_OUT)
    h, hs, state_t, big_bf = x_prompt, x_sample, None, None
    for l in range(DEPTH):
        ctx_fr, dec_fr, cast = _front(l, h, hs, mod, w, w_front, ctx_tabs, dec_tabs, state_t, big_f32,
                                      back_windows if big_bf is None else ())
        state_t = ctx_fr[6]
        big_bf = cast if big_bf is None else big_bf
        next_windows = (_W_FRONT,) + back_windows if l + 1 < DEPTH else ()
        h, hs, cast = _back(l, h, hs, mod, w, big_bf, ctx_fr[:6], dec_fr, kc, vc, big_f32, next_windows)
        if next_windows:
            w_front, big_bf = cast[0], cast[1:]
    return (h, hs, jnp.swapaxes(state_t, 2, 3))
```

```python
import functools

import numpy as np
import jax
import jax.numpy as jnp
from jax import lax
from jax.experimental import pallas as pl
from jax.experimental.pallas import tpu as pltpu

D_MODEL = 1024
DEPTH = 2
GRID_W = 64
EPS = 1e-6
BRANCH_DIM = 512
N_HEADS = 8
QK_NOPE = 64
QK_ROPE = 32
V_HEAD = 64
Q_LORA = 384
KV_LORA = 256
MLA_LATENT = KV_LORA + QK_ROPE
ROPE_THETA = 10000.0
POOL_WINDOWS = (2, 4, 8, 16)
POOL_GROUP = 128

LANES = 128
HEAD_PAD = LANES
HP = N_HEADS * HEAD_PAD
N_PAIRS = N_HEADS * V_HEAD // LANES
FRONT_TILE = 512
BACK_TILE = 256
HALO = 16
COND_ROWS = 8
VMEM_LIMIT = 56 * 1024 * 1024

_R_AB = 0
_R_ACX = BRANCH_DIM
_R_MAIN = 3 * BRANCH_DIM
_MAIN_ROWS = BRANCH_DIM + Q_LORA + KV_LORA + LANES
_R_BZ = 4 * BRANCH_DIM + Q_LORA + MLA_LATENT
_R_CU = _R_BZ + BRANCH_DIM
_R_CZ = _R_CU + BRANCH_DIM
_R_MERGE = _R_CZ + BRANCH_DIM
W_IN_COLS = _R_MERGE + 3 * D_MODEL
_FRONT_ROWS = -(-_R_MERGE // LANES) * LANES
_QD0 = BRANCH_DIM
_CKV0 = _QD0 + Q_LORA
_KR0 = _CKV0 + KV_LORA
_LAT_EXT = KV_LORA + LANES


def _dot(a, b):
    return jnp.dot(a, b, preferred_element_type=jnp.float32)


def _dot_nt(a, b):
    return lax.dot_general(a, b, (((1,), (1,)), ((), ())), preferred_element_type=jnp.float32)


def _rms(x, g):
    return x * lax.rsqrt(jnp.mean(x * x, axis=-1, keepdims=True) + EPS) * g


def _sigmoid(x):
    return 0.5 * jnp.tanh(0.5 * x) + 0.5


def _silu(x):
    return x * _sigmoid(x)


def _modulated_norm(x, g_pre, mod_row):
    shift = mod_row[:, 0:D_MODEL]
    scale = mod_row[:, D_MODEL:2 * D_MODEL]
    return _rms(x, g_pre) * (1.0 + scale) + shift


def _params(n_grid_dims=1):
    return pltpu.CompilerParams(dimension_semantics=("arbitrary",) * n_grid_dims, vmem_limit_bytes=VMEM_LIMIT)


_SRC_W_IN_T, _SRC_W_BRANCH, _SRC_W_O = 0, 1, 2
_BACK_START = (_R_MERGE // 1024) * 1024
_W_FRONT = (_SRC_W_IN_T, 0, _FRONT_ROWS)
_W_BACK = (_SRC_W_IN_T, _BACK_START, W_IN_COLS - _BACK_START)
_W_BRANCH = (_SRC_W_BRANCH, 0, 3 * BRANCH_DIM)
_W_OUT = (_SRC_W_O, 0, D_MODEL)
_BACK_ROWS = _W_BACK[2]


def _chunk_rows(first_row, rows, n_steps):
    if first_row == 0:
        return pl.cdiv(pl.cdiv(rows, n_steps), 16) * 16
    per = LANES
    while first_row % per or pl.cdiv(rows, per) > n_steps:
        per *= 2
        assert per <= first_row
    return per


def _cast_specs(layer, windows, n_steps):
    in_specs, out_specs, out_shapes = [], [], []
    for _, first_row, rows in windows:
        per = _chunk_rows(first_row, rows, n_steps)
        n_blocks = pl.cdiv(rows, per)
        first_blk = first_row // per

        def in_map(i, first_blk=first_blk, last=n_blocks - 1):
            return (layer, first_blk + jnp.minimum(i, last), 0)

        def out_map(i, last=n_blocks - 1):
            return (jnp.minimum(i, last), 0)

        in_specs.append(pl.BlockSpec((None, per, D_MODEL), in_map))
        out_specs.append(pl.BlockSpec((per, D_MODEL), out_map))
        out_shapes.append(jax.ShapeDtypeStruct((rows, D_MODEL), jnp.bfloat16))
    return in_specs, out_specs, out_shapes


def _cast_chunks(in_refs, out_refs):
    for src, dst in zip(in_refs, out_refs):
        dst[...] = src[...].astype(jnp.bfloat16)


def _store_keys_values(k_refs, v_refs, ckv_bf, wukv_ref, kr):
    kv = _dot(ckv_bf, wukv_ref[...].astype(jnp.bfloat16))
    k_rope = pltpu.roll(kr, QK_NOPE, 1) + pltpu.roll(kr, QK_NOPE + QK_ROPE, 1)
    low_half = lax.broadcasted_iota(jnp.int32, kr.shape, 1) < QK_NOPE
    blocks = [kv[:, h * LANES:(h + 1) * LANES] for h in range(N_HEADS)]
    per_seq = kr.shape[0] // len(k_refs)
    for h in range(N_HEADS):
        key = jnp.where(low_half, blocks[h], k_rope).astype(jnp.bfloat16)
        for s, k_ref in enumerate(k_refs):
            k_ref[h] = key[s * per_seq:(s + 1) * per_seq]
    for p in range(N_PAIRS):
        even_v = pltpu.roll(blocks[2 * p], LANES - V_HEAD, 1)
        value = jnp.where(low_half, even_v, blocks[2 * p + 1]).astype(jnp.bfloat16)
        for s, v_ref in enumerate(v_refs):
            v_ref[p] = value[s * per_seq:(s + 1) * per_seq]


def _cache_kv_kernel(lat_t_ref, wukv_ref, k_ref, v_ref):
    lat_t = lat_t_ref[...]
    pad = jnp.zeros((_LAT_EXT - MLA_LATENT, lat_t.shape[1]), jnp.float32)
    lat = jnp.concatenate([lat_t, pad], axis=0).T
    _store_keys_values([k_ref], [v_ref], lat[:, 0:KV_LORA].astype(jnp.bfloat16), wukv_ref, lat[:, KV_LORA:_LAT_EXT])


_MOD_COL_TILES = 4


def _prologue_kernel(n_cache_jobs, cctx_ref, c_ref, wmod_ref, bmod_ref, wsrc_ref, lat_t_ref, wukv_ref,
                     mod_ref, wfront_ref, kc_ref, vc_ref):
    i = pl.program_id(0)
    n_pad = COND_ROWS - 1 - c_ref.shape[0]
    rows = [cctx_ref[...], c_ref[...]] + ([jnp.zeros((n_pad, D_MODEL), jnp.float32)] if n_pad else [])
    h = _silu(jnp.concatenate(rows, axis=0)).astype(jnp.bfloat16)
    mod_ref[...] = _dot(h, wmod_ref[...].astype(jnp.bfloat16)) + bmod_ref[pl.ds(i // _MOD_COL_TILES, 1), :]
    _cast_chunks([wsrc_ref], [wfront_ref])

    @pl.when(i < n_cache_jobs)
    def _():
        _cache_kv_kernel(lat_t_ref, wukv_ref, kc_ref, vc_ref)


def _prologue(c_ctx, c, w_mod, b_mod, big_f32, cache_t, w_ukv):
    n_steps = DEPTH * _MOD_COL_TILES
    col_tile = 3 * D_MODEL // _MOD_COL_TILES
    nb, _, _, past = cache_t.shape
    n_cache_jobs = DEPTH * nb
    assert n_cache_jobs <= n_steps

    def mod_block(i):
        return (i // _MOD_COL_TILES, 0, i % _MOD_COL_TILES)

    def cache_layer(i):
        return jnp.minimum(i, n_cache_jobs - 1) // nb

    def cache_request(i):
        return jnp.minimum(i, n_cache_jobs - 1) % nb

    def kv(n):
        return (jax.ShapeDtypeStruct((DEPTH, nb, n, past, LANES), jnp.bfloat16),
                pl.BlockSpec((None, None, n, past, LANES), lambda i: (cache_layer(i), cache_request(i), 0, 0, 0)))

    (k_shape, k_spec), (v_shape, v_spec) = kv(N_HEADS), kv(N_PAIRS)
    cast_in, cast_out, cast_shapes = _cast_specs(0, [_W_FRONT], n_steps)
    return pl.pallas_call(
        functools.partial(_prologue_kernel, n_cache_jobs),
        grid=(n_steps,),
        in_specs=[
            pl.BlockSpec((1, D_MODEL), lambda i: (0, 0)),
            pl.BlockSpec(c.shape, lambda i: (0, 0)),
            pl.BlockSpec((None, D_MODEL, col_tile), mod_block),
            pl.BlockSpec((DEPTH, col_tile), lambda i: (0, i % _MOD_COL_TILES)),
            cast_in[0],
            pl.BlockSpec((None, None, MLA_LATENT, past), lambda i: (cache_request(i), cache_layer(i), 0, 0)),
            pl.BlockSpec((None, KV_LORA, HP), lambda i: (cache_layer(i), 0, 0)),
        ],
        out_specs=[pl.BlockSpec((None, COND_ROWS, col_tile), mod_block), cast_out[0], k_spec, v_spec],
        out_shape=[jax.ShapeDtypeStruct((DEPTH, COND_ROWS, 3 * D_MODEL), jnp.float32), cast_shapes[0],
                   k_shape, v_shape],
        compiler_params=_params(),
        name="prologue",
    )(c_ctx[None, :], c, w_mod, b_mod, big_f32[_SRC_W_IN_T], cache_t, w_ukv)


class _Pass:
    def __init__(self, x, tile, first_step, first_cond_row, shared_cond, group=1):
        self.nb, self.seq, _ = x.shape
        self.tm = min(tile, self.seq)
        assert self.seq % self.tm == 0
        self.nt = self.seq // self.tm
        self.group = group if self.nt == 1 and shared_cond else 1
        assert self.nb % self.group == 0
        self.steps = self.nb // self.group * self.nt
        self.first = first_step
        self.first_cond_row = first_cond_row
        self.shared_cond = shared_cond

    def local(self, i):
        return jnp.clip(i - self.first, 0, self.steps - 1)

    def batch(self, i):
        return self.local(i) // self.nt

    def tile(self, i):
        return self.local(i) % self.nt

    def cond_row(self, i):
        return self.first_cond_row if self.shared_cond else self.first_cond_row + self.batch(i)


def _passes(x_prompt, x_sample, tile, ctx_group=1):
    ctx = _Pass(x_prompt, tile, 0, 0, True, ctx_group)
    dec = _Pass(x_sample, tile, ctx.steps, 1, False)
    return ctx, dec


def _const_spec(l, shape):
    return pl.BlockSpec((None,) + shape, lambda i: (l,) + (0,) * len(shape), pipeline_mode=pl.Buffered(1))


def _whole_spec(arr):
    return pl.BlockSpec(arr.shape, lambda i: (0,) * arr.ndim, pipeline_mode=pl.Buffered(1))


def _row_block_spec(rows, block_index):
    return pl.BlockSpec((rows, D_MODEL), lambda i: (block_index, 0), pipeline_mode=pl.Buffered(1))


_N_FRONT_WEIGHTS = 11
_WUQ_SLOT = 5


def _front_tile(l, seq_len, j, x_ref, xp_ref, xn_ref, qtab_ref, kcos_ref, ksin_ref, mod_row, weights, outs, scratch):
    gpre_ref, w_ref, convw_ref, convb_ref, gq_ref, wuq_ref, gkv_ref, wukv_ref, poolw_ref, pscale_ref = weights
    g_pre, conv_b, g_q, g_kv, p_scale = (r[l:l + 1, :] for r in (gpre_ref, convb_ref, gq_ref, gkv_ref, pscale_ref))
    conv_w = [convw_ref[tap, l:l + 1, :] for tap in range(3)]
    q_out, k_out, v_out, ya_out, yc_out, sbz_out = outs[:6]
    u_scr, cu_scr = scratch
    n_seq, tm = x_ref.shape[0], x_ref.shape[1]
    last_j = seq_len // tm - 1
    stride = tm + HALO

    def seq_rows(s):
        return slice(s * tm, (s + 1) * tm)

    if last_j == 0:
        hn = _modulated_norm(x_ref[...].reshape(n_seq * tm, D_MODEL), g_pre, mod_row).astype(jnp.bfloat16)
        acx = _dot_nt(hn, w_ref[_R_ACX:_R_ACX + 2 * BRANCH_DIM, :])
        u_all = acx[:, 0:BRANCH_DIM] * acx[:, BRANCH_DIM:2 * BRANCH_DIM]
        cu_all = _dot_nt(hn, w_ref[_R_CU:_R_CU + BRANCH_DIM, :])
        zeros = jnp.zeros((HALO, BRANCH_DIM), jnp.float32)
        for scr, val in ((u_scr, u_all), (cu_scr, cu_all)):
            for s in range(n_seq + 1):
                scr[s * stride:s * stride + HALO, :] = zeros
            for s in range(n_seq):
                scr[s * stride + HALO:s * stride + HALO + tm, :] = val[seq_rows(s)]
    else:
        assert n_seq == 1
        x_ext = jnp.concatenate([xp_ref[0], x_ref[0], xn_ref[0]], axis=0)
        hn_ext = _modulated_norm(x_ext, g_pre, mod_row).astype(jnp.bfloat16)
        hn = hn_ext[HALO:HALO + tm]
        acx = _dot_nt(hn_ext, w_ref[_R_ACX:_R_ACX + 2 * BRANCH_DIM, :])
        u_ext = acx[:, 0:BRANCH_DIM] * acx[:, BRANCH_DIM:2 * BRANCH_DIM]
        cu_ext = _dot_nt(hn_ext, w_ref[_R_CU:_R_CU + BRANCH_DIM, :])
        head, body, tail = slice(0, HALO), slice(HALO, HALO + tm), slice(HALO + tm, 2 * HALO + tm)
        for scr, ext in ((u_scr, u_ext), (cu_scr, cu_ext)):
            scr[head, :] = jnp.where(j > 0, ext[head], 0.0)
            scr[body, :] = ext[body]
            scr[tail, :] = jnp.where(j < last_j, ext[tail], 0.0)

    pm = _dot_nt(hn, w_ref[_R_MAIN:_R_MAIN + _MAIN_ROWS, :])
    def conv_of(base):
        return (u_scr[base + HALO - 1:base + HALO - 1 + tm, :] * conv_w[0]
                + u_scr[base + HALO:base + HALO + tm, :] * conv_w[1]
                + u_scr[base + HALO + 1:base + HALO + 1 + tm, :] * conv_w[2]
                + conv_b)

    conv = jnp.concatenate([conv_of(s * stride) for s in range(n_seq)], axis=0)
    ya = _silu(pm[:, 0:BRANCH_DIM]) * (_dot_nt(hn, w_ref[_R_AB:_R_AB + BRANCH_DIM, :]) * conv)
    for s in range(n_seq):
        ya_out[s] = ya[seq_rows(s)].astype(jnp.bfloat16)

    t = j * tm + lax.broadcasted_iota(jnp.int32, (tm, POOL_GROUP), 0)

    def pooled_of(base, gi, win):
        half = win // 2
        cols = slice(gi * POOL_GROUP, (gi + 1) * POOL_GROUP)
        shifts = [1 << b for b in range(win.bit_length() - 1)]
        first = base + HALO - half
        total = cu_scr[first:first + tm + sum(shifts), cols]
        for shift in shifts:
            total = total[:-shift] + total[shift:]
        edge = 8
        assert half <= edge

        def clipped_mean(rows):
            count = jnp.minimum(t[rows] + half, seq_len) - jnp.maximum(t[rows] - half, 0)
            return total[rows] / count.astype(jnp.float32)

        mean = jnp.concatenate([clipped_mean(slice(0, edge)), total[edge:tm - edge] * (1.0 / win),
                                clipped_mean(slice(tm - edge, tm))], axis=0)
        return (mean - cu_scr[base + HALO:base + HALO + tm, cols]).astype(jnp.bfloat16)

    pooled = [jnp.concatenate([pooled_of(s * stride, gi, win) for s in range(n_seq)], axis=0)
              for gi, win in enumerate(POOL_WINDOWS)]
    zero = jnp.zeros((POOL_GROUP, POOL_GROUP), jnp.float32)
    mixed = []
    for n in range(len(pooled) // 2):
        both = jnp.concatenate([jnp.concatenate([poolw_ref[2 * n], zero], axis=1),
                                jnp.concatenate([zero, poolw_ref[2 * n + 1]], axis=1)], axis=0).astype(jnp.bfloat16)
        mixed.append(_dot(jnp.concatenate(pooled[2 * n:2 * n + 2], axis=1), both))
    mixed = jnp.concatenate(mixed, axis=1) * p_scale
    c_z = _dot_nt(hn, w_ref[_R_CZ:_R_CZ + BRANCH_DIM, :])
    yc = (_silu(c_z) * mixed).astype(jnp.bfloat16)
    for s in range(n_seq):
        yc_out[s] = yc[seq_rows(s)]

    qn = _rms(pm[:, _QD0:_QD0 + Q_LORA], g_q).astype(jnp.bfloat16)
    q = _dot(qn, wuq_ref[...])
    qtab = qtab_ref[...]
    ckv = _rms(pm[:, _CKV0:_CKV0 + KV_LORA], g_kv)
    kr = pm[:, _KR0:_KR0 + LANES]
    if len(outs) > 6:
        for s in range(n_seq):
            lat_out = outs[6].at[s]
            if len(lat_out.shape) == 3:
                for later in range(1, lat_out.shape[0]):
                    lat_out[later] = jnp.zeros(lat_out.shape[1:], jnp.float32)
                lat_out = lat_out.at[0]
            lat_out[0:KV_LORA, :] = ckv[seq_rows(s)].T
            lat_out[KV_LORA:MLA_LATENT, :] = kr[seq_rows(s)].T[0:QK_ROPE, :]
    lane = lax.broadcasted_iota(jnp.int32, kr.shape, 1)
    quarter = QK_ROPE // 4
    partner = jnp.where(lane % (2 * quarter) < quarter,
                        pltpu.roll(kr, LANES - quarter, 1), pltpu.roll(kr, quarter, 1))
    kcos, ksin = (jnp.concatenate([tab[...]] * n_seq, axis=0) for tab in (kcos_ref, ksin_ref))
    kr_rot = jnp.where(lane < QK_ROPE, kr * kcos + partner * ksin, 0.0)
    _store_keys_values([k_out.at[s] for s in range(n_seq)], [v_out.at[s] for s in range(n_seq)],
                       ckv.astype(jnp.bfloat16), wukv_ref, kr_rot)
    sbz = _silu(_dot_nt(hn, w_ref[_R_BZ:_R_BZ + BRANCH_DIM, :])).astype(jnp.bfloat16)
    for s in range(n_seq):
        for h in range(N_HEADS):
            q_out[s, h] = (q[seq_rows(s), h * HEAD_PAD:(h + 1) * HEAD_PAD] * qtab).astype(jnp.bfloat16)
        sbz_out[s] = sbz[seq_rows(s)]


def _mod_row(mod_ref, ctx, dec, i):
    row = jnp.where(i < dec.first, ctx.cond_row(i), dec.cond_row(i))
    return mod_ref[pl.ds(row, 1), :]


def _front_kernel(l, ctx, dec, n_cast, n_aliased, *refs):
    n_in = 6
    ctx_in, dec_in = refs[0:n_in], refs[n_in:2 * n_in]
    mod_ref = refs[2 * n_in]
    n_w = 2 * n_in + 1 + _N_FRONT_WEIGHTS
    weights = list(refs[2 * n_in + 1:n_w])
    outs = refs[n_w + n_cast + n_aliased:-3]
    ctx_out, dec_out = outs[0:7], outs[7:13]
    wuq_scr, scratch = refs[-3], refs[-2:]
    i = pl.program_id(0)
    _cast_chunks(refs[n_w:n_w + n_cast], outs[13:])

    place_ref = weights.pop()
    wuq_src = weights[_WUQ_SLOT]

    @pl.when(i == 0)
    def _():
        wuq_scr[...] = _dot(wuq_src[...].astype(jnp.bfloat16), place_ref[...]).astype(jnp.bfloat16)

    weights[_WUQ_SLOT] = wuq_scr

    @pl.when(i < dec.first)
    def _():
        _front_tile(l, ctx.seq, ctx.tile(i), *ctx_in, _mod_row(mod_ref, ctx, dec, i), weights, ctx_out, scratch)

    @pl.when(i >= dec.first)
    def _():
        _front_tile(l, dec.seq, dec.tile(i), *dec_in, _mod_row(mod_ref, ctx, dec, i), weights, dec_out, scratch)


def _front(l, x_prompt, x_sample, mod, w, w_front, ctx_tabs, dec_tabs, state_t, big_f32, cast_windows):
    ctx, dec = _passes(x_prompt, x_sample, FRONT_TILE, max(1, FRONT_TILE // x_prompt.shape[1]))
    n_steps = ctx.steps + dec.steps

    def pass_in_specs(p):
        tm = p.tm
        hb = tm // HALO
        n_hblk = p.seq // HALO
        tab = pl.BlockSpec((tm, LANES), lambda i: (p.tile(i), 0))
        return [
            pl.BlockSpec((p.group, tm, D_MODEL), lambda i: (p.batch(i), p.tile(i), 0)),
            pl.BlockSpec((p.group, HALO, D_MODEL), lambda i: (p.batch(i), jnp.maximum(p.tile(i) * hb - 1, 0), 0)),
            pl.BlockSpec((p.group, HALO, D_MODEL),
                         lambda i: (p.batch(i), jnp.minimum((p.tile(i) + 1) * hb, n_hblk - 1), 0)),
            tab, tab, tab,
        ]

    def pass_out(p, with_latent):
        tm = p.tm

        def blocks(n):
            return (jax.ShapeDtypeStruct((p.nb, n, p.seq, LANES), jnp.bfloat16),
                    pl.BlockSpec((p.group, n, tm, LANES), lambda i: (p.batch(i), 0, p.tile(i), 0)))

        (head_shape, head_spec), (pair_shape, pair_spec) = blocks(N_HEADS), blocks(N_PAIRS)
        br_shape = jax.ShapeDtypeStruct((p.nb, p.seq, BRANCH_DIM), jnp.bfloat16)
        br_spec = pl.BlockSpec((p.group, tm, BRANCH_DIM), lambda i: (p.batch(i), p.tile(i), 0))
        shapes = [head_shape, head_shape, pair_shape] + [br_shape] * 3
        specs = [head_spec, head_spec, pair_spec] + [br_spec] * 3
        if with_latent:
            shapes.append(jax.ShapeDtypeStruct((p.nb, DEPTH, MLA_LATENT, p.seq), jnp.float32))
            if state_t is None:
                specs.append(pl.BlockSpec((p.group, DEPTH, MLA_LATENT, tm), lambda i: (p.batch(i), 0, 0, p.tile(i))))
            else:
                specs.append(pl.BlockSpec((p.group, None, MLA_LATENT, tm), lambda i: (p.batch(i), l, 0, p.tile(i))))
        return shapes, specs

    weight_specs = [
        _whole_spec(w["g_pre"]),
        _row_block_spec(_FRONT_ROWS, 0),
        _whole_spec(w["conv_w"]),
        _whole_spec(w["conv_b"]),
        _whole_spec(w["g_q"]),
        _const_spec(l, (Q_LORA, w["w_uq"].shape[2])),
        _whole_spec(w["g_kv"]),
        _const_spec(l, (KV_LORA, HP)),
        _const_spec(l, (len(POOL_WINDOWS), POOL_GROUP, POOL_GROUP)),
        _whole_spec(w["pool_scale"]),
        _whole_spec(w["q_place"]),
    ]
    assert len(weight_specs) == _N_FRONT_WEIGHTS
    ctx_shapes, ctx_specs = pass_out(ctx, True)
    dec_shapes, dec_specs = pass_out(dec, False)
    in_specs = pass_in_specs(ctx) + pass_in_specs(dec) + [_const_spec(l, mod.shape[1:])] + weight_specs
    args = [x_prompt, x_prompt, x_prompt, *ctx_tabs, x_sample, x_sample, x_sample, *dec_tabs, mod,
            w["g_pre"], w_front, w["conv_w"], w["conv_b"], w["g_q"], w["w_uq"], w["g_kv"],
            w["w_ukv"], w["pool_w"], w["pool_scale"], w["q_place"]]
    cast_in, cast_out, cast_shapes = _cast_specs(l, cast_windows, n_steps)
    in_specs += cast_in
    args += [big_f32[src] for src, _, _ in cast_windows]
    aliases = {}
    if state_t is not None:
        aliases = {len(args): len(ctx_shapes) - 1}
        in_specs.append(pl.BlockSpec(memory_space=pl.ANY))
        args.append(state_t)
    outs = pl.pallas_call(
        functools.partial(_front_kernel, l, ctx, dec, len(cast_windows), len(aliases)),
        grid=(n_steps,),
        in_specs=in_specs,
        out_specs=ctx_specs + dec_specs + cast_out,
        out_shape=ctx_shapes + dec_shapes + cast_shapes,
        scratch_shapes=[pltpu.VMEM((Q_LORA, HP), jnp.bfloat16)]
        + [pltpu.VMEM((max(p.group * (p.tm + HALO) + HALO for p in (ctx, dec)), BRANCH_DIM), jnp.float32)] * 2,
        input_output_aliases=aliases,
        compiler_params=_params(),
        name="front",
    )(*args)
    return outs[:7], outs[7:13], tuple(outs[13:])


_N_BACK_WEIGHTS = 5


def _back_tile(l, x_ref, q_ref, k_ref, v_ref, cache, ya_ref, yc_ref, sbz_ref, mod_row, weights, out_ref):
    gpre_ref, wg_ref, wbr_ref, wo_ref, gpost_ref = weights
    g_pre, g_post = gpre_ref[l:l + 1, :], gpost_ref[l:l + 1, :]
    x = x_ref[...]

    def with_ones(v):
        return jnp.concatenate([v, jnp.ones_like(v)], axis=1)

    low_half = lax.broadcasted_iota(jnp.int32, (x.shape[0], LANES), 1) < V_HEAD
    pairs = []
    for p in range(N_PAIRS):
        v_ext = with_ones(v_ref[p])
        if cache is not None:
            kc_ref, vc_ref = cache
            vc_ext = with_ones(vc_ref[p])
        halves = []
        for h in (2 * p, 2 * p + 1):
            q = q_ref[h]
            s = _dot_nt(q, k_ref[h])
            m = jnp.max(s, axis=-1, keepdims=True)
            if cache is not None:
                sc = _dot_nt(q, kc_ref[h])
                m = jnp.maximum(m, jnp.max(sc, axis=-1, keepdims=True))
            o = _dot(jnp.exp2((s - m).astype(jnp.bfloat16)), v_ext)
            if cache is not None:
                o = o + _dot(jnp.exp2((sc - m).astype(jnp.bfloat16)), vc_ext)
            halves.append(o[:, 0:LANES] / o[:, LANES:2 * LANES])
        pairs.append(jnp.where(low_half, halves[0], halves[1]))
    attn = jnp.concatenate(pairs, axis=1)
    yb = (sbz_ref[...].astype(jnp.float32) * attn).astype(jnp.bfloat16)

    hn = _modulated_norm(x, g_pre, mod_row).astype(jnp.bfloat16)
    merged = None
    for n, y in enumerate((ya_ref[...], yb, yc_ref[...])):
        r0 = _R_MERGE - _BACK_START + n * D_MODEL
        gate = _sigmoid(_dot_nt(hn, wg_ref[r0:r0 + D_MODEL, :]))
        term = gate * _dot(y, wbr_ref[n * BRANCH_DIM:(n + 1) * BRANCH_DIM, :])
        merged = term if merged is None else merged + term
    out = _rms(_dot(merged.astype(jnp.bfloat16), wo_ref[...]), g_post)
    out_ref[...] = x + mod_row[:, 2 * D_MODEL:3 * D_MODEL] * out


def _back_kernel(l, ctx, dec, n_cast, *refs):
    x_c, q_c, k_c, v_c, ya_c, yc_c, sbz_c = refs[0:7]
    x_d, q_d, k_d, v_d, kc_ref, vc_ref, ya_d, yc_d, sbz_d = refs[7:16]
    mod_ref = refs[16]
    n_w = 17 + _N_BACK_WEIGHTS
    weights = refs[17:n_w]
    out_c, out_d = refs[n_w + n_cast:n_w + n_cast + 2]
    i = pl.program_id(0)
    _cast_chunks(refs[n_w:n_w + n_cast], refs[n_w + n_cast + 2:])

    @pl.when(i < dec.first)
    def _():
        _back_tile(l, x_c, q_c, k_c, v_c, None, ya_c, yc_c, sbz_c, _mod_row(mod_ref, ctx, dec, i), weights, out_c)

    @pl.when(i >= dec.first)
    def _():
        _back_tile(l, x_d, q_d, k_d, v_d, (kc_ref, vc_ref), ya_d, yc_d, sbz_d, _mod_row(mod_ref, ctx, dec, i),
                   weights, out_d)


def _back(l, x_prompt, x_sample, mod, w, big_bf, ctx_fr, dec_fr, kc, vc, big_f32, cast_windows):
    ctx, dec = _passes(x_prompt, x_sample, BACK_TILE)
    past = kc.shape[3]
    n_steps = ctx.steps + dec.steps

    def pass_specs(p):
        tq = p.tm
        tile = pl.BlockSpec((None, tq, D_MODEL), lambda i: (p.batch(i), p.tile(i), 0))
        q = pl.BlockSpec((None, N_HEADS, tq, HEAD_PAD), lambda i: (p.batch(i), 0, p.tile(i), 0))
        k = pl.BlockSpec((None, N_HEADS, p.seq, LANES), lambda i: (p.batch(i), 0, 0, 0))
        v = pl.BlockSpec((None, N_PAIRS, p.seq, LANES), lambda i: (p.batch(i), 0, 0, 0))
        br = pl.BlockSpec((None, tq, BRANCH_DIM), lambda i: (p.batch(i), p.tile(i), 0))
        return tile, q, k, v, br

    tile_c, q_c, k_c, v_c, br_c = pass_specs(ctx)
    tile_d, q_d, k_d, v_d, br_d = pass_specs(dec)
    kc_spec = pl.BlockSpec((None, None, N_HEADS, past, LANES), lambda i: (l, dec.batch(i), 0, 0, 0))
    vc_spec = pl.BlockSpec((None, None, N_PAIRS, past, LANES), lambda i: (l, dec.batch(i), 0, 0, 0))
    in_specs = ([tile_c, q_c, k_c, v_c, br_c, br_c, br_c]
                + [tile_d, q_d, k_d, v_d, kc_spec, vc_spec, br_d, br_d, br_d]
                + [_const_spec(l, mod.shape[1:]),
                   _whole_spec(w["g_pre"]),
                   _row_block_spec(_BACK_ROWS, 0),
                   _row_block_spec(3 * BRANCH_DIM, 0),
                   _row_block_spec(D_MODEL, 0),
                   _whole_spec(w["g_post"])])
    q_cx, k_cx, v_cx, ya_cx, yc_cx, sbz_cx = ctx_fr
    q_dx, k_dx, v_dx, ya_dx, yc_dx, sbz_dx = dec_fr
    args = [x_prompt, q_cx, k_cx, v_cx, ya_cx, yc_cx, sbz_cx,
            x_sample, q_dx, k_dx, v_dx, kc, vc, ya_dx, yc_dx, sbz_dx,
            mod, w["g_pre"], *big_bf, w["g_post"]]
    out_specs = [tile_c, tile_d]
    out_shape = [jax.ShapeDtypeStruct(x_prompt.shape, jnp.float32), jax.ShapeDtypeStruct(x_sample.shape, jnp.float32)]
    cast_in, cast_out, cast_shapes = _cast_specs(l + 1, cast_windows, n_steps)
    in_specs += cast_in
    args += [big_f32[src] for src, _, _ in cast_windows]
    out_specs += cast_out
    out_shape += cast_shapes
    outs = pl.pallas_call(
        functools.partial(_back_kernel, l, ctx, dec, len(cast_windows)),
        grid=(n_steps,),
        in_specs=in_specs,
        out_specs=out_specs,
        out_shape=out_shape,
        compiler_params=_params(),
        name="back",
    )(*args)
    return outs[0], outs[1], tuple(outs[2:])


def _rope_swap_perm():
    quarter = QK_ROPE // 4
    idx = np.arange(QK_ROPE).reshape(2, 2, quarter)
    return idx[:, ::-1, :].reshape(-1)


def _rope_tables(n_tokens, rotate):
    if rotate:
        t = np.arange(n_tokens)
        pos = np.stack([t // GRID_W, t % GRID_W], axis=1).astype(np.float64)
        axis_dim = QK_ROPE // 2
        inv = 1.0 / (ROPE_THETA ** (np.arange(0, axis_dim, 2, dtype=np.float64) / axis_dim))
        ang = pos[:, :, None] * inv
        cos, sin = np.cos(ang), np.sin(ang)
        cfull = np.stack([cos, cos], axis=2).reshape(n_tokens, QK_ROPE)
        sfull = np.stack([-sin, sin], axis=2).reshape(n_tokens, QK_ROPE)
    else:
        cfull = np.ones((n_tokens, QK_ROPE))
        sfull = np.zeros((n_tokens, QK_ROPE))
    scale = (QK_NOPE + QK_ROPE) ** -0.5 * np.log2(np.e)
    qtab = np.concatenate([np.full((n_tokens, QK_NOPE), scale), cfull * scale, sfull * scale], axis=1)
    pad = np.zeros((n_tokens, LANES - QK_ROPE))
    tabs = (qtab, np.concatenate([cfull, pad], axis=1), np.concatenate([sfull, pad], axis=1))
    return tuple(jnp.asarray(tab, jnp.float32) for tab in tabs)


def _prepare_weights(g_pre, g_post, conv_w, conv_b, g_q, w_uq, g_kv, w_ukv, pool_w, pool_scale):
    perm = _rope_swap_perm()
    head_in = QK_NOPE + QK_ROPE
    place = np.zeros((N_HEADS * head_in, HP), np.float32)
    for h in range(N_HEADS):
        for c in range(head_in):
            place[h * head_in + c, h * HEAD_PAD + c] = 1.0
        for c in range(QK_ROPE):
            place[h * head_in + QK_NOPE + perm[c], h * HEAD_PAD + head_in + c] = 1.0
    assert w_uq.shape == (DEPTH, Q_LORA, N_HEADS * head_in)
    assert w_ukv.shape == (DEPTH, KV_LORA, HP) and QK_NOPE == V_HEAD and QK_NOPE + V_HEAD == LANES

    return {
        "g_pre": g_pre, "g_post": g_post, "conv_w": jnp.swapaxes(conv_w, 0, 1), "conv_b": conv_b,
        "g_q": g_q, "g_kv": g_kv, "pool_scale": pool_scale,
        "w_uq": w_uq, "q_place": jnp.asarray(place, jnp.bfloat16), "w_ukv": w_ukv, "pool_w": pool_w,
    }


def kernel(x_prompt, x_sample, cache_mla_latent, c, c_ctx, w_mod, b_mod, g_pre, g_post, w_in, conv_w, conv_b,
           g_q, w_uq, g_kv, w_ukv, pool_w, pool_scale, w_branch, w_o):
    n_dec = x_sample.shape[0]
    assert 1 + n_dec <= COND_ROWS
    assert w_in.shape == (DEPTH, D_MODEL, W_IN_COLS)
    w = _prepare_weights(g_pre, g_post, conv_w, conv_b, g_q, w_uq, g_kv, w_ukv, pool_w, pool_scale)
    big_f32 = (jnp.swapaxes(w_in, 1, 2), w_branch.reshape(DEPTH, 3 * BRANCH_DIM, D_MODEL), w_o)
    cache_t = jnp.swapaxes(cache_mla_latent, 2, 3)
    mod, w_front, kc, vc = _prologue(c_ctx, c, w_mod, b_mod, big_f32, cache_t, w_ukv)
    ctx_tabs = _rope_tables(x_prompt.shape[1], rotate=False)
    dec_tabs = _rope_tables(x_sample.shape[1], rotate=True)

    back_windows = (_W_BACK, _W_BRANCH, _W_OUT)
    h, hs, state_t, big_bf = x_prompt, x_sample, None, None
    for l in range(DEPTH):
        ctx_fr, dec_fr, cast = _front(l, h, hs, mod, w, w_front, ctx_tabs, dec_tabs, state_t, big_f32,
                                      back_windows if big_bf is None else ())
        state_t = ctx_fr[6]
        big_bf = cast if big_bf is None else big_bf
        next_windows = (_W_FRONT,) + back_windows if l + 1 < DEPTH else ()
        h, hs, cast = _back(l, h, hs, mod, w, big_bf, ctx_fr[:6], dec_fr, kc, vc, big_f32, next_windows)
        if next_windows:
            w_front, big_bf = cast[0], cast[1:]
    return (h, hs, jnp.swapaxes(state_t, 2, 3))
```
